```python
import math
import jax
import jax.numpy as jnp
from jax import lax
import numpy as np

D_MODEL = 1024
BATCH = 8
SEQ = 2048
DEPTH = 2
DEC_BATCH = 32
DEC_SEQ = 8
PAST_LEN = 8192
PAGE_SIZE = 128

HEAD_DIM = 64
H_SB = 8
H_FOX = 8
H_NSA = 16
KV_NSA = 4
G_NSA = H_NSA // KV_NSA
D_FF = 2816
MACARON = 0.5
QBLK = 128
NSA_QBLK = 32
L_CMP = 32
D_CMP = 16
SLC_BLOCK = 64
N_SELECT = 16
WINDOW = 512
N_BUCKETS = 32
REL_MAX_DIST = 128
FORCE_SCORE = 1e4
NEG = -1e30
EPS = 1e-6

kernel_name = 'hybrid_sb_fox_nsa_macaron_step'


def rms_norm(x, g):
    xf = x.astype(jnp.float32)
    y = xf * lax.rsqrt(jnp.mean(xf * xf, axis=-1, keepdims=True) + EPS)
    return (y * g.astype(jnp.float32)).astype(x.dtype)


def adanorm(x, g, m):
    return rms_norm(x, g) * (1.0 + m[:, None, 1]) + m[:, None, 0]


def swiglu(h, w_in, w_out):
    a, b = jnp.split(h @ w_in, 2, axis=-1)
    return (jax.nn.silu(a) * b) @ w_out


def masked_softmax(s, mask):
    s = jnp.where(mask, s, NEG)
    p = jnp.exp(s - jnp.max(s, axis=-1, keepdims=True)) * mask
    return p / jnp.maximum(jnp.sum(p, axis=-1, keepdims=True), 1e-30)


def rel_bucket(dist):
    n = jnp.maximum(dist, 0)
    exact = N_BUCKETS // 2
    nf = jnp.maximum(n, 1).astype(jnp.float32)
    large = exact + (jnp.log(nf / exact) / math.log(REL_MAX_DIST / exact) * (N_BUCKETS - exact)).astype(jnp.int32)
    return jnp.where(n < exact, n, jnp.minimum(large, N_BUCKETS - 1))


def gather_pages(cache, page_table):
    g = cache[page_table]
    return g.reshape(g.shape[0], g.shape[1] * g.shape[2], *g.shape[3:])


def dyn(x, start, size):
    return lax.dynamic_slice_in_dim(x, start, size, axis=1)


def sweep_query_blocks(fn, n_q, blk):
    out = lax.map(lambda i: fn(i * blk), jnp.arange(n_q // blk))
    out = jnp.moveaxis(out, 0, 1)
    return out.reshape(out.shape[0], n_q, *out.shape[3:])


def sb_attend(q, qpos, k, v, kpos):
    z = jnp.einsum('bthd,bshd->bhts', q, k).astype(jnp.float32) * HEAD_DIM ** -0.5
    causal = kpos[None, :] < qpos[:, None]
    log_keep = jnp.where(causal, jax.nn.log_sigmoid(-z), 0.0)
    log_rest = lax.cumsum(log_keep, axis=3, reverse=True) - log_keep
    w = jnp.where(causal, jnp.exp(jax.nn.log_sigmoid(z) + log_rest), 0.0)
    return jnp.einsum('bhts,bshd->bthd', w.astype(v.dtype), v)


def fox_attend(q, qpos, cum_q, k, v, kpos, cum_k):
    s = jnp.einsum('bthd,bshd->bhts', q, k).astype(jnp.float32) * HEAD_DIM ** -0.5
    s = s + (jnp.transpose(cum_q, (0, 2, 1))[..., :, None] - jnp.transpose(cum_k, (0, 2, 1))[:, :, None, :])
    p = masked_softmax(s, kpos[None, :] <= qpos[:, None])
    return jnp.einsum('bhts,bshd->bthd', p.astype(v.dtype), v)


def ab_project(h, w_in_ab, b_forget, fox_qk_gain):
    B, T, _ = h.shape
    da, df = H_SB * HEAD_DIM, H_FOX * HEAD_DIM
    cuts = np.cumsum([da, da, da, df, df, df]).tolist()
    qa, ka, va, qf, kf, vf, fl = jnp.split(h @ w_in_ab, cuts, axis=-1)
    sbh = lambda t: t.reshape(B, T, H_SB, HEAD_DIM)
    fxh = lambda t: t.reshape(B, T, H_FOX, HEAD_DIM)
    qf = rms_norm(fxh(qf), fox_qk_gain[0])
    kf = rms_norm(fxh(kf), fox_qk_gain[1])
    logf = jax.nn.log_sigmoid((fl + b_forget).astype(jnp.float32))
    return sbh(qa), sbh(ka), sbh(va), qf, kf, fxh(vf), logf


def ab_mixer_prompt(h, w_in_ab, b_forget, fox_qk_gain, w_out_ab):
    B, S, _ = h.shape
    qa, ka, va, qf, kf, vf, logf = ab_project(h, w_in_ab, b_forget, fox_qk_gain)
    pos = jnp.arange(S)
    ar = jnp.arange(QBLK)
    o_sb = sweep_query_blocks(lambda q0: sb_attend(dyn(qa, q0, QBLK), q0 + ar, ka, va, pos), S, QBLK)
    cum = jnp.cumsum(logf, axis=1)
    o_fox = sweep_query_blocks(
        lambda q0: fox_attend(dyn(qf, q0, QBLK), q0 + ar, dyn(cum, q0, QBLK), kf, vf, pos, cum), S, QBLK)
    o = jnp.concatenate([o_sb.reshape(B, S, -1), o_fox.reshape(B, S, -1)], axis=-1) @ w_out_ab
    return o, (jnp.stack([ka, va], axis=2), jnp.stack([kf, vf], axis=2), logf)


def ab_mixer_sample(h, cache_sb_kv, cache_fox_kv, cache_fox_logf, page_table,
                    w_in_ab, b_forget, fox_qk_gain, w_out_ab):
    B, T, _ = h.shape
    P = page_table.shape[1] * PAGE_SIZE
    qa, ka, va, qf, kf, vf, logf = ab_project(h, w_in_ab, b_forget, fox_qk_gain)
    sb_past = gather_pages(cache_sb_kv, page_table)
    fox_past = gather_pages(cache_fox_kv, page_table)
    logf_past = gather_pages(cache_fox_logf, page_table)
    kpos = jnp.arange(P + T)
    qpos = P + jnp.arange(T)
    k_sb = jnp.concatenate([sb_past[:, :, 0], ka], axis=1)
    v_sb = jnp.concatenate([sb_past[:, :, 1], va], axis=1)
    o_sb = sb_attend(qa, qpos, k_sb, v_sb, kpos)
    k_fx = jnp.concatenate([fox_past[:, :, 0], kf], axis=1)
    v_fx = jnp.concatenate([fox_past[:, :, 1], vf], axis=1)
    cum = jnp.cumsum(jnp.concatenate([logf_past.astype(jnp.float32), logf], axis=1), axis=1)
    o_fox = fox_attend(qf, qpos, cum[:, P:], k_fx, v_fx, kpos, cum)
    o = jnp.concatenate([o_sb.reshape(B, T, -1), o_fox.reshape(B, T, -1)], axis=-1) @ w_out_ab
    return o, (jnp.stack([ka, va], axis=2), jnp.stack([kf, vf], axis=2), logf)


def nsa_project(h, w_in_nsa, b_nsa_gate, nsa_qk_gain):
    B, T, _ = h.shape
    dq, dkv = H_NSA * HEAD_DIM, KV_NSA * HEAD_DIM
    cuts = np.cumsum([dq] + [dkv] * 6).tolist()
    q, kc, vc, ks, vs, kw, vw, gl = jnp.split(h @ w_in_nsa, cuts, axis=-1)
    kvh = lambda t: t.reshape(B, T, KV_NSA, HEAD_DIM)
    q = rms_norm(q.reshape(B, T, H_NSA, HEAD_DIM), nsa_qk_gain[0])
    ks = rms_norm(kvh(ks), nsa_qk_gain[2])
    kw = rms_norm(kvh(kw), nsa_qk_gain[3])
    gates = jax.nn.sigmoid(gl + b_nsa_gate).reshape(B, T, 3, H_NSA)
    return q, kvh(kc), kvh(vc), ks, kvh(vs), kw, kvh(vw), gates


def compress(x, w, pe):
    B, N = x.shape[:2]
    n_chunk = N // D_CMP
    xc = x[:, :n_chunk * D_CMP].reshape(B, n_chunk, D_CMP, KV_NSA, HEAD_DIM)
    first = jnp.einsum('bcjgd,jde->bcge', xc + pe[:D_CMP][None, None, :, None, :], w[:D_CMP])
    second = jnp.einsum('bcjgd,jde->bcge', xc + pe[D_CMP:][None, None, :, None, :], w[D_CMP:])
    return first[:, :-1] + second[:, 1:]


def to_blocks(x):
    B, N = x.shape[:2]
    ns = -(-N // SLC_BLOCK)
    x = jnp.pad(x, ((0, 0), (0, ns * SLC_BLOCK - N), (0, 0), (0, 0)))
    return x.reshape(B, ns, SLC_BLOCK, KV_NSA, HEAD_DIM)


def nsa_branch_keys(kc_all, vc_all, ks_all, vs_all, cmp_w, cmp_pe, nsa_qk_gain):
    kcmp = rms_norm(compress(kc_all, cmp_w[0], cmp_pe[0]), nsa_qk_gain[1])
    vcmp = compress(vc_all, cmp_w[1], cmp_pe[1])
    cend = jnp.arange(kcmp.shape[1]) * D_CMP + L_CMP - 1
    return kcmp, vcmp, cend, to_blocks(ks_all), to_blocks(vs_all)


def nsa_attend(q, qpos, gates, kcmp, vcmp, cend, ks_blk, vs_blk, kw, vw, wpos, rel_bias):
    B, Tq = q.shape[:2]
    qg = q.reshape(B, Tq, KV_NSA, G_NSA, HEAD_DIM)
    rel = rel_bias.reshape(N_BUCKETS, KV_NSA, G_NSA)
    sc = HEAD_DIM ** -0.5
    dc = qpos[:, None] - cend[None, :]
    s = jnp.einsum('btgzd,bcgd->btgzc', qg, kcmp).astype(jnp.float32) * sc
    s = s + jnp.transpose(rel[rel_bucket(dc)], (0, 2, 3, 1))
    p_cmp = masked_softmax(s, (dc >= 0)[:, None, None, :])
    o_cmp = jnp.einsum('btgzc,bcgd->btgzd', p_cmp.astype(vcmp.dtype), vcmp)
    nc, ns = kcmp.shape[1], ks_blk.shape[1]
    ci, bj = jnp.arange(nc)[:, None], jnp.arange(ns)[None, :]
    overlap = ((ci * D_CMP <= bj * SLC_BLOCK + SLC_BLOCK - 1) & (ci * D_CMP + L_CMP - 1 >= bj * SLC_BLOCK)).astype(jnp.float32)
    imp = jnp.einsum('btgzc,cn->btgn', p_cmp, overlap)
    blk = jnp.arange(ns)[None, :]
    cur = (qpos // SLC_BLOCK)[:, None]
    valid = (blk * SLC_BLOCK <= qpos[:, None])[:, None, :]
    forced = ((blk == 0) | (blk == cur) | (blk == cur - 1))[:, None, :]
    score = jnp.where(valid, jnp.where(forced, FORCE_SCORE, imp), -jnp.inf)
    _, idx = lax.top_k(score, min(N_SELECT, ns))
    bi = jnp.arange(B)[:, None, None, None]
    gi = jnp.arange(KV_NSA)[None, None, :, None]
    n_sel = idx.shape[-1] * SLC_BLOCK
    kg = ks_blk[bi, idx, :, gi, :].reshape(B, Tq, KV_NSA, n_sel, HEAD_DIM)
    vg = vs_blk[bi, idx, :, gi, :].reshape(B, Tq, KV_NSA, n_sel, HEAD_DIM)
    spos = (idx[..., None] * SLC_BLOCK + jnp.arange(SLC_BLOCK)).reshape(B, Tq, KV_NSA, n_sel)
    ds = qpos[None, :, None, None] - spos
    s = jnp.einsum('btgzd,btgnd->btgzn', qg, kg).astype(jnp.float32) * sc
    s = s + jnp.moveaxis(rel[rel_bucket(ds), gi], -1, 3)
    p = masked_softmax(s, (ds >= 0)[:, :, :, None, :])
    o_slc = jnp.einsum('btgzn,btgnd->btgzd', p.astype(vg.dtype), vg)
    dw = qpos[:, None] - wpos[None, :]
    s = jnp.einsum('btgzd,bsgd->btgzs', qg, kw).astype(jnp.float32) * sc
    s = s + jnp.transpose(rel[rel_bucket(dw)], (0, 2, 3, 1))
    wmask = (dw >= 0) & (dw < WINDOW) & (wpos[None, :] >= 0)
    p = masked_softmax(s, wmask[:, None, None, :])
    o_win = jnp.einsum('btgzs,bsgd->btgzd', p.astype(vw.dtype), vw)
    g = gates.reshape(B, Tq, 3, KV_NSA, G_NSA)[..., None]
    o = g[:, :, 0] * o_cmp + g[:, :, 1] * o_slc + g[:, :, 2] * o_win
    return o.reshape(B, Tq, H_NSA * HEAD_DIM)


def nsa_mixer_prompt(h, w_in_nsa, b_nsa_gate, nsa_qk_gain, cmp_w, cmp_pe, rel_bias, w_out_nsa):
    B, S, _ = h.shape
    q, kc, vc, ks, vs, kw, vw, gates = nsa_project(h, w_in_nsa, b_nsa_gate, nsa_qk_gain)
    kcmp, vcmp, cend, ks_blk, vs_blk = nsa_branch_keys(kc, vc, ks, vs, cmp_w, cmp_pe, nsa_qk_gain)
    padw = ((0, 0), (WINDOW, 0), (0, 0), (0, 0))
    kw_pad, vw_pad = jnp.pad(kw, padw), jnp.pad(vw, padw)
    band = WINDOW + NSA_QBLK
    ar, arb = jnp.arange(NSA_QBLK), jnp.arange(band)

    def block(q0):
        return nsa_attend(dyn(q, q0, NSA_QBLK), q0 + ar, dyn(gates, q0, NSA_QBLK), kcmp, vcmp, cend,
                          ks_blk, vs_blk, dyn(kw_pad, q0, band), dyn(vw_pad, q0, band), q0 - WINDOW + arb, rel_bias)

    o = sweep_query_blocks(block, S, NSA_QBLK) @ w_out_nsa
    win_len = min(WINDOW, S)
    return o, (jnp.stack([kc, vc], axis=2), jnp.stack([ks, vs], axis=2),
               jnp.stack([kw, vw], axis=2)[:, S - win_len:])


def nsa_mixer_sample(h, cache_cmp_kv, cache_slc_kv, state_win_kv, page_table,
                     w_in_nsa, b_nsa_gate, nsa_qk_gain, cmp_w, cmp_pe, rel_bias, w_out_nsa):
    B, T, _ = h.shape
    P = page_table.shape[1] * PAGE_SIZE
    q, kc, vc, ks, vs, kw, vw, gates = nsa_project(h, w_in_nsa, b_nsa_gate, nsa_qk_gain)
    cmp_past = gather_pages(cache_cmp_kv, page_table)
    slc_past = gather_pages(cache_slc_kv, page_table)
    kcmp, vcmp, cend, ks_blk, vs_blk = nsa_branch_keys(
        jnp.concatenate([cmp_past[:, :, 0], kc], axis=1), jnp.concatenate([cmp_past[:, :, 1], vc], axis=1),
        jnp.concatenate([slc_past[:, :, 0], ks], axis=1), jnp.concatenate([slc_past[:, :, 1], vs], axis=1),
        cmp_w, cmp_pe, nsa_qk_gain)
    wb = state_win_kv.shape[1]
    kw_all = jnp.concatenate([state_win_kv[:, :, 0], kw], axis=1)
    vw_all = jnp.concatenate([state_win_kv[:, :, 1], vw], axis=1)
    wpos = P - wb + jnp.arange(wb + T)
    o = nsa_attend(q, P + jnp.arange(T), gates, kcmp, vcmp, cend, ks_blk, vs_blk,
                   kw_all, vw_all, wpos, rel_bias) @ w_out_nsa
    new_len = min(WINDOW, wb + T)
    new_win = jnp.stack([kw_all, vw_all], axis=2)[:, wb + T - new_len:]
    return o, (jnp.stack([kc, vc], axis=2), jnp.stack([ks, vs], axis=2), new_win)


def trunk(x, c, mix_even, mix_odd, norm_gain, w_ada, b_ada, ffn_w_in, ffn_w_out):
    states = []
    cs = jax.nn.silu(c)
    for l in range(DEPTH):
        mod = (cs @ w_ada[l] + b_ada[l]).reshape(c.shape[0], 3, 3, D_MODEL)
        x = x + MACARON * mod[:, 0, 2][:, None] * swiglu(adanorm(x, norm_gain[l, 0], mod[:, 0]), ffn_w_in[l, 0], ffn_w_out[l, 0])
        o, st = (mix_even if l % 2 == 0 else mix_odd)(adanorm(x, norm_gain[l, 1], mod[:, 1]))
        x = x + mod[:, 1, 2][:, None] * o
        x = x + MACARON * mod[:, 2, 2][:, None] * swiglu(adanorm(x, norm_gain[l, 2], mod[:, 2]), ffn_w_in[l, 1], ffn_w_out[l, 1])
        states.append(st)
    return x, states[0], states[1]


def setup_inputs(seed: int = 0) -> dict:
    key = jax.random.key(seed)
    ks = jax.random.split(key, 32)
    nrm = lambda k, shape, s=1.0: s * jax.random.normal(k, shape, jnp.float32)
    hd = HEAD_DIM
    n_pages = PAST_LEN // PAGE_SIZE
    n_used = DEC_BATCH * n_pages
    n_pool = n_used + n_used // 4
    win_buf = min(WINDOW, PAST_LEN)
    d_in_ab = 3 * (H_SB + H_FOX) * hd + H_FOX
    d_in_nsa = H_NSA * hd + 6 * KV_NSA * hd + 3 * H_NSA
    page_table = jax.random.permutation(ks[8], n_pool)[:n_used].reshape(DEC_BATCH, n_pages).astype(jnp.int32)
    return {
        'x_prompt': nrm(ks[0], (BATCH, SEQ, D_MODEL)),
        'x_sample': nrm(ks[1], (DEC_BATCH, DEC_SEQ, D_MODEL)),
        'cache_sb_kv': nrm(ks[2], (n_pool, PAGE_SIZE, 2, H_SB, hd)),
        'cache_fox_kv': nrm(ks[3], (n_pool, PAGE_SIZE, 2, H_FOX, hd)),
        'cache_fox_logf': jax.nn.log_sigmoid(nrm(ks[4], (n_pool, PAGE_SIZE, H_FOX))),
        'cache_cmp_kv': nrm(ks[5], (n_pool, PAGE_SIZE, 2, KV_NSA, hd)),
        'cache_slc_kv': nrm(ks[6], (n_pool, PAGE_SIZE, 2, KV_NSA, hd)),
        'state_win_kv': nrm(ks[7], (DEC_BATCH, win_buf, 2, KV_NSA, hd)),
        'page_table': page_table,
        'c_prompt': nrm(ks[9], (BATCH, D_MODEL)),
        'c_sample': nrm(ks[10], (DEC_BATCH, D_MODEL)),
        'norm_gain': 1.0 + nrm(ks[11], (DEPTH, 3, D_MODEL), 0.05),
        'w_ada': nrm(ks[12], (DEPTH, D_MODEL, 9 * D_MODEL), 0.5 * D_MODEL ** -0.5),
        'b_ada': nrm(ks[13], (DEPTH, 9 * D_MODEL), 0.02),
        'ffn_w_in': nrm(ks[14], (DEPTH, 2, D_MODEL, 2 * D_FF), D_MODEL ** -0.5),
        'ffn_w_out': nrm(ks[15], (DEPTH, 2, D_FF, D_MODEL), D_FF ** -0.5),
        'w_in_ab': nrm(ks[16], (D_MODEL, d_in_ab), D_MODEL ** -0.5),
        'b_forget': nrm(ks[17], (H_FOX,), 0.1),
        'fox_qk_gain': 1.0 + nrm(ks[18], (2, hd), 0.05),
        'w_out_ab': nrm(ks[19], ((H_SB + H_FOX) * hd, D_MODEL), ((H_SB + H_FOX) * hd) ** -0.5),
        'w_in_nsa': nrm(ks[20], (D_MODEL, d_in_nsa), D_MODEL ** -0.5),
        'b_nsa_gate': nrm(ks[21], (3 * H_NSA,), 0.1),
        'nsa_qk_gain': 1.0 + nrm(ks[22], (4, hd), 0.05),
        'cmp_w': nrm(ks[23], (2, L_CMP, hd, hd), (L_CMP * hd) ** -0.5),
        'cmp_pe': nrm(ks[24], (2, L_CMP, hd), 0.5),
        'rel_bias': nrm(ks[25], (N_BUCKETS, H_NSA), 0.5),
        'w_out_nsa': nrm(ks[26], (H_NSA * hd, D_MODEL), (H_NSA * hd) ** -0.5),
    }


def reference(x_prompt, x_sample, cache_sb_kv, cache_fox_kv, cache_fox_logf, cache_cmp_kv, cache_slc_kv,
              state_win_kv, page_table, c_prompt, c_sample, norm_gain, w_ada, b_ada, ffn_w_in, ffn_w_out,
              w_in_ab, b_forget, fox_qk_gain, w_out_ab, w_in_nsa, b_nsa_gate, nsa_qk_gain, cmp_w, cmp_pe,
              rel_bias, w_out_nsa):
    ab_p = lambda h: ab_mixer_prompt(h, w_in_ab, b_forget, fox_qk_gain, w_out_ab)
    nsa_p = lambda h: nsa_mixer_prompt(h, w_in_nsa, b_nsa_gate, nsa_qk_gain, cmp_w, cmp_pe, rel_bias, w_out_nsa)
    ab_s = lambda h: ab_mixer_sample(h, cache_sb_kv, cache_fox_kv, cache_fox_logf, page_table,
                                     w_in_ab, b_forget, fox_qk_gain, w_out_ab)
    nsa_s = lambda h: nsa_mixer_sample(h, cache_cmp_kv, cache_slc_kv, state_win_kv, page_table,
                                       w_in_nsa, b_nsa_gate, nsa_qk_gain, cmp_w, cmp_pe, rel_bias, w_out_nsa)
    y_prompt, (p_sb_kv, p_fox_kv, p_fox_logf), (p_cmp_kv, p_slc_kv, p_win_kv) = trunk(
        x_prompt, c_prompt, ab_p, nsa_p, norm_gain, w_ada, b_ada, ffn_w_in, ffn_w_out)
    y_sample, (s_sb_kv, s_fox_kv, s_fox_logf), (s_cmp_kv, s_slc_kv, s_win_kv) = trunk(
        x_sample, c_sample, ab_s, nsa_s, norm_gain, w_ada, b_ada, ffn_w_in, ffn_w_out)
    return (y_prompt, y_sample, p_sb_kv, s_sb_kv, p_fox_kv, s_fox_kv, p_fox_logf, s_fox_logf,
            p_cmp_kv, s_cmp_kv, p_slc_kv, s_slc_kv, p_win_kv, s_win_kv)
```

```python
import functools
import math

import numpy as np
import jax
import jax.numpy as jnp
from jax import lax
from jax.experimental import pallas as pl
from jax.experimental.pallas import tpu as pltpu

F32 = jnp.float32
BF16 = jnp.bfloat16

HEAD_DIM = 64
PAGE = 128
L_CMP = 32
D_CMP = 16
SLC_BLOCK = 64
N_SELECT = 16
WINDOW = 512
N_BUCKETS = 32
REL_MAX_DIST = 128
FORCE_SCORE = 1e4
NEG = -1e30
EPS = 1e-6
MACARON = 0.5
LANES = 128
VMEM_LIMIT = 56 * 1024 * 1024
FF_CHUNK = 256
TILE = 128
PAGES_PER_STEP = 8


def _params(*sem):
    return pltpu.CompilerParams(dimension_semantics=sem, vmem_limit_bytes=VMEM_LIMIT)


def _dot(a, b):
    return jnp.dot(a, b, preferred_element_type=F32)


def _dot_nt(a, b):
    return lax.dot_general(a, b, (((1,), (1,)), ((), ())), preferred_element_type=F32)


def _split2(x):
    hi = x.astype(BF16)
    lo = (x - hi.astype(F32)).astype(BF16)
    return hi, lo


def _split3(x):
    hi = x.astype(BF16)
    r = x - hi.astype(F32)
    mid = r.astype(BF16)
    lo = (r - mid.astype(F32)).astype(BF16)
    return hi, mid, lo


def _dot_x2(x, w):
    hi, lo = _split2(x)
    return _dot(hi, w) + _dot(lo, w)


def _dot_x3(x, w):
    hi, mid, lo = _split3(x)
    return _dot(hi, w) + _dot(mid, w) + _dot(lo, w)


def _dot_l3(w, x):
    hi, mid, lo = _split3(x)
    return _dot(w, hi) + _dot(w, mid) + _dot(w, lo)


def _log_sigmoid(x):
    return jnp.minimum(x, 0.0) - jnp.log(1.0 + jnp.exp(-jnp.abs(x)))


def _adanorm(x, g, scale, shift):
    ms = jnp.mean(x * x, axis=-1, keepdims=True)
    return (x * lax.rsqrt(ms + EPS) * g) * (1.0 + scale) + shift


def _head_norm(t, bd):
    return t * lax.rsqrt(_dot_x2(t * t, bd) + EPS)


def _iota(shape, dim):
    return lax.broadcasted_iota(jnp.int32, shape, dim)


def _mod_kernel(c_ref, w_ref, b_ref, o_ref):
    c = c_ref[...]
    cs = c * jax.nn.sigmoid(c)
    ch, cl = _split2(cs)
    wh, wl = _split2(w_ref[0])
    o_ref[0] = _dot(ch, wh) + _dot(cl, wh) + _dot(ch, wl) + b_ref[0]


def _modulation(c_all, w_ada, b_ada):
    depth, d, n = w_ada.shape
    rows = c_all.shape[0]
    tn = 1024
    return pl.pallas_call(
        _mod_kernel, name="modulation",
        grid=(depth, n // tn),
        in_specs=[pl.BlockSpec((rows, d), lambda l, j: (0, 0)),
                  pl.BlockSpec((1, d, tn), lambda l, j: (l, 0, j)),
                  pl.BlockSpec((1, 1, tn), lambda l, j: (l, 0, j))],
        out_specs=pl.BlockSpec((1, rows, tn), lambda l, j: (l, 0, j)),
        out_shape=jax.ShapeDtypeStruct((depth, rows, n), F32),
        compiler_params=_params("arbitrary", "arbitrary"),
    )(c_all, w_ada, b_ada.reshape(depth, 1, n))


def _ffn_kernel(x_ref, shift_ref, scale_ref, gate_ref, g_ref, win_ref, wout_ref, o_ref, hid_ref, *, d_ff):
    x = x_ref[...]
    bb, tt, d = x.shape
    h = _adanorm(x, g_ref[...], scale_ref[...], shift_ref[...]).reshape(bb * tt, d).astype(BF16)
    for c in range(d_ff // FF_CHUNK):
        a = _dot(h, win_ref[:, c * FF_CHUNK:(c + 1) * FF_CHUNK])
        b = _dot(h, win_ref[:, d_ff + c * FF_CHUNK:d_ff + (c + 1) * FF_CHUNK])
        hid_ref[:, c * FF_CHUNK:(c + 1) * FF_CHUNK] = (a * jax.nn.sigmoid(a) * b).astype(BF16)
    o = _dot(hid_ref[...], wout_ref[...]).reshape(bb, tt, d)
    o_ref[...] = x + (MACARON * gate_ref[...]) * o


def _ffn(x, mod3, g, w_in, w_out, bb, tt):
    b, t, d = x.shape
    d_ff = w_out.shape[0]
    shift, scale, gate = mod3
    mspec = pl.BlockSpec((bb, 1, d), lambda i, j: (i, 0, 0))
    xspec = pl.BlockSpec((bb, tt, d), lambda i, j: (i, j, 0))
    return pl.pallas_call(
        functools.partial(_ffn_kernel, d_ff=d_ff), name="ffn",
        grid=(b // bb, t // tt),
        in_specs=[xspec, mspec, mspec, mspec,
                  pl.BlockSpec((1, 1, d), lambda i, j: (0, 0, 0)),
                  pl.BlockSpec((d, 2 * d_ff), lambda i, j: (0, 0)),
                  pl.BlockSpec((d_ff, d), lambda i, j: (0, 0))],
        out_specs=xspec,
        out_shape=jax.ShapeDtypeStruct(x.shape, F32),
        scratch_shapes=[pltpu.VMEM((bb * tt, d_ff), BF16)],
        compiler_params=_params("arbitrary", "arbitrary"),
    )(x, shift, scale, gate, g.reshape(1, 1, d), w_in, w_out)


def _outproj_kernel(*refs, n_in):
    x_ref, gate_ref = refs[0], refs[1]
    o_refs = refs[2:2 + n_in]
    w_refs = refs[2 + n_in:2 + 2 * n_in]
    out_ref = refs[2 + 2 * n_in]
    x = x_ref[...]
    bb, tt, d = x.shape
    y = None
    for o_ref, w_ref in zip(o_refs, w_refs):
        o = o_ref[...].astype(F32).reshape(bb * tt, o_ref.shape[-1]).astype(BF16)
        part = _dot(o, w_ref[...])
        y = part if y is None else y + part
    out_ref[...] = x + gate_ref[...] * y.reshape(bb, tt, d)


def _outproj(x, gate, outs, weights, bb, tt):
    b, t, d = x.shape
    n_in = len(outs)
    xspec = pl.BlockSpec((bb, tt, d), lambda i, j: (i, j, 0))
    in_specs = [xspec, pl.BlockSpec((bb, 1, d), lambda i, j: (i, 0, 0))]
    in_specs += [pl.BlockSpec((bb, tt, o.shape[-1]), lambda i, j: (i, j, 0)) for o in outs]
    in_specs += [pl.BlockSpec(w.shape, lambda i, j: (0, 0)) for w in weights]
    return pl.pallas_call(
        functools.partial(_outproj_kernel, n_in=n_in), name="outproj",
        grid=(b // bb, t // tt),
        in_specs=in_specs,
        out_specs=xspec,
        out_shape=jax.ShapeDtypeStruct(x.shape, F32),
        compiler_params=_params("arbitrary", "arbitrary"),
    )(x, gate, *outs, *weights)


def _proj_ab_kernel(x_ref, shift_ref, scale_ref, g_ref, w_ref, bd_ref, gq_ref, gk_ref, bf_ref,
                    qsb_ref, sbkv_ref, qfx_ref, fxkv_ref, logf_ref, logfp_ref, *, da, n_f):
    x = x_ref[...]
    bb, tt, d = x.shape
    m = bb * tt
    h = _adanorm(x, g_ref[...], scale_ref[...], shift_ref[...]).reshape(m, d).astype(BF16)
    sc = HEAD_DIM ** -0.5
    qsb_ref[...] = (_dot(h, w_ref[:, 0:da]) * sc).reshape(bb, tt, da).astype(BF16)
    sbkv_ref[...] = _dot(h, w_ref[:, da:3 * da]).reshape(bb, tt, 2 * da)
    bd = bd_ref[...]
    qf = _head_norm(_dot(h, w_ref[:, 3 * da:4 * da]), bd) * gq_ref[...]
    qfx_ref[...] = (qf * sc).reshape(bb, tt, da).astype(BF16)
    kf = _head_norm(_dot(h, w_ref[:, 4 * da:5 * da]), bd) * gk_ref[...]
    fxkv_ref[:, :, 0:da] = kf.reshape(bb, tt, da)
    fxkv_ref[:, :, da:2 * da] = _dot(h, w_ref[:, 5 * da:6 * da]).reshape(bb, tt, da)
    lf = _log_sigmoid(_dot(h, w_ref[:, 6 * da:6 * da + LANES]) + bf_ref[...])
    lf = jnp.where(_iota(lf.shape, 1) < n_f, lf, 0.0)
    logfp_ref[...] = lf.reshape(bb, tt, LANES)
    logf_ref[...] = lf[:, 0:n_f].reshape(bb, tt, n_f)


def _proj_ab(x, mod3, g, w, bd, gq, gk, bfp, n_f, bb, tt):
    b, t, d = x.shape
    da = bd.shape[0]
    shift, scale, _ = mod3
    mspec = pl.BlockSpec((bb, 1, d), lambda i, j: (i, 0, 0))
    xspec = pl.BlockSpec((bb, tt, d), lambda i, j: (i, j, 0))
    row = lambda width: pl.BlockSpec((1, width), lambda i, j: (0, 0))
    ospec = lambda width: pl.BlockSpec((bb, tt, width), lambda i, j: (i, j, 0))
    return pl.pallas_call(
        functools.partial(_proj_ab_kernel, da=da, n_f=n_f), name="proj_ab",
        grid=(b // bb, t // tt),
        in_specs=[xspec, mspec, mspec, pl.BlockSpec((1, 1, d), lambda i, j: (0, 0, 0)),
                  pl.BlockSpec(w.shape, lambda i, j: (0, 0)), pl.BlockSpec(bd.shape, lambda i, j: (0, 0)),
                  row(da), row(da), row(LANES)],
        out_specs=[ospec(da), ospec(2 * da), ospec(da), ospec(2 * da), ospec(n_f), ospec(LANES)],
        out_shape=[jax.ShapeDtypeStruct((b, t, da), BF16), jax.ShapeDtypeStruct((b, t, 2 * da), F32),
                   jax.ShapeDtypeStruct((b, t, da), BF16), jax.ShapeDtypeStruct((b, t, 2 * da), F32),
                   jax.ShapeDtypeStruct((b, t, n_f), F32), jax.ShapeDtypeStruct((b, t, LANES), F32)],
        compiler_params=_params("arbitrary", "arbitrary"),
    )(x, shift, scale, g.reshape(1, 1, d), w, bd, gq, gk, bfp)


def _proj_nsa_kernel(x_ref, shift_ref, scale_ref, g_ref, w_ref, bd_ref, gq_ref, gs_ref, gw_ref, bg_ref,
                     q_ref, cmp_ref, slc_ref, win_ref, gates_ref, *, dq, dkv):
    x = x_ref[...]
    bb, tt, d = x.shape
    m = bb * tt
    h = _adanorm(x, g_ref[...], scale_ref[...], shift_ref[...]).reshape(m, d).astype(BF16)
    bd = bd_ref[...]
    sc = HEAD_DIM ** -0.5
    for c in range(dq // dkv):
        qc = _head_norm(_dot(h, w_ref[:, c * dkv:(c + 1) * dkv]), bd) * gq_ref[...]
        q_ref[:, :, c * dkv:(c + 1) * dkv] = (qc * sc).reshape(bb, tt, dkv).astype(BF16)
    o = dq
    cmp_ref[...] = _dot(h, w_ref[:, o:o + 2 * dkv]).reshape(bb, tt, 2 * dkv)
    o += 2 * dkv
    ks = _head_norm(_dot(h, w_ref[:, o:o + dkv]), bd) * gs_ref[...]
    slc_ref[:, :, 0:dkv] = ks.reshape(bb, tt, dkv)
    slc_ref[:, :, dkv:2 * dkv] = _dot(h, w_ref[:, o + dkv:o + 2 * dkv]).reshape(bb, tt, dkv)
    o += 2 * dkv
    kw = _head_norm(_dot(h, w_ref[:, o:o + dkv]), bd) * gw_ref[...]
    win_ref[:, :, 0:dkv] = kw.reshape(bb, tt, dkv)
    win_ref[:, :, dkv:2 * dkv] = _dot(h, w_ref[:, o + dkv:o + 2 * dkv]).reshape(bb, tt, dkv)
    o += 2 * dkv
    gates_ref[...] = jax.nn.sigmoid(_dot(h, w_ref[:, o:o + LANES]) + bg_ref[...]).reshape(bb, tt, LANES)


def _proj_nsa(x, mod3, g, w, bd, gq, gs, gw, bg, dq, bb, tt):
    b, t, d = x.shape
    dkv = bd.shape[0]
    shift, scale, _ = mod3
    mspec = pl.BlockSpec((bb, 1, d), lambda i, j: (i, 0, 0))
    xspec = pl.BlockSpec((bb, tt, d), lambda i, j: (i, j, 0))
    row = lambda width: pl.BlockSpec((1, width), lambda i, j: (0, 0))
    ospec = lambda width: pl.BlockSpec((bb, tt, width), lambda i, j: (i, j, 0))
    return pl.pallas_call(
        functools.partial(_proj_nsa_kernel, dq=dq, dkv=dkv), name="proj_nsa",
        grid=(b // bb, t // tt),
        in_specs=[xspec, mspec, mspec, pl.BlockSpec((1, 1, d), lambda i, j: (0, 0, 0)),
                  pl.BlockSpec(w.shape, lambda i, j: (0, 0)), pl.BlockSpec(bd.shape, lambda i, j: (0, 0)),
                  row(dkv), row(dkv), row(dkv), row(LANES)],
        out_specs=[ospec(dq), ospec(2 * dkv), ospec(2 * dkv), ospec(2 * dkv), ospec(LANES)],
        out_shape=[jax.ShapeDtypeStruct((b, t, dq), BF16), jax.ShapeDtypeStruct((b, t, 2 * dkv), F32),
                   jax.ShapeDtypeStruct((b, t, 2 * dkv), F32), jax.ShapeDtypeStruct((b, t, 2 * dkv), F32),
                   jax.ShapeDtypeStruct((b, t, LANES), F32)],
        compiler_params=_params("arbitrary", "arbitrary"),
    )(x, shift, scale, g.reshape(1, 1, d), w, bd, gq, gs, gw, bg)


def _block_diag_rows(q, n_heads):
    q = q.astype(F32)
    head = _iota(q.shape, 1) // HEAD_DIM
    return jnp.concatenate([jnp.where(head == h, q, 0.0) for h in range(n_heads)], axis=0).astype(BF16)


def _gather_diag(acc, tq, n_heads):
    head = _iota((tq, acc.shape[1]), 1) // HEAD_DIM
    out = jnp.zeros((tq, acc.shape[1]), F32)
    for h in range(n_heads):
        out = out + jnp.where(head == h, acc[h * tq:(h + 1) * tq, :], 0.0)
    return out


def _row_query(shape, tq):
    return _iota(shape, 0) % tq


def _suffix_matrix(tk):
    j = _iota((tk, 2 * tk), 0)
    s = _iota((tk, 2 * tk), 1)
    return jnp.where((s >= tk) | (j > s), 1.0, 0.0).astype(BF16)


def _sb_tile(qbd, k, v, um, carry, mask):
    tk = k.shape[0]
    z = _dot_nt(qbd, k)
    lk = -(jnp.maximum(z, 0.0) + jnp.log(1.0 + jnp.exp(-jnp.abs(z))))
    if mask is not None:
        lk = jnp.where(mask, lk, 0.0)
    hi, lo = _split2(lk)
    rs = _dot(hi, um) + _dot(lo, um)
    w = jnp.exp(z + lk + rs[:, :tk] + carry)
    if mask is not None:
        w = jnp.where(mask, w, 0.0)
    return _dot(w.astype(BF16), v), carry + rs[:, tk:]


def _softmax_tile(qbd, k, v, m_ref, l_ref, acc_ref, bias, mask, guard):
    s = _dot_nt(qbd, k)
    if bias is not None:
        s = s + bias
    if mask is not None:
        s = jnp.where(mask, s, NEG)
    m_prev = m_ref[:, 0:1]
    m_new = jnp.maximum(m_prev, jnp.max(s, axis=1, keepdims=True))
    p = jnp.exp(s - m_new)
    if guard:
        p = jnp.where(mask, p, 0.0)
    alpha = jnp.exp(m_prev - m_new)
    l_new = alpha * l_ref[:, 0:1] + jnp.sum(p, axis=1, keepdims=True)
    acc_ref[...] = alpha * acc_ref[...] + _dot(p.astype(BF16), v)
    m_ref[...] = jnp.broadcast_to(m_new, m_ref.shape)
    l_ref[...] = jnp.broadcast_to(l_new, l_ref.shape)


def _softmax_init(m_ref, l_ref, acc_ref):
    m_ref[...] = jnp.full(m_ref.shape, NEG, F32)
    l_ref[...] = jnp.zeros(l_ref.shape, F32)
    acc_ref[...] = jnp.zeros(acc_ref.shape, F32)


def _softmax_out(l_ref, acc_ref):
    return acc_ref[...] / jnp.maximum(l_ref[:, 0:1], 1e-30)


def _sb_prompt_kernel(q_ref, k_ref, v_ref, o_ref, kbf, vbf, acc, car, *, tq, hg):
    qi = pl.program_id(2)

    @pl.when(qi == 0)
    def _():
        kbf[...] = k_ref[0].astype(BF16)
        vbf[...] = v_ref[0].astype(BF16)

    qbd = _block_diag_rows(q_ref[0], hg)
    r = hg * tq
    um = _suffix_matrix(tq)
    q0 = pl.multiple_of(qi * tq, tq)
    mask = _iota((r, tq), 1) < _row_query((r, tq), tq)
    pv, c = _sb_tile(qbd, kbf[pl.ds(q0, tq), :], vbf[pl.ds(q0, tq), :], um, jnp.zeros((r, tq), F32), mask)
    acc[...] = pv
    car[...] = c

    def body(it, _):
        k0 = pl.multiple_of((qi - 1 - it) * tq, tq)
        pv, c = _sb_tile(qbd, kbf[pl.ds(k0, tq), :], vbf[pl.ds(k0, tq), :], um, car[...], None)
        acc[...] += pv
        car[...] = c
        return 0

    lax.fori_loop(0, qi, body, 0)
    o_ref[0] = _gather_diag(acc[...], tq, hg).astype(BF16)


def _sb_prompt(q, kv, tq, hg):
    b, s, da = q.shape
    c = hg * HEAD_DIM
    ng = da // c
    r = hg * tq
    return pl.pallas_call(
        functools.partial(_sb_prompt_kernel, tq=tq, hg=hg), name="sb_prompt",
        grid=(b, ng, s // tq),
        in_specs=[pl.BlockSpec((1, tq, c), lambda i, g, j: (i, j, g)),
                  pl.BlockSpec((1, s, c), lambda i, g, j: (i, 0, g)),
                  pl.BlockSpec((1, s, c), lambda i, g, j: (i, 0, ng + g))],
        out_specs=pl.BlockSpec((1, tq, c), lambda i, g, j: (i, j, g)),
        out_shape=jax.ShapeDtypeStruct((b, s, da), BF16),
        scratch_shapes=[pltpu.VMEM((s, c), BF16), pltpu.VMEM((s, c), BF16),
                        pltpu.VMEM((r, c), F32), pltpu.VMEM((r, tq), F32)],
        compiler_params=_params("arbitrary", "arbitrary", "arbitrary"),
    )(q, kv, kv)


def _sb_decode_kernel(pt_ref, q_ref, new_ref, *rest, tq, hg, npg):
    page_refs = rest[:npg]
    o_ref, qbd_ref, newp, acc, car = rest[npg:]
    st = pl.program_id(1)
    da = hg * HEAD_DIM
    r = hg * tq
    um = _suffix_matrix(PAGE)

    @pl.when(st == 0)
    def _():
        qbd_ref[...] = _block_diag_rows(q_ref[0], hg)
        newp[...] = jnp.zeros(newp.shape, F32)
        newp[0:tq, :] = new_ref[0]
        mask = _iota((r, PAGE), 1) < _row_query((r, PAGE), tq)
        pv, c = _sb_tile(qbd_ref[...], newp[:, 0:da].astype(BF16), newp[:, da:2 * da].astype(BF16), um,
                         jnp.zeros((r, PAGE), F32), mask)
        acc[...] = pv
        car[...] = c

    for pg in page_refs:
        pv, c = _sb_tile(qbd_ref[...], pg[0, :, 0:da].astype(BF16), pg[0, :, da:2 * da].astype(BF16), um,
                         car[...], None)
        acc[...] += pv
        car[...] = c

    @pl.when(st == pl.num_programs(1) - 1)
    def _():
        o_ref[0] = _gather_diag(acc[...], tq, hg).astype(BF16)


def _page_specs(n_pages, width, npg):
    return [pl.BlockSpec((1, PAGE, width), functools.partial(
        lambda i, s, pt, off: (pt[i, n_pages - 1 - s * npg - off], 0, 0), off=off)) for off in range(npg)]


def _sb_decode(q, kv_new, cache, page_table, npg):
    b, tq, da = q.shape
    hg = da // HEAD_DIM
    n_pages = page_table.shape[1]
    r = hg * tq
    grid_spec = pltpu.PrefetchScalarGridSpec(
        num_scalar_prefetch=1,
        grid=(b, n_pages // npg),
        in_specs=[pl.BlockSpec((1, tq, da), lambda i, s, pt: (i, 0, 0)),
                  pl.BlockSpec((1, tq, 2 * da), lambda i, s, pt: (i, 0, 0))] + _page_specs(n_pages, 2 * da, npg),
        out_specs=pl.BlockSpec((1, tq, da), lambda i, s, pt: (i, 0, 0)),
        scratch_shapes=[pltpu.VMEM((r, da), BF16), pltpu.VMEM((PAGE, 2 * da), F32),
                        pltpu.VMEM((r, da), F32), pltpu.VMEM((r, PAGE), F32)],
    )
    return pl.pallas_call(
        functools.partial(_sb_decode_kernel, tq=tq, hg=hg, npg=npg), name="sb_decode",
        grid_spec=grid_spec,
        out_shape=jax.ShapeDtypeStruct((b, tq, da), BF16),
        compiler_params=_params("arbitrary", "arbitrary"),
    )(page_table, q, kv_new, *([cache] * npg))


def _place_matrices(hg, first_head):
    src = _iota((LANES, LANES), 0)
    dst = _iota((LANES, LANES), 1)
    return [jnp.where((dst >= p * hg) & (dst < (p + 1) * hg) & (src == first_head + dst - p * hg), 1.0, 0.0).astype(BF16)
            for p in range(3)]


def _bias_lanes(x, place):
    parts = _split3(x)
    return (_dot(parts[0], place[0]) + _dot(parts[1], place[1]) + _dot(parts[2], place[2])).astype(BF16)


def _query_aug(q, hg):
    tq = q.shape[0]
    qbd = _block_diag_rows(q, hg)
    rh = _iota((hg * tq, LANES), 0) // tq
    ln = _iota((hg * tq, LANES), 1)
    ones = jnp.where((ln < 3 * hg) & (ln % hg == rh), 1.0, 0.0).astype(BF16)
    return jnp.concatenate([qbd, ones], axis=1)


def _fox_prompt_kernel(q_ref, k_ref, v_ref, lf_ref, o_ref, kaug, vbf, m_ref, l_ref, acc, *, tq, hg, chunk):
    g = pl.program_id(1)
    qi = pl.program_id(2)
    c = hg * HEAD_DIM
    s_len = k_ref.shape[1]

    @pl.when(qi == 0)
    def _():
        vbf[...] = v_ref[0].astype(BF16)
        place = _place_matrices(hg, g * hg)
        incl = jnp.where(_iota((chunk, chunk), 1) <= _iota((chunk, chunk), 0), 1.0, 0.0).astype(BF16)
        run = jnp.zeros((1, LANES), F32)
        for ch in range(s_len // chunk):
            sl = slice(ch * chunk, (ch + 1) * chunk)
            cum = _dot_l3(incl, lf_ref[0, sl, :]) + run
            run = cum[chunk - 1:chunk, :]
            kaug[sl, 0:c] = k_ref[0, sl, :].astype(BF16)
            kaug[sl, c:c + LANES] = _bias_lanes(-cum, place)

    qa = _query_aug(q_ref[0], hg)
    r = hg * tq
    _softmax_init(m_ref, l_ref, acc)
    q0 = pl.multiple_of(qi * tq, tq)
    mask = _iota((r, tq), 1) <= _row_query((r, tq), tq)
    _softmax_tile(qa, kaug[pl.ds(q0, tq), :], vbf[pl.ds(q0, tq), :], m_ref, l_ref, acc, None, mask, False)

    def body(it, _):
        k0 = pl.multiple_of(it * tq, tq)
        _softmax_tile(qa, kaug[pl.ds(k0, tq), :], vbf[pl.ds(k0, tq), :], m_ref, l_ref, acc, None, None, False)
        return 0

    lax.fori_loop(0, qi, body, 0)
    o_ref[0] = _gather_diag(_softmax_out(l_ref, acc), tq, hg).astype(BF16)


def _fox_prompt(q, kv, logfp, tq, hg):
    b, s, da = q.shape
    c = hg * HEAD_DIM
    ng = da // c
    r = hg * tq
    return pl.pallas_call(
        functools.partial(_fox_prompt_kernel, tq=tq, hg=hg, chunk=256), name="fox_prompt",
        grid=(b, ng, s // tq),
        in_specs=[pl.BlockSpec((1, tq, c), lambda i, g, j: (i, j, g)),
                  pl.BlockSpec((1, s, c), lambda i, g, j: (i, 0, g)),
                  pl.BlockSpec((1, s, c), lambda i, g, j: (i, 0, ng + g)),
                  pl.BlockSpec((1, s, LANES), lambda i, g, j: (i, 0, 0))],
        out_specs=pl.BlockSpec((1, tq, c), lambda i, g, j: (i, j, g)),
        out_shape=jax.ShapeDtypeStruct((b, s, da), BF16),
        scratch_shapes=[pltpu.VMEM((s, c + LANES), BF16), pltpu.VMEM((s, c), BF16),
                        pltpu.VMEM((r, LANES), F32), pltpu.VMEM((r, LANES), F32), pltpu.VMEM((r, c), F32)],
        compiler_params=_params("arbitrary", "arbitrary", "arbitrary"),
    )(q, kv, kv, logfp)


def _fox_decode_kernel(pt_ref, q_ref, new_ref, lfnew_ref, *rest, tq, hg, npg, n_f):
    page_refs = rest[:npg]
    lf_refs = rest[npg:2 * npg]
    o_ref, qa_ref, newp, lfp, run_ref, m_ref, l_ref, acc = rest[2 * npg:]
    st = pl.program_id(1)
    da = hg * HEAD_DIM
    r = hg * tq
    place = _place_matrices(hg, 0)
    later = jnp.where(_iota((PAGE, PAGE), 1) > _iota((PAGE, PAGE), 0), 1.0, 0.0).astype(BF16)

    def tile(kf32, v, lf, mask):
        suf = _dot_l3(later, lf) + run_ref[0:1, :]
        run_ref[...] = run_ref[...] + jnp.sum(lf, axis=0, keepdims=True)
        kaug = jnp.concatenate([kf32.astype(BF16), _bias_lanes(suf, place)], axis=1)
        _softmax_tile(qa_ref[...], kaug, v.astype(BF16), m_ref, l_ref, acc, None, mask, False)

    @pl.when(st == 0)
    def _():
        qa_ref[...] = _query_aug(q_ref[0], hg)
        newp[...] = jnp.zeros(newp.shape, F32)
        newp[0:tq, :] = new_ref[0]
        lfp[...] = jnp.zeros(lfp.shape, F32)
        lfp[0:tq, :] = lfnew_ref[0]
        run_ref[...] = jnp.zeros(run_ref.shape, F32)
        _softmax_init(m_ref, l_ref, acc)
        mask = _iota((r, PAGE), 1) <= _row_query((r, PAGE), tq)
        tile(newp[:, 0:da], newp[:, da:2 * da], lfp[...], mask)
        lfp[...] = jnp.zeros(lfp.shape, F32)

    for pg, lf in zip(page_refs, lf_refs):
        lfp[:, 0:n_f] = lf[0]
        tile(pg[0, :, 0:da], pg[0, :, da:2 * da], lfp[...], None)

    @pl.when(st == pl.num_programs(1) - 1)
    def _():
        o_ref[0] = _gather_diag(_softmax_out(l_ref, acc), tq, hg).astype(BF16)


def _fox_decode(q, kv_new, lf_new, cache, cache_lf, page_table, npg):
    b, tq, da = q.shape
    hg = da // HEAD_DIM
    n_f = cache_lf.shape[-1]
    n_pages = page_table.shape[1]
    r = hg * tq
    grid_spec = pltpu.PrefetchScalarGridSpec(
        num_scalar_prefetch=1,
        grid=(b, n_pages // npg),
        in_specs=[pl.BlockSpec((1, tq, da), lambda i, s, pt: (i, 0, 0)),
                  pl.BlockSpec((1, tq, 2 * da), lambda i, s, pt: (i, 0, 0)),
                  pl.BlockSpec((1, tq, LANES), lambda i, s, pt: (i, 0, 0))]
        + _page_specs(n_pages, 2 * da, npg) + _page_specs(n_pages, n_f, npg),
        out_specs=pl.BlockSpec((1, tq, da), lambda i, s, pt: (i, 0, 0)),
        scratch_shapes=[pltpu.VMEM((r, da + LANES), BF16), pltpu.VMEM((PAGE, 2 * da), F32),
                        pltpu.VMEM((PAGE, LANES), F32), pltpu.VMEM((8, LANES), F32),
                        pltpu.VMEM((r, LANES), F32), pltpu.VMEM((r, LANES), F32), pltpu.VMEM((r, da), F32)],
    )
    return pl.pallas_call(
        functools.partial(_fox_decode_kernel, tq=tq, hg=hg, npg=npg, n_f=n_f), name="fox_decode",
        grid_spec=grid_spec,
        out_shape=jax.ShapeDtypeStruct((b, tq, da), BF16),
        compiler_params=_params("arbitrary", "arbitrary"),
    )(page_table, q, kv_new, lf_new, *([cache] * npg), *([cache_lf] * npg))


def _compress(xb, bd_ref, pe_ref, n_chunk, dkv):
    first = [jnp.zeros((n_chunk, dkv), F32) for _ in range(2)]
    second = [jnp.zeros((n_chunk, dkv), F32) for _ in range(2)]
    for j in range(D_CMP):
        xj = jnp.concatenate([xb[lb, pl.ds(j, n_chunk, stride=D_CMP), :] for lb in range(xb.shape[0])], axis=1)
        for kv in range(2):
            xx = xj[:, kv * dkv:(kv + 1) * dkv]
            first[kv] += _dot((xx + pe_ref[kv, j:j + 1, :]).astype(BF16), bd_ref[kv, j])
            second[kv] += _dot((xx + pe_ref[kv, D_CMP + j:D_CMP + j + 1, :]).astype(BF16), bd_ref[kv, D_CMP + j])
    return [first[kv] + pltpu.roll(second[kv], n_chunk - 1, axis=0) for kv in range(2)]


def _cmp_select(q, kcmp, vcmp, bias, ov, q0, tq, nc, ns, n_kv, n_rep):
    ncp = kcmp.shape[0]
    dkv = n_kv * HEAD_DIM
    qbd = jnp.concatenate([_block_diag_rows(q[:, z * dkv:(z + 1) * dkv], n_kv) for z in range(n_rep)], axis=0)
    r = qbd.shape[0]
    s = _dot_nt(qbd, kcmp) + bias
    t = q0 + _row_query((r, ncp), tq)
    cidx = _iota((r, ncp), 1)
    mask = (t - (cidx * D_CMP + L_CMP - 1) >= 0) & (cidx < nc)
    s = jnp.where(mask, s, NEG)
    p = jnp.where(mask, jnp.exp(s - jnp.max(s, axis=1, keepdims=True)), 0.0)
    p = p / jnp.maximum(jnp.sum(p, axis=1, keepdims=True), 1e-30)
    o_full = _dot(p.astype(BF16), vcmp)
    rz = n_kv * tq
    o_cmp = jnp.concatenate([_gather_diag(o_full[z * rz:(z + 1) * rz], tq, n_kv) for z in range(n_rep)], axis=1)
    pz = p[0:rz]
    for z in range(1, n_rep):
        pz = pz + p[z * rz:(z + 1) * rz]
    imp = _dot_x2(pz, ov)
    nsp = imp.shape[1]
    tg = q0 + _row_query((rz, nsp), tq)
    blk = _iota((rz, nsp), 1)
    cur = tg // SLC_BLOCK
    valid = (blk * SLC_BLOCK <= tg) & (blk < ns)
    forced = (blk == 0) | (blk == cur) | (blk == cur - 1)
    score = jnp.where(valid, jnp.where(forced, FORCE_SCORE, imp), -jnp.inf)
    cnt = jnp.zeros((rz, nsp), F32)
    for j in range(ns):
        col = score[:, j:j + 1]
        cnt = cnt + jnp.where(col > score, 1.0, 0.0) + jnp.where(col == score, jnp.where(blk > j, 1.0, 0.0), 0.0)
    sel = jnp.where(valid & (cnt < N_SELECT), 1.0, 0.0)
    return o_cmp, sel


def _overlap_matrix(ncp, nsp):
    ci = np.arange(ncp)[:, None]
    bj = np.arange(nsp)[None, :]
    ov = (ci * D_CMP <= bj * SLC_BLOCK + SLC_BLOCK - 1) & (ci * D_CMP + L_CMP - 1 >= bj * SLC_BLOCK)
    return jnp.asarray(ov.astype(np.float32), BF16)


def _cmp_prompt_kernel(q_ref, x_ref, bd_ref, pe_ref, bdn_ref, gk_ref, bias_ref, ov_ref, ocmp_ref, sel_ref,
                       kc_ref, vc_ref, xb, *, tq, n_kv, n_rep, ns):
    qi = pl.program_id(1)
    n_chunk = x_ref.shape[1] // D_CMP
    dkv = n_kv * HEAD_DIM

    @pl.when(qi == 0)
    def _():
        for lb in range(xb.shape[0]):
            xb[lb] = x_ref[0, :, lb * LANES:(lb + 1) * LANES]
        kc, vc = _compress(xb, bd_ref, pe_ref, n_chunk, dkv)
        kc_ref[...] = (_head_norm(kc, bdn_ref[...]) * gk_ref[...]).astype(BF16)
        vc_ref[...] = vc.astype(BF16)

    bias = bias_ref[...].reshape(n_kv * n_rep * tq, n_chunk)
    o_cmp, sel = _cmp_select(q_ref[0], kc_ref[...], vc_ref[...], bias, ov_ref[...], qi * tq, tq,
                             n_chunk - 1, ns, n_kv, n_rep)
    ocmp_ref[0] = o_cmp
    sel_ref[0] = sel.astype(BF16)


def _cmp_prompt(q, cmp_kv, bd, pe, bdn, gk, bias, tq, n_kv):
    b, s, dq = q.shape
    dkv = n_kv * HEAD_DIM
    n_rep = dq // dkv
    n_chunk = s // D_CMP
    ns = -(-s // SLC_BLOCK)
    nsp = -(-ns // LANES) * LANES
    ov = _overlap_matrix(n_chunk, nsp)
    const = lambda a: pl.BlockSpec(a.shape, lambda i, j: (0,) * a.ndim)
    return pl.pallas_call(
        functools.partial(_cmp_prompt_kernel, tq=tq, n_kv=n_kv, n_rep=n_rep, ns=ns), name="cmp_prompt",
        grid=(b, s // tq),
        in_specs=[pl.BlockSpec((1, tq, dq), lambda i, j: (i, j, 0)),
                  pl.BlockSpec((1, s, 2 * dkv), lambda i, j: (i, 0, 0)),
                  const(bd), const(pe), const(bdn), const(gk),
                  pl.BlockSpec((n_rep * n_kv, tq, n_chunk), lambda i, j: (0, j, 0)), const(ov)],
        out_specs=[pl.BlockSpec((1, tq, dq), lambda i, j: (i, j, 0)),
                   pl.BlockSpec((1, n_kv * tq, nsp), lambda i, j: (i, j, 0))],
        out_shape=[jax.ShapeDtypeStruct((b, s, dq), F32), jax.ShapeDtypeStruct((b, (s // tq) * n_kv * tq, nsp), BF16)],
        scratch_shapes=[pltpu.VMEM((n_chunk, dkv), BF16), pltpu.VMEM((n_chunk, dkv), BF16),
                        pltpu.VMEM((2 * dkv // LANES, s, LANES), F32)],
        compiler_params=_params("arbitrary", "arbitrary"),
    )(q, cmp_kv, bd, pe, bdn, gk, bias, ov)


def _cmp_decode_kernel(pt_ref, q_ref, bd_ref, pe_ref, bdn_ref, gk_ref, bias_ref, ov_ref, *rest,
                       tq, n_kv, n_rep, ns, npg, past):
    page_refs = rest[:npg]
    ocmp_ref, sel_ref, xbuf = rest[npg:]
    st = pl.program_id(1)
    n_steps = pl.num_programs(1)
    dkv = n_kv * HEAD_DIM
    n_chunk = past // D_CMP
    for off, pg in enumerate(page_refs):
        pos = (n_steps - 1 - st) * npg + (npg - 1 - off)
        for lb in range(xbuf.shape[0]):
            xbuf[lb, pl.ds(pl.multiple_of(pos * PAGE, PAGE), PAGE), :] = pg[0, :, lb * LANES:(lb + 1) * LANES]

    @pl.when(st == n_steps - 1)
    def _():
        kc, vc = _compress(xbuf, bd_ref, pe_ref, n_chunk, dkv)
        kcb = (_head_norm(kc, bdn_ref[...]) * gk_ref[...]).astype(BF16)
        bias = bias_ref[...].reshape(n_kv * n_rep * tq, n_chunk)
        o_cmp, sel = _cmp_select(q_ref[0], kcb, vc.astype(BF16), bias, ov_ref[...], past, tq,
                                 n_chunk - 1, ns, n_kv, n_rep)
        ocmp_ref[0] = o_cmp
        sel_ref[0] = sel.astype(BF16)


def _cmp_decode(q, cache, page_table, bd, pe, bdn, gk, bias, n_kv, npg):
    b, tq, dq = q.shape
    dkv = n_kv * HEAD_DIM
    n_rep = dq // dkv
    n_pages = page_table.shape[1]
    past = n_pages * PAGE
    n_chunk = (past + tq) // D_CMP
    assert n_chunk == past // D_CMP
    ns = -(-(past + tq) // SLC_BLOCK)
    nsp = -(-ns // LANES) * LANES
    ov = _overlap_matrix(n_chunk, nsp)
    const = lambda a: pl.BlockSpec(a.shape, lambda i, s, pt: (0,) * a.ndim)
    grid_spec = pltpu.PrefetchScalarGridSpec(
        num_scalar_prefetch=1,
        grid=(b, n_pages // npg),
        in_specs=[pl.BlockSpec((1, tq, dq), lambda i, s, pt: (i, 0, 0)),
                  const(bd), const(pe), const(bdn), const(gk), const(bias), const(ov)]
        + _page_specs(n_pages, 2 * dkv, npg),
        out_specs=[pl.BlockSpec((1, tq, dq), lambda i, s, pt: (i, 0, 0)),
                   pl.BlockSpec((1, n_kv * tq, nsp), lambda i, s, pt: (i, 0, 0))],
        scratch_shapes=[pltpu.VMEM((2 * dkv // LANES, past, LANES), F32)],
    )
    return pl.pallas_call(
        functools.partial(_cmp_decode_kernel, tq=tq, n_kv=n_kv, n_rep=n_rep, ns=ns, npg=npg, past=past),
        name="cmp_decode",
        grid_spec=grid_spec,
        out_shape=[jax.ShapeDtypeStruct((b, tq, dq), F32), jax.ShapeDtypeStruct((b, n_kv * tq, nsp), BF16)],
        compiler_params=_params("arbitrary", "arbitrary"),
    )(page_table, q, bd, pe, bdn, gk, bias, ov, *([cache] * npg))


def _nsa_qbd(q, n_kv, n_rep):
    dkv = n_kv * HEAD_DIM
    return jnp.concatenate([_block_diag_rows(q[:, z * dkv:(z + 1) * dkv], n_kv) for z in range(n_rep)], axis=0)


def _nsa_gather(o_full, tq, n_kv, n_rep):
    rz = n_kv * tq
    return jnp.concatenate([_gather_diag(o_full[z * rz:(z + 1) * rz], tq, n_kv) for z in range(n_rep)], axis=1)


def _expand_blocks(sel, n_keys):
    nsp = sel.shape[1]
    blk = _iota((nsp, n_keys), 0)
    key = _iota((nsp, n_keys), 1)
    return _dot(sel, jnp.where(key // SLC_BLOCK == blk, 1.0, 0.0).astype(BF16))


def _merge(gates, ege_ref, o_cmp, o_slc, o_win):
    g = [_dot_x2(gates, ege_ref[br]) for br in range(3)]
    return g[0] * o_cmp + g[1] * o_slc + g[2] * o_win


def _slc_prompt_kernel(q_ref, slc_ref, win_ref, sel_ref, gates_ref, ocmp_ref, t0_ref, t1_ref, ege_ref, o_ref,
                       ks, vs, kw, vw, kmask, m_ref, l_ref, acc, *, tq, n_kv, n_rep):
    qi = pl.program_id(1)
    dkv = n_kv * HEAD_DIM
    s_len = slc_ref.shape[1]

    @pl.when(qi == 0)
    def _():
        ks[...] = slc_ref[0, :, 0:dkv].astype(BF16)
        vs[...] = slc_ref[0, :, dkv:2 * dkv].astype(BF16)
        kw[...] = win_ref[0, :, 0:dkv].astype(BF16)
        vw[...] = win_ref[0, :, dkv:2 * dkv].astype(BF16)

    qbd = _nsa_qbd(q_ref[0], n_kv, n_rep)
    r = qbd.shape[0]
    kmask[...] = _expand_blocks(sel_ref[0], s_len)
    i_row = _row_query((r, tq), tq)
    j_col = _iota((r, tq), 1)

    def sel_mask(k0):
        return jnp.concatenate([kmask[:, pl.ds(k0, tq)]] * n_rep, axis=0) > 0.5

    _softmax_init(m_ref, l_ref, acc)
    q0 = pl.multiple_of(qi * tq, tq)
    _softmax_tile(qbd, ks[pl.ds(q0, tq), :], vs[pl.ds(q0, tq), :], m_ref, l_ref, acc, t0_ref[...],
                  sel_mask(q0) & (j_col <= i_row), True)

    @pl.when(qi >= 1)
    def _():
        k0 = pl.multiple_of((qi - 1) * tq, tq)
        _softmax_tile(qbd, ks[pl.ds(k0, tq), :], vs[pl.ds(k0, tq), :], m_ref, l_ref, acc, t1_ref[...],
                      sel_mask(k0), True)

    def body(it, _):
        k0 = pl.multiple_of(it * tq, tq)
        _softmax_tile(qbd, ks[pl.ds(k0, tq), :], vs[pl.ds(k0, tq), :], m_ref, l_ref, acc, None, sel_mask(k0), True)
        return 0

    lax.fori_loop(0, jnp.maximum(qi - 1, 0), body, 0)
    o_slc = _nsa_gather(_softmax_out(l_ref, acc), tq, n_kv, n_rep)

    _softmax_init(m_ref, l_ref, acc)
    _softmax_tile(qbd, kw[pl.ds(q0, tq), :], vw[pl.ds(q0, tq), :], m_ref, l_ref, acc, t0_ref[...],
                  j_col <= i_row, False)
    n_back = WINDOW // tq
    for back in range(1, n_back + 1):
        @pl.when(qi >= back)
        def _(back=back):
            k0 = pl.multiple_of((qi - back) * tq, tq)
            bias = t1_ref[...] if back == 1 else None
            mask = (j_col > i_row) if back == n_back else None
            _softmax_tile(qbd, kw[pl.ds(k0, tq), :], vw[pl.ds(k0, tq), :], m_ref, l_ref, acc, bias, mask,
                          back == n_back)
    o_win = _nsa_gather(_softmax_out(l_ref, acc), tq, n_kv, n_rep)
    o_ref[0] = _merge(gates_ref[0], ege_ref, ocmp_ref[0], o_slc, o_win).astype(BF16)


def _slc_prompt(q, slc_kv, win_kv, sel, gates, o_cmp, t0, t1, ege, tq, n_kv):
    b, s, dq = q.shape
    dkv = n_kv * HEAD_DIM
    n_rep = dq // dkv
    r = n_rep * n_kv * tq
    nsp = sel.shape[-1]
    const = lambda a: pl.BlockSpec(a.shape, lambda i, j: (0,) * a.ndim)
    return pl.pallas_call(
        functools.partial(_slc_prompt_kernel, tq=tq, n_kv=n_kv, n_rep=n_rep), name="slc_prompt",
        grid=(b, s // tq),
        in_specs=[pl.BlockSpec((1, tq, dq), lambda i, j: (i, j, 0)),
                  pl.BlockSpec((1, s, 2 * dkv), lambda i, j: (i, 0, 0)),
                  pl.BlockSpec((1, s, 2 * dkv), lambda i, j: (i, 0, 0)),
                  pl.BlockSpec((1, n_kv * tq, nsp), lambda i, j: (i, j, 0)),
                  pl.BlockSpec((1, tq, LANES), lambda i, j: (i, j, 0)),
                  pl.BlockSpec((1, tq, dq), lambda i, j: (i, j, 0)),
                  const(t0), const(t1), const(ege)],
        out_specs=pl.BlockSpec((1, tq, dq), lambda i, j: (i, j, 0)),
        out_shape=jax.ShapeDtypeStruct((b, s, dq), BF16),
        scratch_shapes=[pltpu.VMEM((s, dkv), BF16)] * 4 + [
            pltpu.VMEM((n_kv * tq, s), F32), pltpu.VMEM((r, LANES), F32), pltpu.VMEM((r, LANES), F32),
            pltpu.VMEM((r, dkv), F32)],
        compiler_params=_params("arbitrary", "arbitrary"),
    )(q, slc_kv, win_kv, sel, gates, o_cmp, t0, t1, ege)


def _slc_decode_kernel(pt_ref, q_ref, slcnew_ref, winnew_ref, state_ref, sel_ref, gates_ref, ocmp_ref,
                       t0_ref, t1_ref, ege_ref, *rest, tq, n_kv, n_rep, npg, past):
    page_refs = rest[:npg]
    o_ref, qbd_ref, newp, kmask, m_ref, l_ref, acc = rest[npg:]
    st = pl.program_id(1)
    n_steps = pl.num_programs(1)
    dkv = n_kv * HEAD_DIM
    r = n_rep * n_kv * tq
    i_row = _row_query((r, PAGE), tq)
    j_col = _iota((r, PAGE), 1)

    def sel_mask(k0):
        return jnp.concatenate([kmask[:, pl.ds(k0, PAGE)]] * n_rep, axis=0) > 0.5

    @pl.when(st == 0)
    def _():
        qbd_ref[...] = _nsa_qbd(q_ref[0], n_kv, n_rep)
        kmask[...] = _expand_blocks(sel_ref[0], kmask.shape[1])
        newp[...] = jnp.zeros(newp.shape, F32)
        newp[0:tq, :] = slcnew_ref[0]
        _softmax_init(m_ref, l_ref, acc)
        _softmax_tile(qbd_ref[...], newp[:, 0:dkv].astype(BF16), newp[:, dkv:2 * dkv].astype(BF16), m_ref, l_ref, acc,
                      t0_ref[...], sel_mask(past) & (j_col <= i_row), True)

    for off, pg in enumerate(page_refs):
        pos = (n_steps - 1 - st) * npg + (npg - 1 - off)
        k0 = pl.multiple_of(pos * PAGE, PAGE)
        kt = pg[0, :, 0:dkv].astype(BF16)
        vt = pg[0, :, dkv:2 * dkv].astype(BF16)
        if off == 0:
            @pl.when(st == 0)
            def _():
                _softmax_tile(qbd_ref[...], kt, vt, m_ref, l_ref, acc, t1_ref[...], sel_mask(k0), True)

            @pl.when(st > 0)
            def _():
                _softmax_tile(qbd_ref[...], kt, vt, m_ref, l_ref, acc, None, sel_mask(k0), True)
        else:
            _softmax_tile(qbd_ref[...], kt, vt, m_ref, l_ref, acc, None, sel_mask(k0), True)

    @pl.when(st == n_steps - 1)
    def _():
        qbd = qbd_ref[...]
        o_slc = _nsa_gather(_softmax_out(l_ref, acc), tq, n_kv, n_rep)
        _softmax_init(m_ref, l_ref, acc)
        newp[0:tq, :] = winnew_ref[0]
        _softmax_tile(qbd, newp[:, 0:dkv].astype(BF16), newp[:, dkv:2 * dkv].astype(BF16), m_ref, l_ref, acc,
                      t0_ref[...], j_col <= i_row, False)
        n_back = WINDOW // PAGE
        for back in range(1, n_back + 1):
            sl = slice((n_back - back) * PAGE, (n_back - back + 1) * PAGE)
            bias = t1_ref[...] if back == 1 else None
            mask = (j_col > i_row) if back == n_back else None
            _softmax_tile(qbd, state_ref[0, sl, 0:dkv].astype(BF16), state_ref[0, sl, dkv:2 * dkv].astype(BF16),
                          m_ref, l_ref, acc, bias, mask, back == n_back)
        o_win = _nsa_gather(_softmax_out(l_ref, acc), tq, n_kv, n_rep)
        o_ref[0] = _merge(gates_ref[0], ege_ref, ocmp_ref[0], o_slc, o_win).astype(BF16)


def _slc_decode(q, slc_new, win_new, state, sel, gates, o_cmp, cache, page_table, t0, t1, ege, n_kv, npg):
    b, tq, dq = q.shape
    dkv = n_kv * HEAD_DIM
    n_rep = dq // dkv
    r = n_rep * n_kv * tq
    n_pages = page_table.shape[1]
    past = n_pages * PAGE
    nsp = sel.shape[-1]
    const = lambda a: pl.BlockSpec(a.shape, lambda i, s, pt: (0,) * a.ndim)
    per_seq = lambda a: pl.BlockSpec((1,) + a.shape[1:], lambda i, s, pt: (i,) + (0,) * (a.ndim - 1))
    grid_spec = pltpu.PrefetchScalarGridSpec(
        num_scalar_prefetch=1,
        grid=(b, n_pages // npg),
        in_specs=[per_seq(q), per_seq(slc_new), per_seq(win_new), per_seq(state), per_seq(sel), per_seq(gates),
                  per_seq(o_cmp), const(t0), const(t1), const(ege)] + _page_specs(n_pages, 2 * dkv, npg),
        out_specs=pl.BlockSpec((1, tq, dq), lambda i, s, pt: (i, 0, 0)),
        scratch_shapes=[pltpu.VMEM((r, dkv), BF16), pltpu.VMEM((PAGE, 2 * dkv), F32),
                        pltpu.VMEM((n_kv * tq, past + PAGE), F32),
                        pltpu.VMEM((r, LANES), F32), pltpu.VMEM((r, LANES), F32), pltpu.VMEM((r, dkv), F32)],
    )
    return pl.pallas_call(
        functools.partial(_slc_decode_kernel, tq=tq, n_kv=n_kv, n_rep=n_rep, npg=npg, past=past),
        name="slc_decode",
        grid_spec=grid_spec,
        out_shape=jax.ShapeDtypeStruct((b, tq, dq), BF16),
        compiler_params=_params("arbitrary", "arbitrary"),
    )(page_table, q, slc_new, win_new, state, sel, gates, o_cmp, t0, t1, ege, *([cache] * npg))


def _rel_buckets(dist):
    n = np.maximum(dist, 0)
    exact = N_BUCKETS // 2
    nf = np.maximum(n, 1).astype(np.float64)
    large = exact + (np.log(nf / exact) / math.log(REL_MAX_DIST / exact) * (N_BUCKETS - exact)).astype(np.int64)
    return np.where(n < exact, n, np.minimum(large, N_BUCKETS - 1)).astype(np.int32)


def _head_block_diag(width, scale):
    h = np.arange(width) // HEAD_DIM
    return jnp.asarray((h[:, None] == h[None, :]).astype(np.float32) * scale, BF16)


class _NsaLayout:
    def __init__(self, n_heads, n_kv):
        n_rep = n_heads // n_kv
        self.n_kv, self.n_rep = n_kv, n_rep
        new = np.arange(n_heads)
        z, g = new // n_kv, new % n_kv
        self.head_perm = g * n_rep + z
        self.col_perm = (self.head_perm[:, None] * HEAD_DIM + np.arange(HEAD_DIM)[None, :]).reshape(-1)
        self.gate_perm = (np.arange(3)[:, None] * n_heads + self.head_perm[None, :]).reshape(-1)
        ege = np.zeros((3, LANES, n_heads * HEAD_DIM), np.float32)
        for br in range(3):
            for h in range(n_heads):
                ege[br, br * n_heads + h, h * HEAD_DIM:(h + 1) * HEAD_DIM] = 1.0
        self.ege = jnp.asarray(ege, BF16)


def _toeplitz_tiles(rel_hd, tq, tk):
    i = np.arange(tq)[:, None]
    j = np.arange(tk)[None, :]
    far = rel_hd[:, N_BUCKETS - 1][:, None, None]
    t0 = jnp.take(rel_hd, jnp.asarray(_rel_buckets(i - j)), axis=1) - far
    t1 = jnp.take(rel_hd, jnp.asarray(_rel_buckets(tk + i - j)), axis=1) - far
    n = rel_hd.shape[0]
    return t0.reshape(n * tq, tk), t1.reshape(n * tq, tk)


def _cmp_bias(rel_hd, qpos, n_chunk):
    dc = qpos[:, None] - (np.arange(n_chunk)[None, :] * D_CMP + L_CMP - 1)
    return jnp.take(rel_hd, jnp.asarray(_rel_buckets(dc)), axis=1)


def kernel(x_prompt, x_sample, cache_sb_kv, cache_fox_kv, cache_fox_logf, cache_cmp_kv, cache_slc_kv, state_win_kv,
           page_table, c_prompt, c_sample, norm_gain, w_ada, b_ada, ffn_w_in, ffn_w_out, w_in_ab, b_forget,
           fox_qk_gain, w_out_ab, w_in_nsa, b_nsa_gate, nsa_qk_gain, cmp_w, cmp_pe, rel_bias, w_out_nsa):
    bp, s_len, d = x_prompt.shape
    bs, t_dec, _ = x_sample.shape
    n_pool = cache_sb_kv.shape[0]
    h_sb, h_fox = cache_sb_kv.shape[3], cache_fox_kv.shape[3]
    n_kv = cache_cmp_kv.shape[3]
    n_heads = rel_bias.shape[1]
    da = h_sb * HEAD_DIM
    dkv = n_kv * HEAD_DIM
    dq = n_heads * HEAD_DIM
    assert h_sb == h_fox and da == h_fox * HEAD_DIM
    n_pages = page_table.shape[1]
    past = n_pages * PAGE
    lay = _NsaLayout(n_heads, n_kv)

    ffn_in = ffn_w_in.astype(BF16)
    ffn_out = ffn_w_out.astype(BF16)
    w_ab = jnp.pad(w_in_ab, ((0, 0), (0, LANES - h_fox))).astype(BF16)
    bfp = jnp.pad(b_forget, (0, LANES - h_fox)).reshape(1, LANES)
    bd_ab = _head_block_diag(da, 1.0 / HEAD_DIM)
    gq_fox = jnp.tile(fox_qk_gain[0], h_fox).reshape(1, da)
    gk_fox = jnp.tile(fox_qk_gain[1], h_fox).reshape(1, da)
    w_out_sb = w_out_ab[:da].astype(BF16)
    w_out_fox = w_out_ab[da:].astype(BF16)
    w_nsa = jnp.concatenate([w_in_nsa[:, lay.col_perm], w_in_nsa[:, dq:dq + 6 * dkv],
                             w_in_nsa[:, dq + 6 * dkv + lay.gate_perm]], axis=1)
    w_nsa = jnp.pad(w_nsa, ((0, 0), (0, LANES - 3 * n_heads))).astype(BF16)
    bg = jnp.pad(b_nsa_gate[lay.gate_perm], (0, LANES - 3 * n_heads)).reshape(1, LANES)
    bd_kv = _head_block_diag(dkv, 1.0 / HEAD_DIM)
    tile_kv = lambda g: jnp.tile(g, n_kv).reshape(1, dkv)
    w_out_n = w_out_nsa[lay.col_perm].astype(BF16)
    eye = jnp.eye(n_kv, dtype=F32)
    cmp_bd = jnp.einsum('gh,kjde->kjgdhe', eye, cmp_w).reshape(2, L_CMP, dkv, dkv).astype(BF16)
    cmp_pe_t = jnp.tile(cmp_pe, (1, 1, n_kv))
    rel_hd = rel_bias[:, lay.head_perm].T

    mod = _modulation(jnp.concatenate([c_prompt, c_sample], axis=0), w_ada, b_ada)
    mod = mod.reshape(mod.shape[0], bp + bs, 3, 3, 1, d)

    def mods(l, sub, lo, hi):
        return tuple(mod[l, lo:hi, sub, k] for k in range(3))

    def trunk(x, lo, hi, bb, tt, mix0, mix1):
        m = lambda l, sub: mods(l, sub, lo, hi)
        x = _ffn(x, m(0, 0), norm_gain[0, 0], ffn_in[0, 0], ffn_out[0, 0], bb, tt)
        qsb, sbkv, qfx, fxkv, logf, logfp = _proj_ab(x, m(0, 1), norm_gain[0, 1], w_ab, bd_ab, gq_fox, gk_fox, bfp,
                                                     h_fox, bb, tt)
        o_sb, o_fox = mix0(qsb, sbkv, qfx, fxkv, logfp)
        x = _outproj(x, m(0, 1)[2], [o_sb, o_fox], [w_out_sb, w_out_fox], bb, tt)
        x = _ffn(x, m(0, 2), norm_gain[0, 2], ffn_in[0, 1], ffn_out[0, 1], bb, tt)
        x = _ffn(x, m(1, 0), norm_gain[1, 0], ffn_in[1, 0], ffn_out[1, 0], bb, tt)
        q, cmpkv, slckv, winkv, gates = _proj_nsa(x, m(1, 1), norm_gain[1, 1], w_nsa, bd_kv, tile_kv(nsa_qk_gain[0]),
                                                  tile_kv(nsa_qk_gain[2]), tile_kv(nsa_qk_gain[3]), bg, dq, bb, tt)
        o = mix1(q, cmpkv, slckv, winkv, gates)
        x = _outproj(x, m(1, 1)[2], [o], [w_out_n], bb, tt)
        x = _ffn(x, m(1, 2), norm_gain[1, 2], ffn_in[1, 1], ffn_out[1, 1], bb, tt)
        return x, sbkv, fxkv, logf, cmpkv, slckv, winkv

    gk_cmp = tile_kv(nsa_qk_gain[1])

    def mix0_prompt(qsb, sbkv, qfx, fxkv, logfp):
        return _sb_prompt(qsb, sbkv, 256, 2), _fox_prompt(qfx, fxkv, logfp, 256, 2)

    def mix1_prompt(q, cmpkv, slckv, winkv, gates):
        bias = _cmp_bias(rel_hd, np.arange(s_len), s_len // D_CMP)
        o_cmp, sel = _cmp_prompt(q, cmpkv, cmp_bd, cmp_pe_t, bd_kv, gk_cmp, bias, TILE, n_kv)
        t0, t1 = _toeplitz_tiles(rel_hd, TILE, TILE)
        return _slc_prompt(q, slckv, winkv, sel, gates, o_cmp, t0, t1, lay.ege, TILE, n_kv)

    y_p, p_sb, p_fox, p_logf, p_cmp, p_slc, p_win = trunk(x_prompt, 0, bp, 1, 512, mix0_prompt, mix1_prompt)

    c_sb = cache_sb_kv.reshape(n_pool, PAGE, 2 * da)
    c_fox = cache_fox_kv.reshape(n_pool, PAGE, 2 * da)
    c_cmp = cache_cmp_kv.reshape(n_pool, PAGE, 2 * dkv)
    c_slc = cache_slc_kv.reshape(n_pool, PAGE, 2 * dkv)
    state = state_win_kv.reshape(bs, state_win_kv.shape[1], 2 * dkv)
    npg = PAGES_PER_STEP

    def mix0_sample(qsb, sbkv, qfx, fxkv, logfp):
        return (_sb_decode(qsb, sbkv, c_sb, page_table, npg),
                _fox_decode(qfx, fxkv, logfp, c_fox, cache_fox_logf, page_table, npg))

    def mix1_sample(q, cmpkv, slckv, winkv, gates):
        qpos = past + np.arange(t_dec)
        bias = _cmp_bias(rel_hd, qpos, past // D_CMP)
        o_cmp, sel = _cmp_decode(q, c_cmp, page_table, cmp_bd, cmp_pe_t, bd_kv, gk_cmp, bias, n_kv, npg)
        t0, t1 = _toeplitz_tiles(rel_hd, t_dec, PAGE)
        return _slc_decode(q, slckv, winkv, state, sel, gates, o_cmp, c_slc, page_table, t0, t1, lay.ege, n_kv, npg)

    y_s, s_sb, s_fox, s_logf, s_cmp, s_slc, s_win = trunk(x_sample, bp, bp + bs, bs, t_dec, mix0_sample, mix1_sample)

    kv5 = lambda a, h: a.reshape(a.shape[0], a.shape[1], 2, h, HEAD_DIM)
    win_len = min(WINDOW, s_len)
    p_win_out = kv5(p_win, n_kv)[:, s_len - win_len:]
    s_win_all = jnp.concatenate([state, s_win], axis=1)
    new_len = min(WINDOW, s_win_all.shape[1])
    s_win_out = kv5(s_win_all[:, s_win_all.shape[1] - new_len:], n_kv)
    return (y_p, y_s, kv5(p_sb, h_sb), kv5(s_sb, h_sb), kv5(p_fox, h_fox), kv5(s_fox, h_fox), p_logf, s_logf,
            kv5(p_cmp, n_kv), kv5(s_cmp, n_kv), kv5(p_slc, n_kv), kv5(s_slc, n_kv), p_win_out, s_win_out)
```

```python
import functools
import math

import numpy as np
import jax
import jax.numpy as jnp
from jax import lax
from jax.experimental import pallas as pl
from jax.experimental.pallas import tpu as pltpu

F32 = jnp.float32
BF16 = jnp.bfloat16

HEAD_DIM = 64
PAGE = 128
L_CMP = 32
D_CMP = 16
SLC_BLOCK = 64
N_SELECT = 16
WINDOW = 512
N_BUCKETS = 32
REL_MAX_DIST = 128
FORCE_SCORE = 1e4
NEG = -1e30
EPS = 1e-6
MACARON = 0.5
LANES = 128
VMEM_LIMIT = 56 * 1024 * 1024
FF_CHUNK = 256
TILE = 128
FAR_TILE = 512
AB_TILE = 256
PAGES_PER_STEP = 8
BAND_BACK = (REL_MAX_DIST + L_CMP) // D_CMP


def _params(*sem):
    return pltpu.CompilerParams(dimension_semantics=sem, vmem_limit_bytes=VMEM_LIMIT)


def _dot(a, b):
    return jnp.dot(a, b, preferred_element_type=F32)


def _dot_nt(a, b):
    return lax.dot_general(a, b, (((1,), (1,)), ((), ())), preferred_element_type=F32)


def _split2(x):
    hi = x.astype(BF16)
    lo = (x - hi.astype(F32)).astype(BF16)
    return hi, lo


def _split3(x):
    hi = x.astype(BF16)
    r = x - hi.astype(F32)
    mid = r.astype(BF16)
    lo = (r - mid.astype(F32)).astype(BF16)
    return hi, mid, lo


def _dot_x2(x, w):
    hi, lo = _split2(x)
    return _dot(hi, w) + _dot(lo, w)


def _dot_x3(x, w):
    hi, mid, lo = _split3(x)
    return _dot(hi, w) + _dot(mid, w) + _dot(lo, w)


def _dot_l2(w, x):
    hi, lo = _split2(x)
    return _dot(w, hi) + _dot(w, lo)


def _log_sigmoid(x):
    return jnp.minimum(x, 0.0) - jnp.log(1.0 + jnp.exp(-jnp.abs(x)))


def _adanorm(x, g, scale, shift):
    ms = jnp.mean(x * x, axis=-1, keepdims=True)
    return (x * lax.rsqrt(ms + EPS) * g) * (1.0 + scale) + shift


def _head_norm(t, bd):
    return t * lax.rsqrt(_dot_x2(t * t, bd) + EPS)


def _head_norm_t(t, bd):
    return t * lax.rsqrt(_dot_l2(bd, t * t) + EPS)


def _iota(shape, dim):
    return lax.broadcasted_iota(jnp.int32, shape, dim)


def _mod_kernel(c_ref, w_ref, b_ref, o_ref):
    c = c_ref[...]
    cs = c * jax.nn.sigmoid(c)
    ch, cl = _split2(cs)
    wh, wl = _split2(w_ref[0])
    o_ref[0] = _dot(ch, wh) + _dot(cl, wh) + _dot(ch, wl) + b_ref[0]


def _modulation(c_all, w_ada, b_ada):
    depth, d, n = w_ada.shape
    rows = c_all.shape[0]
    tn = 1024
    return pl.pallas_call(
        _mod_kernel, name="modulation",
        grid=(depth, n // tn),
        in_specs=[pl.BlockSpec((rows, d), lambda l, j: (0, 0)),
                  pl.BlockSpec((1, d, tn), lambda l, j: (l, 0, j)),
                  pl.BlockSpec((1, 1, tn), lambda l, j: (l, 0, j))],
        out_specs=pl.BlockSpec((1, rows, tn), lambda l, j: (l, 0, j)),
        out_shape=jax.ShapeDtypeStruct((depth, rows, n), F32),
        compiler_params=_params("arbitrary", "arbitrary"),
    )(c_all, w_ada, b_ada.reshape(depth, 1, n))


def _ffn_kernel(x_ref, shift_ref, scale_ref, gate_ref, g_ref, win_ref, wout_ref, o_ref, hid_ref, *, d_ff):
    x = x_ref[...]
    bb, tt, d = x.shape
    h = _adanorm(x, g_ref[...], scale_ref[...], shift_ref[...]).reshape(bb * tt, d).astype(BF16)
    for c in range(d_ff // FF_CHUNK):
        a = _dot(h, win_ref[:, c * FF_CHUNK:(c + 1) * FF_CHUNK])
        b = _dot(h, win_ref[:, d_ff + c * FF_CHUNK:d_ff + (c + 1) * FF_CHUNK])
        hid_ref[:, c * FF_CHUNK:(c + 1) * FF_CHUNK] = (a * jax.nn.sigmoid(a) * b).astype(BF16)
    o = _dot(hid_ref[...], wout_ref[...]).reshape(bb, tt, d)
    o_ref[...] = x + (MACARON * gate_ref[...]) * o


def _ffn(x, mod3, g, w_in, w_out, bb, tt):
    b, t, d = x.shape
    d_ff = w_out.shape[0]
    shift, scale, gate = mod3
    mspec = pl.BlockSpec((bb, 1, d), lambda i, j: (i, 0, 0))
    xspec = pl.BlockSpec((bb, tt, d), lambda i, j: (i, j, 0))
    return pl.pallas_call(
        functools.partial(_ffn_kernel, d_ff=d_ff), name="ffn",
        grid=(b // bb, t // tt),
        in_specs=[xspec, mspec, mspec, mspec,
                  pl.BlockSpec((1, 1, d), lambda i, j: (0, 0, 0)),
                  pl.BlockSpec((d, 2 * d_ff), lambda i, j: (0, 0)),
                  pl.BlockSpec((d_ff, d), lambda i, j: (0, 0))],
        out_specs=xspec,
        out_shape=jax.ShapeDtypeStruct(x.shape, F32),
        scratch_shapes=[pltpu.VMEM((bb * tt, d_ff), BF16)],
        compiler_params=_params("arbitrary", "arbitrary"),
    )(x, shift, scale, gate, g.reshape(1, 1, d), w_in, w_out)


def _outproj_kernel(*refs, n_in):
    x_ref, gate_ref = refs[0], refs[1]
    o_refs = refs[2:2 + n_in]
    w_refs = refs[2 + n_in:2 + 2 * n_in]
    out_ref = refs[2 + 2 * n_in]
    x = x_ref[...]
    bb, tt, d = x.shape
    y = None
    for o_ref, w_ref in zip(o_refs, w_refs):
        o = o_ref[...].astype(F32).reshape(bb * tt, o_ref.shape[-1]).astype(BF16)
        part = _dot(o, w_ref[...])
        y = part if y is None else y + part
    out_ref[...] = x + gate_ref[...] * y.reshape(bb, tt, d)


def _outproj(x, gate, outs, weights, bb, tt):
    b, t, d = x.shape
    n_in = len(outs)
    xspec = pl.BlockSpec((bb, tt, d), lambda i, j: (i, j, 0))
    in_specs = [xspec, pl.BlockSpec((bb, 1, d), lambda i, j: (i, 0, 0))]
    in_specs += [pl.BlockSpec((bb, tt, o.shape[-1]), lambda i, j: (i, j, 0)) for o in outs]
    in_specs += [pl.BlockSpec(w.shape, lambda i, j: (0, 0)) for w in weights]
    return pl.pallas_call(
        functools.partial(_outproj_kernel, n_in=n_in), name="outproj",
        grid=(b // bb, t // tt),
        in_specs=in_specs,
        out_specs=xspec,
        out_shape=jax.ShapeDtypeStruct(x.shape, F32),
        compiler_params=_params("arbitrary", "arbitrary"),
    )(x, gate, *outs, *weights)


def _proj_ab_kernel(x_ref, shift_ref, scale_ref, g_ref, w_ref, bd_ref, gq_ref, gk_ref, bf_ref,
                    qsb_ref, sbkv_ref, qfx_ref, fxkv_ref, logf_ref, logfp_ref, *, da, n_f):
    x = x_ref[...]
    bb, tt, d = x.shape
    m = bb * tt
    h = _adanorm(x, g_ref[...], scale_ref[...], shift_ref[...]).reshape(m, d).astype(BF16)
    sc = HEAD_DIM ** -0.5
    qsb_ref[...] = (_dot(h, w_ref[:, 0:da]) * sc).reshape(bb, tt, da).astype(BF16)
    sbkv_ref[...] = _dot(h, w_ref[:, da:3 * da]).reshape(bb, tt, 2 * da)
    bd = bd_ref[...]
    qf = _head_norm(_dot(h, w_ref[:, 3 * da:4 * da]), bd) * gq_ref[...]
    qfx_ref[...] = (qf * sc).reshape(bb, tt, da).astype(BF16)
    kf = _head_norm(_dot(h, w_ref[:, 4 * da:5 * da]), bd) * gk_ref[...]
    fxkv_ref[:, :, 0:da] = kf.reshape(bb, tt, da)
    fxkv_ref[:, :, da:2 * da] = _dot(h, w_ref[:, 5 * da:6 * da]).reshape(bb, tt, da)
    lf = _log_sigmoid(_dot(h, w_ref[:, 6 * da:6 * da + LANES]) + bf_ref[...])
    lf = jnp.where(_iota(lf.shape, 1) < n_f, lf, 0.0)
    logfp_ref[...] = lf.reshape(bb, tt, LANES)
    logf_ref[...] = lf[:, 0:n_f].reshape(bb, tt, n_f)


def _proj_ab(x, mod3, g, w, bd, gq, gk, bfp, n_f, bb, tt):
    b, t, d = x.shape
    da = bd.shape[0]
    shift, scale, _ = mod3
    mspec = pl.BlockSpec((bb, 1, d), lambda i, j: (i, 0, 0))
    xspec = pl.BlockSpec((bb, tt, d), lambda i, j: (i, j, 0))
    row = lambda width: pl.BlockSpec((1, width), lambda i, j: (0, 0))
    ospec = lambda width: pl.BlockSpec((bb, tt, width), lambda i, j: (i, j, 0))
    return pl.pallas_call(
        functools.partial(_proj_ab_kernel, da=da, n_f=n_f), name="proj_ab",
        grid=(b // bb, t // tt),
        in_specs=[xspec, mspec, mspec, pl.BlockSpec((1, 1, d), lambda i, j: (0, 0, 0)),
                  pl.BlockSpec(w.shape, lambda i, j: (0, 0)), pl.BlockSpec(bd.shape, lambda i, j: (0, 0)),
                  row(da), row(da), row(LANES)],
        out_specs=[ospec(da), ospec(2 * da), ospec(da), ospec(2 * da), ospec(n_f), ospec(LANES)],
        out_shape=[jax.ShapeDtypeStruct((b, t, da), BF16), jax.ShapeDtypeStruct((b, t, 2 * da), F32),
                   jax.ShapeDtypeStruct((b, t, da), BF16), jax.ShapeDtypeStruct((b, t, 2 * da), F32),
                   jax.ShapeDtypeStruct((b, t, n_f), F32), jax.ShapeDtypeStruct((b, t, LANES), F32)],
        compiler_params=_params("arbitrary", "arbitrary"),
    )(x, shift, scale, g.reshape(1, 1, d), w, bd, gq, gk, bfp)


def _proj_ab_t_kernel(x_ref, shift_ref, scale_ref, g_ref, wq_ref, wkvt_ref, wflt_ref, bd_ref, gq_ref, gk_ref, bf_ref,
                      qsb_ref, qfx_ref, sbt_ref, fxt_ref, lft_ref, *, da, n_f):
    x = x_ref[0]
    h = _adanorm(x, g_ref[0], scale_ref[0], shift_ref[0]).astype(BF16)
    sc = HEAD_DIM ** -0.5
    bd = bd_ref[...]
    qa = _dot(h, wq_ref[:, 0:da]) * sc
    qf = _head_norm(_dot(h, wq_ref[:, da:2 * da]), bd) * (gq_ref[...] * sc)
    for hh in range(da // HEAD_DIM):
        qsb_ref[0, hh] = qa[:, hh * HEAD_DIM:(hh + 1) * HEAD_DIM].astype(BF16)
        qfx_ref[0, hh] = qf[:, hh * HEAD_DIM:(hh + 1) * HEAD_DIM].astype(BF16)
    sbt_ref[0] = _dot_nt(wkvt_ref[0:2 * da, :], h)
    kft = _head_norm_t(_dot_nt(wkvt_ref[2 * da:3 * da, :], h), bd)
    fxt_ref[0, 0:da, :] = kft * gk_ref[...]
    fxt_ref[0, da:2 * da, :] = _dot_nt(wkvt_ref[3 * da:4 * da, :], h)
    lft_ref[0] = _log_sigmoid(_dot_nt(wflt_ref[...], h)[0:n_f, :] + bf_ref[...])


def _proj_ab_t(x, mod3, g, wq, wkvt, wflt, bd, gq, gk_col, bf_col, tt):
    b, t, d = x.shape
    da = bd.shape[0]
    n_h = da // HEAD_DIM
    n_f = bf_col.shape[0]
    shift, scale, _ = mod3
    mspec = pl.BlockSpec((1, 1, d), lambda i, j: (i, 0, 0))
    const = lambda a: pl.BlockSpec(a.shape, lambda i, j: (0,) * a.ndim)
    qspec = pl.BlockSpec((1, n_h, tt, HEAD_DIM), lambda i, j: (i, 0, j, 0))
    tspec = lambda rows: pl.BlockSpec((1, rows, tt), lambda i, j: (i, 0, j))
    return pl.pallas_call(
        functools.partial(_proj_ab_t_kernel, da=da, n_f=n_f), name="proj_ab_t",
        grid=(b, t // tt),
        in_specs=[pl.BlockSpec((1, tt, d), lambda i, j: (i, j, 0)), mspec, mspec,
                  pl.BlockSpec((1, 1, d), lambda i, j: (0, 0, 0)),
                  const(wq), const(wkvt), const(wflt), const(bd), const(gq), const(gk_col), const(bf_col)],
        out_specs=[qspec, qspec, tspec(2 * da), tspec(2 * da), tspec(n_f)],
        out_shape=[jax.ShapeDtypeStruct((b, n_h, t, HEAD_DIM), BF16), jax.ShapeDtypeStruct((b, n_h, t, HEAD_DIM), BF16),
                   jax.ShapeDtypeStruct((b, 2 * da, t), F32), jax.ShapeDtypeStruct((b, 2 * da, t), F32),
                   jax.ShapeDtypeStruct((b, n_f, t), F32)],
        compiler_params=_params("arbitrary", "arbitrary"),
    )(x, shift, scale, g.reshape(1, 1, d), wq, wkvt, wflt, bd, gq, gk_col, bf_col)


def _proj_nsa_kernel(x_ref, shift_ref, scale_ref, g_ref, w_ref, bd_ref, gq_ref, gs_ref, gw_ref, bg_ref,
                     q_ref, cmp_ref, slc_ref, win_ref, gates_ref, *, dq, dkv):
    x = x_ref[...]
    bb, tt, d = x.shape
    m = bb * tt
    h = _adanorm(x, g_ref[...], scale_ref[...], shift_ref[...]).reshape(m, d).astype(BF16)
    bd = bd_ref[...]
    sc = HEAD_DIM ** -0.5
    for c in range(dq // dkv):
        qc = _head_norm(_dot(h, w_ref[:, c * dkv:(c + 1) * dkv]), bd) * gq_ref[...]
        q_ref[:, :, c * dkv:(c + 1) * dkv] = (qc * sc).reshape(bb, tt, dkv).astype(BF16)
    o = dq
    cmp_ref[...] = _dot(h, w_ref[:, o:o + 2 * dkv]).reshape(bb, tt, 2 * dkv)
    o += 2 * dkv
    ks = _head_norm(_dot(h, w_ref[:, o:o + dkv]), bd) * gs_ref[...]
    slc_ref[:, :, 0:dkv] = ks.reshape(bb, tt, dkv)
    slc_ref[:, :, dkv:2 * dkv] = _dot(h, w_ref[:, o + dkv:o + 2 * dkv]).reshape(bb, tt, dkv)
    o += 2 * dkv
    kw = _head_norm(_dot(h, w_ref[:, o:o + dkv]), bd) * gw_ref[...]
    win_ref[:, :, 0:dkv] = kw.reshape(bb, tt, dkv)
    win_ref[:, :, dkv:2 * dkv] = _dot(h, w_ref[:, o + dkv:o + 2 * dkv]).reshape(bb, tt, dkv)
    o += 2 * dkv
    gates_ref[...] = jax.nn.sigmoid(_dot(h, w_ref[:, o:o + LANES]) + bg_ref[...]).reshape(bb, tt, LANES)


def _proj_nsa(x, mod3, g, w, bd, gq, gs, gw, bg, dq, bb, tt):
    b, t, d = x.shape
    dkv = bd.shape[0]
    shift, scale, _ = mod3
    mspec = pl.BlockSpec((bb, 1, d), lambda i, j: (i, 0, 0))
    xspec = pl.BlockSpec((bb, tt, d), lambda i, j: (i, j, 0))
    row = lambda width: pl.BlockSpec((1, width), lambda i, j: (0, 0))
    ospec = lambda width: pl.BlockSpec((bb, tt, width), lambda i, j: (i, j, 0))
    return pl.pallas_call(
        functools.partial(_proj_nsa_kernel, dq=dq, dkv=dkv), name="proj_nsa",
        grid=(b // bb, t // tt),
        in_specs=[xspec, mspec, mspec, pl.BlockSpec((1, 1, d), lambda i, j: (0, 0, 0)),
                  pl.BlockSpec(w.shape, lambda i, j: (0, 0)), pl.BlockSpec(bd.shape, lambda i, j: (0, 0)),
                  row(dkv), row(dkv), row(dkv), row(LANES)],
        out_specs=[ospec(dq), ospec(2 * dkv), ospec(2 * dkv), ospec(2 * dkv), ospec(LANES)],
        out_shape=[jax.ShapeDtypeStruct((b, t, dq), BF16), jax.ShapeDtypeStruct((b, t, 2 * dkv), F32),
                   jax.ShapeDtypeStruct((b, t, 2 * dkv), F32), jax.ShapeDtypeStruct((b, t, 2 * dkv), F32),
                   jax.ShapeDtypeStruct((b, t, LANES), F32)],
        compiler_params=_params("arbitrary", "arbitrary"),
    )(x, shift, scale, g.reshape(1, 1, d), w, bd, gq, gs, gw, bg)


def _proj_nsa_t_kernel(x_ref, shift_ref, scale_ref, g_ref, wq_ref, wcmp_ref, wkvt_ref, wg_ref, bd_ref, gq_ref, gs_ref,
                       gw_ref, bg_ref, q_ref, cmp_ref, slct_ref, wint_ref, gates_ref, *, dq, dkv):
    x = x_ref[0]
    h = _adanorm(x, g_ref[0], scale_ref[0], shift_ref[0]).astype(BF16)
    bd = bd_ref[...]
    sc = HEAD_DIM ** -0.5
    per = dkv // HEAD_DIM
    for c in range(dq // dkv):
        qc = _head_norm(_dot(h, wq_ref[:, c * dkv:(c + 1) * dkv]), bd) * (gq_ref[...] * sc)
        for hh in range(per):
            q_ref[0, c * per + hh] = qc[:, hh * HEAD_DIM:(hh + 1) * HEAD_DIM].astype(BF16)
    cmp_ref[0] = _dot(h, wcmp_ref[...])
    slct_ref[0, 0:dkv, :] = _head_norm_t(_dot_nt(wkvt_ref[0:dkv, :], h), bd) * gs_ref[...]
    slct_ref[0, dkv:2 * dkv, :] = _dot_nt(wkvt_ref[dkv:2 * dkv, :], h)
    wint_ref[0, 0:dkv, :] = _head_norm_t(_dot_nt(wkvt_ref[2 * dkv:3 * dkv, :], h), bd) * gw_ref[...]
    wint_ref[0, dkv:2 * dkv, :] = _dot_nt(wkvt_ref[3 * dkv:4 * dkv, :], h)
    gates_ref[0] = jax.nn.sigmoid(_dot(h, wg_ref[...]) + bg_ref[...])


def _proj_nsa_t(x, mod3, g, wq, wcmp, wkvt, wg, bd, gq, gs_col, gw_col, bg, tt):
    b, t, d = x.shape
    dkv = bd.shape[0]
    dq = wq.shape[1]
    n_h = dq // HEAD_DIM
    shift, scale, _ = mod3
    mspec = pl.BlockSpec((1, 1, d), lambda i, j: (i, 0, 0))
    const = lambda a: pl.BlockSpec(a.shape, lambda i, j: (0,) * a.ndim)
    tspec = lambda rows: pl.BlockSpec((1, rows, tt), lambda i, j: (i, 0, j))
    rspec = lambda width: pl.BlockSpec((1, tt, width), lambda i, j: (i, j, 0))
    return pl.pallas_call(
        functools.partial(_proj_nsa_t_kernel, dq=dq, dkv=dkv), name="proj_nsa_t",
        grid=(b, t // tt),
        in_specs=[rspec(d), mspec, mspec, pl.BlockSpec((1, 1, d), lambda i, j: (0, 0, 0)),
                  const(wq), const(wcmp), const(wkvt), const(wg), const(bd), const(gq), const(gs_col), const(gw_col),
                  const(bg)],
        out_specs=[pl.BlockSpec((1, n_h, tt, HEAD_DIM), lambda i, j: (i, 0, j, 0)), rspec(2 * dkv),
                   tspec(2 * dkv), tspec(2 * dkv), rspec(LANES)],
        out_shape=[jax.ShapeDtypeStruct((b, n_h, t, HEAD_DIM), BF16), jax.ShapeDtypeStruct((b, t, 2 * dkv), F32),
                   jax.ShapeDtypeStruct((b, 2 * dkv, t), F32), jax.ShapeDtypeStruct((b, 2 * dkv, t), F32),
                   jax.ShapeDtypeStruct((b, t, LANES), F32)],
        compiler_params=_params("arbitrary", "arbitrary"),
    )(x, shift, scale, g.reshape(1, 1, d), wq, wcmp, wkvt, wg, bd, gq, gs_col, gw_col, bg)


def _block_diag_rows(q, n_heads):
    q = q.astype(F32)
    head = _iota(q.shape, 1) // HEAD_DIM
    return jnp.concatenate([jnp.where(head == h, q, 0.0) for h in range(n_heads)], axis=0).astype(BF16)


def _gather_diag(acc, tq, n_heads):
    head = _iota((tq, acc.shape[1]), 1) // HEAD_DIM
    out = jnp.zeros((tq, acc.shape[1]), F32)
    for h in range(n_heads):
        out = out + jnp.where(head == h, acc[h * tq:(h + 1) * tq, :], 0.0)
    return out


def _row_query(shape, tq):
    return _iota(shape, 0) % tq


def _running_sum_matrix(tk, prefix):
    j = _iota((tk, 2 * tk), 0)
    s = _iota((tk, 2 * tk), 1)
    tri = (j <= s) if prefix else (j > s)
    return jnp.where((s >= tk) | tri, 1.0, 0.0).astype(BF16)


def _sb_weights(z, um, carry, mask):
    tk = z.shape[1]
    lk = -(jnp.maximum(z, 0.0) + jnp.log(1.0 + jnp.exp(-jnp.abs(z))))
    if mask is not None:
        lk = jnp.where(mask, lk, 0.0)
    rs = _dot_x2(lk, um)
    w = jnp.exp(z + lk + rs[:, :tk] + carry)
    if mask is not None:
        w = jnp.where(mask, w, 0.0)
    return w.astype(BF16), carry + rs[:, tk:]


def _softmax_step(s, pv, m_ref, l_ref, acc_ref, mask=None):
    m_prev = m_ref[...]
    m_new = jnp.maximum(m_prev, jnp.max(s, axis=1, keepdims=True))
    p = jnp.exp(s - m_new)
    if mask is not None:
        p = jnp.where(mask, p, 0.0)
    alpha = jnp.exp(m_prev - m_new)
    l_ref[...] = alpha * l_ref[...] + jnp.sum(p, axis=1, keepdims=True)
    acc_ref[...] = alpha * acc_ref[...] + pv(p.astype(BF16))
    m_ref[...] = m_new


def _softmax_init(m_ref, l_ref, acc_ref):
    m_ref[...] = jnp.full(m_ref.shape, NEG, F32)
    l_ref[...] = jnp.zeros(l_ref.shape, F32)
    acc_ref[...] = jnp.zeros(acc_ref.shape, F32)


def _softmax_out(l_ref, acc_ref):
    return acc_ref[...] / jnp.maximum(l_ref[...], 1e-30)


def _softmax_scratch(r, width):
    return [pltpu.VMEM((r, 1), F32), pltpu.VMEM((r, 1), F32), pltpu.VMEM((r, width), F32)]


def _rank_select(imp, t, ns):
    blk = _iota(imp.shape, 1)
    cur = t // SLC_BLOCK
    valid = (blk * SLC_BLOCK <= t) & (blk < ns)
    forced = (blk == 0) | (blk == cur) | (blk == cur - 1)
    score = jnp.where(valid, jnp.where(forced, FORCE_SCORE, imp), -jnp.inf)
    cnt = jnp.zeros(imp.shape, F32)
    for j in range(ns):
        col = score[:, j:j + 1]
        cnt = cnt + jnp.where(col > score, 1.0, 0.0) + jnp.where(col == score, jnp.where(blk > j, 1.0, 0.0), 0.0)
    return jnp.where(valid & (cnt < N_SELECT), 1.0, 0.0)


def _expand_blocks(sel, n_keys):
    nsp = sel.shape[1]
    blk = _iota((nsp, n_keys), 0)
    key = _iota((nsp, n_keys), 1)
    return _dot(sel, jnp.where(key // SLC_BLOCK == blk, 1.0, 0.0).astype(BF16))


def _sb_prompt_kernel(q_ref, kt_ref, vt_ref, o_ref, ktb, vtb, acc, car, *, tq, nh):
    qi = pl.program_id(2)

    @pl.when(qi == 0)
    def _():
        ktb[...] = kt_ref[0].astype(BF16)
        vtb[...] = vt_ref[0].astype(BF16)

    um = _running_sum_matrix(tq, False)
    q0 = pl.multiple_of(qi * tq, tq)
    mask = _iota((tq, tq), 1) < _iota((tq, tq), 0)
    outs = []
    for h in range(nh):
        q = q_ref[0, h]
        rows = slice(h * HEAD_DIM, (h + 1) * HEAD_DIM)
        w, c = _sb_weights(_dot(q, ktb[rows, pl.ds(q0, tq)]), um, jnp.zeros((tq, tq), F32), mask)
        acc[...] = _dot_nt(w, vtb[rows, pl.ds(q0, tq)])
        car[...] = c

        def body(it, _):
            k0 = pl.multiple_of((qi - 1 - it) * tq, tq)
            w, c = _sb_weights(_dot(q, ktb[rows, pl.ds(k0, tq)]), um, car[...], None)
            acc[...] += _dot_nt(w, vtb[rows, pl.ds(k0, tq)])
            car[...] = c
            return 0

        lax.fori_loop(0, qi, body, 0)
        outs.append(acc[...])
    o_ref[0] = jnp.concatenate(outs, axis=1).astype(BF16)


def _sb_prompt(q, kvt, tq, nh):
    b, n_h, s, _ = q.shape
    c = nh * HEAD_DIM
    ng = n_h // nh
    return pl.pallas_call(
        functools.partial(_sb_prompt_kernel, tq=tq, nh=nh), name="sb_prompt",
        grid=(b, ng, s // tq),
        in_specs=[pl.BlockSpec((1, nh, tq, HEAD_DIM), lambda i, g, j: (i, g, j, 0)),
                  pl.BlockSpec((1, c, s), lambda i, g, j: (i, g, 0)),
                  pl.BlockSpec((1, c, s), lambda i, g, j: (i, ng + g, 0))],
        out_specs=pl.BlockSpec((1, tq, c), lambda i, g, j: (i, j, g)),
        out_shape=jax.ShapeDtypeStruct((b, s, n_h * HEAD_DIM), BF16),
        scratch_shapes=[pltpu.VMEM((c, s), BF16), pltpu.VMEM((c, s), BF16),
                        pltpu.VMEM((tq, HEAD_DIM), F32), pltpu.VMEM((tq, tq), F32)],
        compiler_params=_params("arbitrary", "arbitrary", "arbitrary"),
    )(q, kvt, kvt)


def _fox_prompt_kernel(q_ref, kt_ref, vt_ref, lft_ref, o_ref, ktb, vtb, ncum, m_ref, l_ref, acc, *, tq, nh):
    g = pl.program_id(1)
    qi = pl.program_id(2)
    s_len = kt_ref.shape[2]

    @pl.when(qi == 0)
    def _():
        ktb[...] = kt_ref[0].astype(BF16)
        vtb[...] = vt_ref[0].astype(BF16)
        um = _running_sum_matrix(tq, True)
        run = jnp.zeros((lft_ref.shape[1], tq), F32)
        for ch in range(s_len // tq):
            sl = slice(ch * tq, (ch + 1) * tq)
            rs = _dot_x3(lft_ref[0, :, sl], um)
            ncum[:, sl] = -(rs[:, :tq] + run)
            run = run + rs[:, tq:]

    q0 = pl.multiple_of(qi * tq, tq)
    mask = _iota((tq, tq), 1) <= _iota((tq, tq), 0)
    outs = []
    for h in range(nh):
        q = q_ref[0, h]
        rows = slice(h * HEAD_DIM, (h + 1) * HEAD_DIM)
        head = g * nh + h

        def tile(k0, causal):
            s = _dot(q, ktb[rows, pl.ds(k0, tq)]) + ncum[pl.ds(head, 1), pl.ds(k0, tq)]
            if causal:
                s = jnp.where(mask, s, NEG)
            _softmax_step(s, lambda p: _dot_nt(p, vtb[rows, pl.ds(k0, tq)]), m_ref, l_ref, acc)

        _softmax_init(m_ref, l_ref, acc)
        tile(q0, True)

        def body(it, _):
            tile(pl.multiple_of(it * tq, tq), False)
            return 0

        lax.fori_loop(0, qi, body, 0)
        outs.append(_softmax_out(l_ref, acc))
    o_ref[0] = jnp.concatenate(outs, axis=1).astype(BF16)


def _fox_prompt(q, kvt, lft, tq, nh):
    b, n_h, s, _ = q.shape
    c = nh * HEAD_DIM
    ng = n_h // nh
    n_f = lft.shape[1]
    return pl.pallas_call(
        functools.partial(_fox_prompt_kernel, tq=tq, nh=nh), name="fox_prompt",
        grid=(b, ng, s // tq),
        in_specs=[pl.BlockSpec((1, nh, tq, HEAD_DIM), lambda i, g, j: (i, g, j, 0)),
                  pl.BlockSpec((1, c, s), lambda i, g, j: (i, g, 0)),
                  pl.BlockSpec((1, c, s), lambda i, g, j: (i, ng + g, 0)),
                  pl.BlockSpec((1, n_f, s), lambda i, g, j: (i, 0, 0))],
        out_specs=pl.BlockSpec((1, tq, c), lambda i, g, j: (i, j, g)),
        out_shape=jax.ShapeDtypeStruct((b, s, n_h * HEAD_DIM), BF16),
        scratch_shapes=[pltpu.VMEM((c, s), BF16), pltpu.VMEM((c, s), BF16), pltpu.VMEM((n_f, s), F32)]
        + _softmax_scratch(tq, HEAD_DIM),
        compiler_params=_params("arbitrary", "arbitrary", "arbitrary"),
    )(q, kvt, kvt, lft)


def _page_specs(n_pages, rows, npg):
    return [pl.BlockSpec((1, rows, PAGE), functools.partial(
        lambda i, s, pt, off: (pt[i, n_pages - 1 - s * npg - off], 0, 0), off=off)) for off in range(npg)]


def _sb_decode_kernel(pt_ref, q_ref, new_ref, *rest, tq, hg, npg):
    page_refs = rest[:npg]
    o_ref, qbd_ref, newp, acc, car = rest[npg:]
    st = pl.program_id(1)
    da = hg * HEAD_DIM
    r = hg * tq
    um = _running_sum_matrix(PAGE, False)

    @pl.when(st == 0)
    def _():
        qbd_ref[...] = _block_diag_rows(q_ref[0], hg)
        newp[...] = jnp.zeros(newp.shape, F32)
        newp[0:tq, :] = new_ref[0]
        mask = _iota((r, PAGE), 1) < _row_query((r, PAGE), tq)
        w, c = _sb_weights(_dot_nt(qbd_ref[...], newp[:, 0:da].astype(BF16)), um, jnp.zeros((r, PAGE), F32), mask)
        acc[...] = _dot(w, newp[:, da:2 * da].astype(BF16))
        car[...] = c

    for pg in page_refs:
        w, c = _sb_weights(_dot(qbd_ref[...], pg[0, 0:da, :].astype(BF16)), um, car[...], None)
        acc[...] += _dot_nt(w, pg[0, da:2 * da, :].astype(BF16))
        car[...] = c

    @pl.when(st == pl.num_programs(1) - 1)
    def _():
        o_ref[0] = _gather_diag(acc[...], tq, hg).astype(BF16)


def _sb_decode(q, kv_new, cache_t, page_table, npg):
    b, tq, da = q.shape
    hg = da // HEAD_DIM
    n_pages = page_table.shape[1]
    r = hg * tq
    grid_spec = pltpu.PrefetchScalarGridSpec(
        num_scalar_prefetch=1,
        grid=(b, n_pages // npg),
        in_specs=[pl.BlockSpec((1, tq, da), lambda i, s, pt: (i, 0, 0)),
                  pl.BlockSpec((1, tq, 2 * da), lambda i, s, pt: (i, 0, 0))] + _page_specs(n_pages, 2 * da, npg),
        out_specs=pl.BlockSpec((1, tq, da), lambda i, s, pt: (i, 0, 0)),
        scratch_shapes=[pltpu.VMEM((r, da), BF16), pltpu.VMEM((PAGE, 2 * da), F32),
                        pltpu.VMEM((r, da), F32), pltpu.VMEM((r, PAGE), F32)],
    )
    return pl.pallas_call(
        functools.partial(_sb_decode_kernel, tq=tq, hg=hg, npg=npg), name="sb_decode",
        grid_spec=grid_spec,
        out_shape=jax.ShapeDtypeStruct((b, tq, da), BF16),
        compiler_params=_params("arbitrary", "arbitrary"),
    )(page_table, q, kv_new, *([cache_t] * npg))


def _fox_decode_kernel(pt_ref, q_ref, new_ref, lfnew_ref, *rest, tq, hg, npg):
    page_refs = rest[:npg]
    lf_refs = rest[npg:2 * npg]
    o_ref, qbd_ref, newp, lfp, run_ref, m_ref, l_ref, acc = rest[2 * npg:]
    st = pl.program_id(1)
    da = hg * HEAD_DIM
    r = hg * tq
    um = _running_sum_matrix(PAGE, False)

    def key_bias(lft):
        rs = _dot_x3(lft, um)
        suf = rs[:, :PAGE] + run_ref[...]
        run_ref[...] = run_ref[...] + rs[:, PAGE:]
        return jnp.concatenate([jnp.broadcast_to(suf[h:h + 1, :], (tq, PAGE)) for h in range(hg)], axis=0)

    @pl.when(st == 0)
    def _():
        qbd_ref[...] = _block_diag_rows(q_ref[0], hg)
        newp[...] = jnp.zeros(newp.shape, F32)
        newp[0:tq, :] = new_ref[0]
        lfp[...] = jnp.zeros(lfp.shape, F32)
        lfp[0:tq, :] = lfnew_ref[0]
        run_ref[...] = jnp.zeros(run_ref.shape, F32)
        _softmax_init(m_ref, l_ref, acc)
        mask = _iota((r, PAGE), 1) <= _row_query((r, PAGE), tq)
        s = _dot_nt(qbd_ref[...], newp[:, 0:da].astype(BF16)) + key_bias(lfp[...].T[0:hg, :])
        _softmax_step(jnp.where(mask, s, NEG), lambda p: _dot(p, newp[:, da:2 * da].astype(BF16)), m_ref, l_ref, acc)

    for pg, lf in zip(page_refs, lf_refs):
        s = _dot(qbd_ref[...], pg[0, 0:da, :].astype(BF16)) + key_bias(lf[0])
        _softmax_step(s, lambda p, pg=pg: _dot_nt(p, pg[0, da:2 * da, :].astype(BF16)), m_ref, l_ref, acc)

    @pl.when(st == pl.num_programs(1) - 1)
    def _():
        o_ref[0] = _gather_diag(_softmax_out(l_ref, acc), tq, hg).astype(BF16)


def _fox_decode(q, kv_new, lf_new, cache_t, cache_lf_t, page_table, npg):
    b, tq, da = q.shape
    hg = da // HEAD_DIM
    n_pages = page_table.shape[1]
    r = hg * tq
    assert cache_lf_t.shape[1] == hg
    grid_spec = pltpu.PrefetchScalarGridSpec(
        num_scalar_prefetch=1,
        grid=(b, n_pages // npg),
        in_specs=[pl.BlockSpec((1, tq, da), lambda i, s, pt: (i, 0, 0)),
                  pl.BlockSpec((1, tq, 2 * da), lambda i, s, pt: (i, 0, 0)),
                  pl.BlockSpec((1, tq, LANES), lambda i, s, pt: (i, 0, 0))]
        + _page_specs(n_pages, 2 * da, npg) + _page_specs(n_pages, hg, npg),
        out_specs=pl.BlockSpec((1, tq, da), lambda i, s, pt: (i, 0, 0)),
        scratch_shapes=[pltpu.VMEM((r, da), BF16), pltpu.VMEM((PAGE, 2 * da), F32),
                        pltpu.VMEM((PAGE, LANES), F32), pltpu.VMEM((hg, PAGE), F32)] + _softmax_scratch(r, da),
    )
    return pl.pallas_call(
        functools.partial(_fox_decode_kernel, tq=tq, hg=hg, npg=npg), name="fox_decode",
        grid_spec=grid_spec,
        out_shape=jax.ShapeDtypeStruct((b, tq, da), BF16),
        compiler_params=_params("arbitrary", "arbitrary"),
    )(page_table, q, kv_new, lf_new, *([cache_t] * npg), *([cache_lf_t] * npg))


def _compress(xb, bd_ref, pe_ref, n_chunk, dkv):
    first = [jnp.zeros((n_chunk, dkv), F32) for _ in range(2)]
    second = [jnp.zeros((n_chunk, dkv), F32) for _ in range(2)]
    for j in range(D_CMP):
        xj = jnp.concatenate([xb[lb, pl.ds(j, n_chunk, stride=D_CMP), :] for lb in range(xb.shape[0])], axis=1)
        for kv in range(2):
            xx = xj[:, kv * dkv:(kv + 1) * dkv]
            first[kv] += _dot((xx + pe_ref[kv, j:j + 1, :]).astype(BF16), bd_ref[kv, j])
            second[kv] += _dot((xx + pe_ref[kv, D_CMP + j:D_CMP + j + 1, :]).astype(BF16), bd_ref[kv, D_CMP + j])
    return [first[kv] + pltpu.roll(second[kv], n_chunk - 1, axis=0) for kv in range(2)]


def _compress_prompt_kernel(x_ref, bd_ref, pe_ref, bdn_ref, gk_ref, kct_ref, vct_ref, xb):
    n_chunk = x_ref.shape[1] // D_CMP
    dkv = bdn_ref.shape[0]
    for lb in range(xb.shape[0]):
        xb[lb] = x_ref[0, :, lb * LANES:(lb + 1) * LANES]
    kc, vc = _compress(xb, bd_ref, pe_ref, n_chunk, dkv)
    kct_ref[0] = (_head_norm(kc, bdn_ref[...]) * gk_ref[...]).T.astype(BF16)
    vct_ref[0] = vc.T.astype(BF16)


def _compress_prompt(cmp_kv, bd, pe, bdn, gk):
    b, s, w = cmp_kv.shape
    dkv = w // 2
    n_chunk = s // D_CMP
    const = lambda a: pl.BlockSpec(a.shape, lambda i: (0,) * a.ndim)
    ospec = pl.BlockSpec((1, dkv, n_chunk), lambda i: (i, 0, 0))
    return pl.pallas_call(
        _compress_prompt_kernel, name="compress_prompt",
        grid=(b,),
        in_specs=[pl.BlockSpec((1, s, w), lambda i: (i, 0, 0)), const(bd), const(pe), const(bdn), const(gk)],
        out_specs=[ospec, ospec],
        out_shape=[jax.ShapeDtypeStruct((b, dkv, n_chunk), BF16)] * 2,
        scratch_shapes=[pltpu.VMEM((w // LANES, s, LANES), F32)],
        compiler_params=_params("arbitrary"),
    )(cmp_kv, bd, pe, bdn, gk)


def _nsa_prompt_kernel(q_ref, kst_ref, vst_ref, kwt_ref, vwt_ref, kct_ref, vct_ref, gates_ref, band_ref, t0_ref, t1_ref,
                       ov_ref, o_ref, ksb, vsb, kwb, vwb, madd, m_ref, l_ref, acc, *, tq, nz, ns):
    g = pl.program_id(1)
    qi = pl.program_id(2)
    s_len = kst_ref.shape[2]
    ncp = kct_ref.shape[2]
    r = nz * tq

    @pl.when(qi == 0)
    def _():
        ksb[...] = kst_ref[0].astype(BF16)
        vsb[...] = vst_ref[0].astype(BF16)
        kwb[...] = kwt_ref[0].astype(BF16)
        vwb[...] = vwt_ref[0].astype(BF16)

    q = q_ref[0].reshape(r, HEAD_DIM)
    q0 = pl.multiple_of(qi * tq, tq)
    i2 = _iota((tq, tq), 0)
    j2 = _iota((tq, tq), 1)
    per_z = lambda a: jnp.concatenate([a] * nz, axis=0)

    band = pltpu.roll(band_ref[...].reshape(r, ncp), (qi * (tq // D_CMP) + ncp - BAND_BACK) % ncp, axis=1)
    t_r = q0 + _row_query((r, ncp), tq)
    c_r = _iota((r, ncp), 1)
    cmask = (t_r >= c_r * D_CMP + (L_CMP - 1)) & (c_r < ncp - 1)
    s = jnp.where(cmask, _dot(q, kct_ref[0]) + band, NEG)
    p = jnp.where(cmask, jnp.exp(s - jnp.max(s, axis=1, keepdims=True)), 0.0)
    p = p / jnp.maximum(jnp.sum(p, axis=1, keepdims=True), 1e-30)
    o_cmp = _dot_nt(p.astype(BF16), vct_ref[0])
    pz = p[0:tq]
    for z in range(1, nz):
        pz = pz + p[z * tq:(z + 1) * tq]
    imp = _dot_x2(pz, ov_ref[...])
    sel = _rank_select(imp, q0 + _iota(imp.shape, 0), ns)
    madd[...] = (_expand_blocks(sel.astype(BF16), s_len) - 1.0) * (-NEG)

    def slc_scores(k0, tk, extra):
        ma = madd[:, pl.ds(k0, tk)]
        if extra is not None:
            ma = ma + extra
        return _dot(q, ksb[:, pl.ds(k0, tk)]) + per_z(ma)

    def slc_tile(k0, tk, extra):
        _softmax_step(slc_scores(k0, tk, extra), lambda p: _dot_nt(p, vsb[:, pl.ds(k0, tk)]), m_ref, l_ref, acc)

    _softmax_init(m_ref, l_ref, acc)
    causal = jnp.where(j2 <= i2, 0.0, NEG)
    t0 = t0_ref[...].reshape(r, tq)
    t1 = t1_ref[...].reshape(r, tq)
    _softmax_step(slc_scores(q0, tq, causal) + t0, lambda p: _dot_nt(p, vsb[:, pl.ds(q0, tq)]), m_ref, l_ref, acc)

    @pl.when(qi >= 1)
    def _():
        k0 = pl.multiple_of(q0 - tq, tq)
        _softmax_step(slc_scores(k0, tq, None) + t1, lambda p: _dot_nt(p, vsb[:, pl.ds(k0, tq)]), m_ref, l_ref, acc)

    n_far = jnp.maximum(qi - 1, 0)
    per_far = FAR_TILE // tq

    def far_body(it, _):
        slc_tile(pl.multiple_of(it * FAR_TILE, FAR_TILE), FAR_TILE, None)
        return 0

    lax.fori_loop(0, n_far // per_far, far_body, 0)

    def rem_body(it, _):
        slc_tile(pl.multiple_of(((n_far // per_far) * per_far + it) * tq, tq), tq, None)
        return 0

    lax.fori_loop(0, n_far % per_far, rem_body, 0)
    o_slc = _softmax_out(l_ref, acc)

    _softmax_init(m_ref, l_ref, acc)

    def win_tile(k0, tk, bias):
        s = _dot(q, kwb[:, pl.ds(k0, tk)])
        if bias is not None:
            s = s + bias
        _softmax_step(s, lambda p: _dot_nt(p, vwb[:, pl.ds(k0, tk)]), m_ref, l_ref, acc)

    win_tile(q0, tq, t0 + per_z(causal))
    n_back = WINDOW // tq

    @pl.when(qi >= 1)
    def _():
        win_tile(pl.multiple_of(q0 - tq, tq), tq, t1)

    for back in range(2, n_back):
        @pl.when(qi >= back)
        def _(back=back):
            win_tile(pl.multiple_of(q0 - back * tq, tq), tq, None)

    @pl.when(qi >= n_back)
    def _():
        win_tile(pl.multiple_of(q0 - n_back * tq, tq), tq, per_z(jnp.where(j2 > i2, 0.0, NEG)))

    o_win = _softmax_out(l_ref, acc)

    n_heads = nz * pl.num_programs(1)
    src = _iota((LANES, LANES), 0)
    dst = _iota((LANES, LANES), 1)
    pick = jnp.where((dst < 3 * nz) & (src == (dst // nz) * n_heads + g * nz + dst % nz), 1.0, 0.0).astype(BF16)
    gsel = _dot_x3(gates_ref[0], pick)
    gate = lambda br: jnp.concatenate([gsel[:, br * nz + z:br * nz + z + 1] for z in range(nz)], axis=0)
    o = gate(0) * o_cmp + gate(1) * o_slc + gate(2) * o_win
    o_ref[0] = jnp.concatenate([o[z * tq:(z + 1) * tq] for z in range(nz)], axis=1).astype(BF16)


def _nsa_prompt(q, slct, wint, kct, vct, gates, band, t0, t1, tq, n_kv):
    b, n_heads, s, _ = q.shape
    nz = n_heads // n_kv
    r = nz * tq
    ncp = kct.shape[2]
    ns = -(-s // SLC_BLOCK)
    nsp = -(-ns // LANES) * LANES
    ov = _overlap_matrix(ncp, nsp)
    kv_spec = lambda off: pl.BlockSpec((1, HEAD_DIM, s), lambda i, g, j: (i, off + g, 0))
    c_spec = pl.BlockSpec((1, HEAD_DIM, ncp), lambda i, g, j: (i, g, 0))
    tab_spec = lambda a: pl.BlockSpec((nz,) + a.shape[1:], lambda i, g, j: (g, 0, 0))
    return pl.pallas_call(
        functools.partial(_nsa_prompt_kernel, tq=tq, nz=nz, ns=ns), name="nsa_prompt",
        grid=(b, n_kv, s // tq),
        in_specs=[pl.BlockSpec((1, nz, tq, HEAD_DIM), lambda i, g, j: (i, g, j, 0)),
                  kv_spec(0), kv_spec(n_kv), kv_spec(0), kv_spec(n_kv), c_spec, c_spec,
                  pl.BlockSpec((1, tq, LANES), lambda i, g, j: (i, j, 0)),
                  tab_spec(band), tab_spec(t0), tab_spec(t1),
                  pl.BlockSpec(ov.shape, lambda i, g, j: (0, 0))],
        out_specs=pl.BlockSpec((1, tq, nz * HEAD_DIM), lambda i, g, j: (i, j, g)),
        out_shape=jax.ShapeDtypeStruct((b, s, n_heads * HEAD_DIM), BF16),
        scratch_shapes=[pltpu.VMEM((HEAD_DIM, s), BF16)] * 4 + [pltpu.VMEM((tq, s), F32)]
        + _softmax_scratch(r, HEAD_DIM),
        compiler_params=_params("arbitrary", "arbitrary", "arbitrary"),
    )(q, slct, slct, wint, wint, kct, vct, gates, band, t0, t1, ov)


def _cmp_select(q, kcmp, vcmp, bias, ov, q0, tq, nc, ns, n_kv, n_rep):
    ncp = kcmp.shape[0]
    qbd = _nsa_qbd(q, n_kv, n_rep)
    r = qbd.shape[0]
    t = q0 + _row_query((r, ncp), tq)
    cidx = _iota((r, ncp), 1)
    mask = (t - (cidx * D_CMP + L_CMP - 1) >= 0) & (cidx < nc)
    s = jnp.where(mask, _dot_nt(qbd, kcmp) + bias, NEG)
    p = jnp.where(mask, jnp.exp(s - jnp.max(s, axis=1, keepdims=True)), 0.0)
    p = p / jnp.maximum(jnp.sum(p, axis=1, keepdims=True), 1e-30)
    o_cmp = _nsa_gather(_dot(p.astype(BF16), vcmp), tq, n_kv, n_rep)
    rz = n_kv * tq
    pz = p[0:rz]
    for z in range(1, n_rep):
        pz = pz + p[z * rz:(z + 1) * rz]
    imp = _dot_x2(pz, ov)
    return o_cmp, _rank_select(imp, q0 + _row_query(imp.shape, tq), ns)


def _overlap_matrix(ncp, nsp):
    ci = np.arange(ncp)[:, None]
    bj = np.arange(nsp)[None, :]
    ov = (ci * D_CMP <= bj * SLC_BLOCK + SLC_BLOCK - 1) & (ci * D_CMP + L_CMP - 1 >= bj * SLC_BLOCK)
    return jnp.asarray(ov.astype(np.float32), BF16)


def _cmp_decode_kernel(pt_ref, q_ref, bd_ref, pe_ref, bdn_ref, gk_ref, bias_ref, ov_ref, *rest,
                       tq, n_kv, n_rep, ns, npg, past):
    page_refs = rest[:npg]
    ocmp_ref, sel_ref, xbuf = rest[npg:]
    st = pl.program_id(1)
    n_steps = pl.num_programs(1)
    dkv = n_kv * HEAD_DIM
    n_chunk = past // D_CMP
    for off, pg in enumerate(page_refs):
        pos = (n_steps - 1 - st) * npg + (npg - 1 - off)
        rows = pg[0].T
        for lb in range(xbuf.shape[0]):
            xbuf[lb, pl.ds(pl.multiple_of(pos * PAGE, PAGE), PAGE), :] = rows[:, lb * LANES:(lb + 1) * LANES]

    @pl.when(st == n_steps - 1)
    def _():
        kc, vc = _compress(xbuf, bd_ref, pe_ref, n_chunk, dkv)
        kcb = (_head_norm(kc, bdn_ref[...]) * gk_ref[...]).astype(BF16)
        bias = bias_ref[...].reshape(n_kv * n_rep * tq, n_chunk)
        o_cmp, sel = _cmp_select(q_ref[0], kcb, vc.astype(BF16), bias, ov_ref[...], past, tq,
                                 n_chunk - 1, ns, n_kv, n_rep)
        ocmp_ref[0] = o_cmp
        sel_ref[0] = sel.astype(BF16)


def _cmp_decode(q, cache_t, page_table, bd, pe, bdn, gk, bias, n_kv, npg):
    b, tq, dq = q.shape
    dkv = n_kv * HEAD_DIM
    n_rep = dq // dkv
    n_pages = page_table.shape[1]
    past = n_pages * PAGE
    n_chunk = (past + tq) // D_CMP
    assert n_chunk == past // D_CMP
    ns = -(-(past + tq) // SLC_BLOCK)
    nsp = -(-ns // LANES) * LANES
    ov = _overlap_matrix(n_chunk, nsp)
    const = lambda a: pl.BlockSpec(a.shape, lambda i, s, pt: (0,) * a.ndim)
    grid_spec = pltpu.PrefetchScalarGridSpec(
        num_scalar_prefetch=1,
        grid=(b, n_pages // npg),
        in_specs=[pl.BlockSpec((1, tq, dq), lambda i, s, pt: (i, 0, 0)),
                  const(bd), const(pe), const(bdn), const(gk), const(bias), const(ov)]
        + _page_specs(n_pages, 2 * dkv, npg),
        out_specs=[pl.BlockSpec((1, tq, dq), lambda i, s, pt: (i, 0, 0)),
                   pl.BlockSpec((1, n_kv * tq, nsp), lambda i, s, pt: (i, 0, 0))],
        scratch_shapes=[pltpu.VMEM((2 * dkv // LANES, past, LANES), F32)],
    )
    return pl.pallas_call(
        functools.partial(_cmp_decode_kernel, tq=tq, n_kv=n_kv, n_rep=n_rep, ns=ns, npg=npg, past=past),
        name="cmp_decode",
        grid_spec=grid_spec,
        out_shape=[jax.ShapeDtypeStruct((b, tq, dq), F32), jax.ShapeDtypeStruct((b, n_kv * tq, nsp), BF16)],
        compiler_params=_params("arbitrary", "arbitrary"),
    )(page_table, q, bd, pe, bdn, gk, bias, ov, *([cache_t] * npg))


def _nsa_qbd(q, n_kv, n_rep):
    dkv = n_kv * HEAD_DIM
    return jnp.concatenate([_block_diag_rows(q[:, z * dkv:(z + 1) * dkv], n_kv) for z in range(n_rep)], axis=0)


def _nsa_gather(o_full, tq, n_kv, n_rep):
    rz = n_kv * tq
    return jnp.concatenate([_gather_diag(o_full[z * rz:(z + 1) * rz], tq, n_kv) for z in range(n_rep)], axis=1)


def _merge(gates, ege_ref, o_cmp, o_slc, o_win):
    g = [_dot_x2(gates, ege_ref[br]) for br in range(3)]
    return g[0] * o_cmp + g[1] * o_slc + g[2] * o_win


def _slc_decode_kernel(pt_ref, q_ref, slcnew_ref, winnew_ref, state_ref, sel_ref, gates_ref, ocmp_ref,
                       t0_ref, t1_ref, ege_ref, *rest, tq, n_kv, n_rep, npg, past):
    page_refs = rest[:npg]
    o_ref, qbd_ref, newp, kmask, m_ref, l_ref, acc = rest[npg:]
    st = pl.program_id(1)
    n_steps = pl.num_programs(1)
    dkv = n_kv * HEAD_DIM
    r = n_rep * n_kv * tq
    i_row = _row_query((r, PAGE), tq)
    j_col = _iota((r, PAGE), 1)

    def sel_mask(k0):
        return jnp.concatenate([kmask[:, pl.ds(k0, PAGE)]] * n_rep, axis=0) > 0.5

    def step(s, mask, pv):
        _softmax_step(jnp.where(mask, s, NEG) if mask is not None else s, pv, m_ref, l_ref, acc, mask)

    def new_tile(bias, mask):
        k = newp[:, 0:dkv].astype(BF16)
        v = newp[:, dkv:2 * dkv].astype(BF16)
        step(_dot_nt(qbd_ref[...], k) + bias, mask, lambda p: _dot(p, v))

    @pl.when(st == 0)
    def _():
        qbd_ref[...] = _nsa_qbd(q_ref[0], n_kv, n_rep)
        kmask[...] = _expand_blocks(sel_ref[0], kmask.shape[1])
        newp[...] = jnp.zeros(newp.shape, F32)
        newp[0:tq, :] = slcnew_ref[0]
        _softmax_init(m_ref, l_ref, acc)
        new_tile(t0_ref[...], sel_mask(past) & (j_col <= i_row))

    for off, pg in enumerate(page_refs):
        pos = (n_steps - 1 - st) * npg + (npg - 1 - off)
        k0 = pl.multiple_of(pos * PAGE, PAGE)
        s = _dot(qbd_ref[...], pg[0, 0:dkv, :].astype(BF16))
        if off == 0:
            s = s + jnp.where(st == 0, 1.0, 0.0) * t1_ref[...]
        step(s, sel_mask(k0), lambda p, pg=pg: _dot_nt(p, pg[0, dkv:2 * dkv, :].astype(BF16)))

    @pl.when(st == n_steps - 1)
    def _():
        o_slc = _nsa_gather(_softmax_out(l_ref, acc), tq, n_kv, n_rep)
        _softmax_init(m_ref, l_ref, acc)
        newp[0:tq, :] = winnew_ref[0]
        new_tile(t0_ref[...], j_col <= i_row)
        n_back = WINDOW // PAGE
        for back in range(1, n_back + 1):
            sl = slice((n_back - back) * PAGE, (n_back - back + 1) * PAGE)
            s = _dot(qbd_ref[...], state_ref[0, 0:dkv, sl].astype(BF16))
            if back == 1:
                s = s + t1_ref[...]
            mask = (j_col > i_row) if back == n_back else None
            step(s, mask, lambda p, sl=sl: _dot_nt(p, state_ref[0, dkv:2 * dkv, sl].astype(BF16)))
        o_win = _nsa_gather(_softmax_out(l_ref, acc), tq, n_kv, n_rep)
        o_ref[0] = _merge(gates_ref[0], ege_ref, ocmp_ref[0], o_slc, o_win).astype(BF16)


def _slc_decode(q, slc_new, win_new, state_t, sel, gates, o_cmp, cache_t, page_table, t0, t1, ege, n_kv, npg):
    b, tq, dq = q.shape
    dkv = n_kv * HEAD_DIM
    n_rep = dq // dkv
    r = n_rep * n_kv * tq
    n_pages = page_table.shape[1]
    past = n_pages * PAGE
    const = lambda a: pl.BlockSpec(a.shape, lambda i, s, pt: (0,) * a.ndim)
    per_seq = lambda a: pl.BlockSpec((1,) + a.shape[1:], lambda i, s, pt: (i,) + (0,) * (a.ndim - 1))
    grid_spec = pltpu.PrefetchScalarGridSpec(
        num_scalar_prefetch=1,
        grid=(b, n_pages // npg),
        in_specs=[per_seq(q), per_seq(slc_new), per_seq(win_new), per_seq(state_t), per_seq(sel), per_seq(gates),
                  per_seq(o_cmp), const(t0), const(t1), const(ege)] + _page_specs(n_pages, 2 * dkv, npg),
        out_specs=pl.BlockSpec((1, tq, dq), lambda i, s, pt: (i, 0, 0)),
        scratch_shapes=[pltpu.VMEM((r, dkv), BF16), pltpu.VMEM((PAGE, 2 * dkv), F32),
                        pltpu.VMEM((n_kv * tq, past + PAGE), F32)] + _softmax_scratch(r, dkv),
    )
    return pl.pallas_call(
        functools.partial(_slc_decode_kernel, tq=tq, n_kv=n_kv, n_rep=n_rep, npg=npg, past=past),
        name="slc_decode",
        grid_spec=grid_spec,
        out_shape=jax.ShapeDtypeStruct((b, tq, dq), BF16),
        compiler_params=_params("arbitrary", "arbitrary"),
    )(page_table, q, slc_new, win_new, state_t, sel, gates, o_cmp, t0, t1, ege, *([cache_t] * npg))


def _rel_buckets(dist):
    n = np.maximum(dist, 0)
    exact = N_BUCKETS // 2
    nf = np.maximum(n, 1).astype(np.float64)
    large = exact + (np.log(nf / exact) / math.log(REL_MAX_DIST / exact) * (N_BUCKETS - exact)).astype(np.int64)
    return np.where(n < exact, n, np.minimum(large, N_BUCKETS - 1)).astype(np.int32)


def _head_block_diag(width, scale):
    h = np.arange(width) // HEAD_DIM
    return jnp.asarray((h[:, None] == h[None, :]).astype(np.float32) * scale, BF16)


class _NsaLayout:
    def __init__(self, n_heads, n_kv):
        n_rep = n_heads // n_kv
        self.n_kv, self.n_rep = n_kv, n_rep
        new = np.arange(n_heads)
        z, g = new // n_kv, new % n_kv
        self.head_perm = g * n_rep + z
        self.col_perm = (self.head_perm[:, None] * HEAD_DIM + np.arange(HEAD_DIM)[None, :]).reshape(-1)
        self.gate_perm = (np.arange(3)[:, None] * n_heads + self.head_perm[None, :]).reshape(-1)
        ege = np.zeros((3, LANES, n_heads * HEAD_DIM), np.float32)
        for br in range(3):
            for h in range(n_heads):
                ege[br, br * n_heads + h, h * HEAD_DIM:(h + 1) * HEAD_DIM] = 1.0
        self.ege = jnp.asarray(ege, BF16)


def _near_bias(rel_hd, dist):
    far = rel_hd[:, N_BUCKETS - 1]
    tab = jnp.take(rel_hd, jnp.asarray(_rel_buckets(dist)), axis=1) - far[:, None, None]
    return jnp.where(jnp.asarray(dist >= 0)[None], tab, 0.0)


def _toeplitz_tiles(rel_hd, tq, tk):
    i = np.arange(tq)[:, None]
    j = np.arange(tk)[None, :]
    return _near_bias(rel_hd, i - j), _near_bias(rel_hd, tk + i - j)


def _cmp_band(rel_hd, tq, ncp):
    i = np.arange(tq)[:, None]
    m = np.arange(ncp)[None, :]
    dist = D_CMP * (BAND_BACK - m) + i - (L_CMP - 1)
    return _near_bias(rel_hd, np.where(m < 2 * BAND_BACK + tq // D_CMP, dist, -1))


def _cmp_bias(rel_hd, qpos, n_chunk):
    dc = qpos[:, None] - (np.arange(n_chunk)[None, :] * D_CMP + L_CMP - 1)
    return jnp.take(rel_hd, jnp.asarray(_rel_buckets(dc)), axis=1)


def kernel(x_prompt, x_sample, cache_sb_kv, cache_fox_kv, cache_fox_logf, cache_cmp_kv, cache_slc_kv, state_win_kv,
           page_table, c_prompt, c_sample, norm_gain, w_ada, b_ada, ffn_w_in, ffn_w_out, w_in_ab, b_forget,
           fox_qk_gain, w_out_ab, w_in_nsa, b_nsa_gate, nsa_qk_gain, cmp_w, cmp_pe, rel_bias, w_out_nsa):
    bp, s_len, d = x_prompt.shape
    bs, t_dec, _ = x_sample.shape
    n_pool = cache_sb_kv.shape[0]
    h_sb, h_fox = cache_sb_kv.shape[3], cache_fox_kv.shape[3]
    n_kv = cache_cmp_kv.shape[3]
    n_heads = rel_bias.shape[1]
    da = h_sb * HEAD_DIM
    dkv = n_kv * HEAD_DIM
    dq = n_heads * HEAD_DIM
    assert h_sb == h_fox
    n_pages = page_table.shape[1]
    past = n_pages * PAGE
    lay = _NsaLayout(n_heads, n_kv)

    ffn_in = ffn_w_in.astype(BF16)
    ffn_out = ffn_w_out.astype(BF16)
    w_ab = jnp.pad(w_in_ab, ((0, 0), (0, LANES - h_fox))).astype(BF16)
    bfp = jnp.pad(b_forget, (0, LANES - h_fox)).reshape(1, LANES)
    bd_ab = _head_block_diag(da, 1.0 / HEAD_DIM)
    gq_fox = jnp.tile(fox_qk_gain[0], h_fox).reshape(1, da)
    gk_fox = jnp.tile(fox_qk_gain[1], h_fox).reshape(1, da)
    wq_ab = jnp.concatenate([w_in_ab[:, 0:da], w_in_ab[:, 3 * da:4 * da]], axis=1).astype(BF16)
    wkvt_ab = jnp.concatenate([w_in_ab[:, da:3 * da], w_in_ab[:, 4 * da:6 * da]], axis=1).T.astype(BF16)
    wflt_ab = jnp.pad(w_in_ab[:, 6 * da:].T, ((0, 16 - h_fox), (0, 0))).astype(BF16)
    w_out_sb = w_out_ab[:da].astype(BF16)
    w_out_fox = w_out_ab[da:].astype(BF16)
    w_nsa = jnp.concatenate([w_in_nsa[:, lay.col_perm], w_in_nsa[:, dq:dq + 6 * dkv],
                             w_in_nsa[:, dq + 6 * dkv + lay.gate_perm]], axis=1)
    w_nsa = jnp.pad(w_nsa, ((0, 0), (0, LANES - 3 * n_heads))).astype(BF16)
    bg_perm = jnp.pad(b_nsa_gate[lay.gate_perm], (0, LANES - 3 * n_heads)).reshape(1, LANES)
    wq_nsa = w_in_nsa[:, 0:dq].astype(BF16)
    wcmp_nsa = w_in_nsa[:, dq:dq + 2 * dkv].astype(BF16)
    wkvt_nsa = w_in_nsa[:, dq + 2 * dkv:dq + 6 * dkv].T.astype(BF16)
    wg_nsa = jnp.pad(w_in_nsa[:, dq + 6 * dkv:], ((0, 0), (0, LANES - 3 * n_heads))).astype(BF16)
    bg = jnp.pad(b_nsa_gate, (0, LANES - 3 * n_heads)).reshape(1, LANES)
    bd_kv = _head_block_diag(dkv, 1.0 / HEAD_DIM)
    tile_kv = lambda g: jnp.tile(g, n_kv).reshape(1, dkv)
    w_out_n = w_out_nsa.astype(BF16)
    w_out_n_perm = w_out_nsa[lay.col_perm].astype(BF16)
    eye = jnp.eye(n_kv, dtype=F32)
    cmp_bd = jnp.einsum('gh,kjde->kjgdhe', eye, cmp_w).reshape(2, L_CMP, dkv, dkv).astype(BF16)
    cmp_pe_t = jnp.tile(cmp_pe, (1, 1, n_kv))
    rel_orig = rel_bias.T
    rel_perm = rel_bias[:, lay.head_perm].T
    gk_cmp = tile_kv(nsa_qk_gain[1])

    mod = _modulation(jnp.concatenate([c_prompt, c_sample], axis=0), w_ada, b_ada)
    mod = mod.reshape(mod.shape[0], bp + bs, 3, 3, 1, d)

    def mods(l, sub, lo, hi):
        return tuple(mod[l, lo:hi, sub, k] for k in range(3))

    mp = lambda l, sub: mods(l, sub, 0, bp)
    tt = 512
    x = _ffn(x_prompt, mp(0, 0), norm_gain[0, 0], ffn_in[0, 0], ffn_out[0, 0], 1, tt)
    qsb, qfx, sbt, fxt, lft = _proj_ab_t(x, mp(0, 1), norm_gain[0, 1], wq_ab, wkvt_ab, wflt_ab, bd_ab, gq_fox,
                                         gk_fox.reshape(da, 1), b_forget.reshape(h_fox, 1), tt)
    o_sb = _sb_prompt(qsb, sbt, AB_TILE, 2)
    o_fox = _fox_prompt(qfx, fxt, lft, AB_TILE, 2)
    x = _outproj(x, mp(0, 1)[2], [o_sb, o_fox], [w_out_sb, w_out_fox], 1, tt)
    x = _ffn(x, mp(0, 2), norm_gain[0, 2], ffn_in[0, 1], ffn_out[0, 1], 1, tt)
    x = _ffn(x, mp(1, 0), norm_gain[1, 0], ffn_in[1, 0], ffn_out[1, 0], 1, tt)
    q, p_cmp, slct, wint, gates = _proj_nsa_t(x, mp(1, 1), norm_gain[1, 1], wq_nsa, wcmp_nsa, wkvt_nsa, wg_nsa, bd_kv,
                                              tile_kv(nsa_qk_gain[0]), tile_kv(nsa_qk_gain[2]).reshape(dkv, 1),
                                              tile_kv(nsa_qk_gain[3]).reshape(dkv, 1), bg, tt)
    kct, vct = _compress_prompt(p_cmp, cmp_bd, cmp_pe_t, bd_kv, gk_cmp)
    t0, t1 = _toeplitz_tiles(rel_orig, TILE, TILE)
    o = _nsa_prompt(q, slct, wint, kct, vct, gates, _cmp_band(rel_orig, TILE, s_len // D_CMP), t0, t1, TILE, n_kv)
    x = _outproj(x, mp(1, 1)[2], [o], [w_out_n], 1, tt)
    y_p = _ffn(x, mp(1, 2), norm_gain[1, 2], ffn_in[1, 1], ffn_out[1, 1], 1, tt)

    c_sb = jnp.transpose(cache_sb_kv, (0, 2, 3, 4, 1)).reshape(n_pool, 2 * da, PAGE)
    c_fox = jnp.transpose(cache_fox_kv, (0, 2, 3, 4, 1)).reshape(n_pool, 2 * da, PAGE)
    c_lf = jnp.transpose(cache_fox_logf, (0, 2, 1))
    c_cmp = jnp.transpose(cache_cmp_kv, (0, 2, 3, 4, 1)).reshape(n_pool, 2 * dkv, PAGE)
    c_slc = jnp.transpose(cache_slc_kv, (0, 2, 3, 4, 1)).reshape(n_pool, 2 * dkv, PAGE)
    state_t = jnp.transpose(state_win_kv, (0, 2, 3, 4, 1)).reshape(bs, 2 * dkv, state_win_kv.shape[1])
    npg = PAGES_PER_STEP
    ms = lambda l, sub: mods(l, sub, bp, bp + bs)
    x = _ffn(x_sample, ms(0, 0), norm_gain[0, 0], ffn_in[0, 0], ffn_out[0, 0], bs, t_dec)
    qsb, s_sb, qfx, s_fox, s_logf, logfp = _proj_ab(x, ms(0, 1), norm_gain[0, 1], w_ab, bd_ab, gq_fox, gk_fox, bfp,
                                                    h_fox, bs, t_dec)
    o_sb = _sb_decode(qsb, s_sb, c_sb, page_table, npg)
    o_fox = _fox_decode(qfx, s_fox, logfp, c_fox, c_lf, page_table, npg)
    x = _outproj(x, ms(0, 1)[2], [o_sb, o_fox], [w_out_sb, w_out_fox], bs, t_dec)
    x = _ffn(x, ms(0, 2), norm_gain[0, 2], ffn_in[0, 1], ffn_out[0, 1], bs, t_dec)
    x = _ffn(x, ms(1, 0), norm_gain[1, 0], ffn_in[1, 0], ffn_out[1, 0], bs, t_dec)
    q, s_cmp, s_slc, s_win, gates = _proj_nsa(x, ms(1, 1), norm_gain[1, 1], w_nsa, bd_kv, tile_kv(nsa_qk_gain[0]),
                                              tile_kv(nsa_qk_gain[2]), tile_kv(nsa_qk_gain[3]), bg_perm, dq, bs, t_dec)
    bias = _cmp_bias(rel_perm, past + np.arange(t_dec), past // D_CMP)
    o_cmp, sel = _cmp_decode(q, c_cmp, page_table, cmp_bd, cmp_pe_t, bd_kv, gk_cmp, bias, n_kv, npg)
    t0, t1 = _toeplitz_tiles(rel_perm, t_dec, PAGE)
    o = _slc_decode(q, s_slc, s_win, state_t, sel, gates, o_cmp, c_slc, page_table,
                    t0.reshape(-1, PAGE), t1.reshape(-1, PAGE), lay.ege, n_kv, npg)
    x = _outproj(x, ms(1, 1)[2], [o], [w_out_n_perm], bs, t_dec)
    y_s = _ffn(x, ms(1, 2), norm_gain[1, 2], ffn_in[1, 1], ffn_out[1, 1], bs, t_dec)

    kv5 = lambda a, h: a.reshape(a.shape[0], a.shape[1], 2, h, HEAD_DIM)
    from_t = lambda a, h: kv5(jnp.transpose(a, (0, 2, 1)), h)
    win_len = min(WINDOW, s_len)
    state_rows = state_win_kv.reshape(bs, state_win_kv.shape[1], 2 * dkv)
    s_win_all = jnp.concatenate([state_rows, s_win], axis=1)
    new_len = min(WINDOW, s_win_all.shape[1])
    return (y_p, y_s, from_t(sbt, h_sb), kv5(s_sb, h_sb), from_t(fxt, h_fox), kv5(s_fox, h_fox),
            jnp.transpose(lft, (0, 2, 1)), s_logf, kv5(p_cmp, n_kv), kv5(s_cmp, n_kv), from_t(slct, n_kv),
            kv5(s_slc, n_kv), from_t(wint[:, :, s_len - win_len:], n_kv),
            kv5(s_win_all[:, s_win_all.shape[1] - new_len:], n_kv))
```

```python
import functools
import math

import numpy as np
import jax
import jax.numpy as jnp
from jax import lax
from jax.experimental import pallas as pl
from jax.experimental.pallas import tpu as pltpu

F32 = jnp.float32
BF16 = jnp.bfloat16

HEAD_DIM = 64
PAGE = 128
L_CMP = 32
D_CMP = 16
SLC_BLOCK = 64
N_SELECT = 16
WINDOW = 512
N_BUCKETS = 32
REL_MAX_DIST = 128
FORCE_SCORE = 1e4
NEG = -1e30
EPS = 1e-6
MACARON = 0.5
LANES = 128
VMEM_LIMIT = 56 * 1024 * 1024
FF_CHUNK = 256
TILE = 128
FAR_TILE = 512
AB_TILE = 256
PAGES_PER_STEP = 8
BAND_BACK = (REL_MAX_DIST + L_CMP) // D_CMP


def _params(*sem):
    return pltpu.CompilerParams(dimension_semantics=sem, vmem_limit_bytes=VMEM_LIMIT)


def _dot(a, b):
    return jnp.dot(a, b, preferred_element_type=F32)


def _dot_nt(a, b):
    return lax.dot_general(a, b, (((1,), (1,)), ((), ())), preferred_element_type=F32)


def _split2(x):
    hi = x.astype(BF16)
    lo = (x - hi.astype(F32)).astype(BF16)
    return hi, lo


def _split3(x):
    hi = x.astype(BF16)
    r = x - hi.astype(F32)
    mid = r.astype(BF16)
    lo = (r - mid.astype(F32)).astype(BF16)
    return hi, mid, lo


def _dot_x2(x, w):
    hi, lo = _split2(x)
    return _dot(hi, w) + _dot(lo, w)


def _dot_x3(x, w):
    hi, mid, lo = _split3(x)
    return _dot(hi, w) + _dot(mid, w) + _dot(lo, w)


def _dot_l2(w, x):
    hi, lo = _split2(x)
    return _dot(w, hi) + _dot(w, lo)


def _log_sigmoid(x):
    return jnp.minimum(x, 0.0) - jnp.log(1.0 + jnp.exp(-jnp.abs(x)))


def _adanorm(x, g, scale, shift):
    ms = jnp.mean(x * x, axis=-1, keepdims=True)
    return (x * lax.rsqrt(ms + EPS) * g) * (1.0 + scale) + shift


def _head_norm(t, bd):
    return t * lax.rsqrt(_dot_x2(t * t, bd) + EPS)


def _head_norm_t(t, bd):
    return t * lax.rsqrt(_dot_l2(bd, t * t) + EPS)


def _iota(shape, dim):
    return lax.broadcasted_iota(jnp.int32, shape, dim)


def _mod_kernel(c_ref, w_ref, b_ref, o_ref):
    c = c_ref[...]
    cs = c * jax.nn.sigmoid(c)
    ch, cl = _split2(cs)
    wh, wl = _split2(w_ref[0])
    o_ref[0] = _dot(ch, wh) + _dot(cl, wh) + _dot(ch, wl) + b_ref[0]


def _modulation(c_all, w_ada, b_ada):
    depth, d, n = w_ada.shape
    rows = c_all.shape[0]
    tn = 1024
    return pl.pallas_call(
        _mod_kernel, name="modulation",
        grid=(depth, n // tn),
        in_specs=[pl.BlockSpec((rows, d), lambda l, j: (0, 0)),
                  pl.BlockSpec((1, d, tn), lambda l, j: (l, 0, j)),
                  pl.BlockSpec((1, 1, tn), lambda l, j: (l, 0, j))],
        out_specs=pl.BlockSpec((1, rows, tn), lambda l, j: (l, 0, j)),
        out_shape=jax.ShapeDtypeStruct((depth, rows, n), F32),
        compiler_params=_params("arbitrary", "arbitrary"),
    )(c_all, w_ada, b_ada.reshape(depth, 1, n))


def _ffn_kernel(x_ref, shift_ref, scale_ref, gate_ref, g_ref, win_ref, wout_ref, o_ref, hid_ref, *, d_ff):
    x = x_ref[...]
    bb, tt, d = x.shape
    h = _adanorm(x, g_ref[...], scale_ref[...], shift_ref[...]).reshape(bb * tt, d).astype(BF16)
    for c in range(d_ff // FF_CHUNK):
        a = _dot(h, win_ref[:, c * FF_CHUNK:(c + 1) * FF_CHUNK])
        b = _dot(h, win_ref[:, d_ff + c * FF_CHUNK:d_ff + (c + 1) * FF_CHUNK])
        hid_ref[:, c * FF_CHUNK:(c + 1) * FF_CHUNK] = (a * jax.nn.sigmoid(a) * b).astype(BF16)
    o = _dot(hid_ref[...], wout_ref[...]).reshape(bb, tt, d)
    o_ref[...] = x + (MACARON * gate_ref[...]) * o


def _ffn(x, mod3, g, w_in, w_out, bb, tt):
    b, t, d = x.shape
    d_ff = w_out.shape[0]
    shift, scale, gate = mod3
    mspec = pl.BlockSpec((bb, 1, d), lambda i, j: (i, 0, 0))
    xspec = pl.BlockSpec((bb, tt, d), lambda i, j: (i, j, 0))
    return pl.pallas_call(
        functools.partial(_ffn_kernel, d_ff=d_ff), name="ffn",
        grid=(b // bb, t // tt),
        in_specs=[xspec, mspec, mspec, mspec,
                  pl.BlockSpec((1, 1, d), lambda i, j: (0, 0, 0)),
                  pl.BlockSpec((d, 2 * d_ff), lambda i, j: (0, 0)),
                  pl.BlockSpec((d_ff, d), lambda i, j: (0, 0))],
        out_specs=xspec,
        out_shape=jax.ShapeDtypeStruct(x.shape, F32),
        scratch_shapes=[pltpu.VMEM((bb * tt, d_ff), BF16)],
        compiler_params=_params("arbitrary", "arbitrary"),
    )(x, shift, scale, gate, g.reshape(1, 1, d), w_in, w_out)


def _outproj_kernel(*refs, n_in):
    x_ref, gate_ref = refs[0], refs[1]
    o_refs = refs[2:2 + n_in]
    w_refs = refs[2 + n_in:2 + 2 * n_in]
    out_ref = refs[2 + 2 * n_in]
    x = x_ref[...]
    bb, tt, d = x.shape
    y = None
    for o_ref, w_ref in zip(o_refs, w_refs):
        o = o_ref[...].astype(F32).reshape(bb * tt, o_ref.shape[-1]).astype(BF16)
        part = _dot(o, w_ref[...])
        y = part if y is None else y + part
    out_ref[...] = x + gate_ref[...] * y.reshape(bb, tt, d)


def _outproj(x, gate, outs, weights, bb, tt):
    b, t, d = x.shape
    n_in = len(outs)
    xspec = pl.BlockSpec((bb, tt, d), lambda i, j: (i, j, 0))
    in_specs = [xspec, pl.BlockSpec((bb, 1, d), lambda i, j: (i, 0, 0))]
    in_specs += [pl.BlockSpec((bb, tt, o.shape[-1]), lambda i, j: (i, j, 0)) for o in outs]
    in_specs += [pl.BlockSpec(w.shape, lambda i, j: (0, 0)) for w in weights]
    return pl.pallas_call(
        functools.partial(_outproj_kernel, n_in=n_in), name="outproj",
        grid=(b // bb, t // tt),
        in_specs=in_specs,
        out_specs=xspec,
        out_shape=jax.ShapeDtypeStruct(x.shape, F32),
        compiler_params=_params("arbitrary", "arbitrary"),
    )(x, gate, *outs, *weights)


def _proj_ab_kernel(x_ref, shift_ref, scale_ref, g_ref, w_ref, bd_ref, gq_ref, gk_ref, bf_ref,
                    qsb_ref, sbkv_ref, qfx_ref, fxkv_ref, logf_ref, logfp_ref, *, da, n_f):
    x = x_ref[...]
    bb, tt, d = x.shape
    m = bb * tt
    h = _adanorm(x, g_ref[...], scale_ref[...], shift_ref[...]).reshape(m, d).astype(BF16)
    sc = HEAD_DIM ** -0.5
    qsb_ref[...] = (_dot(h, w_ref[:, 0:da]) * sc).reshape(bb, tt, da).astype(BF16)
    sbkv_ref[...] = _dot(h, w_ref[:, da:3 * da]).reshape(bb, tt, 2 * da)
    bd = bd_ref[...]
    qf = _head_norm(_dot(h, w_ref[:, 3 * da:4 * da]), bd) * gq_ref[...]
    qfx_ref[...] = (qf * sc).reshape(bb, tt, da).astype(BF16)
    kf = _head_norm(_dot(h, w_ref[:, 4 * da:5 * da]), bd) * gk_ref[...]
    fxkv_ref[:, :, 0:da] = kf.reshape(bb, tt, da)
    fxkv_ref[:, :, da:2 * da] = _dot(h, w_ref[:, 5 * da:6 * da]).reshape(bb, tt, da)
    lf = _log_sigmoid(_dot(h, w_ref[:, 6 * da:6 * da + LANES]) + bf_ref[...])
    lf = jnp.where(_iota(lf.shape, 1) < n_f, lf, 0.0)
    logfp_ref[...] = lf.reshape(bb, tt, LANES)
    logf_ref[...] = lf[:, 0:n_f].reshape(bb, tt, n_f)


def _proj_ab(x, mod3, g, w, bd, gq, gk, bfp, n_f, bb, tt):
    b, t, d = x.shape
    da = bd.shape[0]
    shift, scale, _ = mod3
    mspec = pl.BlockSpec((bb, 1, d), lambda i, j: (i, 0, 0))
    xspec = pl.BlockSpec((bb, tt, d), lambda i, j: (i, j, 0))
    row = lambda width: pl.BlockSpec((1, width), lambda i, j: (0, 0))
    ospec = lambda width: pl.BlockSpec((bb, tt, width), lambda i, j: (i, j, 0))
    return pl.pallas_call(
        functools.partial(_proj_ab_kernel, da=da, n_f=n_f), name="proj_ab",
        grid=(b // bb, t // tt),
        in_specs=[xspec, mspec, mspec, pl.BlockSpec((1, 1, d), lambda i, j: (0, 0, 0)),
                  pl.BlockSpec(w.shape, lambda i, j: (0, 0)), pl.BlockSpec(bd.shape, lambda i, j: (0, 0)),
                  row(da), row(da), row(LANES)],
        out_specs=[ospec(da), ospec(2 * da), ospec(da), ospec(2 * da), ospec(n_f), ospec(LANES)],
        out_shape=[jax.ShapeDtypeStruct((b, t, da), BF16), jax.ShapeDtypeStruct((b, t, 2 * da), F32),
                   jax.ShapeDtypeStruct((b, t, da), BF16), jax.ShapeDtypeStruct((b, t, 2 * da), F32),
                   jax.ShapeDtypeStruct((b, t, n_f), F32), jax.ShapeDtypeStruct((b, t, LANES), F32)],
        compiler_params=_params("arbitrary", "arbitrary"),
    )(x, shift, scale, g.reshape(1, 1, d), w, bd, gq, gk, bfp)


def _proj_ab_t_kernel(x_ref, shift_ref, scale_ref, g_ref, wq_ref, wkvt_ref, wflt_ref, bd_ref, gq_ref, gk_ref, bf_ref,
                      qsb_ref, qfx_ref, sbt_ref, fxt_ref, lft_ref, *, da, n_f):
    x = x_ref[0]
    h = _adanorm(x, g_ref[0], scale_ref[0], shift_ref[0]).astype(BF16)
    sc = HEAD_DIM ** -0.5
    bd = bd_ref[...]
    qa = _dot(h, wq_ref[:, 0:da]) * sc
    qf = _head_norm(_dot(h, wq_ref[:, da:2 * da]), bd) * (gq_ref[...] * sc)
    for hh in range(da // HEAD_DIM):
        qsb_ref[0, hh] = qa[:, hh * HEAD_DIM:(hh + 1) * HEAD_DIM].astype(BF16)
        qfx_ref[0, hh] = qf[:, hh * HEAD_DIM:(hh + 1) * HEAD_DIM].astype(BF16)
    sbt_ref[0] = _dot_nt(wkvt_ref[0:2 * da, :], h)
    kft = _head_norm_t(_dot_nt(wkvt_ref[2 * da:3 * da, :], h), bd)
    fxt_ref[0, 0:da, :] = kft * gk_ref[...]
    fxt_ref[0, da:2 * da, :] = _dot_nt(wkvt_ref[3 * da:4 * da, :], h)
    lft_ref[0] = _log_sigmoid(_dot_nt(wflt_ref[...], h)[0:n_f, :] + bf_ref[...])


def _proj_ab_t(x, mod3, g, wq, wkvt, wflt, bd, gq, gk_col, bf_col, tt):
    b, t, d = x.shape
    da = bd.shape[0]
    n_h = da // HEAD_DIM
    n_f = bf_col.shape[0]
    shift, scale, _ = mod3
    mspec = pl.BlockSpec((1, 1, d), lambda i, j: (i, 0, 0))
    const = lambda a: pl.BlockSpec(a.shape, lambda i, j: (0,) * a.ndim)
    qspec = pl.BlockSpec((1, n_h, tt, HEAD_DIM), lambda i, j: (i, 0, j, 0))
    tspec = lambda rows: pl.BlockSpec((1, rows, tt), lambda i, j: (i, 0, j))
    return pl.pallas_call(
        functools.partial(_proj_ab_t_kernel, da=da, n_f=n_f), name="proj_ab_t",
        grid=(b, t // tt),
        in_specs=[pl.BlockSpec((1, tt, d), lambda i, j: (i, j, 0)), mspec, mspec,
                  pl.BlockSpec((1, 1, d), lambda i, j: (0, 0, 0)),
                  const(wq), const(wkvt), const(wflt), const(bd), const(gq), const(gk_col), const(bf_col)],
        out_specs=[qspec, qspec, tspec(2 * da), tspec(2 * da), tspec(n_f)],
        out_shape=[jax.ShapeDtypeStruct((b, n_h, t, HEAD_DIM), BF16), jax.ShapeDtypeStruct((b, n_h, t, HEAD_DIM), BF16),
                   jax.ShapeDtypeStruct((b, 2 * da, t), F32), jax.ShapeDtypeStruct((b, 2 * da, t), F32),
                   jax.ShapeDtypeStruct((b, n_f, t), F32)],
        compiler_params=_params("arbitrary", "arbitrary"),
    )(x, shift, scale, g.reshape(1, 1, d), wq, wkvt, wflt, bd, gq, gk_col, bf_col)


def _proj_nsa_kernel(x_ref, shift_ref, scale_ref, g_ref, w_ref, bd_ref, gq_ref, gs_ref, gw_ref, bg_ref,
                     q_ref, cmp_ref, slc_ref, win_ref, gates_ref, *, dq, dkv):
    x = x_ref[...]
    bb, tt, d = x.shape
    m = bb * tt
    h = _adanorm(x, g_ref[...], scale_ref[...], shift_ref[...]).reshape(m, d).astype(BF16)
    bd = bd_ref[...]
    sc = HEAD_DIM ** -0.5
    for c in range(dq // dkv):
        qc = _head_norm(_dot(h, w_ref[:, c * dkv:(c + 1) * dkv]), bd) * gq_ref[...]
        q_ref[:, :, c * dkv:(c + 1) * dkv] = (qc * sc).reshape(bb, tt, dkv).astype(BF16)
    o = dq
    cmp_ref[...] = _dot(h, w_ref[:, o:o + 2 * dkv]).reshape(bb, tt, 2 * dkv)
    o += 2 * dkv
    ks = _head_norm(_dot(h, w_ref[:, o:o + dkv]), bd) * gs_ref[...]
    slc_ref[:, :, 0:dkv] = ks.reshape(bb, tt, dkv)
    slc_ref[:, :, dkv:2 * dkv] = _dot(h, w_ref[:, o + dkv:o + 2 * dkv]).reshape(bb, tt, dkv)
    o += 2 * dkv
    kw = _head_norm(_dot(h, w_ref[:, o:o + dkv]), bd) * gw_ref[...]
    win_ref[:, :, 0:dkv] = kw.reshape(bb, tt, dkv)
    win_ref[:, :, dkv:2 * dkv] = _dot(h, w_ref[:, o + dkv:o + 2 * dkv]).reshape(bb, tt, dkv)
    o += 2 * dkv
    gates_ref[...] = jax.nn.sigmoid(_dot(h, w_ref[:, o:o + LANES]) + bg_ref[...]).reshape(bb, tt, LANES)


def _proj_nsa(x, mod3, g, w, bd, gq, gs, gw, bg, dq, bb, tt):
    b, t, d = x.shape
    dkv = bd.shape[0]
    shift, scale, _ = mod3
    mspec = pl.BlockSpec((bb, 1, d), lambda i, j: (i, 0, 0))
    xspec = pl.BlockSpec((bb, tt, d), lambda i, j: (i, j, 0))
    row = lambda width: pl.BlockSpec((1, width), lambda i, j: (0, 0))
    ospec = lambda width: pl.BlockSpec((bb, tt, width), lambda i, j: (i, j, 0))
    return pl.pallas_call(
        functools.partial(_proj_nsa_kernel, dq=dq, dkv=dkv), name="proj_nsa",
        grid=(b // bb, t // tt),
        in_specs=[xspec, mspec, mspec, pl.BlockSpec((1, 1, d), lambda i, j: (0, 0, 0)),
                  pl.BlockSpec(w.shape, lambda i, j: (0, 0)), pl.BlockSpec(bd.shape, lambda i, j: (0, 0)),
                  row(dkv), row(dkv), row(dkv), row(LANES)],
        out_specs=[ospec(dq), ospec(2 * dkv), ospec(2 * dkv), ospec(2 * dkv), ospec(LANES)],
        out_shape=[jax.ShapeDtypeStruct((b, t, dq), BF16), jax.ShapeDtypeStruct((b, t, 2 * dkv), F32),
                   jax.ShapeDtypeStruct((b, t, 2 * dkv), F32), jax.ShapeDtypeStruct((b, t, 2 * dkv), F32),
                   jax.ShapeDtypeStruct((b, t, LANES), F32)],
        compiler_params=_params("arbitrary", "arbitrary"),
    )(x, shift, scale, g.reshape(1, 1, d), w, bd, gq, gs, gw, bg)


def _proj_nsa_t_kernel(x_ref, shift_ref, scale_ref, g_ref, wq_ref, wcmp_ref, wkvt_ref, wg_ref, bd_ref, gq_ref, gs_ref,
                       gw_ref, bg_ref, q_ref, cmp_ref, slct_ref, wint_ref, gates_ref, *, dq, dkv):
    x = x_ref[0]
    h = _adanorm(x, g_ref[0], scale_ref[0], shift_ref[0]).astype(BF16)
    bd = bd_ref[...]
    sc = HEAD_DIM ** -0.5
    per = dkv // HEAD_DIM
    for c in range(dq // dkv):
        qc = _head_norm(_dot(h, wq_ref[:, c * dkv:(c + 1) * dkv]), bd) * (gq_ref[...] * sc)
        for hh in range(per):
            q_ref[0, c * per + hh] = qc[:, hh * HEAD_DIM:(hh + 1) * HEAD_DIM].astype(BF16)
    cmp_ref[0] = _dot(h, wcmp_ref[...])
    slct_ref[0, 0:dkv, :] = _head_norm_t(_dot_nt(wkvt_ref[0:dkv, :], h), bd) * gs_ref[...]
    slct_ref[0, dkv:2 * dkv, :] = _dot_nt(wkvt_ref[dkv:2 * dkv, :], h)
    wint_ref[0, 0:dkv, :] = _head_norm_t(_dot_nt(wkvt_ref[2 * dkv:3 * dkv, :], h), bd) * gw_ref[...]
    wint_ref[0, dkv:2 * dkv, :] = _dot_nt(wkvt_ref[3 * dkv:4 * dkv, :], h)
    gates_ref[0] = jax.nn.sigmoid(_dot(h, wg_ref[...]) + bg_ref[...])


def _proj_nsa_t(x, mod3, g, wq, wcmp, wkvt, wg, bd, gq, gs_col, gw_col, bg, tt):
    b, t, d = x.shape
    dkv = bd.shape[0]
    dq = wq.shape[1]
    n_h = dq // HEAD_DIM
    shift, scale, _ = mod3
    mspec = pl.BlockSpec((1, 1, d), lambda i, j: (i, 0, 0))
    const = lambda a: pl.BlockSpec(a.shape, lambda i, j: (0,) * a.ndim)
    tspec = lambda rows: pl.BlockSpec((1, rows, tt), lambda i, j: (i, 0, j))
    rspec = lambda width: pl.BlockSpec((1, tt, width), lambda i, j: (i, j, 0))
    return pl.pallas_call(
        functools.partial(_proj_nsa_t_kernel, dq=dq, dkv=dkv), name="proj_nsa_t",
        grid=(b, t // tt),
        in_specs=[rspec(d), mspec, mspec, pl.BlockSpec((1, 1, d), lambda i, j: (0, 0, 0)),
                  const(wq), const(wcmp), const(wkvt), const(wg), const(bd), const(gq), const(gs_col), const(gw_col),
                  const(bg)],
        out_specs=[pl.BlockSpec((1, n_h, tt, HEAD_DIM), lambda i, j: (i, 0, j, 0)), rspec(2 * dkv),
                   tspec(2 * dkv), tspec(2 * dkv), rspec(LANES)],
        out_shape=[jax.ShapeDtypeStruct((b, n_h, t, HEAD_DIM), BF16), jax.ShapeDtypeStruct((b, t, 2 * dkv), F32),
                   jax.ShapeDtypeStruct((b, 2 * dkv, t), F32), jax.ShapeDtypeStruct((b, 2 * dkv, t), F32),
                   jax.ShapeDtypeStruct((b, t, LANES), F32)],
        compiler_params=_params("arbitrary", "arbitrary"),
    )(x, shift, scale, g.reshape(1, 1, d), wq, wcmp, wkvt, wg, bd, gq, gs_col, gw_col, bg)


def _block_diag_rows(q, n_heads):
    q = q.astype(F32)
    head = _iota(q.shape, 1) // HEAD_DIM
    return jnp.concatenate([jnp.where(head == h, q, 0.0) for h in range(n_heads)], axis=0).astype(BF16)


def _gather_diag(acc, tq, n_heads):
    head = _iota((tq, acc.shape[1]), 1) // HEAD_DIM
    out = jnp.zeros((tq, acc.shape[1]), F32)
    for h in range(n_heads):
        out = out + jnp.where(head == h, acc[h * tq:(h + 1) * tq, :], 0.0)
    return out


def _row_query(shape, tq):
    return _iota(shape, 0) % tq


def _running_sum_matrix(tk, prefix):
    j = _iota((tk, 2 * tk), 0)
    s = _iota((tk, 2 * tk), 1)
    tri = (j <= s) if prefix else (j > s)
    return jnp.where((s >= tk) | tri, 1.0, 0.0).astype(BF16)


def _later_matrix(tk):
    return jnp.where(_iota((tk, tk), 0) > _iota((tk, tk), 1), 1.0, 0.0).astype(BF16)


def _across_lane_tiles(x, op):
    out = x[:, 0:LANES]
    for c in range(1, x.shape[1] // LANES):
        out = op(out, x[:, c * LANES:(c + 1) * LANES])
    return out


def _row_max(x):
    return jnp.max(_across_lane_tiles(x, jnp.maximum), axis=1, keepdims=True)


def _row_sum(x):
    return jnp.sum(_across_lane_tiles(x, jnp.add), axis=1, keepdims=True)


def _log_keep(z, mask):
    lk = -(jnp.maximum(z, 0.0) + jnp.log(1.0 + jnp.exp(-jnp.abs(z))))
    return lk if mask is None else jnp.where(mask, lk, 0.0)


def _sb_weights(z, um, carry, mask):
    lk = _log_keep(z, mask)
    blk = um.shape[0]
    nb = z.shape[1] // blk
    rest = [None] * nb
    for b in reversed(range(nb)):
        lkb = lk[:, b * blk:(b + 1) * blk]
        rest[b] = _dot_x2(lkb, um) + carry
        carry = carry + _row_sum(lkb)
    w = jnp.exp(z + lk + (rest[0] if nb == 1 else jnp.concatenate(rest, axis=1)))
    if mask is not None:
        w = jnp.where(mask, w, 0.0)
    return w.astype(BF16), carry


def _softmax_first(s, pv, m_ref, l_ref, acc_ref):
    m = _row_max(s)
    p = jnp.exp(s - m)
    l_ref[...] = _row_sum(p)
    acc_ref[...] = pv(p.astype(BF16))
    m_ref[...] = m


def _softmax_step(s, pv, m_ref, l_ref, acc_ref, mask=None):
    m_prev = m_ref[...]
    m_new = jnp.maximum(m_prev, _row_max(s))
    p = jnp.exp(s - m_new)
    if mask is not None:
        p = jnp.where(mask, p, 0.0)
    alpha = jnp.exp(m_prev - m_new)
    l_ref[...] = alpha * l_ref[...] + _row_sum(p)
    acc_ref[...] = alpha * acc_ref[...] + pv(p.astype(BF16))
    m_ref[...] = m_new


def _softmax_single(s, pv):
    p = jnp.exp(s - _row_max(s))
    return pv(p.astype(BF16)) / _row_sum(p)


def _softmax_init(m_ref, l_ref, acc_ref):
    m_ref[...] = jnp.full(m_ref.shape, NEG, F32)
    l_ref[...] = jnp.zeros(l_ref.shape, F32)
    acc_ref[...] = jnp.zeros(acc_ref.shape, F32)


def _softmax_out(l_ref, acc_ref):
    return acc_ref[...] / jnp.maximum(l_ref[...], 1e-30)


def _softmax_scratch(r, width):
    return [pltpu.VMEM((r, 1), F32), pltpu.VMEM((r, 1), F32), pltpu.VMEM((r, width), F32)]


def _rank_select(imp, t, ns):
    blk = _iota(imp.shape, 1)
    cur = t // SLC_BLOCK
    valid = (blk * SLC_BLOCK <= t) & (blk < ns)
    forced = (blk == 0) | (blk == cur) | (blk == cur - 1)
    score = jnp.where(valid, jnp.where(forced, FORCE_SCORE, imp), -jnp.inf)
    cnt = jnp.zeros(imp.shape, F32)
    for j in range(ns):
        col = score[:, j:j + 1]
        cnt = cnt + jnp.where(col > score, 1.0, 0.0) + jnp.where(col == score, jnp.where(blk > j, 1.0, 0.0), 0.0)
    return jnp.where(valid & (cnt < N_SELECT), 1.0, 0.0)


def _expand_blocks(sel, n_keys):
    nsp = sel.shape[1]
    blk = _iota((nsp, n_keys), 0)
    key = _iota((nsp, n_keys), 1)
    return _dot(sel, jnp.where(key // SLC_BLOCK == blk, 1.0, 0.0).astype(BF16))


def _sb_prompt_kernel(q_ref, kt_ref, vt_ref, o_ref, ktb, vtb, acc, car, *, tq, nh):
    qi = pl.program_id(2)

    @pl.when(qi == 0)
    def _():
        ktb[...] = kt_ref[0].astype(BF16)
        vtb[...] = vt_ref[0].astype(BF16)

    um = _later_matrix(tq)
    q0 = pl.multiple_of(qi * tq, tq)
    mask = _iota((tq, tq), 1) < _iota((tq, tq), 0)

    def tile(k0, first):
        for h in range(nh):
            rows = slice(h * HEAD_DIM, (h + 1) * HEAD_DIM)
            z = _dot(q_ref[0, h], ktb[rows, pl.ds(k0, tq)])
            w, c = _sb_weights(z, um, jnp.zeros((tq, 1), F32) if first else car[h], mask if first else None)
            pv = _dot_nt(w, vtb[rows, pl.ds(k0, tq)])
            acc[h] = pv if first else acc[h] + pv
            car[h] = c

    tile(q0, True)

    def body(it, _):
        tile(pl.multiple_of((qi - 1 - it) * tq, tq), False)
        return 0

    lax.fori_loop(0, qi, body, 0)
    o_ref[0] = jnp.concatenate([acc[h] for h in range(nh)], axis=1).astype(BF16)


def _sb_prompt(q, kvt, tq, nh):
    b, n_h, s, _ = q.shape
    c = nh * HEAD_DIM
    ng = n_h // nh
    return pl.pallas_call(
        functools.partial(_sb_prompt_kernel, tq=tq, nh=nh), name="sb_prompt",
        grid=(b, ng, s // tq),
        in_specs=[pl.BlockSpec((1, nh, tq, HEAD_DIM), lambda i, g, j: (i, g, j, 0)),
                  pl.BlockSpec((1, c, s), lambda i, g, j: (i, g, 0)),
                  pl.BlockSpec((1, c, s), lambda i, g, j: (i, ng + g, 0))],
        out_specs=pl.BlockSpec((1, tq, c), lambda i, g, j: (i, j, g)),
        out_shape=jax.ShapeDtypeStruct((b, s, n_h * HEAD_DIM), BF16),
        scratch_shapes=[pltpu.VMEM((c, s), BF16), pltpu.VMEM((c, s), BF16),
                        pltpu.VMEM((nh, tq, HEAD_DIM), F32), pltpu.VMEM((nh, tq, 1), F32)],
        compiler_params=_params("arbitrary", "arbitrary", "arbitrary"),
    )(q, kvt, kvt)


def _fox_prompt_kernel(q_ref, kt_ref, vt_ref, lft_ref, o_ref, ktb, vtb, ncum, m_ref, l_ref, acc, *, tq, nh):
    g = pl.program_id(1)
    qi = pl.program_id(2)
    s_len = kt_ref.shape[2]

    @pl.when(qi == 0)
    def _():
        ktb[...] = kt_ref[0].astype(BF16)
        vtb[...] = vt_ref[0].astype(BF16)
        um = _running_sum_matrix(tq, True)
        run = jnp.zeros((lft_ref.shape[1], tq), F32)
        for ch in range(s_len // tq):
            sl = slice(ch * tq, (ch + 1) * tq)
            rs = _dot_x3(lft_ref[0, :, sl], um)
            ncum[:, sl] = -(rs[:, :tq] + run)
            run = run + rs[:, tq:]

    q0 = pl.multiple_of(qi * tq, tq)
    mask = _iota((tq, tq), 1) <= _iota((tq, tq), 0)

    def tile(k0, first):
        for h in range(nh):
            rows = slice(h * HEAD_DIM, (h + 1) * HEAD_DIM)
            s = _dot(q_ref[0, h], ktb[rows, pl.ds(k0, tq)]) + ncum[pl.ds(g * nh + h, 1), pl.ds(k0, tq)]
            pv = lambda p, rows=rows: _dot_nt(p, vtb[rows, pl.ds(k0, tq)])
            if first:
                _softmax_first(jnp.where(mask, s, NEG), pv, m_ref.at[h], l_ref.at[h], acc.at[h])
            else:
                _softmax_step(s, pv, m_ref.at[h], l_ref.at[h], acc.at[h])

    tile(q0, True)

    def body(it, _):
        tile(pl.multiple_of(it * tq, tq), False)
        return 0

    lax.fori_loop(0, qi, body, 0)
    o_ref[0] = jnp.concatenate([_softmax_out(l_ref.at[h], acc.at[h]) for h in range(nh)], axis=1).astype(BF16)


def _fox_prompt(q, kvt, lft, tq, nh):
    b, n_h, s, _ = q.shape
    c = nh * HEAD_DIM
    ng = n_h // nh
    n_f = lft.shape[1]
    return pl.pallas_call(
        functools.partial(_fox_prompt_kernel, tq=tq, nh=nh), name="fox_prompt",
        grid=(b, ng, s // tq),
        in_specs=[pl.BlockSpec((1, nh, tq, HEAD_DIM), lambda i, g, j: (i, g, j, 0)),
                  pl.BlockSpec((1, c, s), lambda i, g, j: (i, g, 0)),
                  pl.BlockSpec((1, c, s), lambda i, g, j: (i, ng + g, 0)),
                  pl.BlockSpec((1, n_f, s), lambda i, g, j: (i, 0, 0))],
        out_specs=pl.BlockSpec((1, tq, c), lambda i, g, j: (i, j, g)),
        out_shape=jax.ShapeDtypeStruct((b, s, n_h * HEAD_DIM), BF16),
        scratch_shapes=[pltpu.VMEM((c, s), BF16), pltpu.VMEM((c, s), BF16), pltpu.VMEM((n_f, s), F32),
                        pltpu.VMEM((nh, tq, 1), F32), pltpu.VMEM((nh, tq, 1), F32), pltpu.VMEM((nh, tq, HEAD_DIM), F32)],
        compiler_params=_params("arbitrary", "arbitrary", "arbitrary"),
    )(q, kvt, kvt, lft)


def _page_specs(n_pages, rows, npg):
    return [pl.BlockSpec((1, rows, PAGE), functools.partial(
        lambda i, s, pt, off: (pt[i, n_pages - 1 - s * npg - off], 0, 0), off=off)) for off in range(npg)]


def _sb_decode_kernel(pt_ref, q_ref, new_ref, *rest, tq, hg, npg):
    page_refs = rest[:npg]
    o_ref, qbd_ref, newp, acc, car = rest[npg:]
    st = pl.program_id(1)
    da = hg * HEAD_DIM
    r = hg * tq

    @pl.when(st == 0)
    def _():
        qbd_ref[...] = _block_diag_rows(q_ref[0], hg)
        newp[...] = jnp.zeros(newp.shape, F32)
        newp[0:tq, :] = new_ref[0]
        mask = _iota((r, PAGE), 1) < _row_query((r, PAGE), tq)
        w, c = _sb_weights(_dot_nt(qbd_ref[...], newp[:, 0:da].astype(BF16)), _later_matrix(PAGE),
                           jnp.zeros((r, 1), F32), mask)
        acc[...] = _dot(w, newp[:, da:2 * da].astype(BF16))
        car[...] = c

    kt = jnp.concatenate([pg[0, 0:da, :] for pg in page_refs[::-1]], axis=1).astype(BF16)
    vt = jnp.concatenate([pg[0, da:2 * da, :] for pg in page_refs[::-1]], axis=1).astype(BF16)
    w, c = _sb_weights(_dot(qbd_ref[...], kt), _later_matrix(2 * PAGE), car[...], None)
    acc[...] += _dot_nt(w, vt)
    car[...] = c

    @pl.when(st == pl.num_programs(1) - 1)
    def _():
        o_ref[0] = _gather_diag(acc[...], tq, hg).astype(BF16)


def _sb_decode(q, kv_new, cache_t, page_table, npg):
    b, tq, da = q.shape
    hg = da // HEAD_DIM
    n_pages = page_table.shape[1]
    r = hg * tq
    grid_spec = pltpu.PrefetchScalarGridSpec(
        num_scalar_prefetch=1,
        grid=(b, n_pages // npg),
        in_specs=[pl.BlockSpec((1, tq, da), lambda i, s, pt: (i, 0, 0)),
                  pl.BlockSpec((1, tq, 2 * da), lambda i, s, pt: (i, 0, 0))] + _page_specs(n_pages, 2 * da, npg),
        out_specs=pl.BlockSpec((1, tq, da), lambda i, s, pt: (i, 0, 0)),
        scratch_shapes=[pltpu.VMEM((r, da), BF16), pltpu.VMEM((PAGE, 2 * da), F32),
                        pltpu.VMEM((r, da), F32), pltpu.VMEM((r, 1), F32)],
    )
    return pl.pallas_call(
        functools.partial(_sb_decode_kernel, tq=tq, hg=hg, npg=npg), name="sb_decode",
        grid_spec=grid_spec,
        out_shape=jax.ShapeDtypeStruct((b, tq, da), BF16),
        compiler_params=_params("arbitrary", "arbitrary"),
    )(page_table, q, kv_new, *([cache_t] * npg))


def _fox_decode_kernel(pt_ref, q_ref, new_ref, lfnew_ref, *rest, tq, hg, npg):
    page_refs = rest[:npg]
    lf_refs = rest[npg:2 * npg]
    o_ref, qbd_ref, newp, lfp, run_ref, m_ref, l_ref, acc = rest[2 * npg:]
    st = pl.program_id(1)
    da = hg * HEAD_DIM
    r = hg * tq
    um = _running_sum_matrix(PAGE, False)

    def key_bias(lfts):
        rs = [_dot_x3(lft, um) for lft in lfts]
        run = run_ref[...]
        suf = [None] * len(lfts)
        for b in reversed(range(len(lfts))):
            suf[b] = rs[b][:, :PAGE] + run
            run = run + rs[b][:, PAGE:]
        run_ref[...] = run
        suf = suf[0] if len(suf) == 1 else jnp.concatenate(suf, axis=1)
        return jnp.concatenate([jnp.broadcast_to(suf[h:h + 1, :], (tq, suf.shape[1])) for h in range(hg)], axis=0)

    @pl.when(st == 0)
    def _():
        qbd_ref[...] = _block_diag_rows(q_ref[0], hg)
        newp[...] = jnp.zeros(newp.shape, F32)
        newp[0:tq, :] = new_ref[0]
        lfp[...] = jnp.zeros(lfp.shape, F32)
        lfp[0:tq, :] = lfnew_ref[0]
        run_ref[...] = jnp.zeros(run_ref.shape, F32)
        mask = _iota((r, PAGE), 1) <= _row_query((r, PAGE), tq)
        s = _dot_nt(qbd_ref[...], newp[:, 0:da].astype(BF16)) + key_bias([lfp[...].T[0:hg, :]])
        _softmax_first(jnp.where(mask, s, NEG), lambda p: _dot(p, newp[:, da:2 * da].astype(BF16)), m_ref, l_ref, acc)

    kt = jnp.concatenate([pg[0, 0:da, :] for pg in page_refs[::-1]], axis=1).astype(BF16)
    vt = jnp.concatenate([pg[0, da:2 * da, :] for pg in page_refs[::-1]], axis=1).astype(BF16)
    s = _dot(qbd_ref[...], kt) + key_bias([lf[0] for lf in lf_refs[::-1]])
    _softmax_step(s, lambda p: _dot_nt(p, vt), m_ref, l_ref, acc)

    @pl.when(st == pl.num_programs(1) - 1)
    def _():
        o_ref[0] = _gather_diag(_softmax_out(l_ref, acc), tq, hg).astype(BF16)


def _fox_decode(q, kv_new, lf_new, cache_t, cache_lf_t, page_table, npg):
    b, tq, da = q.shape
    hg = da // HEAD_DIM
    n_pages = page_table.shape[1]
    r = hg * tq
    assert cache_lf_t.shape[1] == hg
    grid_spec = pltpu.PrefetchScalarGridSpec(
        num_scalar_prefetch=1,
        grid=(b, n_pages // npg),
        in_specs=[pl.BlockSpec((1, tq, da), lambda i, s, pt: (i, 0, 0)),
                  pl.BlockSpec((1, tq, 2 * da), lambda i, s, pt: (i, 0, 0)),
                  pl.BlockSpec((1, tq, LANES), lambda i, s, pt: (i, 0, 0))]
        + _page_specs(n_pages, 2 * da, npg) + _page_specs(n_pages, hg, npg),
        out_specs=pl.BlockSpec((1, tq, da), lambda i, s, pt: (i, 0, 0)),
        scratch_shapes=[pltpu.VMEM((r, da), BF16), pltpu.VMEM((PAGE, 2 * da), F32),
                        pltpu.VMEM((PAGE, LANES), F32), pltpu.VMEM((hg, PAGE), F32)] + _softmax_scratch(r, da),
    )
    return pl.pallas_call(
        functools.partial(_fox_decode_kernel, tq=tq, hg=hg, npg=npg), name="fox_decode",
        grid_spec=grid_spec,
        out_shape=jax.ShapeDtypeStruct((b, tq, da), BF16),
        compiler_params=_params("arbitrary", "arbitrary"),
    )(page_table, q, kv_new, lf_new, *([cache_t] * npg), *([cache_lf_t] * npg))


def _compress(xb, bd_ref, pe_ref, n_chunk, dkv):
    first = [jnp.zeros((n_chunk, dkv), F32) for _ in range(2)]
    second = [jnp.zeros((n_chunk, dkv), F32) for _ in range(2)]
    for j in range(D_CMP):
        xj = jnp.concatenate([xb[lb, pl.ds(j, n_chunk, stride=D_CMP), :] for lb in range(xb.shape[0])], axis=1)
        for kv in range(2):
            xx = xj[:, kv * dkv:(kv + 1) * dkv]
            first[kv] += _dot((xx + pe_ref[kv, j:j + 1, :]).astype(BF16), bd_ref[kv, j])
            second[kv] += _dot((xx + pe_ref[kv, D_CMP + j:D_CMP + j + 1, :]).astype(BF16), bd_ref[kv, D_CMP + j])
    return [first[kv] + pltpu.roll(second[kv], n_chunk - 1, axis=0) for kv in range(2)]


def _compress_prompt_kernel(x_ref, bd_ref, pe_ref, bdn_ref, gk_ref, kct_ref, vct_ref, xb):
    n_chunk = x_ref.shape[1] // D_CMP
    dkv = bdn_ref.shape[0]
    for lb in range(xb.shape[0]):
        xb[lb] = x_ref[0, :, lb * LANES:(lb + 1) * LANES]
    kc, vc = _compress(xb, bd_ref, pe_ref, n_chunk, dkv)
    kct_ref[0] = (_head_norm(kc, bdn_ref[...]) * gk_ref[...]).T.astype(BF16)
    vct_ref[0] = vc.T.astype(BF16)


def _compress_prompt(cmp_kv, bd, pe, bdn, gk):
    b, s, w = cmp_kv.shape
    dkv = w // 2
    n_chunk = s // D_CMP
    const = lambda a: pl.BlockSpec(a.shape, lambda i: (0,) * a.ndim)
    ospec = pl.BlockSpec((1, dkv, n_chunk), lambda i: (i, 0, 0))
    return pl.pallas_call(
        _compress_prompt_kernel, name="compress_prompt",
        grid=(b,),
        in_specs=[pl.BlockSpec((1, s, w), lambda i: (i, 0, 0)), const(bd), const(pe), const(bdn), const(gk)],
        out_specs=[ospec, ospec],
        out_shape=[jax.ShapeDtypeStruct((b, dkv, n_chunk), BF16)] * 2,
        scratch_shapes=[pltpu.VMEM((w // LANES, s, LANES), F32)],
        compiler_params=_params("arbitrary"),
    )(cmp_kv, bd, pe, bdn, gk)


def _nsa_prompt_kernel(q_ref, kst_ref, vst_ref, kwt_ref, vwt_ref, kct_ref, vct_ref, gates_ref, band_ref, near_ref,
                       wadd_ref, ov_ref, o_ref, ksb, vsb, kwb, vwb, madd, m_ref, l_ref, acc, ocmp, owin, *, tq, nz, ns):
    g = pl.program_id(1)
    qi = pl.program_id(2)
    s_len = kst_ref.shape[2]
    ncp = kct_ref.shape[2]
    wslab = WINDOW + tq

    @pl.when(qi == 0)
    def _():
        ksb[...] = kst_ref[0].astype(BF16)
        vsb[...] = vst_ref[0].astype(BF16)
        kwb[:, 0:WINDOW] = jnp.zeros((HEAD_DIM, WINDOW), BF16)
        vwb[:, 0:WINDOW] = jnp.zeros((HEAD_DIM, WINDOW), BF16)
        kwb[:, WINDOW:] = kwt_ref[0].astype(BF16)
        vwb[:, WINDOW:] = vwt_ref[0].astype(BF16)

    q0 = pl.multiple_of(qi * tq, tq)
    zrows = lambda z: slice(z * tq, (z + 1) * tq)

    shift = (qi * (tq // D_CMP) + ncp - BAND_BACK) % ncp
    t_c = q0 + _iota((tq, ncp), 0)
    c_c = _iota((tq, ncp), 1)
    cmask = (t_c >= c_c * D_CMP + (L_CMP - 1)) & (c_c < ncp - 1)
    pz = None
    for z in range(nz):
        s = jnp.where(cmask, _dot(q_ref[0, z], kct_ref[0]) + pltpu.roll(band_ref[z], shift, axis=1), NEG)
        p = jnp.where(cmask, jnp.exp(s - _row_max(s)), 0.0)
        p = p / jnp.maximum(_row_sum(p), 1e-30)
        ocmp[zrows(z), :] = _dot_nt(p.astype(BF16), vct_ref[0])
        pz = p if pz is None else pz + p
    imp = _dot_x2(pz, ov_ref[...])
    sel = _rank_select(imp, q0 + _iota(imp.shape, 0), ns)
    madd[...] = (_expand_blocks(sel.astype(BF16), s_len) - 1.0) * (-NEG)

    def slc_tile(k0, tk, extra, first):
        ma = madd[:, pl.ds(k0, tk)]
        for z in range(nz):
            s = _dot(q_ref[0, z], ksb[:, pl.ds(k0, tk)]) + (ma if extra is None else ma + extra(z))
            (_softmax_first if first else _softmax_step)(
                s, lambda p: _dot_nt(p, vsb[:, pl.ds(k0, tk)]), m_ref.at[zrows(z)], l_ref.at[zrows(z)], acc.at[zrows(z)])

    causal = jnp.where(_iota((tq, tq), 1) <= _iota((tq, tq), 0), 0.0, NEG)

    @pl.when(qi == 0)
    def _():
        slc_tile(0, tq, lambda z: near_ref[z, :, tq:2 * tq] + causal, True)

    @pl.when(qi >= 1)
    def _():
        edge = jnp.concatenate([jnp.zeros((tq, tq), F32), causal], axis=1)
        slc_tile(pl.multiple_of(q0 - tq, tq), 2 * tq, lambda z: near_ref[z] + edge, True)

    n_far = jnp.maximum(qi - 1, 0)
    per_far = FAR_TILE // tq

    def far_body(it, _):
        slc_tile(pl.multiple_of(it * FAR_TILE, FAR_TILE), FAR_TILE, None, False)
        return 0

    lax.fori_loop(0, n_far // per_far, far_body, 0)

    def rem_body(it, _):
        slc_tile(pl.multiple_of(((n_far // per_far) * per_far + it) * tq, tq), tq, None, False)
        return 0

    lax.fori_loop(0, n_far % per_far, rem_body, 0)

    kslab = kwb[:, pl.ds(q0, wslab)]
    vslab = vwb[:, pl.ds(q0, wslab)]
    wadd = wadd_ref[...] + jnp.where(_iota((tq, wslab), 1) >= WINDOW - q0, 0.0, NEG)
    for z in range(nz):
        s = _dot(q_ref[0, z], kslab) + wadd
        s = jnp.concatenate([s[:, :wslab - 2 * tq], s[:, wslab - 2 * tq:] + near_ref[z]], axis=1)
        owin[zrows(z), :] = _softmax_single(s, lambda p: _dot_nt(p, vslab))

    n_heads = nz * pl.num_programs(1)
    src = _iota((LANES, LANES), 0)
    dst = _iota((LANES, LANES), 1)
    pick = jnp.where((dst < 3 * nz) & (src == (dst // nz) * n_heads + g * nz + dst % nz), 1.0, 0.0).astype(BF16)
    gsel = _dot_x3(gates_ref[0], pick)
    outs = []
    for z in range(nz):
        gate = lambda br: gsel[:, br * nz + z:br * nz + z + 1]
        o_slc = acc[zrows(z), :] / jnp.maximum(l_ref[zrows(z), :], 1e-30)
        outs.append(gate(0) * ocmp[zrows(z), :] + gate(1) * o_slc + gate(2) * owin[zrows(z), :])
    o_ref[0] = jnp.concatenate(outs, axis=1).astype(BF16)


def _window_mask(tq):
    i = np.arange(tq)[:, None]
    c = np.arange(WINDOW + tq)[None, :]
    return jnp.asarray(np.where((c > i) & (c <= i + WINDOW), 0.0, NEG).astype(np.float32))


def _nsa_prompt(q, slct, wint, kct, vct, gates, band, near, tq, n_kv):
    b, n_heads, s, _ = q.shape
    nz = n_heads // n_kv
    r = nz * tq
    ncp = kct.shape[2]
    ns = -(-s // SLC_BLOCK)
    nsp = -(-ns // LANES) * LANES
    ov = _overlap_matrix(ncp, nsp)
    wadd = _window_mask(tq)
    kv_spec = lambda off: pl.BlockSpec((1, HEAD_DIM, s), lambda i, g, j: (i, off + g, 0))
    c_spec = pl.BlockSpec((1, HEAD_DIM, ncp), lambda i, g, j: (i, g, 0))
    tab_spec = lambda a: pl.BlockSpec((nz,) + a.shape[1:], lambda i, g, j: (g, 0, 0))
    return pl.pallas_call(
        functools.partial(_nsa_prompt_kernel, tq=tq, nz=nz, ns=ns), name="nsa_prompt",
        grid=(b, n_kv, s // tq),
        in_specs=[pl.BlockSpec((1, nz, tq, HEAD_DIM), lambda i, g, j: (i, g, j, 0)),
                  kv_spec(0), kv_spec(n_kv), kv_spec(0), kv_spec(n_kv), c_spec, c_spec,
                  pl.BlockSpec((1, tq, LANES), lambda i, g, j: (i, j, 0)),
                  tab_spec(band), tab_spec(near),
                  pl.BlockSpec(wadd.shape, lambda i, g, j: (0, 0)),
                  pl.BlockSpec(ov.shape, lambda i, g, j: (0, 0))],
        out_specs=pl.BlockSpec((1, tq, nz * HEAD_DIM), lambda i, g, j: (i, j, g)),
        out_shape=jax.ShapeDtypeStruct((b, s, n_heads * HEAD_DIM), BF16),
        scratch_shapes=[pltpu.VMEM((HEAD_DIM, s), BF16)] * 2 + [pltpu.VMEM((HEAD_DIM, WINDOW + s), BF16)] * 2
        + [pltpu.VMEM((tq, s), F32)] + _softmax_scratch(r, HEAD_DIM)
        + [pltpu.VMEM((r, HEAD_DIM), F32)] * 2,
        compiler_params=_params("arbitrary", "arbitrary", "arbitrary"),
    )(q, slct, slct, wint, wint, kct, vct, gates, band, near, wadd, ov)


def _cmp_select(q, kcmp, vcmp, bias, ov, q0, tq, nc, ns, n_kv, n_rep):
    ncp = kcmp.shape[0]
    qbd = _nsa_qbd(q, n_kv, n_rep)
    r = qbd.shape[0]
    t = q0 + _row_query((r, ncp), tq)
    cidx = _iota((r, ncp), 1)
    mask = (t - (cidx * D_CMP + L_CMP - 1) >= 0) & (cidx < nc)
    s = jnp.where(mask, _dot_nt(qbd, kcmp) + bias, NEG)
    p = jnp.where(mask, jnp.exp(s - jnp.max(s, axis=1, keepdims=True)), 0.0)
    p = p / jnp.maximum(jnp.sum(p, axis=1, keepdims=True), 1e-30)
    o_cmp = _nsa_gather(_dot(p.astype(BF16), vcmp), tq, n_kv, n_rep)
    rz = n_kv * tq
    pz = p[0:rz]
    for z in range(1, n_rep):
        pz = pz + p[z * rz:(z + 1) * rz]
    imp = _dot_x2(pz, ov)
    return o_cmp, _rank_select(imp, q0 + _row_query(imp.shape, tq), ns)


def _overlap_matrix(ncp, nsp):
    ci = np.arange(ncp)[:, None]
    bj = np.arange(nsp)[None, :]
    ov = (ci * D_CMP <= bj * SLC_BLOCK + SLC_BLOCK - 1) & (ci * D_CMP + L_CMP - 1 >= bj * SLC_BLOCK)
    return jnp.asarray(ov.astype(np.float32), BF16)


def _cmp_decode_kernel(pt_ref, q_ref, bd_ref, pe_ref, bdn_ref, gk_ref, bias_ref, ov_ref, *rest,
                       tq, n_kv, n_rep, ns, npg, past):
    page_refs = rest[:npg]
    ocmp_ref, sel_ref, xbuf = rest[npg:]
    st = pl.program_id(1)
    n_steps = pl.num_programs(1)
    dkv = n_kv * HEAD_DIM
    n_chunk = past // D_CMP
    for off, pg in enumerate(page_refs):
        pos = (n_steps - 1 - st) * npg + (npg - 1 - off)
        rows = pg[0].T
        for lb in range(xbuf.shape[0]):
            xbuf[lb, pl.ds(pl.multiple_of(pos * PAGE, PAGE), PAGE), :] = rows[:, lb * LANES:(lb + 1) * LANES]

    @pl.when(st == n_steps - 1)
    def _():
        kc, vc = _compress(xbuf, bd_ref, pe_ref, n_chunk, dkv)
        kcb = (_head_norm(kc, bdn_ref[...]) * gk_ref[...]).astype(BF16)
        bias = bias_ref[...].reshape(n_kv * n_rep * tq, n_chunk)
        o_cmp, sel = _cmp_select(q_ref[0], kcb, vc.astype(BF16), bias, ov_ref[...], past, tq,
                                 n_chunk - 1, ns, n_kv, n_rep)
        ocmp_ref[0] = o_cmp
        sel_ref[0] = sel.astype(BF16)


def _cmp_decode(q, cache_t, page_table, bd, pe, bdn, gk, bias, n_kv, npg):
    b, tq, dq = q.shape
    dkv = n_kv * HEAD_DIM
    n_rep = dq // dkv
    n_pages = page_table.shape[1]
    past = n_pages * PAGE
    n_chunk = (past + tq) // D_CMP
    assert n_chunk == past // D_CMP
    ns = -(-(past + tq) // SLC_BLOCK)
    nsp = -(-ns // LANES) * LANES
    ov = _overlap_matrix(n_chunk, nsp)
    const = lambda a: pl.BlockSpec(a.shape, lambda i, s, pt: (0,) * a.ndim)
    grid_spec = pltpu.PrefetchScalarGridSpec(
        num_scalar_prefetch=1,
        grid=(b, n_pages // npg),
        in_specs=[pl.BlockSpec((1, tq, dq), lambda i, s, pt: (i, 0, 0)),
                  const(bd), const(pe), const(bdn), const(gk), const(bias), const(ov)]
        + _page_specs(n_pages, 2 * dkv, npg),
        out_specs=[pl.BlockSpec((1, tq, dq), lambda i, s, pt: (i, 0, 0)),
                   pl.BlockSpec((1, n_kv * tq, nsp), lambda i, s, pt: (i, 0, 0))],
        scratch_shapes=[pltpu.VMEM((2 * dkv // LANES, past, LANES), F32)],
    )
    return pl.pallas_call(
        functools.partial(_cmp_decode_kernel, tq=tq, n_kv=n_kv, n_rep=n_rep, ns=ns, npg=npg, past=past),
        name="cmp_decode",
        grid_spec=grid_spec,
        out_shape=[jax.ShapeDtypeStruct((b, tq, dq), F32), jax.ShapeDtypeStruct((b, n_kv * tq, nsp), BF16)],
        compiler_params=_params("arbitrary", "arbitrary"),
    )(page_table, q, bd, pe, bdn, gk, bias, ov, *([cache_t] * npg))


def _nsa_qbd(q, n_kv, n_rep):
    dkv = n_kv * HEAD_DIM
    return jnp.concatenate([_block_diag_rows(q[:, z * dkv:(z + 1) * dkv], n_kv) for z in range(n_rep)], axis=0)


def _nsa_gather(o_full, tq, n_kv, n_rep):
    rz = n_kv * tq
    return jnp.concatenate([_gather_diag(o_full[z * rz:(z + 1) * rz], tq, n_kv) for z in range(n_rep)], axis=1)


def _merge(gates, ege_ref, o_cmp, o_slc, o_win):
    g = [_dot_x2(gates, ege_ref[br]) for br in range(3)]
    return g[0] * o_cmp + g[1] * o_slc + g[2] * o_win


def _slc_decode_kernel(pt_ref, q_ref, slcnew_ref, winnew_ref, state_ref, sel_ref, gates_ref, ocmp_ref,
                       t0_ref, t1_ref, ege_ref, *rest, tq, n_kv, n_rep, npg, past):
    page_refs = rest[:npg]
    o_ref, qbd_ref, newp, kmask, m_ref, l_ref, acc = rest[npg:]
    st = pl.program_id(1)
    n_steps = pl.num_programs(1)
    dkv = n_kv * HEAD_DIM
    r = n_rep * n_kv * tq
    i_row = _row_query((r, PAGE), tq)
    j_col = _iota((r, PAGE), 1)

    def sel_mask(k0, width=PAGE):
        return jnp.concatenate([kmask[:, pl.ds(k0, width)]] * n_rep, axis=0) > 0.5

    def step(s, mask, pv):
        _softmax_step(jnp.where(mask, s, NEG) if mask is not None else s, pv, m_ref, l_ref, acc, mask)

    def new_tile(bias, mask):
        k = newp[:, 0:dkv].astype(BF16)
        v = newp[:, dkv:2 * dkv].astype(BF16)
        step(_dot_nt(qbd_ref[...], k) + bias, mask, lambda p: _dot(p, v))

    @pl.when(st == 0)
    def _():
        qbd_ref[...] = _nsa_qbd(q_ref[0], n_kv, n_rep)
        kmask[...] = _expand_blocks(sel_ref[0], kmask.shape[1])
        newp[...] = jnp.zeros(newp.shape, F32)
        newp[0:tq, :] = slcnew_ref[0]
        _softmax_init(m_ref, l_ref, acc)
        new_tile(t0_ref[...], sel_mask(past) & (j_col <= i_row))

    width = npg * PAGE
    k0 = pl.multiple_of((n_steps - 1 - st) * width, width)
    kt = jnp.concatenate([pg[0, 0:dkv, :] for pg in page_refs[::-1]], axis=1).astype(BF16)
    vt = jnp.concatenate([pg[0, dkv:2 * dkv, :] for pg in page_refs[::-1]], axis=1).astype(BF16)
    s = _dot(qbd_ref[...], kt)
    near = s[:, width - PAGE:] + jnp.where(st == 0, 1.0, 0.0) * t1_ref[...]
    s = jnp.concatenate([s[:, :width - PAGE], near], axis=1)
    step(s, sel_mask(k0, width), lambda p: _dot_nt(p, vt))

    @pl.when(st == n_steps - 1)
    def _():
        o_slc = _nsa_gather(_softmax_out(l_ref, acc), tq, n_kv, n_rep)
        _softmax_init(m_ref, l_ref, acc)
        newp[0:tq, :] = winnew_ref[0]
        new_tile(t0_ref[...], j_col <= i_row)
        n_back = WINDOW // PAGE
        for back in range(1, n_back + 1):
            sl = slice((n_back - back) * PAGE, (n_back - back + 1) * PAGE)
            s = _dot(qbd_ref[...], state_ref[0, 0:dkv, sl].astype(BF16))
            if back == 1:
                s = s + t1_ref[...]
            mask = (j_col > i_row) if back == n_back else None
            step(s, mask, lambda p, sl=sl: _dot_nt(p, state_ref[0, dkv:2 * dkv, sl].astype(BF16)))
        o_win = _nsa_gather(_softmax_out(l_ref, acc), tq, n_kv, n_rep)
        o_ref[0] = _merge(gates_ref[0], ege_ref, ocmp_ref[0], o_slc, o_win).astype(BF16)


def _slc_decode(q, slc_new, win_new, state_t, sel, gates, o_cmp, cache_t, page_table, t0, t1, ege, n_kv, npg):
    b, tq, dq = q.shape
    dkv = n_kv * HEAD_DIM
    n_rep = dq // dkv
    r = n_rep * n_kv * tq
    n_pages = page_table.shape[1]
    past = n_pages * PAGE
    const = lambda a: pl.BlockSpec(a.shape, lambda i, s, pt: (0,) * a.ndim)
    per_seq = lambda a: pl.BlockSpec((1,) + a.shape[1:], lambda i, s, pt: (i,) + (0,) * (a.ndim - 1))
    grid_spec = pltpu.PrefetchScalarGridSpec(
        num_scalar_prefetch=1,
        grid=(b, n_pages // npg),
        in_specs=[per_seq(q), per_seq(slc_new), per_seq(win_new), per_seq(state_t), per_seq(sel), per_seq(gates),
                  per_seq(o_cmp), const(t0), const(t1), const(ege)] + _page_specs(n_pages, 2 * dkv, npg),
        out_specs=pl.BlockSpec((1, tq, dq), lambda i, s, pt: (i, 0, 0)),
        scratch_shapes=[pltpu.VMEM((r, dkv), BF16), pltpu.VMEM((PAGE, 2 * dkv), F32),
                        pltpu.VMEM((n_kv * tq, past + PAGE), F32)] + _softmax_scratch(r, dkv),
    )
    return pl.pallas_call(
        functools.partial(_slc_decode_kernel, tq=tq, n_kv=n_kv, n_rep=n_rep, npg=npg, past=past),
        name="slc_decode",
        grid_spec=grid_spec,
        out_shape=jax.ShapeDtypeStruct((b, tq, dq), BF16),
        compiler_params=_params("arbitrary", "arbitrary"),
    )(page_table, q, slc_new, win_new, state_t, sel, gates, o_cmp, t0, t1, ege, *([cache_t] * npg))


def _rel_buckets(dist):
    n = np.maximum(dist, 0)
    exact = N_BUCKETS // 2
    nf = np.maximum(n, 1).astype(np.float64)
    large = exact + (np.log(nf / exact) / math.log(REL_MAX_DIST / exact) * (N_BUCKETS - exact)).astype(np.int64)
    return np.where(n < exact, n, np.minimum(large, N_BUCKETS - 1)).astype(np.int32)


def _head_block_diag(width, scale):
    h = np.arange(width) // HEAD_DIM
    return jnp.asarray((h[:, None] == h[None, :]).astype(np.float32) * scale, BF16)


class _NsaLayout:
    def __init__(self, n_heads, n_kv):
        n_rep = n_heads // n_kv
        self.n_kv, self.n_rep = n_kv, n_rep
        new = np.arange(n_heads)
        z, g = new // n_kv, new % n_kv
        self.head_perm = g * n_rep + z
        self.col_perm = (self.head_perm[:, None] * HEAD_DIM + np.arange(HEAD_DIM)[None, :]).reshape(-1)
        self.gate_perm = (np.arange(3)[:, None] * n_heads + self.head_perm[None, :]).reshape(-1)
        ege = np.zeros((3, LANES, n_heads * HEAD_DIM), np.float32)
        for br in range(3):
            for h in range(n_heads):
                ege[br, br * n_heads + h, h * HEAD_DIM:(h + 1) * HEAD_DIM] = 1.0
        self.ege = jnp.asarray(ege, BF16)


def _near_bias(rel_hd, dist):
    far = rel_hd[:, N_BUCKETS - 1]
    tab = jnp.take(rel_hd, jnp.asarray(_rel_buckets(dist)), axis=1) - far[:, None, None]
    return jnp.where(jnp.asarray(dist >= 0)[None], tab, 0.0)


def _toeplitz_tiles(rel_hd, tq, tk):
    i = np.arange(tq)[:, None]
    j = np.arange(tk)[None, :]
    return _near_bias(rel_hd, i - j), _near_bias(rel_hd, tk + i - j)


def _cmp_band(rel_hd, tq, ncp):
    i = np.arange(tq)[:, None]
    m = np.arange(ncp)[None, :]
    dist = D_CMP * (BAND_BACK - m) + i - (L_CMP - 1)
    return _near_bias(rel_hd, np.where(m < 2 * BAND_BACK + tq // D_CMP, dist, -1))


def _cmp_bias(rel_hd, qpos, n_chunk):
    dc = qpos[:, None] - (np.arange(n_chunk)[None, :] * D_CMP + L_CMP - 1)
    return jnp.take(rel_hd, jnp.asarray(_rel_buckets(dc)), axis=1)


def kernel(x_prompt, x_sample, cache_sb_kv, cache_fox_kv, cache_fox_logf, cache_cmp_kv, cache_slc_kv, state_win_kv,
           page_table, c_prompt, c_sample, norm_gain, w_ada, b_ada, ffn_w_in, ffn_w_out, w_in_ab, b_forget,
           fox_qk_gain, w_out_ab, w_in_nsa, b_nsa_gate, nsa_qk_gain, cmp_w, cmp_pe, rel_bias, w_out_nsa):
    bp, s_len, d = x_prompt.shape
    bs, t_dec, _ = x_sample.shape
    n_pool = cache_sb_kv.shape[0]
    h_sb, h_fox = cache_sb_kv.shape[3], cache_fox_kv.shape[3]
    n_kv = cache_cmp_kv.shape[3]
    n_heads = rel_bias.shape[1]
    da = h_sb * HEAD_DIM
    dkv = n_kv * HEAD_DIM
    dq = n_heads * HEAD_DIM
    assert h_sb == h_fox
    n_pages = page_table.shape[1]
    past = n_pages * PAGE
    lay = _NsaLayout(n_heads, n_kv)

    ffn_in = ffn_w_in.astype(BF16)
    ffn_out = ffn_w_out.astype(BF16)
    w_ab = jnp.pad(w_in_ab, ((0, 0), (0, LANES - h_fox))).astype(BF16)
    bfp = jnp.pad(b_forget, (0, LANES - h_fox)).reshape(1, LANES)
    bd_ab = _head_block_diag(da, 1.0 / HEAD_DIM)
    gq_fox = jnp.tile(fox_qk_gain[0], h_fox).reshape(1, da)
    gk_fox = jnp.tile(fox_qk_gain[1], h_fox).reshape(1, da)
    wq_ab = jnp.concatenate([w_in_ab[:, 0:da], w_in_ab[:, 3 * da:4 * da]], axis=1).astype(BF16)
    wkvt_ab = jnp.concatenate([w_in_ab[:, da:3 * da], w_in_ab[:, 4 * da:6 * da]], axis=1).T.astype(BF16)
    wflt_ab = jnp.pad(w_in_ab[:, 6 * da:].T, ((0, 16 - h_fox), (0, 0))).astype(BF16)
    w_out_sb = w_out_ab[:da].astype(BF16)
    w_out_fox = w_out_ab[da:].astype(BF16)
    w_nsa = jnp.concatenate([w_in_nsa[:, lay.col_perm], w_in_nsa[:, dq:dq + 6 * dkv],
                             w_in_nsa[:, dq + 6 * dkv + lay.gate_perm]], axis=1)
    w_nsa = jnp.pad(w_nsa, ((0, 0), (0, LANES - 3 * n_heads))).astype(BF16)
    bg_perm = jnp.pad(b_nsa_gate[lay.gate_perm], (0, LANES - 3 * n_heads)).reshape(1, LANES)
    wq_nsa = w_in_nsa[:, 0:dq].astype(BF16)
    wcmp_nsa = w_in_nsa[:, dq:dq + 2 * dkv].astype(BF16)
    wkvt_nsa = w_in_nsa[:, dq + 2 * dkv:dq + 6 * dkv].T.astype(BF16)
    wg_nsa = jnp.pad(w_in_nsa[:, dq + 6 * dkv:], ((0, 0), (0, LANES - 3 * n_heads))).astype(BF16)
    bg = jnp.pad(b_nsa_gate, (0, LANES - 3 * n_heads)).reshape(1, LANES)
    bd_kv = _head_block_diag(dkv, 1.0 / HEAD_DIM)
    tile_kv = lambda g: jnp.tile(g, n_kv).reshape(1, dkv)
    w_out_n = w_out_nsa.astype(BF16)
    w_out_n_perm = w_out_nsa[lay.col_perm].astype(BF16)
    eye = jnp.eye(n_kv, dtype=F32)
    cmp_bd = jnp.einsum('gh,kjde->kjgdhe', eye, cmp_w).reshape(2, L_CMP, dkv, dkv).astype(BF16)
    cmp_pe_t = jnp.tile(cmp_pe, (1, 1, n_kv))
    rel_orig = rel_bias.T
    rel_perm = rel_bias[:, lay.head_perm].T
    gk_cmp = tile_kv(nsa_qk_gain[1])

    mod = _modulation(jnp.concatenate([c_prompt, c_sample], axis=0), w_ada, b_ada)
    mod = mod.reshape(mod.shape[0], bp + bs, 3, 3, 1, d)

    def mods(l, sub, lo, hi):
        return tuple(mod[l, lo:hi, sub, k] for k in range(3))

    mp = lambda l, sub: mods(l, sub, 0, bp)
    tt = 512
    x = _ffn(x_prompt, mp(0, 0), norm_gain[0, 0], ffn_in[0, 0], ffn_out[0, 0], 1, tt)
    qsb, qfx, sbt, fxt, lft = _proj_ab_t(x, mp(0, 1), norm_gain[0, 1], wq_ab, wkvt_ab, wflt_ab, bd_ab, gq_fox,
                                         gk_fox.reshape(da, 1), b_forget.reshape(h_fox, 1), tt)
    o_sb = _sb_prompt(qsb, sbt, AB_TILE, 2)
    o_fox = _fox_prompt(qfx, fxt, lft, AB_TILE, 2)
    x = _outproj(x, mp(0, 1)[2], [o_sb, o_fox], [w_out_sb, w_out_fox], 1, tt)
    x = _ffn(x, mp(0, 2), norm_gain[0, 2], ffn_in[0, 1], ffn_out[0, 1], 1, tt)
    x = _ffn(x, mp(1, 0), norm_gain[1, 0], ffn_in[1, 0], ffn_out[1, 0], 1, tt)
    q, p_cmp, slct, wint, gates = _proj_nsa_t(x, mp(1, 1), norm_gain[1, 1], wq_nsa, wcmp_nsa, wkvt_nsa, wg_nsa, bd_kv,
                                              tile_kv(nsa_qk_gain[0]), tile_kv(nsa_qk_gain[2]).reshape(dkv, 1),
                                              tile_kv(nsa_qk_gain[3]).reshape(dkv, 1), bg, tt)
    kct, vct = _compress_prompt(p_cmp, cmp_bd, cmp_pe_t, bd_kv, gk_cmp)
    t0, t1 = _toeplitz_tiles(rel_orig, TILE, TILE)
    o = _nsa_prompt(q, slct, wint, kct, vct, gates, _cmp_band(rel_orig, TILE, s_len // D_CMP),
                    jnp.concatenate([t1, t0], axis=2), TILE, n_kv)
    x = _outproj(x, mp(1, 1)[2], [o], [w_out_n], 1, tt)
    y_p = _ffn(x, mp(1, 2), norm_gain[1, 2], ffn_in[1, 1], ffn_out[1, 1], 1, tt)

    c_sb = jnp.transpose(cache_sb_kv, (0, 2, 3, 4, 1)).reshape(n_pool, 2 * da, PAGE)
    c_fox = jnp.transpose(cache_fox_kv, (0, 2, 3, 4, 1)).reshape(n_pool, 2 * da, PAGE)
    c_lf = jnp.transpose(cache_fox_logf, (0, 2, 1))
    c_cmp = jnp.transpose(cache_cmp_kv, (0, 2, 3, 4, 1)).reshape(n_pool, 2 * dkv, PAGE)
    c_slc = jnp.transpose(cache_slc_kv, (0, 2, 3, 4, 1)).reshape(n_pool, 2 * dkv, PAGE)
    state_t = jnp.transpose(state_win_kv, (0, 2, 3, 4, 1)).reshape(bs, 2 * dkv, state_win_kv.shape[1])
    npg = PAGES_PER_STEP
    ms = lambda l, sub: mods(l, sub, bp, bp + bs)
    x = _ffn(x_sample, ms(0, 0), norm_gain[0, 0], ffn_in[0, 0], ffn_out[0, 0], bs, t_dec)
    qsb, s_sb, qfx, s_fox, s_logf, logfp = _proj_ab(x, ms(0, 1), norm_gain[0, 1], w_ab, bd_ab, gq_fox, gk_fox, bfp,
                                                    h_fox, bs, t_dec)
    o_sb = _sb_decode(qsb, s_sb, c_sb, page_table, npg)
    o_fox = _fox_decode(qfx, s_fox, logfp, c_fox, c_lf, page_table, npg)
    x = _outproj(x, ms(0, 1)[2], [o_sb, o_fox], [w_out_sb, w_out_fox], bs, t_dec)
    x = _ffn(x, ms(0, 2), norm_gain[0, 2], ffn_in[0, 1], ffn_out[0, 1], bs, t_dec)
    x = _ffn(x, ms(1, 0), norm_gain[1, 0], ffn_in[1, 0], ffn_out[1, 0], bs, t_dec)
    q, s_cmp, s_slc, s_win, gates = _proj_nsa(x, ms(1, 1), norm_gain[1, 1], w_nsa, bd_kv, tile_kv(nsa_qk_gain[0]),
                                              tile_kv(nsa_qk_gain[2]), tile_kv(nsa_qk_gain[3]), bg_perm, dq, bs, t_dec)
    bias = _cmp_bias(rel_perm, past + np.arange(t_dec), past // D_CMP)
    o_cmp, sel = _cmp_decode(q, c_cmp, page_table, cmp_bd, cmp_pe_t, bd_kv, gk_cmp, bias, n_kv, npg)
    t0, t1 = _toeplitz_tiles(rel_perm, t_dec, PAGE)
    o = _slc_decode(q, s_slc, s_win, state_t, sel, gates, o_cmp, c_slc, page_table,
                    t0.reshape(-1, PAGE), t1.reshape(-1, PAGE), lay.ege, n_kv, npg)
    x = _outproj(x, ms(1, 1)[2], [o], [w_out_n_perm], bs, t_dec)
    y_s = _ffn(x, ms(1, 2), norm_gain[1, 2], ffn_in[1, 1], ffn_out[1, 1], bs, t_dec)

    kv5 = lambda a, h: a.reshape(a.shape[0], a.shape[1], 2, h, HEAD_DIM)
    from_t = lambda a, h: kv5(jnp.transpose(a, (0, 2, 1)), h)
    win_len = min(WINDOW, s_len)
    state_rows = state_win_kv.reshape(bs, state_win_kv.shape[1], 2 * dkv)
    s_win_all = jnp.concatenate([state_rows, s_win], axis=1)
    new_len = min(WINDOW, s_win_all.shape[1])
    return (y_p, y_s, from_t(sbt, h_sb), kv5(s_sb, h_sb), from_t(fxt, h_fox), kv5(s_fox, h_fox),
            jnp.transpose(lft, (0, 2, 1)), s_logf, kv5(p_cmp, n_kv), kv5(s_cmp, n_kv), from_t(slct, n_kv),
            kv5(s_slc, n_kv), from_t(wint[:, :, s_len - win_len:], n_kv),
            kv5(s_win_all[:, s_win_all.shape[1] - new_len:], n_kv))
```

```python
import functools
import math

import numpy as np
import jax
import jax.numpy as jnp
from jax import lax
from jax.experimental import pallas as pl
from jax.experimental.pallas import tpu as pltpu

F32 = jnp.float32
BF16 = jnp.bfloat16

HEAD_DIM = 64
PAGE = 128
L_CMP = 32
D_CMP = 16
SLC_BLOCK = 64
N_SELECT = 16
WINDOW = 512
N_BUCKETS = 32
REL_MAX_DIST = 128
FORCE_SCORE = 1e4
NEG = -1e30
EPS = 1e-6
MACARON = 0.5
LANES = 128
VMEM_LIMIT = 56 * 1024 * 1024
FF_CHUNK = 256
TILE = 128
FAR_TILE = 512
AB_TILE = 256
PAGES_PER_STEP = 16
DEAD_LOG = -104.0
BAND_BACK = (REL_MAX_DIST + L_CMP) // D_CMP


def _params(*sem):
    return pltpu.CompilerParams(dimension_semantics=sem, vmem_limit_bytes=VMEM_LIMIT)


def _dot(a, b):
    return jnp.dot(a, b, preferred_element_type=F32)


def _dot_nt(a, b):
    return lax.dot_general(a, b, (((1,), (1,)), ((), ())), preferred_element_type=F32)


def _split2(x):
    hi = x.astype(BF16)
    lo = (x - hi.astype(F32)).astype(BF16)
    return hi, lo


def _split3(x):
    hi = x.astype(BF16)
    r = x - hi.astype(F32)
    mid = r.astype(BF16)
    lo = (r - mid.astype(F32)).astype(BF16)
    return hi, mid, lo


def _dot_x2(x, w):
    hi, lo = _split2(x)
    return _dot(hi, w) + _dot(lo, w)


def _dot_x3(x, w):
    hi, mid, lo = _split3(x)
    return _dot(hi, w) + _dot(mid, w) + _dot(lo, w)


def _dot_l2(w, x):
    hi, lo = _split2(x)
    return _dot(w, hi) + _dot(w, lo)


def _log_sigmoid(x):
    return jnp.minimum(x, 0.0) - jnp.log(1.0 + jnp.exp(-jnp.abs(x)))


def _adanorm(x, g, scale, shift):
    ms = jnp.mean(x * x, axis=-1, keepdims=True)
    return (x * lax.rsqrt(ms + EPS) * g) * (1.0 + scale) + shift


def _head_norm(t, bd):
    return t * lax.rsqrt(_dot_x2(t * t, bd) + EPS)


def _head_norm_t(t, bd):
    return t * lax.rsqrt(_dot_l2(bd, t * t) + EPS)


def _iota(shape, dim):
    return lax.broadcasted_iota(jnp.int32, shape, dim)


def _mod_kernel(c_ref, w_ref, b_ref, o_ref):
    c = c_ref[...]
    cs = c * jax.nn.sigmoid(c)
    ch, cl = _split2(cs)
    wh, wl = _split2(w_ref[0])
    o_ref[0] = _dot(ch, wh) + _dot(cl, wh) + _dot(ch, wl) + b_ref[0]


def _modulation(c_all, w_ada, b_ada):
    depth, d, n = w_ada.shape
    rows = c_all.shape[0]
    tn = 1024
    return pl.pallas_call(
        _mod_kernel, name="modulation",
        grid=(depth, n // tn),
        in_specs=[pl.BlockSpec((rows, d), lambda l, j: (0, 0)),
                  pl.BlockSpec((1, d, tn), lambda l, j: (l, 0, j)),
                  pl.BlockSpec((1, 1, tn), lambda l, j: (l, 0, j))],
        out_specs=pl.BlockSpec((1, rows, tn), lambda l, j: (l, 0, j)),
        out_shape=jax.ShapeDtypeStruct((depth, rows, n), F32),
        compiler_params=_params("arbitrary", "arbitrary"),
    )(c_all, w_ada, b_ada.reshape(depth, 1, n))


def _ffn_kernel(x_ref, shift_ref, scale_ref, gate_ref, g_ref, win_ref, wout_ref, o_ref, hid_ref, *, d_ff):
    x = x_ref[...]
    bb, tt, d = x.shape
    h = _adanorm(x, g_ref[...], scale_ref[...], shift_ref[...]).reshape(bb * tt, d).astype(BF16)
    for c in range(d_ff // FF_CHUNK):
        a = _dot(h, win_ref[:, c * FF_CHUNK:(c + 1) * FF_CHUNK])
        b = _dot(h, win_ref[:, d_ff + c * FF_CHUNK:d_ff + (c + 1) * FF_CHUNK])
        hid_ref[:, c * FF_CHUNK:(c + 1) * FF_CHUNK] = (a * jax.nn.sigmoid(a) * b).astype(BF16)
    o = _dot(hid_ref[...], wout_ref[...]).reshape(bb, tt, d)
    o_ref[...] = x + (MACARON * gate_ref[...]) * o


def _ffn(x, mod3, g, w_in, w_out, bb, tt):
    b, t, d = x.shape
    d_ff = w_out.shape[0]
    shift, scale, gate = mod3
    mspec = pl.BlockSpec((bb, 1, d), lambda i, j: (i, 0, 0))
    xspec = pl.BlockSpec((bb, tt, d), lambda i, j: (i, j, 0))
    return pl.pallas_call(
        functools.partial(_ffn_kernel, d_ff=d_ff), name="ffn",
        grid=(b // bb, t // tt),
        in_specs=[xspec, mspec, mspec, mspec,
                  pl.BlockSpec((1, 1, d), lambda i, j: (0, 0, 0)),
                  pl.BlockSpec((d, 2 * d_ff), lambda i, j: (0, 0)),
                  pl.BlockSpec((d_ff, d), lambda i, j: (0, 0))],
        out_specs=xspec,
        out_shape=jax.ShapeDtypeStruct(x.shape, F32),
        scratch_shapes=[pltpu.VMEM((bb * tt, d_ff), BF16)],
        compiler_params=_params("arbitrary", "arbitrary"),
    )(x, shift, scale, gate, g.reshape(1, 1, d), w_in, w_out)


def _outproj_kernel(*refs, n_in):
    x_ref, gate_ref = refs[0], refs[1]
    o_refs = refs[2:2 + n_in]
    w_refs = refs[2 + n_in:2 + 2 * n_in]
    out_ref = refs[2 + 2 * n_in]
    x = x_ref[...]
    bb, tt, d = x.shape
    y = None
    for o_ref, w_ref in zip(o_refs, w_refs):
        o = o_ref[...].astype(F32).reshape(bb * tt, o_ref.shape[-1]).astype(BF16)
        part = _dot(o, w_ref[...])
        y = part if y is None else y + part
    out_ref[...] = x + gate_ref[...] * y.reshape(bb, tt, d)


def _outproj(x, gate, outs, weights, bb, tt):
    b, t, d = x.shape
    n_in = len(outs)
    xspec = pl.BlockSpec((bb, tt, d), lambda i, j: (i, j, 0))
    in_specs = [xspec, pl.BlockSpec((bb, 1, d), lambda i, j: (i, 0, 0))]
    in_specs += [pl.BlockSpec((bb, tt, o.shape[-1]), lambda i, j: (i, j, 0)) for o in outs]
    in_specs += [pl.BlockSpec(w.shape, lambda i, j: (0, 0)) for w in weights]
    return pl.pallas_call(
        functools.partial(_outproj_kernel, n_in=n_in), name="outproj",
        grid=(b // bb, t // tt),
        in_specs=in_specs,
        out_specs=xspec,
        out_shape=jax.ShapeDtypeStruct(x.shape, F32),
        compiler_params=_params("arbitrary", "arbitrary"),
    )(x, gate, *outs, *weights)


def _proj_ab_kernel(x_ref, shift_ref, scale_ref, g_ref, w_ref, bd_ref, gq_ref, gk_ref, bf_ref,
                    qsb_ref, sbkv_ref, qfx_ref, fxkv_ref, logf_ref, logfp_ref, *, da, n_f):
    x = x_ref[...]
    bb, tt, d = x.shape
    m = bb * tt
    h = _adanorm(x, g_ref[...], scale_ref[...], shift_ref[...]).reshape(m, d).astype(BF16)
    sc = HEAD_DIM ** -0.5
    qsb_ref[...] = (_dot(h, w_ref[:, 0:da]) * sc).reshape(bb, tt, da).astype(BF16)
    sbkv_ref[...] = _dot(h, w_ref[:, da:3 * da]).reshape(bb, tt, 2 * da)
    bd = bd_ref[...]
    qf = _head_norm(_dot(h, w_ref[:, 3 * da:4 * da]), bd) * gq_ref[...]
    qfx_ref[...] = (qf * sc).reshape(bb, tt, da).astype(BF16)
    kf = _head_norm(_dot(h, w_ref[:, 4 * da:5 * da]), bd) * gk_ref[...]
    fxkv_ref[:, :, 0:da] = kf.reshape(bb, tt, da)
    fxkv_ref[:, :, da:2 * da] = _dot(h, w_ref[:, 5 * da:6 * da]).reshape(bb, tt, da)
    lf = _log_sigmoid(_dot(h, w_ref[:, 6 * da:6 * da + LANES]) + bf_ref[...])
    lf = jnp.where(_iota(lf.shape, 1) < n_f, lf, 0.0)
    logfp_ref[...] = lf.reshape(bb, tt, LANES)
    logf_ref[...] = lf[:, 0:n_f].reshape(bb, tt, n_f)


def _proj_ab(x, mod3, g, w, bd, gq, gk, bfp, n_f, bb, tt):
    b, t, d = x.shape
    da = bd.shape[0]
    shift, scale, _ = mod3
    mspec = pl.BlockSpec((bb, 1, d), lambda i, j: (i, 0, 0))
    xspec = pl.BlockSpec((bb, tt, d), lambda i, j: (i, j, 0))
    row = lambda width: pl.BlockSpec((1, width), lambda i, j: (0, 0))
    ospec = lambda width: pl.BlockSpec((bb, tt, width), lambda i, j: (i, j, 0))
    return pl.pallas_call(
        functools.partial(_proj_ab_kernel, da=da, n_f=n_f), name="proj_ab",
        grid=(b // bb, t // tt),
        in_specs=[xspec, mspec, mspec, pl.BlockSpec((1, 1, d), lambda i, j: (0, 0, 0)),
                  pl.BlockSpec(w.shape, lambda i, j: (0, 0)), pl.BlockSpec(bd.shape, lambda i, j: (0, 0)),
                  row(da), row(da), row(LANES)],
        out_specs=[ospec(da), ospec(2 * da), ospec(da), ospec(2 * da), ospec(n_f), ospec(LANES)],
        out_shape=[jax.ShapeDtypeStruct((b, t, da), BF16), jax.ShapeDtypeStruct((b, t, 2 * da), F32),
                   jax.ShapeDtypeStruct((b, t, da), BF16), jax.ShapeDtypeStruct((b, t, 2 * da), F32),
                   jax.ShapeDtypeStruct((b, t, n_f), F32), jax.ShapeDtypeStruct((b, t, LANES), F32)],
        compiler_params=_params("arbitrary", "arbitrary"),
    )(x, shift, scale, g.reshape(1, 1, d), w, bd, gq, gk, bfp)


def _proj_ab_t_kernel(x_ref, shift_ref, scale_ref, g_ref, wq_ref, wkvt_ref, wflt_ref, bd_ref, gq_ref, gk_ref, bf_ref,
                      qsb_ref, qfx_ref, sbt_ref, fxt_ref, lft_ref, *, da, n_f):
    x = x_ref[0]
    h = _adanorm(x, g_ref[0], scale_ref[0], shift_ref[0]).astype(BF16)
    sc = HEAD_DIM ** -0.5
    bd = bd_ref[...]
    qa = _dot(h, wq_ref[:, 0:da]) * sc
    qf = _head_norm(_dot(h, wq_ref[:, da:2 * da]), bd) * (gq_ref[...] * sc)
    for hh in range(da // HEAD_DIM):
        qsb_ref[0, hh] = qa[:, hh * HEAD_DIM:(hh + 1) * HEAD_DIM].astype(BF16)
        qfx_ref[0, hh] = qf[:, hh * HEAD_DIM:(hh + 1) * HEAD_DIM].astype(BF16)
    sbt_ref[0] = _dot_nt(wkvt_ref[0:2 * da, :], h)
    kft = _head_norm_t(_dot_nt(wkvt_ref[2 * da:3 * da, :], h), bd)
    fxt_ref[0, 0:da, :] = kft * gk_ref[...]
    fxt_ref[0, da:2 * da, :] = _dot_nt(wkvt_ref[3 * da:4 * da, :], h)
    lft_ref[0] = _log_sigmoid(_dot_nt(wflt_ref[...], h)[0:n_f, :] + bf_ref[...])


def _proj_ab_t(x, mod3, g, wq, wkvt, wflt, bd, gq, gk_col, bf_col, tt):
    b, t, d = x.shape
    da = bd.shape[0]
    n_h = da // HEAD_DIM
    n_f = bf_col.shape[0]
    shift, scale, _ = mod3
    mspec = pl.BlockSpec((1, 1, d), lambda i, j: (i, 0, 0))
    const = lambda a: pl.BlockSpec(a.shape, lambda i, j: (0,) * a.ndim)
    qspec = pl.BlockSpec((1, n_h, tt, HEAD_DIM), lambda i, j: (i, 0, j, 0))
    tspec = lambda rows: pl.BlockSpec((1, rows, tt), lambda i, j: (i, 0, j))
    return pl.pallas_call(
        functools.partial(_proj_ab_t_kernel, da=da, n_f=n_f), name="proj_ab_t",
        grid=(b, t // tt),
        in_specs=[pl.BlockSpec((1, tt, d), lambda i, j: (i, j, 0)), mspec, mspec,
                  pl.BlockSpec((1, 1, d), lambda i, j: (0, 0, 0)),
                  const(wq), const(wkvt), const(wflt), const(bd), const(gq), const(gk_col), const(bf_col)],
        out_specs=[qspec, qspec, tspec(2 * da), tspec(2 * da), tspec(n_f)],
        out_shape=[jax.ShapeDtypeStruct((b, n_h, t, HEAD_DIM), BF16), jax.ShapeDtypeStruct((b, n_h, t, HEAD_DIM), BF16),
                   jax.ShapeDtypeStruct((b, 2 * da, t), F32), jax.ShapeDtypeStruct((b, 2 * da, t), F32),
                   jax.ShapeDtypeStruct((b, n_f, t), F32)],
        compiler_params=_params("arbitrary", "arbitrary"),
    )(x, shift, scale, g.reshape(1, 1, d), wq, wkvt, wflt, bd, gq, gk_col, bf_col)


def _proj_nsa_kernel(x_ref, shift_ref, scale_ref, g_ref, w_ref, bd_ref, gq_ref, gs_ref, gw_ref, bg_ref,
                     q_ref, cmp_ref, slc_ref, win_ref, gates_ref, *, dq, dkv):
    x = x_ref[...]
    bb, tt, d = x.shape
    m = bb * tt
    h = _adanorm(x, g_ref[...], scale_ref[...], shift_ref[...]).reshape(m, d).astype(BF16)
    bd = bd_ref[...]
    sc = HEAD_DIM ** -0.5
    for c in range(dq // dkv):
        qc = _head_norm(_dot(h, w_ref[:, c * dkv:(c + 1) * dkv]), bd) * gq_ref[...]
        q_ref[:, :, c * dkv:(c + 1) * dkv] = (qc * sc).reshape(bb, tt, dkv).astype(BF16)
    o = dq
    cmp_ref[...] = _dot(h, w_ref[:, o:o + 2 * dkv]).reshape(bb, tt, 2 * dkv)
    o += 2 * dkv
    ks = _head_norm(_dot(h, w_ref[:, o:o + dkv]), bd) * gs_ref[...]
    slc_ref[:, :, 0:dkv] = ks.reshape(bb, tt, dkv)
    slc_ref[:, :, dkv:2 * dkv] = _dot(h, w_ref[:, o + dkv:o + 2 * dkv]).reshape(bb, tt, dkv)
    o += 2 * dkv
    kw = _head_norm(_dot(h, w_ref[:, o:o + dkv]), bd) * gw_ref[...]
    win_ref[:, :, 0:dkv] = kw.reshape(bb, tt, dkv)
    win_ref[:, :, dkv:2 * dkv] = _dot(h, w_ref[:, o + dkv:o + 2 * dkv]).reshape(bb, tt, dkv)
    o += 2 * dkv
    gates_ref[...] = jax.nn.sigmoid(_dot(h, w_ref[:, o:o + LANES]) + bg_ref[...]).reshape(bb, tt, LANES)


def _proj_nsa(x, mod3, g, w, bd, gq, gs, gw, bg, dq, bb, tt):
    b, t, d = x.shape
    dkv = bd.shape[0]
    shift, scale, _ = mod3
    mspec = pl.BlockSpec((bb, 1, d), lambda i, j: (i, 0, 0))
    xspec = pl.BlockSpec((bb, tt, d), lambda i, j: (i, j, 0))
    row = lambda width: pl.BlockSpec((1, width), lambda i, j: (0, 0))
    ospec = lambda width: pl.BlockSpec((bb, tt, width), lambda i, j: (i, j, 0))
    return pl.pallas_call(
        functools.partial(_proj_nsa_kernel, dq=dq, dkv=dkv), name="proj_nsa",
        grid=(b // bb, t // tt),
        in_specs=[xspec, mspec, mspec, pl.BlockSpec((1, 1, d), lambda i, j: (0, 0, 0)),
                  pl.BlockSpec(w.shape, lambda i, j: (0, 0)), pl.BlockSpec(bd.shape, lambda i, j: (0, 0)),
                  row(dkv), row(dkv), row(dkv), row(LANES)],
        out_specs=[ospec(dq), ospec(2 * dkv), ospec(2 * dkv), ospec(2 * dkv), ospec(LANES)],
        out_shape=[jax.ShapeDtypeStruct((b, t, dq), BF16), jax.ShapeDtypeStruct((b, t, 2 * dkv), F32),
                   jax.ShapeDtypeStruct((b, t, 2 * dkv), F32), jax.ShapeDtypeStruct((b, t, 2 * dkv), F32),
                   jax.ShapeDtypeStruct((b, t, LANES), F32)],
        compiler_params=_params("arbitrary", "arbitrary"),
    )(x, shift, scale, g.reshape(1, 1, d), w, bd, gq, gs, gw, bg)


def _proj_nsa_t_kernel(x_ref, shift_ref, scale_ref, g_ref, wq_ref, wcmp_ref, wkvt_ref, wg_ref, bd_ref, gq_ref, gs_ref,
                       gw_ref, bg_ref, q_ref, cmp_ref, slct_ref, wint_ref, gates_ref, *, dq, dkv):
    x = x_ref[0]
    h = _adanorm(x, g_ref[0], scale_ref[0], shift_ref[0]).astype(BF16)
    bd = bd_ref[...]
    sc = HEAD_DIM ** -0.5
    per = dkv // HEAD_DIM
    for c in range(dq // dkv):
        qc = _head_norm(_dot(h, wq_ref[:, c * dkv:(c + 1) * dkv]), bd) * (gq_ref[...] * sc)
        for hh in range(per):
            q_ref[0, c * per + hh] = qc[:, hh * HEAD_DIM:(hh + 1) * HEAD_DIM].astype(BF16)
    cmp_ref[0] = _dot(h, wcmp_ref[...])
    slct_ref[0, 0:dkv, :] = _head_norm_t(_dot_nt(wkvt_ref[0:dkv, :], h), bd) * gs_ref[...]
    slct_ref[0, dkv:2 * dkv, :] = _dot_nt(wkvt_ref[dkv:2 * dkv, :], h)
    wint_ref[0, 0:dkv, :] = _head_norm_t(_dot_nt(wkvt_ref[2 * dkv:3 * dkv, :], h), bd) * gw_ref[...]
    wint_ref[0, dkv:2 * dkv, :] = _dot_nt(wkvt_ref[3 * dkv:4 * dkv, :], h)
    gates_ref[0] = jax.nn.sigmoid(_dot(h, wg_ref[...]) + bg_ref[...])


def _proj_nsa_t(x, mod3, g, wq, wcmp, wkvt, wg, bd, gq, gs_col, gw_col, bg, tt):
    b, t, d = x.shape
    dkv = bd.shape[0]
    dq = wq.shape[1]
    n_h = dq // HEAD_DIM
    shift, scale, _ = mod3
    mspec = pl.BlockSpec((1, 1, d), lambda i, j: (i, 0, 0))
    const = lambda a: pl.BlockSpec(a.shape, lambda i, j: (0,) * a.ndim)
    tspec = lambda rows: pl.BlockSpec((1, rows, tt), lambda i, j: (i, 0, j))
    rspec = lambda width: pl.BlockSpec((1, tt, width), lambda i, j: (i, j, 0))
    return pl.pallas_call(
        functools.partial(_proj_nsa_t_kernel, dq=dq, dkv=dkv), name="proj_nsa_t",
        grid=(b, t // tt),
        in_specs=[rspec(d), mspec, mspec, pl.BlockSpec((1, 1, d), lambda i, j: (0, 0, 0)),
                  const(wq), const(wcmp), const(wkvt), const(wg), const(bd), const(gq), const(gs_col), const(gw_col),
                  const(bg)],
        out_specs=[pl.BlockSpec((1, n_h, tt, HEAD_DIM), lambda i, j: (i, 0, j, 0)), rspec(2 * dkv),
                   tspec(2 * dkv), tspec(2 * dkv), rspec(LANES)],
        out_shape=[jax.ShapeDtypeStruct((b, n_h, t, HEAD_DIM), BF16), jax.ShapeDtypeStruct((b, t, 2 * dkv), F32),
                   jax.ShapeDtypeStruct((b, 2 * dkv, t), F32), jax.ShapeDtypeStruct((b, 2 * dkv, t), F32),
                   jax.ShapeDtypeStruct((b, t, LANES), F32)],
        compiler_params=_params("arbitrary", "arbitrary"),
    )(x, shift, scale, g.reshape(1, 1, d), wq, wcmp, wkvt, wg, bd, gq, gs_col, gw_col, bg)


def _block_diag_rows(q, n_heads):
    q = q.astype(F32)
    head = _iota(q.shape, 1) // HEAD_DIM
    return jnp.concatenate([jnp.where(head == h, q, 0.0) for h in range(n_heads)], axis=0).astype(BF16)


def _gather_diag(acc, tq, n_heads):
    head = _iota((tq, acc.shape[1]), 1) // HEAD_DIM
    out = jnp.zeros((tq, acc.shape[1]), F32)
    for h in range(n_heads):
        out = out + jnp.where(head == h, acc[h * tq:(h + 1) * tq, :], 0.0)
    return out


def _row_query(shape, tq):
    return _iota(shape, 0) % tq


def _running_sum_matrix(tk, prefix):
    j = _iota((tk, 2 * tk), 0)
    s = _iota((tk, 2 * tk), 1)
    tri = (j <= s) if prefix else (j > s)
    return jnp.where((s >= tk) | tri, 1.0, 0.0).astype(BF16)


def _later_matrix(tk):
    return jnp.where(_iota((tk, tk), 0) > _iota((tk, tk), 1), 1.0, 0.0).astype(BF16)


def _across_lane_tiles(x, op):
    out = x[:, 0:LANES]
    for c in range(1, x.shape[1] // LANES):
        out = op(out, x[:, c * LANES:(c + 1) * LANES])
    return out


def _with_row_vector(x, v, op):
    vb = jnp.broadcast_to(v, (x.shape[0], LANES))
    tiles = [op(x[:, c * LANES:(c + 1) * LANES], vb) for c in range(x.shape[1] // LANES)]
    return tiles[0] if len(tiles) == 1 else jnp.concatenate(tiles, axis=1)


def _exp_minus(s, m):
    return jnp.exp(_with_row_vector(s, m, jnp.subtract))


def _row_max(x):
    return jnp.max(_across_lane_tiles(x, jnp.maximum), axis=1, keepdims=True)


def _row_sum(x):
    return jnp.sum(_across_lane_tiles(x, jnp.add), axis=1, keepdims=True)


def _log_keep(z, mask):
    lk = -(jnp.maximum(z, 0.0) + jnp.log(1.0 + jnp.exp(-jnp.abs(z))))
    return lk if mask is None else jnp.where(mask, lk, 0.0)


def _sb_weights(z, um, carry, mask):
    lk = _log_keep(z, mask)
    blk = um.shape[0]
    nb = z.shape[1] // blk
    rest = [None] * nb
    for b in reversed(range(nb)):
        lkb = lk[:, b * blk:(b + 1) * blk]
        rest[b] = _with_row_vector(_dot_x2(lkb, um), carry, jnp.add)
        carry = carry + _row_sum(lkb)
    w = jnp.exp(z + lk + (rest[0] if nb == 1 else jnp.concatenate(rest, axis=1)))
    if mask is not None:
        w = jnp.where(mask, w, 0.0)
    return w.astype(BF16), carry


def _softmax_first(s, pv, m_ref, l_ref, acc_ref):
    m = _row_max(s)
    p = _exp_minus(s, m)
    l_ref[...] = _row_sum(p)
    acc_ref[...] = pv(p.astype(BF16))
    m_ref[...] = m


def _softmax_step(s, pv, m_ref, l_ref, acc_ref, mask=None):
    m_prev = m_ref[...]
    m_new = jnp.maximum(m_prev, _row_max(s))
    p = _exp_minus(s, m_new)
    if mask is not None:
        p = jnp.where(mask, p, 0.0)
    alpha = jnp.exp(m_prev - m_new)
    l_ref[...] = alpha * l_ref[...] + _row_sum(p)
    acc_ref[...] = alpha * acc_ref[...] + pv(p.astype(BF16))
    m_ref[...] = m_new


def _softmax_chains(scores, vt, refs, first):
    if first:
        ms = [_row_max(s) for s in scores]
    else:
        prev = [(m[...], l[...], a[...]) for m, l, a in refs]
        ms = [jnp.maximum(pm, _row_max(s)) for (pm, _, _), s in zip(prev, scores)]
    ps = [_exp_minus(s, m) for s, m in zip(scores, ms)]
    pvs = [_dot_nt(p.astype(BF16), vt) for p in ps]
    sums = [_row_sum(p) for p in ps]
    for i, (m_ref, l_ref, acc_ref) in enumerate(refs):
        if first:
            l_ref[...] = sums[i]
            acc_ref[...] = pvs[i]
        else:
            alpha = jnp.exp(prev[i][0] - ms[i])
            l_ref[...] = alpha * prev[i][1] + sums[i]
            acc_ref[...] = alpha * prev[i][2] + pvs[i]
        m_ref[...] = ms[i]


def _softmax_single_chains(scores, vt):
    ps = [_exp_minus(s, _row_max(s)) for s in scores]
    pvs = [_dot_nt(p.astype(BF16), vt) for p in ps]
    return [pv / _row_sum(p) for pv, p in zip(pvs, ps)]


def _softmax_init(m_ref, l_ref, acc_ref):
    m_ref[...] = jnp.full(m_ref.shape, NEG, F32)
    l_ref[...] = jnp.zeros(l_ref.shape, F32)
    acc_ref[...] = jnp.zeros(acc_ref.shape, F32)


def _softmax_out(l_ref, acc_ref):
    return acc_ref[...] / jnp.maximum(l_ref[...], 1e-30)


def _softmax_scratch(r, width):
    return [pltpu.VMEM((r, 1), F32), pltpu.VMEM((r, 1), F32), pltpu.VMEM((r, width), F32)]


def _rank_select(imp, t, ns):
    blk = _iota(imp.shape, 1)
    cur = t // SLC_BLOCK
    valid = (blk * SLC_BLOCK <= t) & (blk < ns)
    forced = (blk == 0) | (blk == cur) | (blk == cur - 1)
    score = jnp.where(valid, jnp.where(forced, FORCE_SCORE, imp), -jnp.inf)
    cnt = jnp.zeros(imp.shape, F32)
    for j in range(ns):
        col = score[:, j:j + 1]
        cnt = cnt + jnp.where(col > score, 1.0, 0.0) + jnp.where(col == score, jnp.where(blk > j, 1.0, 0.0), 0.0)
    return jnp.where(valid & (cnt < N_SELECT), 1.0, 0.0)


def _rank_select_t(imp, q0, ns):
    rows = -(-ns // 8) * 8
    imp_t = imp.T[0:rows, :]
    blk = _iota(imp_t.shape, 0)
    t = q0 + _iota(imp_t.shape, 1)
    cur = t // SLC_BLOCK
    valid = (blk * SLC_BLOCK <= t) & (blk < ns)
    forced = (blk == 0) | (blk == cur) | (blk == cur - 1)
    score = jnp.where(valid, jnp.where(forced, FORCE_SCORE, imp_t), -jnp.inf)
    cnt = jnp.zeros(imp_t.shape, F32)
    for j in range(ns):
        row = score[j:j + 1, :]
        cnt = cnt + jnp.where(row > score, 1.0, 0.0) + jnp.where(row == score, jnp.where(blk > j, 1.0, 0.0), 0.0)
    sel_t = jnp.where(valid & (cnt < N_SELECT), 1.0, 0.0)
    return jnp.concatenate([sel_t, jnp.zeros((imp.shape[1] - rows, imp.shape[0]), F32)], axis=0).T


def _expand_blocks(sel, n_keys):
    nsp = sel.shape[1]
    blk = _iota((nsp, n_keys), 0)
    key = _iota((nsp, n_keys), 1)
    return _dot(sel, jnp.where(key // SLC_BLOCK == blk, 1.0, 0.0).astype(BF16))


def _sb_prompt_kernel(q_ref, kt_ref, vt_ref, o_ref, ktb, vtb, acc, car, *, tq, nh):
    qi = pl.program_id(2)

    @pl.when(qi == 0)
    def _():
        ktb[...] = kt_ref[0].astype(BF16)
        vtb[...] = vt_ref[0].astype(BF16)

    um = _later_matrix(tq)
    q0 = pl.multiple_of(qi * tq, tq)
    mask = _iota((tq, tq), 1) < _iota((tq, tq), 0)

    qs = [q_ref[0, h] for h in range(nh)]

    def tile(k0, first):
        kt = ktb[:, pl.ds(k0, tq)]
        vt = vtb[:, pl.ds(k0, tq)]
        hrows = lambda a, h: a[h * HEAD_DIM:(h + 1) * HEAD_DIM]
        prev = [(jnp.zeros((tq, 1), F32), None) if first else (car[h], acc[h]) for h in range(nh)]
        zs = [_dot(qs[h], hrows(kt, h)) for h in range(nh)]
        wc = [_sb_weights(zs[h], um, prev[h][0], mask if first else None) for h in range(nh)]
        pvs = [_dot_nt(wc[h][0], hrows(vt, h)) for h in range(nh)]
        for h in range(nh):
            acc[h] = pvs[h] if first else prev[h][1] + pvs[h]
            car[h] = wc[h][1]
        live = wc[0][1]
        for h in range(1, nh):
            live = jnp.maximum(live, wc[h][1])
        return jnp.max(live) > DEAD_LOG

    alive = tile(q0, True)

    def body(state):
        it, _ = state
        return it + 1, tile(pl.multiple_of((qi - 1 - it) * tq, tq), False)

    lax.while_loop(lambda state: (state[0] < qi) & state[1], body, (jnp.int32(0), alive))
    o_ref[0] = jnp.concatenate([acc[h] for h in range(nh)], axis=1).astype(BF16)


def _sb_prompt(q, kvt, tq, nh):
    b, n_h, s, _ = q.shape
    c = nh * HEAD_DIM
    ng = n_h // nh
    return pl.pallas_call(
        functools.partial(_sb_prompt_kernel, tq=tq, nh=nh), name="sb_prompt",
        grid=(b, ng, s // tq),
        in_specs=[pl.BlockSpec((1, nh, tq, HEAD_DIM), lambda i, g, j: (i, g, j, 0)),
                  pl.BlockSpec((1, c, s), lambda i, g, j: (i, g, 0)),
                  pl.BlockSpec((1, c, s), lambda i, g, j: (i, ng + g, 0))],
        out_specs=pl.BlockSpec((1, tq, c), lambda i, g, j: (i, j, g)),
        out_shape=jax.ShapeDtypeStruct((b, s, n_h * HEAD_DIM), BF16),
        scratch_shapes=[pltpu.VMEM((c, s), BF16), pltpu.VMEM((c, s), BF16),
                        pltpu.VMEM((nh, tq, HEAD_DIM), F32), pltpu.VMEM((nh, tq, 1), F32)],
        compiler_params=_params("arbitrary", "arbitrary", "arbitrary"),
    )(q, kvt, kvt)


def _fox_prompt_kernel(q_ref, kt_ref, vt_ref, lft_ref, o_ref, ktb, vtb, ncum, knorm, m_ref, l_ref, acc, *, tq, nh):
    g = pl.program_id(1)
    qi = pl.program_id(2)
    s_len = kt_ref.shape[2]

    @pl.when(qi == 0)
    def _():
        kb = kt_ref[0].astype(BF16)
        ktb[...] = kb
        vtb[...] = vt_ref[0].astype(BF16)
        kf = kb.astype(F32)
        for h in range(nh):
            kh = kf[h * HEAD_DIM:(h + 1) * HEAD_DIM]
            knorm[h] = jnp.full(knorm.shape[1:], jnp.sqrt(jnp.max(jnp.sum(kh * kh, axis=0, keepdims=True))), F32)
        um = _running_sum_matrix(tq, True)
        run = jnp.zeros((lft_ref.shape[1], tq), F32)
        for ch in range(s_len // tq):
            sl = slice(ch * tq, (ch + 1) * tq)
            rs = _dot_x3(lft_ref[0, :, sl], um)
            ncum[:, sl] = -(rs[:, :tq] + run)
            run = run + rs[:, tq:]

    q0 = pl.multiple_of(qi * tq, tq)
    mask = _iota((tq, tq), 1) <= _iota((tq, tq), 0)

    qs = [q_ref[0, h] for h in range(nh)]

    def tile(k0, first):
        kt = ktb[:, pl.ds(k0, tq)]
        vt = vtb[:, pl.ds(k0, tq)]
        bias = [ncum[pl.ds(g * nh + h, 1), pl.ds(k0, tq)] for h in range(nh)]
        hrows = lambda a, h: a[h * HEAD_DIM:(h + 1) * HEAD_DIM]
        if not first:
            prev = [(m_ref[h], l_ref[h], acc[h]) for h in range(nh)]
        ss = [_dot(qs[h], hrows(kt, h)) + bias[h] for h in range(nh)]
        if first:
            ss = [jnp.where(mask, s, NEG) for s in ss]
            ms = [_row_max(s) for s in ss]
        else:
            ms = [jnp.maximum(prev[h][0], _row_max(ss[h])) for h in range(nh)]
        ps = [_exp_minus(s, m) for s, m in zip(ss, ms)]
        pvs = [_dot_nt(ps[h].astype(BF16), hrows(vt, h)) for h in range(nh)]
        sums = [_row_sum(p) for p in ps]
        for h in range(nh):
            if first:
                l_ref[h] = sums[h]
                acc[h] = pvs[h]
            else:
                alpha = jnp.exp(prev[h][0] - ms[h])
                l_ref[h] = alpha * prev[h][1] + sums[h]
                acc[h] = alpha * prev[h][2] + pvs[h]
            m_ref[h] = ms[h]
        live = None
        for h in range(nh):
            bound = qn[h] * knorm[h][0:1, 0:1] + (jnp.min(bias[h], axis=1, keepdims=True) - ms[h])
            live = bound if live is None else jnp.maximum(live, bound)
        return jnp.max(live) > DEAD_LOG

    qn = [jnp.sqrt(jnp.sum(jnp.square(q.astype(F32)), axis=1, keepdims=True)) * 1.001 for q in qs]
    alive = tile(q0, True)

    def body(state):
        it, _ = state
        return it + 1, tile(pl.multiple_of((qi - 1 - it) * tq, tq), False)

    lax.while_loop(lambda state: (state[0] < qi) & state[1], body, (jnp.int32(0), alive))
    o_ref[0] = jnp.concatenate([_softmax_out(l_ref.at[h], acc.at[h]) for h in range(nh)], axis=1).astype(BF16)


def _fox_prompt(q, kvt, lft, tq, nh):
    b, n_h, s, _ = q.shape
    c = nh * HEAD_DIM
    ng = n_h // nh
    n_f = lft.shape[1]
    return pl.pallas_call(
        functools.partial(_fox_prompt_kernel, tq=tq, nh=nh), name="fox_prompt",
        grid=(b, ng, s // tq),
        in_specs=[pl.BlockSpec((1, nh, tq, HEAD_DIM), lambda i, g, j: (i, g, j, 0)),
                  pl.BlockSpec((1, c, s), lambda i, g, j: (i, g, 0)),
                  pl.BlockSpec((1, c, s), lambda i, g, j: (i, ng + g, 0)),
                  pl.BlockSpec((1, n_f, s), lambda i, g, j: (i, 0, 0))],
        out_specs=pl.BlockSpec((1, tq, c), lambda i, g, j: (i, j, g)),
        out_shape=jax.ShapeDtypeStruct((b, s, n_h * HEAD_DIM), BF16),
        scratch_shapes=[pltpu.VMEM((c, s), BF16), pltpu.VMEM((c, s), BF16), pltpu.VMEM((n_f, s), F32),
                        pltpu.VMEM((nh, 8, LANES), F32), pltpu.VMEM((nh, tq, 1), F32), pltpu.VMEM((nh, tq, 1), F32), pltpu.VMEM((nh, tq, HEAD_DIM), F32)],
        compiler_params=_params("arbitrary", "arbitrary", "arbitrary"),
    )(q, kvt, kvt, lft)


def _page_specs(n_pages, rows, npg):
    return [pl.BlockSpec((1, rows, PAGE), functools.partial(
        lambda i, s, pt, off: (pt[i, n_pages - 1 - s * npg - off], 0, 0), off=off)) for off in range(npg)]


def _sb_decode_kernel(pt_ref, q_ref, new_ref, *rest, tq, hg, npg):
    page_refs = rest[:npg]
    o_ref, qbd_ref, newp, acc, car = rest[npg:]
    st = pl.program_id(1)
    da = hg * HEAD_DIM
    r = hg * tq

    @pl.when(st == 0)
    def _():
        qbd_ref[...] = _block_diag_rows(q_ref[0], hg)
        newp[...] = jnp.zeros(newp.shape, F32)
        newp[0:tq, :] = new_ref[0]
        mask = _iota((r, PAGE), 1) < _row_query((r, PAGE), tq)
        w, c = _sb_weights(_dot_nt(qbd_ref[...], newp[:, 0:da].astype(BF16)), _later_matrix(PAGE),
                           jnp.zeros((r, 1), F32), mask)
        acc[...] = _dot(w, newp[:, da:2 * da].astype(BF16))
        car[...] = c

    kt = jnp.concatenate([pg[0, 0:da, :] for pg in page_refs[::-1]], axis=1).astype(BF16)
    vt = jnp.concatenate([pg[0, da:2 * da, :] for pg in page_refs[::-1]], axis=1).astype(BF16)
    w, c = _sb_weights(_dot(qbd_ref[...], kt), _later_matrix(2 * PAGE), car[...], None)
    acc[...] += _dot_nt(w, vt)
    car[...] = c

    @pl.when(st == pl.num_programs(1) - 1)
    def _():
        o_ref[0] = _gather_diag(acc[...], tq, hg).astype(BF16)


def _sb_decode(q, kv_new, cache_t, page_table, npg):
    b, tq, da = q.shape
    hg = da // HEAD_DIM
    n_pages = page_table.shape[1]
    r = hg * tq
    grid_spec = pltpu.PrefetchScalarGridSpec(
        num_scalar_prefetch=1,
        grid=(b, n_pages // npg),
        in_specs=[pl.BlockSpec((1, tq, da), lambda i, s, pt: (i, 0, 0)),
                  pl.BlockSpec((1, tq, 2 * da), lambda i, s, pt: (i, 0, 0))] + _page_specs(n_pages, 2 * da, npg),
        out_specs=pl.BlockSpec((1, tq, da), lambda i, s, pt: (i, 0, 0)),
        scratch_shapes=[pltpu.VMEM((r, da), BF16), pltpu.VMEM((PAGE, 2 * da), F32),
                        pltpu.VMEM((r, da), F32), pltpu.VMEM((r, 1), F32)],
    )
    return pl.pallas_call(
        functools.partial(_sb_decode_kernel, tq=tq, hg=hg, npg=npg), name="sb_decode",
        grid_spec=grid_spec,
        out_shape=jax.ShapeDtypeStruct((b, tq, da), BF16),
        compiler_params=_params("arbitrary", "arbitrary"),
    )(page_table, q, kv_new, *([cache_t] * npg))


def _fox_decode_kernel(pt_ref, q_ref, new_ref, lfnew_ref, *rest, tq, hg, npg):
    page_refs = rest[:npg]
    lf_refs = rest[npg:2 * npg]
    o_ref, qbd_ref, newp, lfp, run_ref, m_ref, l_ref, acc = rest[2 * npg:]
    st = pl.program_id(1)
    da = hg * HEAD_DIM
    r = hg * tq
    um = _running_sum_matrix(PAGE, False)

    def key_bias(lfts):
        rs = [_dot_x3(lft, um) for lft in lfts]
        run = run_ref[...]
        suf = [None] * len(lfts)
        for b in reversed(range(len(lfts))):
            suf[b] = rs[b][:, :PAGE] + run
            run = run + rs[b][:, PAGE:]
        run_ref[...] = run
        suf = suf[0] if len(suf) == 1 else jnp.concatenate(suf, axis=1)
        return jnp.concatenate([jnp.broadcast_to(suf[h:h + 1, :], (tq, suf.shape[1])) for h in range(hg)], axis=0)

    @pl.when(st == 0)
    def _():
        qbd_ref[...] = _block_diag_rows(q_ref[0], hg)
        newp[...] = jnp.zeros(newp.shape, F32)
        newp[0:tq, :] = new_ref[0]
        lfp[...] = jnp.zeros(lfp.shape, F32)
        lfp[0:tq, :] = lfnew_ref[0]
        run_ref[...] = jnp.zeros(run_ref.shape, F32)
        mask = _iota((r, PAGE), 1) <= _row_query((r, PAGE), tq)
        s = _dot_nt(qbd_ref[...], newp[:, 0:da].astype(BF16)) + key_bias([lfp[...].T[0:hg, :]])
        _softmax_first(jnp.where(mask, s, NEG), lambda p: _dot(p, newp[:, da:2 * da].astype(BF16)), m_ref, l_ref, acc)

    kt = jnp.concatenate([pg[0, 0:da, :] for pg in page_refs[::-1]], axis=1).astype(BF16)
    vt = jnp.concatenate([pg[0, da:2 * da, :] for pg in page_refs[::-1]], axis=1).astype(BF16)
    s = _dot(qbd_ref[...], kt) + key_bias([lf[0] for lf in lf_refs[::-1]])
    _softmax_step(s, lambda p: _dot_nt(p, vt), m_ref, l_ref, acc)

    @pl.when(st == pl.num_programs(1) - 1)
    def _():
        o_ref[0] = _gather_diag(_softmax_out(l_ref, acc), tq, hg).astype(BF16)


def _fox_decode(q, kv_new, lf_new, cache_t, cache_lf_t, page_table, npg):
    b, tq, da = q.shape
    hg = da // HEAD_DIM
    n_pages = page_table.shape[1]
    r = hg * tq
    assert cache_lf_t.shape[1] == hg
    grid_spec = pltpu.PrefetchScalarGridSpec(
        num_scalar_prefetch=1,
        grid=(b, n_pages // npg),
        in_specs=[pl.BlockSpec((1, tq, da), lambda i, s, pt: (i, 0, 0)),
                  pl.BlockSpec((1, tq, 2 * da), lambda i, s, pt: (i, 0, 0)),
                  pl.BlockSpec((1, tq, LANES), lambda i, s, pt: (i, 0, 0))]
        + _page_specs(n_pages, 2 * da, npg) + _page_specs(n_pages, hg, npg),
        out_specs=pl.BlockSpec((1, tq, da), lambda i, s, pt: (i, 0, 0)),
        scratch_shapes=[pltpu.VMEM((r, da), BF16), pltpu.VMEM((PAGE, 2 * da), F32),
                        pltpu.VMEM((PAGE, LANES), F32), pltpu.VMEM((hg, PAGE), F32)] + _softmax_scratch(r, da),
    )
    return pl.pallas_call(
        functools.partial(_fox_decode_kernel, tq=tq, hg=hg, npg=npg), name="fox_decode",
        grid_spec=grid_spec,
        out_shape=jax.ShapeDtypeStruct((b, tq, da), BF16),
        compiler_params=_params("arbitrary", "arbitrary"),
    )(page_table, q, kv_new, lf_new, *([cache_t] * npg), *([cache_lf_t] * npg))


def _compress(xb, bd_ref, pe_ref, n_chunk, dkv):
    first = [jnp.zeros((n_chunk, dkv), F32) for _ in range(2)]
    second = [jnp.zeros((n_chunk, dkv), F32) for _ in range(2)]
    for j in range(D_CMP):
        xj = jnp.concatenate([xb[lb, pl.ds(j, n_chunk, stride=D_CMP), :] for lb in range(xb.shape[0])], axis=1)
        for kv in range(2):
            xx = xj[:, kv * dkv:(kv + 1) * dkv]
            first[kv] += _dot((xx + pe_ref[kv, j:j + 1, :]).astype(BF16), bd_ref[kv, j])
            second[kv] += _dot((xx + pe_ref[kv, D_CMP + j:D_CMP + j + 1, :]).astype(BF16), bd_ref[kv, D_CMP + j])
    return [first[kv] + pltpu.roll(second[kv], n_chunk - 1, axis=0) for kv in range(2)]


def _compress_prompt_kernel(x_ref, bd_ref, pe_ref, bdn_ref, gk_ref, kct_ref, vct_ref, xb):
    n_chunk = x_ref.shape[1] // D_CMP
    dkv = bdn_ref.shape[0]
    for lb in range(xb.shape[0]):
        xb[lb] = x_ref[0, :, lb * LANES:(lb + 1) * LANES]
    kc, vc = _compress(xb, bd_ref, pe_ref, n_chunk, dkv)
    kct_ref[0] = (_head_norm(kc, bdn_ref[...]) * gk_ref[...]).T.astype(BF16)
    vct_ref[0] = vc.T.astype(BF16)


def _compress_prompt(cmp_kv, bd, pe, bdn, gk):
    b, s, w = cmp_kv.shape
    dkv = w // 2
    n_chunk = s // D_CMP
    const = lambda a: pl.BlockSpec(a.shape, lambda i: (0,) * a.ndim)
    ospec = pl.BlockSpec((1, dkv, n_chunk), lambda i: (i, 0, 0))
    return pl.pallas_call(
        _compress_prompt_kernel, name="compress_prompt",
        grid=(b,),
        in_specs=[pl.BlockSpec((1, s, w), lambda i: (i, 0, 0)), const(bd), const(pe), const(bdn), const(gk)],
        out_specs=[ospec, ospec],
        out_shape=[jax.ShapeDtypeStruct((b, dkv, n_chunk), BF16)] * 2,
        scratch_shapes=[pltpu.VMEM((w // LANES, s, LANES), F32)],
        compiler_params=_params("arbitrary"),
    )(cmp_kv, bd, pe, bdn, gk)


def _nsa_prompt_kernel(q_ref, kst_ref, vst_ref, kwt_ref, vwt_ref, kct_ref, vct_ref, gates_ref, band_ref, near_ref,
                       wadd_ref, ov_ref, o_ref, ksb, vsb, kwb, vwb, madd, m_ref, l_ref, acc, ocmp, owin, *, tq, nz, ns):
    g = pl.program_id(1)
    qi = pl.program_id(2)
    s_len = kst_ref.shape[2]
    ncp = kct_ref.shape[2]
    wslab = WINDOW + tq

    @pl.when(qi == 0)
    def _():
        ksb[...] = kst_ref[0].astype(BF16)
        vsb[...] = vst_ref[0].astype(BF16)
        kwb[:, 0:WINDOW] = jnp.zeros((HEAD_DIM, WINDOW), BF16)
        vwb[:, 0:WINDOW] = jnp.zeros((HEAD_DIM, WINDOW), BF16)
        kwb[:, WINDOW:] = kwt_ref[0].astype(BF16)
        vwb[:, WINDOW:] = vwt_ref[0].astype(BF16)

    q0 = pl.multiple_of(qi * tq, tq)
    zrows = lambda z: slice(z * tq, (z + 1) * tq)

    shift = (qi * (tq // D_CMP) + ncp - BAND_BACK) % ncp
    t_c = q0 + _iota((tq, ncp), 0)
    c_c = _iota((tq, ncp), 1)
    cmask = (t_c >= c_c * D_CMP + (L_CMP - 1)) & (c_c < ncp - 1)
    qs = [q_ref[0, z] for z in range(nz)]
    kct = kct_ref[0]
    vct = vct_ref[0]
    ss = [jnp.where(cmask, _dot(qs[z], kct) + pltpu.roll(band_ref[z], shift, axis=1), NEG) for z in range(nz)]
    ps = [jnp.where(cmask, _exp_minus(s, _row_max(s)), 0.0) for s in ss]
    ps = [_with_row_vector(p, jnp.maximum(_row_sum(p), 1e-30), jnp.divide) for p in ps]
    oc = [_dot_nt(p.astype(BF16), vct) for p in ps]
    pz = ps[0]
    for z in range(1, nz):
        pz = pz + ps[z]
    imp = _dot_x2(pz, ov_ref[...])
    sel = _rank_select_t(imp, q0, ns)
    for z in range(nz):
        ocmp[zrows(z), :] = oc[z]
    madd[...] = (_expand_blocks(sel.astype(BF16), s_len) - 1.0) * (-NEG)
    refs = [(m_ref.at[zrows(z)], l_ref.at[zrows(z)], acc.at[zrows(z)]) for z in range(nz)]

    def slc_tile(k0, tk, extra, first):
        kt = ksb[:, pl.ds(k0, tk)]
        vt = vsb[:, pl.ds(k0, tk)]
        ma = madd[:, pl.ds(k0, tk)]
        scores = [_dot(qs[z], kt) + (ma if extra is None else ma + extra(z)) for z in range(nz)]
        _softmax_chains(scores, vt, refs, first)

    causal = jnp.where(_iota((tq, tq), 1) <= _iota((tq, tq), 0), 0.0, NEG)

    @pl.when(qi == 0)
    def _():
        slc_tile(0, tq, lambda z: near_ref[z, :, tq:2 * tq] + causal, True)

    @pl.when(qi >= 1)
    def _():
        edge = jnp.concatenate([jnp.zeros((tq, tq), F32), causal], axis=1)
        slc_tile(pl.multiple_of(q0 - tq, tq), 2 * tq, lambda z: near_ref[z] + edge, True)

    n_far = jnp.maximum(qi - 1, 0)
    per_far = FAR_TILE // tq

    def far_body(it, _):
        slc_tile(pl.multiple_of(it * FAR_TILE, FAR_TILE), FAR_TILE, None, False)
        return 0

    lax.fori_loop(0, n_far // per_far, far_body, 0)

    def rem_body(it, _):
        slc_tile(pl.multiple_of(((n_far // per_far) * per_far + it) * tq, tq), tq, None, False)
        return 0

    lax.fori_loop(0, n_far % per_far, rem_body, 0)

    kslab = kwb[:, pl.ds(q0, wslab)]
    vslab = vwb[:, pl.ds(q0, wslab)]
    wadd = wadd_ref[...] + jnp.where(_iota((tq, wslab), 1) >= WINDOW - q0, 0.0, NEG)
    ss = [_dot(qs[z], kslab) + wadd for z in range(nz)]
    ss = [jnp.concatenate([s[:, :wslab - 2 * tq], s[:, wslab - 2 * tq:] + near_ref[z]], axis=1)
          for z, s in enumerate(ss)]
    for z, o in enumerate(_softmax_single_chains(ss, vslab)):
        owin[zrows(z), :] = o

    n_heads = nz * pl.num_programs(1)
    src = _iota((LANES, LANES), 0)
    dst = _iota((LANES, LANES), 1)
    pick = jnp.where((dst < 3 * nz) & (src == (dst // nz) * n_heads + g * nz + dst % nz), 1.0, 0.0).astype(BF16)
    gsel = _dot_x3(gates_ref[0], pick)
    outs = []
    for z in range(nz):
        gate = lambda br: gsel[:, br * nz + z:br * nz + z + 1]
        o_slc = acc[zrows(z), :] / jnp.maximum(l_ref[zrows(z), :], 1e-30)
        outs.append(gate(0) * ocmp[zrows(z), :] + gate(1) * o_slc + gate(2) * owin[zrows(z), :])
    o_ref[0] = jnp.concatenate(outs, axis=1).astype(BF16)


def _window_mask(tq):
    i = np.arange(tq)[:, None]
    c = np.arange(WINDOW + tq)[None, :]
    return jnp.asarray(np.where((c > i) & (c <= i + WINDOW), 0.0, NEG).astype(np.float32))


def _nsa_prompt(q, slct, wint, kct, vct, gates, band, near, tq, n_kv):
    b, n_heads, s, _ = q.shape
    nz = n_heads // n_kv
    r = nz * tq
    ncp = kct.shape[2]
    ns = -(-s // SLC_BLOCK)
    nsp = -(-ns // LANES) * LANES
    ov = _overlap_matrix(ncp, nsp)
    wadd = _window_mask(tq)
    kv_spec = lambda off: pl.BlockSpec((1, HEAD_DIM, s), lambda i, g, j: (i, off + g, 0))
    c_spec = pl.BlockSpec((1, HEAD_DIM, ncp), lambda i, g, j: (i, g, 0))
    tab_spec = lambda a: pl.BlockSpec((nz,) + a.shape[1:], lambda i, g, j: (g, 0, 0))
    return pl.pallas_call(
        functools.partial(_nsa_prompt_kernel, tq=tq, nz=nz, ns=ns), name="nsa_prompt",
        grid=(b, n_kv, s // tq),
        in_specs=[pl.BlockSpec((1, nz, tq, HEAD_DIM), lambda i, g, j: (i, g, j, 0)),
                  kv_spec(0), kv_spec(n_kv), kv_spec(0), kv_spec(n_kv), c_spec, c_spec,
                  pl.BlockSpec((1, tq, LANES), lambda i, g, j: (i, j, 0)),
                  tab_spec(band), tab_spec(near),
                  pl.BlockSpec(wadd.shape, lambda i, g, j: (0, 0)),
                  pl.BlockSpec(ov.shape, lambda i, g, j: (0, 0))],
        out_specs=pl.BlockSpec((1, tq, nz * HEAD_DIM), lambda i, g, j: (i, j, g)),
        out_shape=jax.ShapeDtypeStruct((b, s, n_heads * HEAD_DIM), BF16),
        scratch_shapes=[pltpu.VMEM((HEAD_DIM, s), BF16)] * 2 + [pltpu.VMEM((HEAD_DIM, WINDOW + s), BF16)] * 2
        + [pltpu.VMEM((tq, s), F32)] + _softmax_scratch(r, HEAD_DIM)
        + [pltpu.VMEM((r, HEAD_DIM), F32)] * 2,
        compiler_params=_params("arbitrary", "arbitrary", "arbitrary"),
    )(q, slct, slct, wint, wint, kct, vct, gates, band, near, wadd, ov)


def _cmp_select(q, kcmp, vcmp, bias, ov, q0, tq, nc, ns, n_kv, n_rep):
    ncp = kcmp.shape[0]
    qbd = _nsa_qbd(q, n_kv, n_rep)
    r = qbd.shape[0]
    t = q0 + _row_query((r, ncp), tq)
    cidx = _iota((r, ncp), 1)
    mask = (t - (cidx * D_CMP + L_CMP - 1) >= 0) & (cidx < nc)
    s = jnp.where(mask, _dot_nt(qbd, kcmp) + bias, NEG)
    p = jnp.where(mask, jnp.exp(s - jnp.max(s, axis=1, keepdims=True)), 0.0)
    p = p / jnp.maximum(jnp.sum(p, axis=1, keepdims=True), 1e-30)
    o_cmp = _nsa_gather(_dot(p.astype(BF16), vcmp), tq, n_kv, n_rep)
    rz = n_kv * tq
    pz = p[0:rz]
    for z in range(1, n_rep):
        pz = pz + p[z * rz:(z + 1) * rz]
    imp = _dot_x2(pz, ov)
    return o_cmp, _rank_select(imp, q0 + _row_query(imp.shape, tq), ns)


def _overlap_matrix(ncp, nsp):
    ci = np.arange(ncp)[:, None]
    bj = np.arange(nsp)[None, :]
    ov = (ci * D_CMP <= bj * SLC_BLOCK + SLC_BLOCK - 1) & (ci * D_CMP + L_CMP - 1 >= bj * SLC_BLOCK)
    return jnp.asarray(ov.astype(np.float32), BF16)


def _cmp_decode_kernel(pt_ref, q_ref, bd_ref, pe_ref, bdn_ref, gk_ref, bias_ref, ov_ref, *rest,
                       tq, n_kv, n_rep, ns, npg, past):
    page_refs = rest[:npg]
    ocmp_ref, sel_ref, xbuf = rest[npg:]
    st = pl.program_id(1)
    n_steps = pl.num_programs(1)
    dkv = n_kv * HEAD_DIM
    n_chunk = past // D_CMP
    for off, pg in enumerate(page_refs):
        pos = (n_steps - 1 - st) * npg + (npg - 1 - off)
        rows = pg[0].T
        for lb in range(xbuf.shape[0]):
            xbuf[lb, pl.ds(pl.multiple_of(pos * PAGE, PAGE), PAGE), :] = rows[:, lb * LANES:(lb + 1) * LANES]

    @pl.when(st == n_steps - 1)
    def _():
        kc, vc = _compress(xbuf, bd_ref, pe_ref, n_chunk, dkv)
        kcb = (_head_norm(kc, bdn_ref[...]) * gk_ref[...]).astype(BF16)
        bias = bias_ref[...].reshape(n_kv * n_rep * tq, n_chunk)
        o_cmp, sel = _cmp_select(q_ref[0], kcb, vc.astype(BF16), bias, ov_ref[...], past, tq,
                                 n_chunk - 1, ns, n_kv, n_rep)
        ocmp_ref[0] = o_cmp
        sel_ref[0] = sel.astype(BF16)


def _cmp_decode(q, cache_t, page_table, bd, pe, bdn, gk, bias, n_kv, npg):
    b, tq, dq = q.shape
    dkv = n_kv * HEAD_DIM
    n_rep = dq // dkv
    n_pages = page_table.shape[1]
    past = n_pages * PAGE
    n_chunk = (past + tq) // D_CMP
    assert n_chunk == past // D_CMP
    ns = -(-(past + tq) // SLC_BLOCK)
    nsp = -(-ns // LANES) * LANES
    ov = _overlap_matrix(n_chunk, nsp)
    const = lambda a: pl.BlockSpec(a.shape, lambda i, s, pt: (0,) * a.ndim)
    grid_spec = pltpu.PrefetchScalarGridSpec(
        num_scalar_prefetch=1,
        grid=(b, n_pages // npg),
        in_specs=[pl.BlockSpec((1, tq, dq), lambda i, s, pt: (i, 0, 0)),
                  const(bd), const(pe), const(bdn), const(gk), const(bias), const(ov)]
        + _page_specs(n_pages, 2 * dkv, npg),
        out_specs=[pl.BlockSpec((1, tq, dq), lambda i, s, pt: (i, 0, 0)),
                   pl.BlockSpec((1, n_kv * tq, nsp), lambda i, s, pt: (i, 0, 0))],
        scratch_shapes=[pltpu.VMEM((2 * dkv // LANES, past, LANES), F32)],
    )
    return pl.pallas_call(
        functools.partial(_cmp_decode_kernel, tq=tq, n_kv=n_kv, n_rep=n_rep, ns=ns, npg=npg, past=past),
        name="cmp_decode",
        grid_spec=grid_spec,
        out_shape=[jax.ShapeDtypeStruct((b, tq, dq), F32), jax.ShapeDtypeStruct((b, n_kv * tq, nsp), BF16)],
        compiler_params=_params("arbitrary", "arbitrary"),
    )(page_table, q, bd, pe, bdn, gk, bias, ov, *([cache_t] * npg))


def _nsa_qbd(q, n_kv, n_rep):
    dkv = n_kv * HEAD_DIM
    return jnp.concatenate([_block_diag_rows(q[:, z * dkv:(z + 1) * dkv], n_kv) for z in range(n_rep)], axis=0)


def _nsa_gather(o_full, tq, n_kv, n_rep):
    rz = n_kv * tq
    return jnp.concatenate([_gather_diag(o_full[z * rz:(z + 1) * rz], tq, n_kv) for z in range(n_rep)], axis=1)


def _merge(gates, ege_ref, o_cmp, o_slc, o_win):
    g = [_dot_x2(gates, ege_ref[br]) for br in range(3)]
    return g[0] * o_cmp + g[1] * o_slc + g[2] * o_win


def _slc_decode_kernel(pt_ref, q_ref, slcnew_ref, winnew_ref, state_ref, sel_ref, gates_ref, ocmp_ref,
                       t0_ref, t1_ref, ege_ref, *rest, tq, n_kv, n_rep, npg, past):
    page_refs = rest[:npg]
    o_ref, qbd_ref, newp, kmask, m_ref, l_ref, acc = rest[npg:]
    st = pl.program_id(1)
    n_steps = pl.num_programs(1)
    dkv = n_kv * HEAD_DIM
    r = n_rep * n_kv * tq
    i_row = _row_query((r, PAGE), tq)
    j_col = _iota((r, PAGE), 1)

    def sel_mask(k0, width=PAGE):
        return jnp.concatenate([kmask[:, pl.ds(k0, width)]] * n_rep, axis=0) > 0.5

    def step(s, mask, pv):
        _softmax_step(jnp.where(mask, s, NEG) if mask is not None else s, pv, m_ref, l_ref, acc, mask)

    def new_tile(bias, mask):
        k = newp[:, 0:dkv].astype(BF16)
        v = newp[:, dkv:2 * dkv].astype(BF16)
        step(_dot_nt(qbd_ref[...], k) + bias, mask, lambda p: _dot(p, v))

    @pl.when(st == 0)
    def _():
        qbd_ref[...] = _nsa_qbd(q_ref[0], n_kv, n_rep)
        kmask[...] = _expand_blocks(sel_ref[0], kmask.shape[1])
        newp[...] = jnp.zeros(newp.shape, F32)
        newp[0:tq, :] = slcnew_ref[0]
        _softmax_init(m_ref, l_ref, acc)
        new_tile(t0_ref[...], sel_mask(past) & (j_col <= i_row))

    width = npg * PAGE
    k0 = pl.multiple_of((n_steps - 1 - st) * width, width)
    kt = jnp.concatenate([pg[0, 0:dkv, :] for pg in page_refs[::-1]], axis=1).astype(BF16)
    vt = jnp.concatenate([pg[0, dkv:2 * dkv, :] for pg in page_refs[::-1]], axis=1).astype(BF16)
    s = _dot(qbd_ref[...], kt)
    near = s[:, width - PAGE:] + jnp.where(st == 0, 1.0, 0.0) * t1_ref[...]
    s = jnp.concatenate([s[:, :width - PAGE], near], axis=1)
    step(s, sel_mask(k0, width), lambda p: _dot_nt(p, vt))

    @pl.when(st == n_steps - 1)
    def _():
        o_slc = _nsa_gather(_softmax_out(l_ref, acc), tq, n_kv, n_rep)
        _softmax_init(m_ref, l_ref, acc)
        newp[0:tq, :] = winnew_ref[0]
        new_tile(t0_ref[...], j_col <= i_row)
        n_back = WINDOW // PAGE
        for back in range(1, n_back + 1):
            sl = slice((n_back - back) * PAGE, (n_back - back + 1) * PAGE)
            s = _dot(qbd_ref[...], state_ref[0, 0:dkv, sl].astype(BF16))
            if back == 1:
                s = s + t1_ref[...]
            mask = (j_col > i_row) if back == n_back else None
            step(s, mask, lambda p, sl=sl: _dot_nt(p, state_ref[0, dkv:2 * dkv, sl].astype(BF16)))
        o_win = _nsa_gather(_softmax_out(l_ref, acc), tq, n_kv, n_rep)
        o_ref[0] = _merge(gates_ref[0], ege_ref, ocmp_ref[0], o_slc, o_win).astype(BF16)


def _slc_decode(q, slc_new, win_new, state_t, sel, gates, o_cmp, cache_t, page_table, t0, t1, ege, n_kv, npg):
    b, tq, dq = q.shape
    dkv = n_kv * HEAD_DIM
    n_rep = dq // dkv
    r = n_rep * n_kv * tq
    n_pages = page_table.shape[1]
    past = n_pages * PAGE
    const = lambda a: pl.BlockSpec(a.shape, lambda i, s, pt: (0,) * a.ndim)
    per_seq = lambda a: pl.BlockSpec((1,) + a.shape[1:], lambda i, s, pt: (i,) + (0,) * (a.ndim - 1))
    grid_spec = pltpu.PrefetchScalarGridSpec(
        num_scalar_prefetch=1,
        grid=(b, n_pages // npg),
        in_specs=[per_seq(q), per_seq(slc_new), per_seq(win_new), per_seq(state_t), per_seq(sel), per_seq(gates),
                  per_seq(o_cmp), const(t0), const(t1), const(ege)] + _page_specs(n_pages, 2 * dkv, npg),
        out_specs=pl.BlockSpec((1, tq, dq), lambda i, s, pt: (i, 0, 0)),
        scratch_shapes=[pltpu.VMEM((r, dkv), BF16), pltpu.VMEM((PAGE, 2 * dkv), F32),
                        pltpu.VMEM((n_kv * tq, past + PAGE), F32)] + _softmax_scratch(r, dkv),
    )
    return pl.pallas_call(
        functools.partial(_slc_decode_kernel, tq=tq, n_kv=n_kv, n_rep=n_rep, npg=npg, past=past),
        name="slc_decode",
        grid_spec=grid_spec,
        out_shape=jax.ShapeDtypeStruct((b, tq, dq), BF16),
        compiler_params=_params("arbitrary", "arbitrary"),
    )(page_table, q, slc_new, win_new, state_t, sel, gates, o_cmp, t0, t1, ege, *([cache_t] * npg))


def _rel_buckets(dist):
    n = np.maximum(dist, 0)
    exact = N_BUCKETS // 2
    nf = np.maximum(n, 1).astype(np.float64)
    large = exact + (np.log(nf / exact) / math.log(REL_MAX_DIST / exact) * (N_BUCKETS - exact)).astype(np.int64)
    return np.where(n < exact, n, np.minimum(large, N_BUCKETS - 1)).astype(np.int32)


def _head_block_diag(width, scale):
    h = np.arange(width) // HEAD_DIM
    return jnp.asarray((h[:, None] == h[None, :]).astype(np.float32) * scale, BF16)


class _NsaLayout:
    def __init__(self, n_heads, n_kv):
        n_rep = n_heads // n_kv
        self.n_kv, self.n_rep = n_kv, n_rep
        new = np.arange(n_heads)
        z, g = new // n_kv, new % n_kv
        self.head_perm = g * n_rep + z
        self.col_perm = (self.head_perm[:, None] * HEAD_DIM + np.arange(HEAD_DIM)[None, :]).reshape(-1)
        self.gate_perm = (np.arange(3)[:, None] * n_heads + self.head_perm[None, :]).reshape(-1)
        ege = np.zeros((3, LANES, n_heads * HEAD_DIM), np.float32)
        for br in range(3):
            for h in range(n_heads):
                ege[br, br * n_heads + h, h * HEAD_DIM:(h + 1) * HEAD_DIM] = 1.0
        self.ege = jnp.asarray(ege, BF16)


def _near_bias(rel_hd, dist):
    far = rel_hd[:, N_BUCKETS - 1]
    tab = jnp.take(rel_hd, jnp.asarray(_rel_buckets(dist)), axis=1) - far[:, None, None]
    return jnp.where(jnp.asarray(dist >= 0)[None], tab, 0.0)


def _toeplitz_tiles(rel_hd, tq, tk):
    i = np.arange(tq)[:, None]
    j = np.arange(tk)[None, :]
    return _near_bias(rel_hd, i - j), _near_bias(rel_hd, tk + i - j)


def _cmp_band(rel_hd, tq, ncp):
    i = np.arange(tq)[:, None]
    m = np.arange(ncp)[None, :]
    dist = D_CMP * (BAND_BACK - m) + i - (L_CMP - 1)
    return _near_bias(rel_hd, np.where(m < 2 * BAND_BACK + tq // D_CMP, dist, -1))


def _cmp_bias(rel_hd, qpos, n_chunk):
    dc = qpos[:, None] - (np.arange(n_chunk)[None, :] * D_CMP + L_CMP - 1)
    return jnp.take(rel_hd, jnp.asarray(_rel_buckets(dc)), axis=1)


def kernel(x_prompt, x_sample, cache_sb_kv, cache_fox_kv, cache_fox_logf, cache_cmp_kv, cache_slc_kv, state_win_kv,
           page_table, c_prompt, c_sample, norm_gain, w_ada, b_ada, ffn_w_in, ffn_w_out, w_in_ab, b_forget,
           fox_qk_gain, w_out_ab, w_in_nsa, b_nsa_gate, nsa_qk_gain, cmp_w, cmp_pe, rel_bias, w_out_nsa):
    bp, s_len, d = x_prompt.shape
    bs, t_dec, _ = x_sample.shape
    n_pool = cache_sb_kv.shape[0]
    h_sb, h_fox = cache_sb_kv.shape[3], cache_fox_kv.shape[3]
    n_kv = cache_cmp_kv.shape[3]
    n_heads = rel_bias.shape[1]
    da = h_sb * HEAD_DIM
    dkv = n_kv * HEAD_DIM
    dq = n_heads * HEAD_DIM
    assert h_sb == h_fox
    n_pages = page_table.shape[1]
    past = n_pages * PAGE
    lay = _NsaLayout(n_heads, n_kv)

    ffn_in = ffn_w_in.astype(BF16)
    ffn_out = ffn_w_out.astype(BF16)
    w_ab = jnp.pad(w_in_ab, ((0, 0), (0, LANES - h_fox))).astype(BF16)
    bfp = jnp.pad(b_forget, (0, LANES - h_fox)).reshape(1, LANES)
    bd_ab = _head_block_diag(da, 1.0 / HEAD_DIM)
    gq_fox = jnp.tile(fox_qk_gain[0], h_fox).reshape(1, da)
    gk_fox = jnp.tile(fox_qk_gain[1], h_fox).reshape(1, da)
    wq_ab = jnp.concatenate([w_in_ab[:, 0:da], w_in_ab[:, 3 * da:4 * da]], axis=1).astype(BF16)
    wkvt_ab = jnp.concatenate([w_in_ab[:, da:3 * da], w_in_ab[:, 4 * da:6 * da]], axis=1).T.astype(BF16)
    wflt_ab = jnp.pad(w_in_ab[:, 6 * da:].T, ((0, 16 - h_fox), (0, 0))).astype(BF16)
    w_out_sb = w_out_ab[:da].astype(BF16)
    w_out_fox = w_out_ab[da:].astype(BF16)
    w_nsa = jnp.concatenate([w_in_nsa[:, lay.col_perm], w_in_nsa[:, dq:dq + 6 * dkv],
                             w_in_nsa[:, dq + 6 * dkv + lay.gate_perm]], axis=1)
    w_nsa = jnp.pad(w_nsa, ((0, 0), (0, LANES - 3 * n_heads))).astype(BF16)
    bg_perm = jnp.pad(b_nsa_gate[lay.gate_perm], (0, LANES - 3 * n_heads)).reshape(1, LANES)
    wq_nsa = w_in_nsa[:, 0:dq].astype(BF16)
    wcmp_nsa = w_in_nsa[:, dq:dq + 2 * dkv].astype(BF16)
    wkvt_nsa = w_in_nsa[:, dq + 2 * dkv:dq + 6 * dkv].T.astype(BF16)
    wg_nsa = jnp.pad(w_in_nsa[:, dq + 6 * dkv:], ((0, 0), (0, LANES - 3 * n_heads))).astype(BF16)
    bg = jnp.pad(b_nsa_gate, (0, LANES - 3 * n_heads)).reshape(1, LANES)
    bd_kv = _head_block_diag(dkv, 1.0 / HEAD_DIM)
    tile_kv = lambda g: jnp.tile(g, n_kv).reshape(1, dkv)
    w_out_n = w_out_nsa.astype(BF16)
    w_out_n_perm = w_out_nsa[lay.col_perm].astype(BF16)
    eye = jnp.eye(n_kv, dtype=F32)
    cmp_bd = jnp.einsum('gh,kjde->kjgdhe', eye, cmp_w).reshape(2, L_CMP, dkv, dkv).astype(BF16)
    cmp_pe_t = jnp.tile(cmp_pe, (1, 1, n_kv))
    rel_orig = rel_bias.T
    rel_perm = rel_bias[:, lay.head_perm].T
    gk_cmp = tile_kv(nsa_qk_gain[1])

    mod = _modulation(jnp.concatenate([c_prompt, c_sample], axis=0), w_ada, b_ada)
    mod = mod.reshape(mod.shape[0], bp + bs, 3, 3, 1, d)

    def mods(l, sub, lo, hi):
        return tuple(mod[l, lo:hi, sub, k] for k in range(3))

    mp = lambda l, sub: mods(l, sub, 0, bp)
    tt = 512
    x = _ffn(x_prompt, mp(0, 0), norm_gain[0, 0], ffn_in[0, 0], ffn_out[0, 0], 1, tt)
    qsb, qfx, sbt, fxt, lft = _proj_ab_t(x, mp(0, 1), norm_gain[0, 1], wq_ab, wkvt_ab, wflt_ab, bd_ab, gq_fox,
                                         gk_fox.reshape(da, 1), b_forget.reshape(h_fox, 1), tt)
    o_sb = _sb_prompt(qsb, sbt, AB_TILE, 2)
    o_fox = _fox_prompt(qfx, fxt, lft, AB_TILE, 2)
    x = _outproj(x, mp(0, 1)[2], [o_sb, o_fox], [w_out_sb, w_out_fox], 1, tt)
    x = _ffn(x, mp(0, 2), norm_gain[0, 2], ffn_in[0, 1], ffn_out[0, 1], 1, tt)
    x = _ffn(x, mp(1, 0), norm_gain[1, 0], ffn_in[1, 0], ffn_out[1, 0], 1, tt)
    q, p_cmp, slct, wint, gates = _proj_nsa_t(x, mp(1, 1), norm_gain[1, 1], wq_nsa, wcmp_nsa, wkvt_nsa, wg_nsa, bd_kv,
                                              tile_kv(nsa_qk_gain[0]), tile_kv(nsa_qk_gain[2]).reshape(dkv, 1),
                                              tile_kv(nsa_qk_gain[3]).reshape(dkv, 1), bg, tt)
    kct, vct = _compress_prompt(p_cmp, cmp_bd, cmp_pe_t, bd_kv, gk_cmp)
    t0, t1 = _toeplitz_tiles(rel_orig, TILE, TILE)
    o = _nsa_prompt(q, slct, wint, kct, vct, gates, _cmp_band(rel_orig, TILE, s_len // D_CMP),
                    jnp.concatenate([t1, t0], axis=2), TILE, n_kv)
    x = _outproj(x, mp(1, 1)[2], [o], [w_out_n], 1, tt)
    y_p = _ffn(x, mp(1, 2), norm_gain[1, 2], ffn_in[1, 1], ffn_out[1, 1], 1, tt)

    c_sb = jnp.transpose(cache_sb_kv, (0, 2, 3, 4, 1)).reshape(n_pool, 2 * da, PAGE)
    c_fox = jnp.transpose(cache_fox_kv, (0, 2, 3, 4, 1)).reshape(n_pool, 2 * da, PAGE)
    c_lf = jnp.transpose(cache_fox_logf, (0, 2, 1))
    c_cmp = jnp.transpose(cache_cmp_kv, (0, 2, 3, 4, 1)).reshape(n_pool, 2 * dkv, PAGE)
    c_slc = jnp.transpose(cache_slc_kv, (0, 2, 3, 4, 1)).reshape(n_pool, 2 * dkv, PAGE)
    state_t = jnp.transpose(state_win_kv, (0, 2, 3, 4, 1)).reshape(bs, 2 * dkv, state_win_kv.shape[1])
    npg = PAGES_PER_STEP
    ms = lambda l, sub: mods(l, sub, bp, bp + bs)
    x = _ffn(x_sample, ms(0, 0), norm_gain[0, 0], ffn_in[0, 0], ffn_out[0, 0], bs, t_dec)
    qsb, s_sb, qfx, s_fox, s_logf, logfp = _proj_ab(x, ms(0, 1), norm_gain[0, 1], w_ab, bd_ab, gq_fox, gk_fox, bfp,
                                                    h_fox, bs, t_dec)
    o_sb = _sb_decode(qsb, s_sb, c_sb, page_table, npg)
    o_fox = _fox_decode(qfx, s_fox, logfp, c_fox, c_lf, page_table, npg)
    x = _outproj(x, ms(0, 1)[2], [o_sb, o_fox], [w_out_sb, w_out_fox], bs, t_dec)
    x = _ffn(x, ms(0, 2), norm_gain[0, 2], ffn_in[0, 1], ffn_out[0, 1], bs, t_dec)
    x = _ffn(x, ms(1, 0), norm_gain[1, 0], ffn_in[1, 0], ffn_out[1, 0], bs, t_dec)
    q, s_cmp, s_slc, s_win, gates = _proj_nsa(x, ms(1, 1), norm_gain[1, 1], w_nsa, bd_kv, tile_kv(nsa_qk_gain[0]),
                                              tile_kv(nsa_qk_gain[2]), tile_kv(nsa_qk_gain[3]), bg_perm, dq, bs, t_dec)
    bias = _cmp_bias(rel_perm, past + np.arange(t_dec), past // D_CMP)
    o_cmp, sel = _cmp_decode(q, c_cmp, page_table, cmp_bd, cmp_pe_t, bd_kv, gk_cmp, bias, n_kv, npg)
    t0, t1 = _toeplitz_tiles(rel_perm, t_dec, PAGE)
    o = _slc_decode(q, s_slc, s_win, state_t, sel, gates, o_cmp, c_slc, page_table,
                    t0.reshape(-1, PAGE), t1.reshape(-1, PAGE), lay.ege, n_kv, npg)
    x = _outproj(x, ms(1, 1)[2], [o], [w_out_n_perm], bs, t_dec)
    y_s = _ffn(x, ms(1, 2), norm_gain[1, 2], ffn_in[1, 1], ffn_out[1, 1], bs, t_dec)

    kv5 = lambda a, h: a.reshape(a.shape[0], a.shape[1], 2, h, HEAD_DIM)
    from_t = lambda a, h: kv5(jnp.transpose(a, (0, 2, 1)), h)
    win_len = min(WINDOW, s_len)
    state_rows = state_win_kv.reshape(bs, state_win_kv.shape[1], 2 * dkv)
    s_win_all = jnp.concatenate([state_rows, s_win], axis=1)
    new_len = min(WINDOW, s_win_all.shape[1])
    return (y_p, y_s, from_t(sbt, h_sb), kv5(s_sb, h_sb), from_t(fxt, h_fox), kv5(s_fox, h_fox),
            jnp.transpose(lft, (0, 2, 1)), s_logf, kv5(p_cmp, n_kv), kv5(s_cmp, n_kv), from_t(slct, n_kv),
            kv5(s_slc, n_kv), from_t(wint[:, :, s_len - win_len:], n_kv),
            kv5(s_win_all[:, s_win_all.shape[1] - new_len:], n_kv))
```

```python
import functools
import math

import numpy as np
import jax
import jax.numpy as jnp
from jax import lax
from jax.experimental import pallas as pl
from jax.experimental.pallas import tpu as pltpu

F32 = jnp.float32
BF16 = jnp.bfloat16

HEAD_DIM = 64
PAGE = 128
L_CMP = 32
D_CMP = 16
SLC_BLOCK = 64
N_SELECT = 16
WINDOW = 512
N_BUCKETS = 32
REL_MAX_DIST = 128
FORCE_SCORE = 1e4
NEG = -1e30
EPS = 1e-6
MACARON = 0.5
LANES = 128
VMEM_LIMIT = 56 * 1024 * 1024
FF_CHUNK = 256
TILE = 128
FAR_TILE = 256
AB_TILE = 256
PAGES_PER_STEP = 16
DEAD_LOG = -104.0
SUM_ROWS = 16
BAND_BACK = (REL_MAX_DIST + L_CMP) // D_CMP


def _params(*sem):
    return pltpu.CompilerParams(dimension_semantics=sem, vmem_limit_bytes=VMEM_LIMIT)


def _dot(a, b):
    return jnp.dot(a, b, preferred_element_type=F32)


def _dot_nt(a, b):
    return lax.dot_general(a, b, (((1,), (1,)), ((), ())), preferred_element_type=F32)


def _split2(x):
    hi = x.astype(BF16)
    lo = (x - hi.astype(F32)).astype(BF16)
    return hi, lo


def _split3(x):
    hi = x.astype(BF16)
    r = x - hi.astype(F32)
    mid = r.astype(BF16)
    lo = (r - mid.astype(F32)).astype(BF16)
    return hi, mid, lo


def _dot_x2(x, w):
    hi, lo = _split2(x)
    return _dot(hi, w) + _dot(lo, w)


def _dot_x3(x, w):
    hi, mid, lo = _split3(x)
    return _dot(hi, w) + _dot(mid, w) + _dot(lo, w)


def _dot_l2(w, x):
    hi, lo = _split2(x)
    return _dot(w, hi) + _dot(w, lo)


def _log_sigmoid(x):
    return jnp.minimum(x, 0.0) - jnp.log(1.0 + jnp.exp(-jnp.abs(x)))


def _adanorm(x, g, scale, shift):
    ms = jnp.mean(x * x, axis=-1, keepdims=True)
    return (x * lax.rsqrt(ms + EPS) * g) * (1.0 + scale) + shift


def _head_norm(t, bd):
    return t * lax.rsqrt(_dot_x2(t * t, bd) + EPS)


def _head_norm_t(t, bd):
    return t * lax.rsqrt(_dot_l2(bd, t * t) + EPS)


def _iota(shape, dim):
    return lax.broadcasted_iota(jnp.int32, shape, dim)


def _mod_kernel(c_ref, w_ref, b_ref, o_ref):
    c = c_ref[...]
    cs = c * jax.nn.sigmoid(c)
    ch, cl = _split2(cs)
    wh, wl = _split2(w_ref[0])
    o_ref[0] = _dot(ch, wh) + _dot(cl, wh) + _dot(ch, wl) + b_ref[0]


def _modulation(c_all, w_ada, b_ada):
    depth, d, n = w_ada.shape
    rows = c_all.shape[0]
    tn = 1024
    return pl.pallas_call(
        _mod_kernel, name="modulation",
        grid=(depth, n // tn),
        in_specs=[pl.BlockSpec((rows, d), lambda l, j: (0, 0)),
                  pl.BlockSpec((1, d, tn), lambda l, j: (l, 0, j)),
                  pl.BlockSpec((1, 1, tn), lambda l, j: (l, 0, j))],
        out_specs=pl.BlockSpec((1, rows, tn), lambda l, j: (l, 0, j)),
        out_shape=jax.ShapeDtypeStruct((depth, rows, n), F32),
        compiler_params=_params("arbitrary", "arbitrary"),
    )(c_all, w_ada, b_ada.reshape(depth, 1, n))


def _ffn_kernel(x_ref, shift_ref, scale_ref, gate_ref, g_ref, win_ref, wout_ref, o_ref, hid_ref, *, d_ff):
    x = x_ref[...]
    bb, tt, d = x.shape
    h = _adanorm(x, g_ref[...], scale_ref[...], shift_ref[...]).reshape(bb * tt, d).astype(BF16)
    for c in range(d_ff // FF_CHUNK):
        a = _dot(h, win_ref[:, c * FF_CHUNK:(c + 1) * FF_CHUNK])
        b = _dot(h, win_ref[:, d_ff + c * FF_CHUNK:d_ff + (c + 1) * FF_CHUNK])
        hid_ref[:, c * FF_CHUNK:(c + 1) * FF_CHUNK] = (a * jax.nn.sigmoid(a) * b).astype(BF16)
    o = _dot(hid_ref[...], wout_ref[...]).reshape(bb, tt, d)
    o_ref[...] = x + (MACARON * gate_ref[...]) * o


def _ffn(x, mod3, g, w_in, w_out, bb, tt):
    b, t, d = x.shape
    d_ff = w_out.shape[0]
    shift, scale, gate = mod3
    mspec = pl.BlockSpec((bb, 1, d), lambda i, j: (i, 0, 0))
    xspec = pl.BlockSpec((bb, tt, d), lambda i, j: (i, j, 0))
    return pl.pallas_call(
        functools.partial(_ffn_kernel, d_ff=d_ff), name="ffn",
        grid=(b // bb, t // tt),
        in_specs=[xspec, mspec, mspec, mspec,
                  pl.BlockSpec((1, 1, d), lambda i, j: (0, 0, 0)),
                  pl.BlockSpec((d, 2 * d_ff), lambda i, j: (0, 0)),
                  pl.BlockSpec((d_ff, d), lambda i, j: (0, 0))],
        out_specs=xspec,
        out_shape=jax.ShapeDtypeStruct(x.shape, F32),
        scratch_shapes=[pltpu.VMEM((bb * tt, d_ff), BF16)],
        compiler_params=_params("arbitrary", "arbitrary"),
    )(x, shift, scale, gate, g.reshape(1, 1, d), w_in, w_out)


def _outproj_kernel(*refs, n_in):
    x_ref, gate_ref = refs[0], refs[1]
    o_refs = refs[2:2 + n_in]
    w_refs = refs[2 + n_in:2 + 2 * n_in]
    out_ref = refs[2 + 2 * n_in]
    x = x_ref[...]
    bb, tt, d = x.shape
    y = None
    for o_ref, w_ref in zip(o_refs, w_refs):
        o = o_ref[...].astype(F32).reshape(bb * tt, o_ref.shape[-1]).astype(BF16)
        part = _dot(o, w_ref[...])
        y = part if y is None else y + part
    out_ref[...] = x + gate_ref[...] * y.reshape(bb, tt, d)


def _outproj(x, gate, outs, weights, bb, tt):
    b, t, d = x.shape
    n_in = len(outs)
    xspec = pl.BlockSpec((bb, tt, d), lambda i, j: (i, j, 0))
    in_specs = [xspec, pl.BlockSpec((bb, 1, d), lambda i, j: (i, 0, 0))]
    in_specs += [pl.BlockSpec((bb, tt, o.shape[-1]), lambda i, j: (i, j, 0)) for o in outs]
    in_specs += [pl.BlockSpec(w.shape, lambda i, j: (0, 0)) for w in weights]
    return pl.pallas_call(
        functools.partial(_outproj_kernel, n_in=n_in), name="outproj",
        grid=(b // bb, t // tt),
        in_specs=in_specs,
        out_specs=xspec,
        out_shape=jax.ShapeDtypeStruct(x.shape, F32),
        compiler_params=_params("arbitrary", "arbitrary"),
    )(x, gate, *outs, *weights)


def _proj_ab_kernel(x_ref, shift_ref, scale_ref, g_ref, w_ref, bd_ref, gq_ref, gk_ref, bf_ref,
                    qsb_ref, sbkv_ref, qfx_ref, fxkv_ref, logf_ref, logfp_ref, *, da, n_f):
    x = x_ref[...]
    bb, tt, d = x.shape
    m = bb * tt
    h = _adanorm(x, g_ref[...], scale_ref[...], shift_ref[...]).reshape(m, d).astype(BF16)
    sc = HEAD_DIM ** -0.5
    qsb_ref[...] = (_dot(h, w_ref[:, 0:da]) * sc).reshape(bb, tt, da).astype(BF16)
    sbkv_ref[...] = _dot(h, w_ref[:, da:3 * da]).reshape(bb, tt, 2 * da)
    bd = bd_ref[...]
    qf = _head_norm(_dot(h, w_ref[:, 3 * da:4 * da]), bd) * gq_ref[...]
    qfx_ref[...] = (qf * sc).reshape(bb, tt, da).astype(BF16)
    kf = _head_norm(_dot(h, w_ref[:, 4 * da:5 * da]), bd) * gk_ref[...]
    fxkv_ref[:, :, 0:da] = kf.reshape(bb, tt, da)
    fxkv_ref[:, :, da:2 * da] = _dot(h, w_ref[:, 5 * da:6 * da]).reshape(bb, tt, da)
    lf = _log_sigmoid(_dot(h, w_ref[:, 6 * da:6 * da + LANES]) + bf_ref[...])
    lf = jnp.where(_iota(lf.shape, 1) < n_f, lf, 0.0)
    logfp_ref[...] = lf.reshape(bb, tt, LANES)
    logf_ref[...] = lf[:, 0:n_f].reshape(bb, tt, n_f)


def _proj_ab(x, mod3, g, w, bd, gq, gk, bfp, n_f, bb, tt):
    b, t, d = x.shape
    da = bd.shape[0]
    shift, scale, _ = mod3
    mspec = pl.BlockSpec((bb, 1, d), lambda i, j: (i, 0, 0))
    xspec = pl.BlockSpec((bb, tt, d), lambda i, j: (i, j, 0))
    row = lambda width: pl.BlockSpec((1, width), lambda i, j: (0, 0))
    ospec = lambda width: pl.BlockSpec((bb, tt, width), lambda i, j: (i, j, 0))
    return pl.pallas_call(
        functools.partial(_proj_ab_kernel, da=da, n_f=n_f), name="proj_ab",
        grid=(b // bb, t // tt),
        in_specs=[xspec, mspec, mspec, pl.BlockSpec((1, 1, d), lambda i, j: (0, 0, 0)),
                  pl.BlockSpec(w.shape, lambda i, j: (0, 0)), pl.BlockSpec(bd.shape, lambda i, j: (0, 0)),
                  row(da), row(da), row(LANES)],
        out_specs=[ospec(da), ospec(2 * da), ospec(da), ospec(2 * da), ospec(n_f), ospec(LANES)],
        out_shape=[jax.ShapeDtypeStruct((b, t, da), BF16), jax.ShapeDtypeStruct((b, t, 2 * da), F32),
                   jax.ShapeDtypeStruct((b, t, da), BF16), jax.ShapeDtypeStruct((b, t, 2 * da), F32),
                   jax.ShapeDtypeStruct((b, t, n_f), F32), jax.ShapeDtypeStruct((b, t, LANES), F32)],
        compiler_params=_params("arbitrary", "arbitrary"),
    )(x, shift, scale, g.reshape(1, 1, d), w, bd, gq, gk, bfp)


def _proj_ab_t_kernel(x_ref, shift_ref, scale_ref, g_ref, wq_ref, wkvt_ref, wflt_ref, bd_ref, gq_ref, gk_ref, bf_ref,
                      qsb_ref, qfx_ref, sbt_ref, fxt_ref, lft_ref, *, da, n_f):
    x = x_ref[0]
    h = _adanorm(x, g_ref[0], scale_ref[0], shift_ref[0]).astype(BF16)
    sc = HEAD_DIM ** -0.5
    bd = bd_ref[...]
    qa = _dot(h, wq_ref[:, 0:da]) * sc
    qf = _head_norm(_dot(h, wq_ref[:, da:2 * da]), bd) * (gq_ref[...] * sc)
    for hh in range(da // HEAD_DIM):
        qsb_ref[0, hh] = qa[:, hh * HEAD_DIM:(hh + 1) * HEAD_DIM].astype(BF16)
        qfx_ref[0, hh] = qf[:, hh * HEAD_DIM:(hh + 1) * HEAD_DIM].astype(BF16)
    sbt_ref[0] = _dot_nt(wkvt_ref[0:2 * da, :], h)
    kft = _head_norm_t(_dot_nt(wkvt_ref[2 * da:3 * da, :], h), bd)
    fxt_ref[0, 0:da, :] = kft * gk_ref[...]
    fxt_ref[0, da:2 * da, :] = _dot_nt(wkvt_ref[3 * da:4 * da, :], h)
    lft_ref[0] = _log_sigmoid(_dot_nt(wflt_ref[...], h)[0:n_f, :] + bf_ref[...])


def _proj_ab_t(x, mod3, g, wq, wkvt, wflt, bd, gq, gk_col, bf_col, tt):
    b, t, d = x.shape
    da = bd.shape[0]
    n_h = da // HEAD_DIM
    n_f = bf_col.shape[0]
    shift, scale, _ = mod3
    mspec = pl.BlockSpec((1, 1, d), lambda i, j: (i, 0, 0))
    const = lambda a: pl.BlockSpec(a.shape, lambda i, j: (0,) * a.ndim)
    qspec = pl.BlockSpec((1, n_h, tt, HEAD_DIM), lambda i, j: (i, 0, j, 0))
    tspec = lambda rows: pl.BlockSpec((1, rows, tt), lambda i, j: (i, 0, j))
    return pl.pallas_call(
        functools.partial(_proj_ab_t_kernel, da=da, n_f=n_f), name="proj_ab_t",
        grid=(b, t // tt),
        in_specs=[pl.BlockSpec((1, tt, d), lambda i, j: (i, j, 0)), mspec, mspec,
                  pl.BlockSpec((1, 1, d), lambda i, j: (0, 0, 0)),
                  const(wq), const(wkvt), const(wflt), const(bd), const(gq), const(gk_col), const(bf_col)],
        out_specs=[qspec, qspec, tspec(2 * da), tspec(2 * da), tspec(n_f)],
        out_shape=[jax.ShapeDtypeStruct((b, n_h, t, HEAD_DIM), BF16), jax.ShapeDtypeStruct((b, n_h, t, HEAD_DIM), BF16),
                   jax.ShapeDtypeStruct((b, 2 * da, t), F32), jax.ShapeDtypeStruct((b, 2 * da, t), F32),
                   jax.ShapeDtypeStruct((b, n_f, t), F32)],
        compiler_params=_params("arbitrary", "arbitrary"),
    )(x, shift, scale, g.reshape(1, 1, d), wq, wkvt, wflt, bd, gq, gk_col, bf_col)


def _proj_nsa_kernel(x_ref, shift_ref, scale_ref, g_ref, w_ref, bd_ref, gq_ref, gs_ref, gw_ref, bg_ref,
                     q_ref, cmp_ref, slc_ref, win_ref, gates_ref, *, dq, dkv):
    x = x_ref[...]
    bb, tt, d = x.shape
    m = bb * tt
    h = _adanorm(x, g_ref[...], scale_ref[...], shift_ref[...]).reshape(m, d).astype(BF16)
    bd = bd_ref[...]
    sc = HEAD_DIM ** -0.5
    for c in range(dq // dkv):
        qc = _head_norm(_dot(h, w_ref[:, c * dkv:(c + 1) * dkv]), bd) * gq_ref[...]
        q_ref[:, :, c * dkv:(c + 1) * dkv] = (qc * sc).reshape(bb, tt, dkv).astype(BF16)
    o = dq
    cmp_ref[...] = _dot(h, w_ref[:, o:o + 2 * dkv]).reshape(bb, tt, 2 * dkv)
    o += 2 * dkv
    ks = _head_norm(_dot(h, w_ref[:, o:o + dkv]), bd) * gs_ref[...]
    slc_ref[:, :, 0:dkv] = ks.reshape(bb, tt, dkv)
    slc_ref[:, :, dkv:2 * dkv] = _dot(h, w_ref[:, o + dkv:o + 2 * dkv]).reshape(bb, tt, dkv)
    o += 2 * dkv
    kw = _head_norm(_dot(h, w_ref[:, o:o + dkv]), bd) * gw_ref[...]
    win_ref[:, :, 0:dkv] = kw.reshape(bb, tt, dkv)
    win_ref[:, :, dkv:2 * dkv] = _dot(h, w_ref[:, o + dkv:o + 2 * dkv]).reshape(bb, tt, dkv)
    o += 2 * dkv
    gates_ref[...] = jax.nn.sigmoid(_dot(h, w_ref[:, o:o + LANES]) + bg_ref[...]).reshape(bb, tt, LANES)


def _proj_nsa(x, mod3, g, w, bd, gq, gs, gw, bg, dq, bb, tt):
    b, t, d = x.shape
    dkv = bd.shape[0]
    shift, scale, _ = mod3
    mspec = pl.BlockSpec((bb, 1, d), lambda i, j: (i, 0, 0))
    xspec = pl.BlockSpec((bb, tt, d), lambda i, j: (i, j, 0))
    row = lambda width: pl.BlockSpec((1, width), lambda i, j: (0, 0))
    ospec = lambda width: pl.BlockSpec((bb, tt, width), lambda i, j: (i, j, 0))
    return pl.pallas_call(
        functools.partial(_proj_nsa_kernel, dq=dq, dkv=dkv), name="proj_nsa",
        grid=(b // bb, t // tt),
        in_specs=[xspec, mspec, mspec, pl.BlockSpec((1, 1, d), lambda i, j: (0, 0, 0)),
                  pl.BlockSpec(w.shape, lambda i, j: (0, 0)), pl.BlockSpec(bd.shape, lambda i, j: (0, 0)),
                  row(dkv), row(dkv), row(dkv), row(LANES)],
        out_specs=[ospec(dq), ospec(2 * dkv), ospec(2 * dkv), ospec(2 * dkv), ospec(LANES)],
        out_shape=[jax.ShapeDtypeStruct((b, t, dq), BF16), jax.ShapeDtypeStruct((b, t, 2 * dkv), F32),
                   jax.ShapeDtypeStruct((b, t, 2 * dkv), F32), jax.ShapeDtypeStruct((b, t, 2 * dkv), F32),
                   jax.ShapeDtypeStruct((b, t, LANES), F32)],
        compiler_params=_params("arbitrary", "arbitrary"),
    )(x, shift, scale, g.reshape(1, 1, d), w, bd, gq, gs, gw, bg)


def _proj_nsa_t_kernel(x_ref, shift_ref, scale_ref, g_ref, wq_ref, wcmp_ref, wkvt_ref, wg_ref, bd_ref, gq_ref, gs_ref,
                       gw_ref, bg_ref, q_ref, cmp_ref, slct_ref, wint_ref, gates_ref, *, dq, dkv):
    x = x_ref[0]
    h = _adanorm(x, g_ref[0], scale_ref[0], shift_ref[0]).astype(BF16)
    bd = bd_ref[...]
    sc = HEAD_DIM ** -0.5
    per = dkv // HEAD_DIM
    for c in range(dq // dkv):
        qc = _head_norm(_dot(h, wq_ref[:, c * dkv:(c + 1) * dkv]), bd) * (gq_ref[...] * sc)
        for hh in range(per):
            q_ref[0, c * per + hh] = qc[:, hh * HEAD_DIM:(hh + 1) * HEAD_DIM].astype(BF16)
    cmp_ref[0] = _dot(h, wcmp_ref[...])
    slct_ref[0, 0:dkv, :] = _head_norm_t(_dot_nt(wkvt_ref[0:dkv, :], h), bd) * gs_ref[...]
    slct_ref[0, dkv:2 * dkv, :] = _dot_nt(wkvt_ref[dkv:2 * dkv, :], h)
    wint_ref[0, 0:dkv, :] = _head_norm_t(_dot_nt(wkvt_ref[2 * dkv:3 * dkv, :], h), bd) * gw_ref[...]
    wint_ref[0, dkv:2 * dkv, :] = _dot_nt(wkvt_ref[3 * dkv:4 * dkv, :], h)
    gates_ref[0] = jax.nn.sigmoid(_dot(h, wg_ref[...]) + bg_ref[...])


def _proj_nsa_t(x, mod3, g, wq, wcmp, wkvt, wg, bd, gq, gs_col, gw_col, bg, tt):
    b, t, d = x.shape
    dkv = bd.shape[0]
    dq = wq.shape[1]
    n_h = dq // HEAD_DIM
    shift, scale, _ = mod3
    mspec = pl.BlockSpec((1, 1, d), lambda i, j: (i, 0, 0))
    const = lambda a: pl.BlockSpec(a.shape, lambda i, j: (0,) * a.ndim)
    tspec = lambda rows: pl.BlockSpec((1, rows, tt), lambda i, j: (i, 0, j))
    rspec = lambda width: pl.BlockSpec((1, tt, width), lambda i, j: (i, j, 0))
    return pl.pallas_call(
        functools.partial(_proj_nsa_t_kernel, dq=dq, dkv=dkv), name="proj_nsa_t",
        grid=(b, t // tt),
        in_specs=[rspec(d), mspec, mspec, pl.BlockSpec((1, 1, d), lambda i, j: (0, 0, 0)),
                  const(wq), const(wcmp), const(wkvt), const(wg), const(bd), const(gq), const(gs_col), const(gw_col),
                  const(bg)],
        out_specs=[pl.BlockSpec((1, n_h, tt, HEAD_DIM), lambda i, j: (i, 0, j, 0)), rspec(2 * dkv),
                   tspec(2 * dkv), tspec(2 * dkv), rspec(LANES)],
        out_shape=[jax.ShapeDtypeStruct((b, n_h, t, HEAD_DIM), BF16), jax.ShapeDtypeStruct((b, t, 2 * dkv), F32),
                   jax.ShapeDtypeStruct((b, 2 * dkv, t), F32), jax.ShapeDtypeStruct((b, 2 * dkv, t), F32),
                   jax.ShapeDtypeStruct((b, t, LANES), F32)],
        compiler_params=_params("arbitrary", "arbitrary"),
    )(x, shift, scale, g.reshape(1, 1, d), wq, wcmp, wkvt, wg, bd, gq, gs_col, gw_col, bg)


def _block_diag_rows(q, n_heads):
    q = q.astype(F32)
    head = _iota(q.shape, 1) // HEAD_DIM
    return jnp.concatenate([jnp.where(head == h, q, 0.0) for h in range(n_heads)], axis=0).astype(BF16)


def _gather_diag(acc, tq, n_heads):
    head = _iota((tq, acc.shape[1]), 1) // HEAD_DIM
    out = jnp.zeros((tq, acc.shape[1]), F32)
    for h in range(n_heads):
        out = out + jnp.where(head == h, acc[h * tq:(h + 1) * tq, :], 0.0)
    return out


def _row_query(shape, tq):
    return _iota(shape, 0) % tq


def _running_sum_matrix(tk, prefix):
    j = _iota((tk, 2 * tk), 0)
    s = _iota((tk, 2 * tk), 1)
    tri = (j <= s) if prefix else (j > s)
    return jnp.where((s >= tk) | tri, 1.0, 0.0).astype(BF16)


def _later_matrix(tk):
    return jnp.where(_iota((tk, tk), 0) > _iota((tk, tk), 1), 1.0, 0.0).astype(BF16)


def _across_lane_tiles(x, op):
    out = x[:, 0:LANES]
    for c in range(1, x.shape[1] // LANES):
        out = op(out, x[:, c * LANES:(c + 1) * LANES])
    return out


def _with_row_vector(x, v, op):
    vb = jnp.broadcast_to(v, (x.shape[0], LANES))
    tiles = [op(x[:, c * LANES:(c + 1) * LANES], vb) for c in range(x.shape[1] // LANES)]
    return tiles[0] if len(tiles) == 1 else jnp.concatenate(tiles, axis=1)


def _exp_minus(s, m):
    return jnp.exp(_with_row_vector(s, m, jnp.subtract))


def _row_max(x):
    return jnp.max(_across_lane_tiles(x, jnp.maximum), axis=1, keepdims=True)


def _row_sum(x):
    return jnp.sum(_across_lane_tiles(x, jnp.add), axis=1, keepdims=True)


def _log_keep(z, mask):
    lk = -(jnp.maximum(z, 0.0) + jnp.log(1.0 + jnp.exp(-jnp.abs(z))))
    return lk if mask is None else jnp.where(mask, lk, 0.0)


def _sb_weights(z, um, carry, mask):
    lk = _log_keep(z, mask)
    blk = um.shape[0]
    nb = z.shape[1] // blk
    rest = [None] * nb
    for b in reversed(range(nb)):
        lkb = lk[:, b * blk:(b + 1) * blk]
        rest[b] = _with_row_vector(_dot_x2(lkb, um), carry, jnp.add)
        carry = carry + _row_sum(lkb)
    w = jnp.exp(z + lk + (rest[0] if nb == 1 else jnp.concatenate(rest, axis=1)))
    if mask is not None:
        w = jnp.where(mask, w, 0.0)
    return w.astype(BF16), carry


def _softmax_first(s, pv, m_ref, l_ref, acc_ref):
    m = _row_max(s)
    p = _exp_minus(s, m)
    l_ref[...] = _row_sum(p)
    acc_ref[...] = pv(p.astype(BF16))
    m_ref[...] = m


def _softmax_step(s, pv, m_ref, l_ref, acc_ref, mask=None):
    m_prev = m_ref[...]
    m_new = jnp.maximum(m_prev, _row_max(s))
    p = _exp_minus(s, m_new)
    if mask is not None:
        p = jnp.where(mask, p, 0.0)
    alpha = jnp.exp(m_prev - m_new)
    l_ref[...] = alpha * l_ref[...] + _row_sum(p)
    acc_ref[...] = alpha * acc_ref[...] + pv(p.astype(BF16))
    m_ref[...] = m_new


def _softmax_chains(scores, vt, refs, first):
    if first:
        ms = [_row_max(s) for s in scores]
    else:
        prev = [(m[...], a[...]) for m, a in refs]
        ms = [jnp.maximum(pm, _row_max(s)) for (pm, _), s in zip(prev, scores)]
    ps = [_exp_minus(s, m) for s, m in zip(scores, ms)]
    pvs = [_dot_nt(p.astype(BF16), vt) for p in ps]
    for i, (m_ref, acc_ref) in enumerate(refs):
        acc_ref[...] = pvs[i] if first else jnp.exp(prev[i][0] - ms[i]) * prev[i][1] + pvs[i]
        m_ref[...] = ms[i]


def _softmax_single_chains(scores, vt):
    ps = [_exp_minus(s, _row_max(s)) for s in scores]
    return [_dot_nt(p.astype(BF16), vt) for p in ps]


def _with_sum_row(vt):
    extra = jnp.where(_iota((SUM_ROWS, vt.shape[1]), 0) == 0, 1.0, 0.0).astype(BF16)
    return jnp.concatenate([vt, extra], axis=0)


def _normalized(acc):
    return acc[:, 0:HEAD_DIM] / jnp.maximum(acc[:, HEAD_DIM:HEAD_DIM + 1], 1e-30)


def _softmax_init(m_ref, l_ref, acc_ref):
    m_ref[...] = jnp.full(m_ref.shape, NEG, F32)
    l_ref[...] = jnp.zeros(l_ref.shape, F32)
    acc_ref[...] = jnp.zeros(acc_ref.shape, F32)


def _softmax_out(l_ref, acc_ref):
    return acc_ref[...] / jnp.maximum(l_ref[...], 1e-30)


def _softmax_scratch(r, width):
    return [pltpu.VMEM((r, 1), F32), pltpu.VMEM((r, 1), F32), pltpu.VMEM((r, width), F32)]


def _rank_select(imp, t, ns):
    blk = _iota(imp.shape, 1)
    cur = t // SLC_BLOCK
    valid = (blk * SLC_BLOCK <= t) & (blk < ns)
    forced = (blk == 0) | (blk == cur) | (blk == cur - 1)
    score = jnp.where(valid, jnp.where(forced, FORCE_SCORE, imp), -jnp.inf)
    cnt = jnp.zeros(imp.shape, F32)
    for j in range(ns):
        col = score[:, j:j + 1]
        cnt = cnt + jnp.where(col > score, 1.0, 0.0) + jnp.where(col == score, jnp.where(blk > j, 1.0, 0.0), 0.0)
    return jnp.where(valid & (cnt < N_SELECT), 1.0, 0.0)


def _rank_select_t(imp, q0, ns):
    rows = -(-ns // 8) * 8
    imp_t = imp.T[0:rows, :]
    blk = _iota(imp_t.shape, 0)
    t = q0 + _iota(imp_t.shape, 1)
    cur = t // SLC_BLOCK
    valid = (blk * SLC_BLOCK <= t) & (blk < ns)
    forced = (blk == 0) | (blk == cur) | (blk == cur - 1)
    score = jnp.where(valid, jnp.where(forced, FORCE_SCORE, imp_t), -jnp.inf)
    cnt = jnp.zeros(imp_t.shape, F32)
    for j in range(ns):
        row = score[j:j + 1, :]
        cnt = cnt + jnp.where(row > score, 1.0, 0.0) + jnp.where(row == score, jnp.where(blk > j, 1.0, 0.0), 0.0)
    sel_t = jnp.where(valid & (cnt < N_SELECT), 1.0, 0.0)
    return jnp.concatenate([sel_t, jnp.zeros((imp.shape[1] - rows, imp.shape[0]), F32)], axis=0).T


def _expand_blocks(sel, n_keys):
    nsp = sel.shape[1]
    blk = _iota((nsp, n_keys), 0)
    key = _iota((nsp, n_keys), 1)
    return _dot(sel, jnp.where(key // SLC_BLOCK == blk, 1.0, 0.0).astype(BF16))


def _sb_prompt_kernel(q_ref, kt_ref, vt_ref, o_ref, ktb, vtb, acc, car, *, tq, nh):
    qi = pl.program_id(2)

    @pl.when(qi == 0)
    def _():
        ktb[...] = kt_ref[0].astype(BF16)
        vtb[...] = vt_ref[0].astype(BF16)

    um = _later_matrix(tq)
    q0 = pl.multiple_of(qi * tq, tq)
    mask = _iota((tq, tq), 1) < _iota((tq, tq), 0)

    qs = [q_ref[0, h] for h in range(nh)]

    def tile(k0, first):
        kt = ktb[:, pl.ds(k0, tq)]
        vt = vtb[:, pl.ds(k0, tq)]
        hrows = lambda a, h: a[h * HEAD_DIM:(h + 1) * HEAD_DIM]
        prev = [(jnp.zeros((tq, 1), F32), None) if first else (car[h], acc[h]) for h in range(nh)]
        zs = [_dot(qs[h], hrows(kt, h)) for h in range(nh)]
        wc = [_sb_weights(zs[h], um, prev[h][0], mask if first else None) for h in range(nh)]
        pvs = [_dot_nt(wc[h][0], hrows(vt, h)) for h in range(nh)]
        for h in range(nh):
            acc[h] = pvs[h] if first else prev[h][1] + pvs[h]
            car[h] = wc[h][1]
        live = wc[0][1]
        for h in range(1, nh):
            live = jnp.maximum(live, wc[h][1])
        return jnp.max(live) > DEAD_LOG

    alive = tile(q0, True)

    def body(state):
        it, _ = state
        return it + 1, tile(pl.multiple_of((qi - 1 - it) * tq, tq), False)

    lax.while_loop(lambda state: (state[0] < qi) & state[1], body, (jnp.int32(0), alive))
    o_ref[0] = jnp.concatenate([acc[h] for h in range(nh)], axis=1).astype(BF16)


def _sb_prompt(q, kvt, tq, nh):
    b, n_h, s, _ = q.shape
    c = nh * HEAD_DIM
    ng = n_h // nh
    return pl.pallas_call(
        functools.partial(_sb_prompt_kernel, tq=tq, nh=nh), name="sb_prompt",
        grid=(b, ng, s // tq),
        in_specs=[pl.BlockSpec((1, nh, tq, HEAD_DIM), lambda i, g, j: (i, g, j, 0)),
                  pl.BlockSpec((1, c, s), lambda i, g, j: (i, g, 0)),
                  pl.BlockSpec((1, c, s), lambda i, g, j: (i, ng + g, 0))],
        out_specs=pl.BlockSpec((1, tq, c), lambda i, g, j: (i, j, g)),
        out_shape=jax.ShapeDtypeStruct((b, s, n_h * HEAD_DIM), BF16),
        scratch_shapes=[pltpu.VMEM((c, s), BF16), pltpu.VMEM((c, s), BF16),
                        pltpu.VMEM((nh, tq, HEAD_DIM), F32), pltpu.VMEM((nh, tq, 1), F32)],
        compiler_params=_params("arbitrary", "arbitrary", "arbitrary"),
    )(q, kvt, kvt)


def _fox_prompt_kernel(q_ref, kt_ref, vt_ref, lft_ref, o_ref, ktb, vtb, ncum, knorm, m_ref, l_ref, acc, *, tq, nh):
    g = pl.program_id(1)
    qi = pl.program_id(2)
    s_len = kt_ref.shape[2]

    @pl.when(qi == 0)
    def _():
        kb = kt_ref[0].astype(BF16)
        ktb[...] = kb
        vtb[...] = vt_ref[0].astype(BF16)
        kf = kb.astype(F32)
        for h in range(nh):
            kh = kf[h * HEAD_DIM:(h + 1) * HEAD_DIM]
            knorm[h] = jnp.full(knorm.shape[1:], jnp.sqrt(jnp.max(jnp.sum(kh * kh, axis=0, keepdims=True))), F32)
        um = _running_sum_matrix(tq, True)
        run = jnp.zeros((lft_ref.shape[1], tq), F32)
        for ch in range(s_len // tq):
            sl = slice(ch * tq, (ch + 1) * tq)
            rs = _dot_x3(lft_ref[0, :, sl], um)
            ncum[:, sl] = -(rs[:, :tq] + run)
            run = run + rs[:, tq:]

    q0 = pl.multiple_of(qi * tq, tq)
    mask = _iota((tq, tq), 1) <= _iota((tq, tq), 0)

    qs = [q_ref[0, h] for h in range(nh)]

    def tile(k0, first):
        kt = ktb[:, pl.ds(k0, tq)]
        vt = vtb[:, pl.ds(k0, tq)]
        bias = [ncum[pl.ds(g * nh + h, 1), pl.ds(k0, tq)] for h in range(nh)]
        hrows = lambda a, h: a[h * HEAD_DIM:(h + 1) * HEAD_DIM]
        if not first:
            prev = [(m_ref[h], l_ref[h], acc[h]) for h in range(nh)]
        ss = [_dot(qs[h], hrows(kt, h)) + bias[h] for h in range(nh)]
        if first:
            ss = [jnp.where(mask, s, NEG) for s in ss]
            ms = [_row_max(s) for s in ss]
        else:
            ms = [jnp.maximum(prev[h][0], _row_max(ss[h])) for h in range(nh)]
        ps = [_exp_minus(s, m) for s, m in zip(ss, ms)]
        pvs = [_dot_nt(ps[h].astype(BF16), hrows(vt, h)) for h in range(nh)]
        sums = [_row_sum(p) for p in ps]
        for h in range(nh):
            if first:
                l_ref[h] = sums[h]
                acc[h] = pvs[h]
            else:
                alpha = jnp.exp(prev[h][0] - ms[h])
                l_ref[h] = alpha * prev[h][1] + sums[h]
                acc[h] = alpha * prev[h][2] + pvs[h]
            m_ref[h] = ms[h]
        live = None
        for h in range(nh):
            bound = qn[h] * knorm[h][0:1, 0:1] + (jnp.min(bias[h], axis=1, keepdims=True) - ms[h])
            live = bound if live is None else jnp.maximum(live, bound)
        return jnp.max(live) > DEAD_LOG

    qn = [jnp.sqrt(jnp.sum(jnp.square(q.astype(F32)), axis=1, keepdims=True)) * 1.001 for q in qs]
    alive = tile(q0, True)

    def body(state):
        it, _ = state
        return it + 1, tile(pl.multiple_of((qi - 1 - it) * tq, tq), False)

    lax.while_loop(lambda state: (state[0] < qi) & state[1], body, (jnp.int32(0), alive))
    o_ref[0] = jnp.concatenate([_softmax_out(l_ref.at[h], acc.at[h]) for h in range(nh)], axis=1).astype(BF16)


def _fox_prompt(q, kvt, lft, tq, nh):
    b, n_h, s, _ = q.shape
    c = nh * HEAD_DIM
    ng = n_h // nh
    n_f = lft.shape[1]
    return pl.pallas_call(
        functools.partial(_fox_prompt_kernel, tq=tq, nh=nh), name="fox_prompt",
        grid=(b, ng, s // tq),
        in_specs=[pl.BlockSpec((1, nh, tq, HEAD_DIM), lambda i, g, j: (i, g, j, 0)),
                  pl.BlockSpec((1, c, s), lambda i, g, j: (i, g, 0)),
                  pl.BlockSpec((1, c, s), lambda i, g, j: (i, ng + g, 0)),
                  pl.BlockSpec((1, n_f, s), lambda i, g, j: (i, 0, 0))],
        out_specs=pl.BlockSpec((1, tq, c), lambda i, g, j: (i, j, g)),
        out_shape=jax.ShapeDtypeStruct((b, s, n_h * HEAD_DIM), BF16),
        scratch_shapes=[pltpu.VMEM((c, s), BF16), pltpu.VMEM((c, s), BF16), pltpu.VMEM((n_f, s), F32),
                        pltpu.VMEM((nh, 8, LANES), F32), pltpu.VMEM((nh, tq, 1), F32), pltpu.VMEM((nh, tq, 1), F32), pltpu.VMEM((nh, tq, HEAD_DIM), F32)],
        compiler_params=_params("arbitrary", "arbitrary", "arbitrary"),
    )(q, kvt, kvt, lft)


def _page_specs(n_pages, rows, npg):
    return [pl.BlockSpec((1, rows, PAGE), functools.partial(
        lambda i, s, pt, off: (pt[i, n_pages - 1 - s * npg - off], 0, 0), off=off)) for off in range(npg)]


def _sb_decode_kernel(pt_ref, q_ref, new_ref, *rest, tq, hg, npg):
    page_refs = rest[:npg]
    o_ref, qbd_ref, newp, acc, car = rest[npg:]
    st = pl.program_id(1)
    da = hg * HEAD_DIM
    r = hg * tq

    @pl.when(st == 0)
    def _():
        qbd_ref[...] = _block_diag_rows(q_ref[0], hg)
        newp[...] = jnp.zeros(newp.shape, F32)
        newp[0:tq, :] = new_ref[0]
        mask = _iota((r, PAGE), 1) < _row_query((r, PAGE), tq)
        w, c = _sb_weights(_dot_nt(qbd_ref[...], newp[:, 0:da].astype(BF16)), _later_matrix(PAGE),
                           jnp.zeros((r, 1), F32), mask)
        acc[...] = _dot(w, newp[:, da:2 * da].astype(BF16))
        car[...] = c

    kt = jnp.concatenate([pg[0, 0:da, :] for pg in page_refs[::-1]], axis=1).astype(BF16)
    vt = jnp.concatenate([pg[0, da:2 * da, :] for pg in page_refs[::-1]], axis=1).astype(BF16)
    w, c = _sb_weights(_dot(qbd_ref[...], kt), _later_matrix(2 * PAGE), car[...], None)
    acc[...] += _dot_nt(w, vt)
    car[...] = c

    @pl.when(st == pl.num_programs(1) - 1)
    def _():
        o_ref[0] = _gather_diag(acc[...], tq, hg).astype(BF16)


def _sb_decode(q, kv_new, cache_t, page_table, npg):
    b, tq, da = q.shape
    hg = da // HEAD_DIM
    n_pages = page_table.shape[1]
    r = hg * tq
    grid_spec = pltpu.PrefetchScalarGridSpec(
        num_scalar_prefetch=1,
        grid=(b, n_pages // npg),
        in_specs=[pl.BlockSpec((1, tq, da), lambda i, s, pt: (i, 0, 0)),
                  pl.BlockSpec((1, tq, 2 * da), lambda i, s, pt: (i, 0, 0))] + _page_specs(n_pages, 2 * da, npg),
        out_specs=pl.BlockSpec((1, tq, da), lambda i, s, pt: (i, 0, 0)),
        scratch_shapes=[pltpu.VMEM((r, da), BF16), pltpu.VMEM((PAGE, 2 * da), F32),
                        pltpu.VMEM((r, da), F32), pltpu.VMEM((r, 1), F32)],
    )
    return pl.pallas_call(
        functools.partial(_sb_decode_kernel, tq=tq, hg=hg, npg=npg), name="sb_decode",
        grid_spec=grid_spec,
        out_shape=jax.ShapeDtypeStruct((b, tq, da), BF16),
        compiler_params=_params("arbitrary", "arbitrary"),
    )(page_table, q, kv_new, *([cache_t] * npg))


def _fox_decode_kernel(pt_ref, q_ref, new_ref, lfnew_ref, *rest, tq, hg, npg):
    page_refs = rest[:npg]
    lf_refs = rest[npg:2 * npg]
    o_ref, qbd_ref, newp, lfp, run_ref, m_ref, l_ref, acc = rest[2 * npg:]
    st = pl.program_id(1)
    da = hg * HEAD_DIM
    r = hg * tq
    um = _running_sum_matrix(PAGE, False)

    def key_bias(lfts):
        rs = [_dot_x3(lft, um) for lft in lfts]
        run = run_ref[...]
        suf = [None] * len(lfts)
        for b in reversed(range(len(lfts))):
            suf[b] = rs[b][:, :PAGE] + run
            run = run + rs[b][:, PAGE:]
        run_ref[...] = run
        suf = suf[0] if len(suf) == 1 else jnp.concatenate(suf, axis=1)
        return jnp.concatenate([jnp.broadcast_to(suf[h:h + 1, :], (tq, suf.shape[1])) for h in range(hg)], axis=0)

    @pl.when(st == 0)
    def _():
        qbd_ref[...] = _block_diag_rows(q_ref[0], hg)
        newp[...] = jnp.zeros(newp.shape, F32)
        newp[0:tq, :] = new_ref[0]
        lfp[...] = jnp.zeros(lfp.shape, F32)
        lfp[0:tq, :] = lfnew_ref[0]
        run_ref[...] = jnp.zeros(run_ref.shape, F32)
        mask = _iota((r, PAGE), 1) <= _row_query((r, PAGE), tq)
        s = _dot_nt(qbd_ref[...], newp[:, 0:da].astype(BF16)) + key_bias([lfp[...].T[0:hg, :]])
        _softmax_first(jnp.where(mask, s, NEG), lambda p: _dot(p, newp[:, da:2 * da].astype(BF16)), m_ref, l_ref, acc)

    kt = jnp.concatenate([pg[0, 0:da, :] for pg in page_refs[::-1]], axis=1).astype(BF16)
    vt = jnp.concatenate([pg[0, da:2 * da, :] for pg in page_refs[::-1]], axis=1).astype(BF16)
    s = _dot(qbd_ref[...], kt) + key_bias([lf[0] for lf in lf_refs[::-1]])
    _softmax_step(s, lambda p: _dot_nt(p, vt), m_ref, l_ref, acc)

    @pl.when(st == pl.num_programs(1) - 1)
    def _():
        o_ref[0] = _gather_diag(_softmax_out(l_ref, acc), tq, hg).astype(BF16)


def _fox_decode(q, kv_new, lf_new, cache_t, cache_lf_t, page_table, npg):
    b, tq, da = q.shape
    hg = da // HEAD_DIM
    n_pages = page_table.shape[1]
    r = hg * tq
    assert cache_lf_t.shape[1] == hg
    grid_spec = pltpu.PrefetchScalarGridSpec(
        num_scalar_prefetch=1,
        grid=(b, n_pages // npg),
        in_specs=[pl.BlockSpec((1, tq, da), lambda i, s, pt: (i, 0, 0)),
                  pl.BlockSpec((1, tq, 2 * da), lambda i, s, pt: (i, 0, 0)),
                  pl.BlockSpec((1, tq, LANES), lambda i, s, pt: (i, 0, 0))]
        + _page_specs(n_pages, 2 * da, npg) + _page_specs(n_pages, hg, npg),
        out_specs=pl.BlockSpec((1, tq, da), lambda i, s, pt: (i, 0, 0)),
        scratch_shapes=[pltpu.VMEM((r, da), BF16), pltpu.VMEM((PAGE, 2 * da), F32),
                        pltpu.VMEM((PAGE, LANES), F32), pltpu.VMEM((hg, PAGE), F32)] + _softmax_scratch(r, da),
    )
    return pl.pallas_call(
        functools.partial(_fox_decode_kernel, tq=tq, hg=hg, npg=npg), name="fox_decode",
        grid_spec=grid_spec,
        out_shape=jax.ShapeDtypeStruct((b, tq, da), BF16),
        compiler_params=_params("arbitrary", "arbitrary"),
    )(page_table, q, kv_new, lf_new, *([cache_t] * npg), *([cache_lf_t] * npg))


def _compress(xb, bd_ref, pe_ref, n_chunk, dkv):
    first = [jnp.zeros((n_chunk, dkv), F32) for _ in range(2)]
    second = [jnp.zeros((n_chunk, dkv), F32) for _ in range(2)]
    for j in range(D_CMP):
        xj = jnp.concatenate([xb[lb, pl.ds(j, n_chunk, stride=D_CMP), :] for lb in range(xb.shape[0])], axis=1)
        for kv in range(2):
            xx = xj[:, kv * dkv:(kv + 1) * dkv]
            first[kv] += _dot((xx + pe_ref[kv, j:j + 1, :]).astype(BF16), bd_ref[kv, j])
            second[kv] += _dot((xx + pe_ref[kv, D_CMP + j:D_CMP + j + 1, :]).astype(BF16), bd_ref[kv, D_CMP + j])
    return [first[kv] + pltpu.roll(second[kv], n_chunk - 1, axis=0) for kv in range(2)]


def _compress_prompt_kernel(x_ref, bd_ref, pe_ref, bdn_ref, gk_ref, kct_ref, vct_ref, xb):
    n_chunk = x_ref.shape[1] // D_CMP
    dkv = bdn_ref.shape[0]
    for lb in range(xb.shape[0]):
        xb[lb] = x_ref[0, :, lb * LANES:(lb + 1) * LANES]
    kc, vc = _compress(xb, bd_ref, pe_ref, n_chunk, dkv)
    kct_ref[0] = (_head_norm(kc, bdn_ref[...]) * gk_ref[...]).T.astype(BF16)
    vct_ref[0] = vc.T.astype(BF16)


def _compress_prompt(cmp_kv, bd, pe, bdn, gk):
    b, s, w = cmp_kv.shape
    dkv = w // 2
    n_chunk = s // D_CMP
    const = lambda a: pl.BlockSpec(a.shape, lambda i: (0,) * a.ndim)
    ospec = pl.BlockSpec((1, dkv, n_chunk), lambda i: (i, 0, 0))
    return pl.pallas_call(
        _compress_prompt_kernel, name="compress_prompt",
        grid=(b,),
        in_specs=[pl.BlockSpec((1, s, w), lambda i: (i, 0, 0)), const(bd), const(pe), const(bdn), const(gk)],
        out_specs=[ospec, ospec],
        out_shape=[jax.ShapeDtypeStruct((b, dkv, n_chunk), BF16)] * 2,
        scratch_shapes=[pltpu.VMEM((w // LANES, s, LANES), F32)],
        compiler_params=_params("arbitrary"),
    )(cmp_kv, bd, pe, bdn, gk)


def _nsa_prompt_kernel(q_ref, kst_ref, vst_ref, kwt_ref, vwt_ref, kct_ref, vct_ref, gates_ref, band_ref, near_ref,
                       wadd_ref, ov_ref, o_ref, ksb, vsb, kwb, vwb, madd, m_ref, acc, ocmp, owin, *, tq, nz, ns):
    g = pl.program_id(1)
    qi = pl.program_id(2)
    s_len = kst_ref.shape[2]
    ncp = kct_ref.shape[2]
    wslab = WINDOW + tq

    @pl.when(qi == 0)
    def _():
        ksb[...] = kst_ref[0].astype(BF16)
        vsb[...] = _with_sum_row(vst_ref[0].astype(BF16))
        kwb[:, 0:WINDOW] = jnp.zeros((HEAD_DIM, WINDOW), BF16)
        vwb[:, 0:WINDOW] = jnp.zeros((HEAD_DIM + SUM_ROWS, WINDOW), BF16)
        kwb[:, WINDOW:] = kwt_ref[0].astype(BF16)
        vwb[:, WINDOW:] = _with_sum_row(vwt_ref[0].astype(BF16))

    q0 = pl.multiple_of(qi * tq, tq)
    zrows = lambda z: slice(z * tq, (z + 1) * tq)

    shift = (qi * (tq // D_CMP) + ncp - BAND_BACK) % ncp
    t_c = q0 + _iota((tq, ncp), 0)
    c_c = _iota((tq, ncp), 1)
    cmask = (t_c >= c_c * D_CMP + (L_CMP - 1)) & (c_c < ncp - 1)
    qs = [q_ref[0, z] for z in range(nz)]
    kct = kct_ref[0]
    vct = vct_ref[0]
    ss = [jnp.where(cmask, _dot(qs[z], kct) + pltpu.roll(band_ref[z], shift, axis=1), NEG) for z in range(nz)]
    ps = [jnp.where(cmask, _exp_minus(s, _row_max(s)), 0.0) for s in ss]
    ps = [_with_row_vector(p, jnp.maximum(_row_sum(p), 1e-30), jnp.divide) for p in ps]
    oc = [_dot_nt(p.astype(BF16), vct) for p in ps]
    pz = ps[0]
    for z in range(1, nz):
        pz = pz + ps[z]
    imp = _dot_x2(pz, ov_ref[...])
    sel = _rank_select_t(imp, q0, ns)
    for z in range(nz):
        ocmp[zrows(z), :] = oc[z]
    madd[...] = (_expand_blocks(sel.astype(BF16), s_len) - 1.0) * (-NEG)
    refs = [(m_ref.at[zrows(z)], acc.at[zrows(z)]) for z in range(nz)]

    def slc_tile(k0, tk, extra, first):
        kt = ksb[:, pl.ds(k0, tk)]
        vt = vsb[:, pl.ds(k0, tk)]
        ma = madd[:, pl.ds(k0, tk)]
        scores = [_dot(qs[z], kt) + (ma if extra is None else ma + extra(z)) for z in range(nz)]
        _softmax_chains(scores, vt, refs, first)

    causal = jnp.where(_iota((tq, tq), 1) <= _iota((tq, tq), 0), 0.0, NEG)

    @pl.when(qi == 0)
    def _():
        slc_tile(0, tq, lambda z: near_ref[z, :, tq:2 * tq] + causal, True)

    @pl.when(qi >= 1)
    def _():
        edge = jnp.concatenate([jnp.zeros((tq, tq), F32), causal], axis=1)
        slc_tile(pl.multiple_of(q0 - tq, tq), 2 * tq, lambda z: near_ref[z] + edge, True)

    n_far = jnp.maximum(qi - 1, 0)
    per_far = FAR_TILE // tq

    def far_body(it, _):
        slc_tile(pl.multiple_of(it * FAR_TILE, FAR_TILE), FAR_TILE, None, False)
        return 0

    lax.fori_loop(0, n_far // per_far, far_body, 0)

    def rem_body(it, _):
        slc_tile(pl.multiple_of(((n_far // per_far) * per_far + it) * tq, tq), tq, None, False)
        return 0

    lax.fori_loop(0, n_far % per_far, rem_body, 0)

    kslab = kwb[:, pl.ds(q0, wslab)]
    vslab = vwb[:, pl.ds(q0, wslab)]
    wadd = wadd_ref[...] + jnp.where(_iota((tq, wslab), 1) >= WINDOW - q0, 0.0, NEG)
    ss = [_dot(qs[z], kslab) + wadd for z in range(nz)]
    ss = [jnp.concatenate([s[:, :wslab - 2 * tq], s[:, wslab - 2 * tq:] + near_ref[z]], axis=1)
          for z, s in enumerate(ss)]
    for z, o in enumerate(_softmax_single_chains(ss, vslab)):
        owin[zrows(z), :] = _normalized(o)

    n_heads = nz * pl.num_programs(1)
    src = _iota((LANES, LANES), 0)
    dst = _iota((LANES, LANES), 1)
    pick = jnp.where((dst < 3 * nz) & (src == (dst // nz) * n_heads + g * nz + dst % nz), 1.0, 0.0).astype(BF16)
    gsel = _dot_x3(gates_ref[0], pick)
    outs = []
    for z in range(nz):
        gate = lambda br: gsel[:, br * nz + z:br * nz + z + 1]
        outs.append(gate(0) * ocmp[zrows(z), :] + gate(1) * _normalized(acc[zrows(z), :]) + gate(2) * owin[zrows(z), :])
    o_ref[0] = jnp.concatenate(outs, axis=1).astype(BF16)


def _window_mask(tq):
    i = np.arange(tq)[:, None]
    c = np.arange(WINDOW + tq)[None, :]
    return jnp.asarray(np.where((c > i) & (c <= i + WINDOW), 0.0, NEG).astype(np.float32))


def _nsa_prompt(q, slct, wint, kct, vct, gates, band, near, tq, n_kv):
    b, n_heads, s, _ = q.shape
    nz = n_heads // n_kv
    r = nz * tq
    ncp = kct.shape[2]
    ns = -(-s // SLC_BLOCK)
    nsp = -(-ns // LANES) * LANES
    ov = _overlap_matrix(ncp, nsp)
    wadd = _window_mask(tq)
    kv_spec = lambda off: pl.BlockSpec((1, HEAD_DIM, s), lambda i, g, j: (i, off + g, 0))
    c_spec = pl.BlockSpec((1, HEAD_DIM, ncp), lambda i, g, j: (i, g, 0))
    tab_spec = lambda a: pl.BlockSpec((nz,) + a.shape[1:], lambda i, g, j: (g, 0, 0))
    return pl.pallas_call(
        functools.partial(_nsa_prompt_kernel, tq=tq, nz=nz, ns=ns), name="nsa_prompt",
        grid=(b, n_kv, s // tq),
        in_specs=[pl.BlockSpec((1, nz, tq, HEAD_DIM), lambda i, g, j: (i, g, j, 0)),
                  kv_spec(0), kv_spec(n_kv), kv_spec(0), kv_spec(n_kv), c_spec, c_spec,
                  pl.BlockSpec((1, tq, LANES), lambda i, g, j: (i, j, 0)),
                  tab_spec(band), tab_spec(near),
                  pl.BlockSpec(wadd.shape, lambda i, g, j: (0, 0)),
                  pl.BlockSpec(ov.shape, lambda i, g, j: (0, 0))],
        out_specs=pl.BlockSpec((1, tq, nz * HEAD_DIM), lambda i, g, j: (i, j, g)),
        out_shape=jax.ShapeDtypeStruct((b, s, n_heads * HEAD_DIM), BF16),
        scratch_shapes=[pltpu.VMEM((HEAD_DIM, s), BF16), pltpu.VMEM((HEAD_DIM + SUM_ROWS, s), BF16),
                        pltpu.VMEM((HEAD_DIM, WINDOW + s), BF16), pltpu.VMEM((HEAD_DIM + SUM_ROWS, WINDOW + s), BF16),
                        pltpu.VMEM((tq, s), F32), pltpu.VMEM((r, 1), F32), pltpu.VMEM((r, HEAD_DIM + SUM_ROWS), F32),
                        pltpu.VMEM((r, HEAD_DIM), F32), pltpu.VMEM((r, HEAD_DIM), F32)],
        compiler_params=_params("arbitrary", "arbitrary", "arbitrary"),
    )(q, slct, slct, wint, wint, kct, vct, gates, band, near, wadd, ov)


def _cmp_select(q, kcmp, vcmp, bias, ov, q0, tq, nc, ns, n_kv, n_rep):
    ncp = kcmp.shape[0]
    qbd = _nsa_qbd(q, n_kv, n_rep)
    r = qbd.shape[0]
    t = q0 + _row_query((r, ncp), tq)
    cidx = _iota((r, ncp), 1)
    mask = (t - (cidx * D_CMP + L_CMP - 1) >= 0) & (cidx < nc)
    s = jnp.where(mask, _dot_nt(qbd, kcmp) + bias, NEG)
    p = jnp.where(mask, jnp.exp(s - jnp.max(s, axis=1, keepdims=True)), 0.0)
    p = p / jnp.maximum(jnp.sum(p, axis=1, keepdims=True), 1e-30)
    o_cmp = _nsa_gather(_dot(p.astype(BF16), vcmp), tq, n_kv, n_rep)
    rz = n_kv * tq
    pz = p[0:rz]
    for z in range(1, n_rep):
        pz = pz + p[z * rz:(z + 1) * rz]
    imp = _dot_x2(pz, ov)
    return o_cmp, _rank_select(imp, q0 + _row_query(imp.shape, tq), ns)


def _overlap_matrix(ncp, nsp):
    ci = np.arange(ncp)[:, None]
    bj = np.arange(nsp)[None, :]
    ov = (ci * D_CMP <= bj * SLC_BLOCK + SLC_BLOCK - 1) & (ci * D_CMP + L_CMP - 1 >= bj * SLC_BLOCK)
    return jnp.asarray(ov.astype(np.float32), BF16)


def _cmp_decode_kernel(pt_ref, q_ref, bd_ref, pe_ref, bdn_ref, gk_ref, bias_ref, ov_ref, *rest,
                       tq, n_kv, n_rep, ns, npg, past):
    page_refs = rest[:npg]
    ocmp_ref, sel_ref, xbuf = rest[npg:]
    st = pl.program_id(1)
    n_steps = pl.num_programs(1)
    dkv = n_kv * HEAD_DIM
    n_chunk = past // D_CMP
    for off, pg in enumerate(page_refs):
        pos = (n_steps - 1 - st) * npg + (npg - 1 - off)
        rows = pg[0].T
        for lb in range(xbuf.shape[0]):
            xbuf[lb, pl.ds(pl.multiple_of(pos * PAGE, PAGE), PAGE), :] = rows[:, lb * LANES:(lb + 1) * LANES]

    @pl.when(st == n_steps - 1)
    def _():
        kc, vc = _compress(xbuf, bd_ref, pe_ref, n_chunk, dkv)
        kcb = (_head_norm(kc, bdn_ref[...]) * gk_ref[...]).astype(BF16)
        bias = bias_ref[...].reshape(n_kv * n_rep * tq, n_chunk)
        o_cmp, sel = _cmp_select(q_ref[0], kcb, vc.astype(BF16), bias, ov_ref[...], past, tq,
                                 n_chunk - 1, ns, n_kv, n_rep)
        ocmp_ref[0] = o_cmp
        sel_ref[0] = sel.astype(BF16)


def _cmp_decode(q, cache_t, page_table, bd, pe, bdn, gk, bias, n_kv, npg):
    b, tq, dq = q.shape
    dkv = n_kv * HEAD_DIM
    n_rep = dq // dkv
    n_pages = page_table.shape[1]
    past = n_pages * PAGE
    n_chunk = (past + tq) // D_CMP
    assert n_chunk == past // D_CMP
    ns = -(-(past + tq) // SLC_BLOCK)
    nsp = -(-ns // LANES) * LANES
    ov = _overlap_matrix(n_chunk, nsp)
    const = lambda a: pl.BlockSpec(a.shape, lambda i, s, pt: (0,) * a.ndim)
    grid_spec = pltpu.PrefetchScalarGridSpec(
        num_scalar_prefetch=1,
        grid=(b, n_pages // npg),
        in_specs=[pl.BlockSpec((1, tq, dq), lambda i, s, pt: (i, 0, 0)),
                  const(bd), const(pe), const(bdn), const(gk), const(bias), const(ov)]
        + _page_specs(n_pages, 2 * dkv, npg),
        out_specs=[pl.BlockSpec((1, tq, dq), lambda i, s, pt: (i, 0, 0)),
                   pl.BlockSpec((1, n_kv * tq, nsp), lambda i, s, pt: (i, 0, 0))],
        scratch_shapes=[pltpu.VMEM((2 * dkv // LANES, past, LANES), F32)],
    )
    return pl.pallas_call(
        functools.partial(_cmp_decode_kernel, tq=tq, n_kv=n_kv, n_rep=n_rep, ns=ns, npg=npg, past=past),
        name="cmp_decode",
        grid_spec=grid_spec,
        out_shape=[jax.ShapeDtypeStruct((b, tq, dq), F32), jax.ShapeDtypeStruct((b, n_kv * tq, nsp), BF16)],
        compiler_params=_params("arbitrary", "arbitrary"),
    )(page_table, q, bd, pe, bdn, gk, bias, ov, *([cache_t] * npg))


def _nsa_qbd(q, n_kv, n_rep):
    dkv = n_kv * HEAD_DIM
    return jnp.concatenate([_block_diag_rows(q[:, z * dkv:(z + 1) * dkv], n_kv) for z in range(n_rep)], axis=0)


def _nsa_gather(o_full, tq, n_kv, n_rep):
    rz = n_kv * tq
    return jnp.concatenate([_gather_diag(o_full[z * rz:(z + 1) * rz], tq, n_kv) for z in range(n_rep)], axis=1)


def _merge(gates, ege_ref, o_cmp, o_slc, o_win):
    g = [_dot_x2(gates, ege_ref[br]) for br in range(3)]
    return g[0] * o_cmp + g[1] * o_slc + g[2] * o_win


def _slc_decode_kernel(pt_ref, q_ref, slcnew_ref, winnew_ref, state_ref, sel_ref, gates_ref, ocmp_ref,
                       t0_ref, t1_ref, ege_ref, *rest, tq, n_kv, n_rep, npg, past):
    page_refs = rest[:npg]
    o_ref, qbd_ref, newp, kmask, m_ref, l_ref, acc = rest[npg:]
    st = pl.program_id(1)
    n_steps = pl.num_programs(1)
    dkv = n_kv * HEAD_DIM
    r = n_rep * n_kv * tq
    i_row = _row_query((r, PAGE), tq)
    j_col = _iota((r, PAGE), 1)

    def sel_mask(k0, width=PAGE):
        return jnp.concatenate([kmask[:, pl.ds(k0, width)]] * n_rep, axis=0) > 0.5

    def step(s, mask, pv):
        _softmax_step(jnp.where(mask, s, NEG) if mask is not None else s, pv, m_ref, l_ref, acc, mask)

    def new_tile(bias, mask):
        k = newp[:, 0:dkv].astype(BF16)
        v = newp[:, dkv:2 * dkv].astype(BF16)
        step(_dot_nt(qbd_ref[...], k) + bias, mask, lambda p: _dot(p, v))

    @pl.when(st == 0)
    def _():
        qbd_ref[...] = _nsa_qbd(q_ref[0], n_kv, n_rep)
        kmask[...] = _expand_blocks(sel_ref[0], kmask.shape[1])
        newp[...] = jnp.zeros(newp.shape, F32)
        newp[0:tq, :] = slcnew_ref[0]
        _softmax_init(m_ref, l_ref, acc)
        new_tile(t0_ref[...], sel_mask(past) & (j_col <= i_row))

    width = npg * PAGE
    k0 = pl.multiple_of((n_steps - 1 - st) * width, width)
    kt = jnp.concatenate([pg[0, 0:dkv, :] for pg in page_refs[::-1]], axis=1).astype(BF16)
    vt = jnp.concatenate([pg[0, dkv:2 * dkv, :] for pg in page_refs[::-1]], axis=1).astype(BF16)
    s = _dot(qbd_ref[...], kt)
    near = s[:, width - PAGE:] + jnp.where(st == 0, 1.0, 0.0) * t1_ref[...]
    s = jnp.concatenate([s[:, :width - PAGE], near], axis=1)
    step(s, sel_mask(k0, width), lambda p: _dot_nt(p, vt))

    @pl.when(st == n_steps - 1)
    def _():
        o_slc = _nsa_gather(_softmax_out(l_ref, acc), tq, n_kv, n_rep)
        _softmax_init(m_ref, l_ref, acc)
        newp[0:tq, :] = winnew_ref[0]
        new_tile(t0_ref[...], j_col <= i_row)
        n_back = WINDOW // PAGE
        for back in range(1, n_back + 1):
            sl = slice((n_back - back) * PAGE, (n_back - back + 1) * PAGE)
            s = _dot(qbd_ref[...], state_ref[0, 0:dkv, sl].astype(BF16))
            if back == 1:
                s = s + t1_ref[...]
            mask = (j_col > i_row) if back == n_back else None
            step(s, mask, lambda p, sl=sl: _dot_nt(p, state_ref[0, dkv:2 * dkv, sl].astype(BF16)))
        o_win = _nsa_gather(_softmax_out(l_ref, acc), tq, n_kv, n_rep)
        o_ref[0] = _merge(gates_ref[0], ege_ref, ocmp_ref[0], o_slc, o_win).astype(BF16)


def _slc_decode(q, slc_new, win_new, state_t, sel, gates, o_cmp, cache_t, page_table, t0, t1, ege, n_kv, npg):
    b, tq, dq = q.shape
    dkv = n_kv * HEAD_DIM
    n_rep = dq // dkv
    r = n_rep * n_kv * tq
    n_pages = page_table.shape[1]
    past = n_pages * PAGE
    const = lambda a: pl.BlockSpec(a.shape, lambda i, s, pt: (0,) * a.ndim)
    per_seq = lambda a: pl.BlockSpec((1,) + a.shape[1:], lambda i, s, pt: (i,) + (0,) * (a.ndim - 1))
    grid_spec = pltpu.PrefetchScalarGridSpec(
        num_scalar_prefetch=1,
        grid=(b, n_pages // npg),
        in_specs=[per_seq(q), per_seq(slc_new), per_seq(win_new), per_seq(state_t), per_seq(sel), per_seq(gates),
                  per_seq(o_cmp), const(t0), const(t1), const(ege)] + _page_specs(n_pages, 2 * dkv, npg),
        out_specs=pl.BlockSpec((1, tq, dq), lambda i, s, pt: (i, 0, 0)),
        scratch_shapes=[pltpu.VMEM((r, dkv), BF16), pltpu.VMEM((PAGE, 2 * dkv), F32),
                        pltpu.VMEM((n_kv * tq, past + PAGE), F32)] + _softmax_scratch(r, dkv),
    )
    return pl.pallas_call(
        functools.partial(_slc_decode_kernel, tq=tq, n_kv=n_kv, n_rep=n_rep, npg=npg, past=past),
        name="slc_decode",
        grid_spec=grid_spec,
        out_shape=jax.ShapeDtypeStruct((b, tq, dq), BF16),
        compiler_params=_params("arbitrary", "arbitrary"),
    )(page_table, q, slc_new, win_new, state_t, sel, gates, o_cmp, t0, t1, ege, *([cache_t] * npg))


def _rel_buckets(dist):
    n = np.maximum(dist, 0)
    exact = N_BUCKETS // 2
    nf = np.maximum(n, 1).astype(np.float64)
    large = exact + (np.log(nf / exact) / math.log(REL_MAX_DIST / exact) * (N_BUCKETS - exact)).astype(np.int64)
    return np.where(n < exact, n, np.minimum(large, N_BUCKETS - 1)).astype(np.int32)


def _head_block_diag(width, scale):
    h = np.arange(width) // HEAD_DIM
    return jnp.asarray((h[:, None] == h[None, :]).astype(np.float32) * scale, BF16)


class _NsaLayout:
    def __init__(self, n_heads, n_kv):
        n_rep = n_heads // n_kv
        self.n_kv, self.n_rep = n_kv, n_rep
        new = np.arange(n_heads)
        z, g = new // n_kv, new % n_kv
        self.head_perm = g * n_rep + z
        self.col_perm = (self.head_perm[:, None] * HEAD_DIM + np.arange(HEAD_DIM)[None, :]).reshape(-1)
        self.gate_perm = (np.arange(3)[:, None] * n_heads + self.head_perm[None, :]).reshape(-1)
        ege = np.zeros((3, LANES, n_heads * HEAD_DIM), np.float32)
        for br in range(3):
            for h in range(n_heads):
                ege[br, br * n_heads + h, h * HEAD_DIM:(h + 1) * HEAD_DIM] = 1.0
        self.ege = jnp.asarray(ege, BF16)


def _bucket_lookup(rel_hd, dist):
    buckets = jnp.asarray(_rel_buckets(dist).reshape(-1))
    onehot = (jnp.arange(N_BUCKETS, dtype=jnp.int32)[:, None] == buckets[None, :]).astype(F32)
    tab = jnp.dot(rel_hd, onehot, precision=lax.Precision.HIGHEST)
    return tab.reshape((rel_hd.shape[0],) + dist.shape)


def _near_bias(rel_hd, dist):
    far = rel_hd[:, N_BUCKETS - 1]
    tab = _bucket_lookup(rel_hd, dist) - far[:, None, None]
    return jnp.where(jnp.asarray(dist >= 0)[None], tab, 0.0)


def _toeplitz_tiles(rel_hd, tq, tk):
    i = np.arange(tq)[:, None]
    j = np.arange(tk)[None, :]
    return _near_bias(rel_hd, i - j), _near_bias(rel_hd, tk + i - j)


def _cmp_band(rel_hd, tq, ncp):
    i = np.arange(tq)[:, None]
    m = np.arange(ncp)[None, :]
    dist = D_CMP * (BAND_BACK - m) + i - (L_CMP - 1)
    return _near_bias(rel_hd, np.where(m < 2 * BAND_BACK + tq // D_CMP, dist, -1))


def _cmp_bias(rel_hd, qpos, n_chunk):
    dc = qpos[:, None] - (np.arange(n_chunk)[None, :] * D_CMP + L_CMP - 1)
    return _bucket_lookup(rel_hd, dc)


def kernel(x_prompt, x_sample, cache_sb_kv, cache_fox_kv, cache_fox_logf, cache_cmp_kv, cache_slc_kv, state_win_kv,
           page_table, c_prompt, c_sample, norm_gain, w_ada, b_ada, ffn_w_in, ffn_w_out, w_in_ab, b_forget,
           fox_qk_gain, w_out_ab, w_in_nsa, b_nsa_gate, nsa_qk_gain, cmp_w, cmp_pe, rel_bias, w_out_nsa):
    bp, s_len, d = x_prompt.shape
    bs, t_dec, _ = x_sample.shape
    n_pool = cache_sb_kv.shape[0]
    h_sb, h_fox = cache_sb_kv.shape[3], cache_fox_kv.shape[3]
    n_kv = cache_cmp_kv.shape[3]
    n_heads = rel_bias.shape[1]
    da = h_sb * HEAD_DIM
    dkv = n_kv * HEAD_DIM
    dq = n_heads * HEAD_DIM
    assert h_sb == h_fox
    n_pages = page_table.shape[1]
    past = n_pages * PAGE
    lay = _NsaLayout(n_heads, n_kv)

    ffn_in = ffn_w_in.astype(BF16)
    ffn_out = ffn_w_out.astype(BF16)
    w_ab = jnp.pad(w_in_ab, ((0, 0), (0, LANES - h_fox))).astype(BF16)
    bfp = jnp.pad(b_forget, (0, LANES - h_fox)).reshape(1, LANES)
    bd_ab = _head_block_diag(da, 1.0 / HEAD_DIM)
    gq_fox = jnp.tile(fox_qk_gain[0], h_fox).reshape(1, da)
    gk_fox = jnp.tile(fox_qk_gain[1], h_fox).reshape(1, da)
    wq_ab = jnp.concatenate([w_in_ab[:, 0:da], w_in_ab[:, 3 * da:4 * da]], axis=1).astype(BF16)
    wkvt_ab = jnp.concatenate([w_in_ab[:, da:3 * da], w_in_ab[:, 4 * da:6 * da]], axis=1).T.astype(BF16)
    wflt_ab = jnp.pad(w_in_ab[:, 6 * da:].T, ((0, 16 - h_fox), (0, 0))).astype(BF16)
    w_out_sb = w_out_ab[:da].astype(BF16)
    w_out_fox = w_out_ab[da:].astype(BF16)
    n_rep = n_heads // n_kv
    wq_swapped = w_in_nsa[:, 0:dq].reshape(d, n_kv, n_rep, HEAD_DIM).transpose(0, 2, 1, 3).reshape(d, dq)
    wg_swapped = w_in_nsa[:, dq + 6 * dkv:].reshape(d, 3, n_kv, n_rep).transpose(0, 1, 3, 2).reshape(d, 3 * n_heads)
    w_nsa = jnp.concatenate([wq_swapped, w_in_nsa[:, dq:dq + 6 * dkv], wg_swapped], axis=1)
    w_nsa = jnp.pad(w_nsa, ((0, 0), (0, LANES - 3 * n_heads))).astype(BF16)
    bg_swapped = b_nsa_gate.reshape(3, n_kv, n_rep).transpose(0, 2, 1).reshape(3 * n_heads)
    bg_perm = jnp.pad(bg_swapped, (0, LANES - 3 * n_heads)).reshape(1, LANES)
    wq_nsa = w_in_nsa[:, 0:dq].astype(BF16)
    wcmp_nsa = w_in_nsa[:, dq:dq + 2 * dkv].astype(BF16)
    wkvt_nsa = w_in_nsa[:, dq + 2 * dkv:dq + 6 * dkv].T.astype(BF16)
    wg_nsa = jnp.pad(w_in_nsa[:, dq + 6 * dkv:], ((0, 0), (0, LANES - 3 * n_heads))).astype(BF16)
    bg = jnp.pad(b_nsa_gate, (0, LANES - 3 * n_heads)).reshape(1, LANES)
    bd_kv = _head_block_diag(dkv, 1.0 / HEAD_DIM)
    tile_kv = lambda g: jnp.tile(g, n_kv).reshape(1, dkv)
    w_out_n = w_out_nsa.astype(BF16)
    w_out_n_perm = w_out_nsa.reshape(n_kv, n_rep, HEAD_DIM, d).transpose(1, 0, 2, 3).reshape(dq, d).astype(BF16)
    eye = jnp.eye(n_kv, dtype=F32)
    cmp_bd = jnp.einsum('gh,kjde->kjgdhe', eye, cmp_w).reshape(2, L_CMP, dkv, dkv).astype(BF16)
    cmp_pe_t = jnp.tile(cmp_pe, (1, 1, n_kv))
    rel_orig = rel_bias.T
    rel_perm = rel_bias.reshape(N_BUCKETS, n_kv, n_rep).transpose(0, 2, 1).reshape(N_BUCKETS, n_heads).T
    gk_cmp = tile_kv(nsa_qk_gain[1])

    mod = _modulation(jnp.concatenate([c_prompt, c_sample], axis=0), w_ada, b_ada)
    mod = mod.reshape(mod.shape[0], bp + bs, 3, 3, 1, d)

    def mods(l, sub, lo, hi):
        return tuple(mod[l, lo:hi, sub, k] for k in range(3))

    mp = lambda l, sub: mods(l, sub, 0, bp)
    tt = 512
    x = _ffn(x_prompt, mp(0, 0), norm_gain[0, 0], ffn_in[0, 0], ffn_out[0, 0], 1, tt)
    qsb, qfx, sbt, fxt, lft = _proj_ab_t(x, mp(0, 1), norm_gain[0, 1], wq_ab, wkvt_ab, wflt_ab, bd_ab, gq_fox,
                                         gk_fox.reshape(da, 1), b_forget.reshape(h_fox, 1), tt)
    o_sb = _sb_prompt(qsb, sbt, AB_TILE, 2)
    o_fox = _fox_prompt(qfx, fxt, lft, AB_TILE, 2)
    x = _outproj(x, mp(0, 1)[2], [o_sb, o_fox], [w_out_sb, w_out_fox], 1, tt)
    x = _ffn(x, mp(0, 2), norm_gain[0, 2], ffn_in[0, 1], ffn_out[0, 1], 1, tt)
    x = _ffn(x, mp(1, 0), norm_gain[1, 0], ffn_in[1, 0], ffn_out[1, 0], 1, tt)
    q, p_cmp, slct, wint, gates = _proj_nsa_t(x, mp(1, 1), norm_gain[1, 1], wq_nsa, wcmp_nsa, wkvt_nsa, wg_nsa, bd_kv,
                                              tile_kv(nsa_qk_gain[0]), tile_kv(nsa_qk_gain[2]).reshape(dkv, 1),
                                              tile_kv(nsa_qk_gain[3]).reshape(dkv, 1), bg, tt)
    kct, vct = _compress_prompt(p_cmp, cmp_bd, cmp_pe_t, bd_kv, gk_cmp)
    t0, t1 = _toeplitz_tiles(rel_orig, TILE, TILE)
    o = _nsa_prompt(q, slct, wint, kct, vct, gates, _cmp_band(rel_orig, TILE, s_len // D_CMP),
                    jnp.concatenate([t1, t0], axis=2), TILE, n_kv)
    x = _outproj(x, mp(1, 1)[2], [o], [w_out_n], 1, tt)
    y_p = _ffn(x, mp(1, 2), norm_gain[1, 2], ffn_in[1, 1], ffn_out[1, 1], 1, tt)

    c_sb = jnp.transpose(cache_sb_kv, (0, 2, 3, 4, 1)).reshape(n_pool, 2 * da, PAGE)
    c_fox = jnp.transpose(cache_fox_kv, (0, 2, 3, 4, 1)).reshape(n_pool, 2 * da, PAGE)
    c_lf = jnp.transpose(cache_fox_logf, (0, 2, 1))
    c_cmp = jnp.transpose(cache_cmp_kv, (0, 2, 3, 4, 1)).reshape(n_pool, 2 * dkv, PAGE)
    c_slc = jnp.transpose(cache_slc_kv, (0, 2, 3, 4, 1)).reshape(n_pool, 2 * dkv, PAGE)
    state_t = jnp.transpose(state_win_kv, (0, 2, 3, 4, 1)).reshape(bs, 2 * dkv, state_win_kv.shape[1])
    npg = PAGES_PER_STEP
    ms = lambda l, sub: mods(l, sub, bp, bp + bs)
    x = _ffn(x_sample, ms(0, 0), norm_gain[0, 0], ffn_in[0, 0], ffn_out[0, 0], bs, t_dec)
    qsb, s_sb, qfx, s_fox, s_logf, logfp = _proj_ab(x, ms(0, 1), norm_gain[0, 1], w_ab, bd_ab, gq_fox, gk_fox, bfp,
                                                    h_fox, bs, t_dec)
    o_sb = _sb_decode(qsb, s_sb, c_sb, page_table, npg)
    o_fox = _fox_decode(qfx, s_fox, logfp, c_fox, c_lf, page_table, npg)
    x = _outproj(x, ms(0, 1)[2], [o_sb, o_fox], [w_out_sb, w_out_fox], bs, t_dec)
    x = _ffn(x, ms(0, 2), norm_gain[0, 2], ffn_in[0, 1], ffn_out[0, 1], bs, t_dec)
    x = _ffn(x, ms(1, 0), norm_gain[1, 0], ffn_in[1, 0], ffn_out[1, 0], bs, t_dec)
    q, s_cmp, s_slc, s_win, gates = _proj_nsa(x, ms(1, 1), norm_gain[1, 1], w_nsa, bd_kv, tile_kv(nsa_qk_gain[0]),
                                              tile_kv(nsa_qk_gain[2]), tile_kv(nsa_qk_gain[3]), bg_perm, dq, bs, t_dec)
    bias = _cmp_bias(rel_perm, past + np.arange(t_dec), past // D_CMP)
    o_cmp, sel = _cmp_decode(q, c_cmp, page_table, cmp_bd, cmp_pe_t, bd_kv, gk_cmp, bias, n_kv, npg)
    t0, t1 = _toeplitz_tiles(rel_perm, t_dec, PAGE)
    o = _slc_decode(q, s_slc, s_win, state_t, sel, gates, o_cmp, c_slc, page_table,
                    t0.reshape(-1, PAGE), t1.reshape(-1, PAGE), lay.ege, n_kv, npg)
    x = _outproj(x, ms(1, 1)[2], [o], [w_out_n_perm], bs, t_dec)
    y_s = _ffn(x, ms(1, 2), norm_gain[1, 2], ffn_in[1, 1], ffn_out[1, 1], bs, t_dec)

    kv5 = lambda a, h: a.reshape(a.shape[0], a.shape[1], 2, h, HEAD_DIM)
    from_t = lambda a, h: kv5(jnp.transpose(a, (0, 2, 1)), h)
    win_len = min(WINDOW, s_len)
    state_rows = state_win_kv.reshape(bs, state_win_kv.shape[1], 2 * dkv)
    s_win_all = jnp.concatenate([state_rows, s_win], axis=1)
    new_len = min(WINDOW, s_win_all.shape[1])
    return (y_p, y_s, from_t(sbt, h_sb), kv5(s_sb, h_sb), from_t(fxt, h_fox), kv5(s_fox, h_fox),
            jnp.transpose(lft, (0, 2, 1)), s_logf, kv5(p_cmp, n_kv), kv5(s_cmp, n_kv), from_t(slct, n_kv),
            kv5(s_slc, n_kv), from_t(wint[:, :, s_len - win_len:], n_kv),
            kv5(s_win_all[:, s_win_all.shape[1] - new_len:], n_kv))
```

```python
import functools
import math

import numpy as np
import jax
import jax.numpy as jnp
from jax import lax
from jax.experimental import pallas as pl
from jax.experimental.pallas import tpu as pltpu

F32 = jnp.float32
BF16 = jnp.bfloat16

HEAD_DIM = 64
PAGE = 128
L_CMP = 32
D_CMP = 16
SLC_BLOCK = 64
N_SELECT = 16
WINDOW = 512
N_BUCKETS = 32
REL_MAX_DIST = 128
FORCE_SCORE = 1e4
NEG = -1e30
EPS = 1e-6
MACARON = 0.5
LANES = 128
VMEM_LIMIT = 56 * 1024 * 1024
FF_CHUNK = 256
TILE = 128
FAR_TILE = 256
AB_TILE = 256
STEP_BYTES = 8 * 1024 * 1024
DEAD_LOG = -104.0
SUM_ROWS = 16
BAND_BACK = (REL_MAX_DIST + L_CMP) // D_CMP


def _params(*sem):
    return pltpu.CompilerParams(dimension_semantics=sem, vmem_limit_bytes=VMEM_LIMIT)


def _dot(a, b):
    return jnp.dot(a, b, preferred_element_type=F32)


def _dot_nt(a, b):
    return lax.dot_general(a, b, (((1,), (1,)), ((), ())), preferred_element_type=F32)


def _split2(x):
    hi = x.astype(BF16)
    lo = (x - hi.astype(F32)).astype(BF16)
    return hi, lo


def _split3(x):
    hi = x.astype(BF16)
    r = x - hi.astype(F32)
    mid = r.astype(BF16)
    lo = (r - mid.astype(F32)).astype(BF16)
    return hi, mid, lo


def _dot_x2(x, w):
    hi, lo = _split2(x)
    return _dot(hi, w) + _dot(lo, w)


def _dot_x3(x, w):
    hi, mid, lo = _split3(x)
    return _dot(hi, w) + _dot(mid, w) + _dot(lo, w)


def _dot_l2(w, x):
    hi, lo = _split2(x)
    return _dot(w, hi) + _dot(w, lo)


def _log_sigmoid(x):
    return jnp.minimum(x, 0.0) - jnp.log(1.0 + jnp.exp(-jnp.abs(x)))


def _adanorm(x, g, scale, shift):
    ms = jnp.mean(x * x, axis=-1, keepdims=True)
    return (x * lax.rsqrt(ms + EPS) * g) * (1.0 + scale) + shift


def _head_norm(t, bd):
    return t * lax.rsqrt(_dot_x2(t * t, bd) + EPS)


def _head_norm_t(t, bd):
    return t * lax.rsqrt(_dot_l2(bd, t * t) + EPS)


def _iota(shape, dim):
    return lax.broadcasted_iota(jnp.int32, shape, dim)


def _mod_kernel(c_ref, w_ref, b_ref, o_ref):
    c = c_ref[...]
    cs = c * jax.nn.sigmoid(c)
    ch, cl = _split2(cs)
    wh, wl = _split2(w_ref[0])
    o_ref[0] = _dot(ch, wh) + _dot(cl, wh) + _dot(ch, wl) + b_ref[0]


def _modulation(c_all, w_ada, b_ada):
    depth, d, n = w_ada.shape
    rows = c_all.shape[0]
    tn = 1024
    return pl.pallas_call(
        _mod_kernel, name="modulation",
        grid=(depth, n // tn),
        in_specs=[pl.BlockSpec((rows, d), lambda l, j: (0, 0)),
                  pl.BlockSpec((1, d, tn), lambda l, j: (l, 0, j)),
                  pl.BlockSpec((1, 1, tn), lambda l, j: (l, 0, j))],
        out_specs=pl.BlockSpec((1, rows, tn), lambda l, j: (l, 0, j)),
        out_shape=jax.ShapeDtypeStruct((depth, rows, n), F32),
        compiler_params=_params("arbitrary", "arbitrary"),
    )(c_all, w_ada, b_ada.reshape(depth, 1, n))


def _ffn_kernel(x_ref, shift_ref, scale_ref, gate_ref, g_ref, win_ref, wout_ref, o_ref, hid_ref, *, d_ff):
    x = x_ref[...]
    bb, tt, d = x.shape
    h = _adanorm(x, g_ref[...], scale_ref[...], shift_ref[...]).reshape(bb * tt, d).astype(BF16)
    for c in range(d_ff // FF_CHUNK):
        a = _dot(h, win_ref[:, c * FF_CHUNK:(c + 1) * FF_CHUNK])
        b = _dot(h, win_ref[:, d_ff + c * FF_CHUNK:d_ff + (c + 1) * FF_CHUNK])
        hid_ref[:, c * FF_CHUNK:(c + 1) * FF_CHUNK] = (a * jax.nn.sigmoid(a) * b).astype(BF16)
    o = _dot(hid_ref[...], wout_ref[...]).reshape(bb, tt, d)
    o_ref[...] = x + (MACARON * gate_ref[...]) * o


def _ffn(x, mod3, g, w_in, w_out, bb, tt):
    b, t, d = x.shape
    d_ff = w_out.shape[0]
    shift, scale, gate = mod3
    mspec = pl.BlockSpec((bb, 1, d), lambda i, j: (i, 0, 0))
    xspec = pl.BlockSpec((bb, tt, d), lambda i, j: (i, j, 0))
    return pl.pallas_call(
        functools.partial(_ffn_kernel, d_ff=d_ff), name="ffn",
        grid=(b // bb, t // tt),
        in_specs=[xspec, mspec, mspec, mspec,
                  pl.BlockSpec((1, 1, d), lambda i, j: (0, 0, 0)),
                  pl.BlockSpec((d, 2 * d_ff), lambda i, j: (0, 0)),
                  pl.BlockSpec((d_ff, d), lambda i, j: (0, 0))],
        out_specs=xspec,
        out_shape=jax.ShapeDtypeStruct(x.shape, F32),
        scratch_shapes=[pltpu.VMEM((bb * tt, d_ff), BF16)],
        compiler_params=_params("arbitrary", "arbitrary"),
    )(x, shift, scale, gate, g.reshape(1, 1, d), w_in, w_out)


def _outproj_kernel(*refs, n_in):
    x_ref, gate_ref = refs[0], refs[1]
    o_refs = refs[2:2 + n_in]
    w_refs = refs[2 + n_in:2 + 2 * n_in]
    out_ref = refs[2 + 2 * n_in]
    x = x_ref[...]
    bb, tt, d = x.shape
    y = None
    for o_ref, w_ref in zip(o_refs, w_refs):
        o = o_ref[...].astype(F32).reshape(bb * tt, o_ref.shape[-1]).astype(BF16)
        part = _dot(o, w_ref[...])
        y = part if y is None else y + part
    out_ref[...] = x + gate_ref[...] * y.reshape(bb, tt, d)


def _outproj(x, gate, outs, weights, bb, tt):
    b, t, d = x.shape
    n_in = len(outs)
    xspec = pl.BlockSpec((bb, tt, d), lambda i, j: (i, j, 0))
    in_specs = [xspec, pl.BlockSpec((bb, 1, d), lambda i, j: (i, 0, 0))]
    in_specs += [pl.BlockSpec((bb, tt, o.shape[-1]), lambda i, j: (i, j, 0)) for o in outs]
    in_specs += [pl.BlockSpec(w.shape, lambda i, j: (0, 0)) for w in weights]
    return pl.pallas_call(
        functools.partial(_outproj_kernel, n_in=n_in), name="outproj",
        grid=(b // bb, t // tt),
        in_specs=in_specs,
        out_specs=xspec,
        out_shape=jax.ShapeDtypeStruct(x.shape, F32),
        compiler_params=_params("arbitrary", "arbitrary"),
    )(x, gate, *outs, *weights)


def _proj_ab_kernel(x_ref, shift_ref, scale_ref, g_ref, w_ref, bd_ref, gq_ref, gk_ref, bf_ref,
                    qsb_ref, sbkv_ref, qfx_ref, fxkv_ref, logf_ref, logfp_ref, *, da, n_f):
    x = x_ref[...]
    bb, tt, d = x.shape
    m = bb * tt
    h = _adanorm(x, g_ref[...], scale_ref[...], shift_ref[...]).reshape(m, d).astype(BF16)
    sc = HEAD_DIM ** -0.5
    qsb_ref[...] = (_dot(h, w_ref[:, 0:da]) * sc).reshape(bb, tt, da).astype(BF16)
    sbkv_ref[...] = _dot(h, w_ref[:, da:3 * da]).reshape(bb, tt, 2 * da)
    bd = bd_ref[...]
    qf = _head_norm(_dot(h, w_ref[:, 3 * da:4 * da]), bd) * gq_ref[...]
    qfx_ref[...] = (qf * sc).reshape(bb, tt, da).astype(BF16)
    kf = _head_norm(_dot(h, w_ref[:, 4 * da:5 * da]), bd) * gk_ref[...]
    fxkv_ref[:, :, 0:da] = kf.reshape(bb, tt, da)
    fxkv_ref[:, :, da:2 * da] = _dot(h, w_ref[:, 5 * da:6 * da]).reshape(bb, tt, da)
    lf = _log_sigmoid(_dot(h, w_ref[:, 6 * da:6 * da + LANES]) + bf_ref[...])
    lf = jnp.where(_iota(lf.shape, 1) < n_f, lf, 0.0)
    logfp_ref[...] = lf.reshape(bb, tt, LANES)
    logf_ref[...] = lf[:, 0:n_f].reshape(bb, tt, n_f)


def _proj_ab(x, mod3, g, w, bd, gq, gk, bfp, n_f, bb, tt):
    b, t, d = x.shape
    da = bd.shape[0]
    shift, scale, _ = mod3
    mspec = pl.BlockSpec((bb, 1, d), lambda i, j: (i, 0, 0))
    xspec = pl.BlockSpec((bb, tt, d), lambda i, j: (i, j, 0))
    row = lambda width: pl.BlockSpec((1, width), lambda i, j: (0, 0))
    ospec = lambda width: pl.BlockSpec((bb, tt, width), lambda i, j: (i, j, 0))
    return pl.pallas_call(
        functools.partial(_proj_ab_kernel, da=da, n_f=n_f), name="proj_ab",
        grid=(b // bb, t // tt),
        in_specs=[xspec, mspec, mspec, pl.BlockSpec((1, 1, d), lambda i, j: (0, 0, 0)),
                  pl.BlockSpec(w.shape, lambda i, j: (0, 0)), pl.BlockSpec(bd.shape, lambda i, j: (0, 0)),
                  row(da), row(da), row(LANES)],
        out_specs=[ospec(da), ospec(2 * da), ospec(da), ospec(2 * da), ospec(n_f), ospec(LANES)],
        out_shape=[jax.ShapeDtypeStruct((b, t, da), BF16), jax.ShapeDtypeStruct((b, t, 2 * da), F32),
                   jax.ShapeDtypeStruct((b, t, da), BF16), jax.ShapeDtypeStruct((b, t, 2 * da), F32),
                   jax.ShapeDtypeStruct((b, t, n_f), F32), jax.ShapeDtypeStruct((b, t, LANES), F32)],
        compiler_params=_params("arbitrary", "arbitrary"),
    )(x, shift, scale, g.reshape(1, 1, d), w, bd, gq, gk, bfp)


def _proj_ab_t_kernel(x_ref, shift_ref, scale_ref, g_ref, wq_ref, wkvt_ref, wflt_ref, bd_ref, gq_ref, gk_ref, bf_ref,
                      qsb_ref, qfx_ref, sbt_ref, fxt_ref, lft_ref, *, da, n_f):
    x = x_ref[0]
    h = _adanorm(x, g_ref[0], scale_ref[0], shift_ref[0]).astype(BF16)
    sc = HEAD_DIM ** -0.5
    bd = bd_ref[...]
    qa = _dot(h, wq_ref[:, 0:da]) * sc
    qf = _head_norm(_dot(h, wq_ref[:, da:2 * da]), bd) * (gq_ref[...] * sc)
    for hh in range(da // HEAD_DIM):
        qsb_ref[0, hh] = qa[:, hh * HEAD_DIM:(hh + 1) * HEAD_DIM].astype(BF16)
        qfx_ref[0, hh] = qf[:, hh * HEAD_DIM:(hh + 1) * HEAD_DIM].astype(BF16)
    sbt_ref[0] = _dot_nt(wkvt_ref[0:2 * da, :], h)
    kft = _head_norm_t(_dot_nt(wkvt_ref[2 * da:3 * da, :], h), bd)
    fxt_ref[0, 0:da, :] = kft * gk_ref[...]
    fxt_ref[0, da:2 * da, :] = _dot_nt(wkvt_ref[3 * da:4 * da, :], h)
    lft_ref[0] = _log_sigmoid(_dot_nt(wflt_ref[...], h)[0:n_f, :] + bf_ref[...])


def _proj_ab_t(x, mod3, g, wq, wkvt, wflt, bd, gq, gk_col, bf_col, tt):
    b, t, d = x.shape
    da = bd.shape[0]
    n_h = da // HEAD_DIM
    n_f = bf_col.shape[0]
    shift, scale, _ = mod3
    mspec = pl.BlockSpec((1, 1, d), lambda i, j: (i, 0, 0))
    const = lambda a: pl.BlockSpec(a.shape, lambda i, j: (0,) * a.ndim)
    qspec = pl.BlockSpec((1, n_h, tt, HEAD_DIM), lambda i, j: (i, 0, j, 0))
    tspec = lambda rows: pl.BlockSpec((1, rows, tt), lambda i, j: (i, 0, j))
    return pl.pallas_call(
        functools.partial(_proj_ab_t_kernel, da=da, n_f=n_f), name="proj_ab_t",
        grid=(b, t // tt),
        in_specs=[pl.BlockSpec((1, tt, d), lambda i, j: (i, j, 0)), mspec, mspec,
                  pl.BlockSpec((1, 1, d), lambda i, j: (0, 0, 0)),
                  const(wq), const(wkvt), const(wflt), const(bd), const(gq), const(gk_col), const(bf_col)],
        out_specs=[qspec, qspec, tspec(2 * da), tspec(2 * da), tspec(n_f)],
        out_shape=[jax.ShapeDtypeStruct((b, n_h, t, HEAD_DIM), BF16), jax.ShapeDtypeStruct((b, n_h, t, HEAD_DIM), BF16),
                   jax.ShapeDtypeStruct((b, 2 * da, t), F32), jax.ShapeDtypeStruct((b, 2 * da, t), F32),
                   jax.ShapeDtypeStruct((b, n_f, t), F32)],
        compiler_params=_params("arbitrary", "arbitrary"),
    )(x, shift, scale, g.reshape(1, 1, d), wq, wkvt, wflt, bd, gq, gk_col, bf_col)


def _proj_nsa_kernel(x_ref, shift_ref, scale_ref, g_ref, w_ref, bd_ref, gq_ref, gs_ref, gw_ref, bg_ref,
                     q_ref, cmp_ref, slc_ref, win_ref, gates_ref, *, dq, dkv):
    x = x_ref[...]
    bb, tt, d = x.shape
    m = bb * tt
    h = _adanorm(x, g_ref[...], scale_ref[...], shift_ref[...]).reshape(m, d).astype(BF16)
    bd = bd_ref[...]
    sc = HEAD_DIM ** -0.5
    for c in range(dq // dkv):
        qc = _head_norm(_dot(h, w_ref[:, c * dkv:(c + 1) * dkv]), bd) * gq_ref[...]
        q_ref[:, :, c * dkv:(c + 1) * dkv] = (qc * sc).reshape(bb, tt, dkv).astype(BF16)
    o = dq
    cmp_ref[...] = _dot(h, w_ref[:, o:o + 2 * dkv]).reshape(bb, tt, 2 * dkv)
    o += 2 * dkv
    ks = _head_norm(_dot(h, w_ref[:, o:o + dkv]), bd) * gs_ref[...]
    slc_ref[:, :, 0:dkv] = ks.reshape(bb, tt, dkv)
    slc_ref[:, :, dkv:2 * dkv] = _dot(h, w_ref[:, o + dkv:o + 2 * dkv]).reshape(bb, tt, dkv)
    o += 2 * dkv
    kw = _head_norm(_dot(h, w_ref[:, o:o + dkv]), bd) * gw_ref[...]
    win_ref[:, :, 0:dkv] = kw.reshape(bb, tt, dkv)
    win_ref[:, :, dkv:2 * dkv] = _dot(h, w_ref[:, o + dkv:o + 2 * dkv]).reshape(bb, tt, dkv)
    o += 2 * dkv
    gates_ref[...] = jax.nn.sigmoid(_dot(h, w_ref[:, o:o + LANES]) + bg_ref[...]).reshape(bb, tt, LANES)


def _proj_nsa(x, mod3, g, w, bd, gq, gs, gw, bg, dq, bb, tt):
    b, t, d = x.shape
    dkv = bd.shape[0]
    shift, scale, _ = mod3
    mspec = pl.BlockSpec((bb, 1, d), lambda i, j: (i, 0, 0))
    xspec = pl.BlockSpec((bb, tt, d), lambda i, j: (i, j, 0))
    row = lambda width: pl.BlockSpec((1, width), lambda i, j: (0, 0))
    ospec = lambda width: pl.BlockSpec((bb, tt, width), lambda i, j: (i, j, 0))
    return pl.pallas_call(
        functools.partial(_proj_nsa_kernel, dq=dq, dkv=dkv), name="proj_nsa",
        grid=(b // bb, t // tt),
        in_specs=[xspec, mspec, mspec, pl.BlockSpec((1, 1, d), lambda i, j: (0, 0, 0)),
                  pl.BlockSpec(w.shape, lambda i, j: (0, 0)), pl.BlockSpec(bd.shape, lambda i, j: (0, 0)),
                  row(dkv), row(dkv), row(dkv), row(LANES)],
        out_specs=[ospec(dq), ospec(2 * dkv), ospec(2 * dkv), ospec(2 * dkv), ospec(LANES)],
        out_shape=[jax.ShapeDtypeStruct((b, t, dq), BF16), jax.ShapeDtypeStruct((b, t, 2 * dkv), F32),
                   jax.ShapeDtypeStruct((b, t, 2 * dkv), F32), jax.ShapeDtypeStruct((b, t, 2 * dkv), F32),
                   jax.ShapeDtypeStruct((b, t, LANES), F32)],
        compiler_params=_params("arbitrary", "arbitrary"),
    )(x, shift, scale, g.reshape(1, 1, d), w, bd, gq, gs, gw, bg)


def _proj_nsa_t_kernel(x_ref, shift_ref, scale_ref, g_ref, wq_ref, wcmp_ref, wkvt_ref, wg_ref, bd_ref, gq_ref, gs_ref,
                       gw_ref, bg_ref, q_ref, cmp_ref, slct_ref, wint_ref, gates_ref, *, dq, dkv):
    x = x_ref[0]
    h = _adanorm(x, g_ref[0], scale_ref[0], shift_ref[0]).astype(BF16)
    bd = bd_ref[...]
    sc = HEAD_DIM ** -0.5
    per = dkv // HEAD_DIM
    for c in range(dq // dkv):
        qc = _head_norm(_dot(h, wq_ref[:, c * dkv:(c + 1) * dkv]), bd) * (gq_ref[...] * sc)
        for hh in range(per):
            q_ref[0, c * per + hh] = qc[:, hh * HEAD_DIM:(hh + 1) * HEAD_DIM].astype(BF16)
    cmp_ref[0] = _dot(h, wcmp_ref[...])
    slct_ref[0, 0:dkv, :] = _head_norm_t(_dot_nt(wkvt_ref[0:dkv, :], h), bd) * gs_ref[...]
    slct_ref[0, dkv:2 * dkv, :] = _dot_nt(wkvt_ref[dkv:2 * dkv, :], h)
    wint_ref[0, 0:dkv, :] = _head_norm_t(_dot_nt(wkvt_ref[2 * dkv:3 * dkv, :], h), bd) * gw_ref[...]
    wint_ref[0, dkv:2 * dkv, :] = _dot_nt(wkvt_ref[3 * dkv:4 * dkv, :], h)
    gates_ref[0] = jax.nn.sigmoid(_dot(h, wg_ref[...]) + bg_ref[...])


def _proj_nsa_t(x, mod3, g, wq, wcmp, wkvt, wg, bd, gq, gs_col, gw_col, bg, tt):
    b, t, d = x.shape
    dkv = bd.shape[0]
    dq = wq.shape[1]
    n_h = dq // HEAD_DIM
    shift, scale, _ = mod3
    mspec = pl.BlockSpec((1, 1, d), lambda i, j: (i, 0, 0))
    const = lambda a: pl.BlockSpec(a.shape, lambda i, j: (0,) * a.ndim)
    tspec = lambda rows: pl.BlockSpec((1, rows, tt), lambda i, j: (i, 0, j))
    rspec = lambda width: pl.BlockSpec((1, tt, width), lambda i, j: (i, j, 0))
    return pl.pallas_call(
        functools.partial(_proj_nsa_t_kernel, dq=dq, dkv=dkv), name="proj_nsa_t",
        grid=(b, t // tt),
        in_specs=[rspec(d), mspec, mspec, pl.BlockSpec((1, 1, d), lambda i, j: (0, 0, 0)),
                  const(wq), const(wcmp), const(wkvt), const(wg), const(bd), const(gq), const(gs_col), const(gw_col),
                  const(bg)],
        out_specs=[pl.BlockSpec((1, n_h, tt, HEAD_DIM), lambda i, j: (i, 0, j, 0)), rspec(2 * dkv),
                   tspec(2 * dkv), tspec(2 * dkv), rspec(LANES)],
        out_shape=[jax.ShapeDtypeStruct((b, n_h, t, HEAD_DIM), BF16), jax.ShapeDtypeStruct((b, t, 2 * dkv), F32),
                   jax.ShapeDtypeStruct((b, 2 * dkv, t), F32), jax.ShapeDtypeStruct((b, 2 * dkv, t), F32),
                   jax.ShapeDtypeStruct((b, t, LANES), F32)],
        compiler_params=_params("arbitrary", "arbitrary"),
    )(x, shift, scale, g.reshape(1, 1, d), wq, wcmp, wkvt, wg, bd, gq, gs_col, gw_col, bg)


def _block_diag_rows(q, n_heads):
    q = q.astype(F32)
    head = _iota(q.shape, 1) // HEAD_DIM
    return jnp.concatenate([jnp.where(head == h, q, 0.0) for h in range(n_heads)], axis=0).astype(BF16)


def _gather_diag(acc, tq, n_heads):
    head = _iota((tq, acc.shape[1]), 1) // HEAD_DIM
    out = jnp.zeros((tq, acc.shape[1]), F32)
    for h in range(n_heads):
        out = out + jnp.where(head == h, acc[h * tq:(h + 1) * tq, :], 0.0)
    return out


def _row_query(shape, tq):
    return _iota(shape, 0) % tq


def _running_sum_matrix(tk, prefix):
    j = _iota((tk, 2 * tk), 0)
    s = _iota((tk, 2 * tk), 1)
    tri = (j <= s) if prefix else (j > s)
    return jnp.where((s >= tk) | tri, 1.0, 0.0).astype(BF16)


def _later_matrix(tk):
    return jnp.where(_iota((tk, tk), 0) > _iota((tk, tk), 1), 1.0, 0.0).astype(BF16)


def _across_lane_tiles(x, op):
    out = x[:, 0:LANES]
    for c in range(1, x.shape[1] // LANES):
        out = op(out, x[:, c * LANES:(c + 1) * LANES])
    return out


def _with_row_vector(x, v, op):
    vb = jnp.broadcast_to(v, (x.shape[0], LANES))
    tiles = [op(x[:, c * LANES:(c + 1) * LANES], vb) for c in range(x.shape[1] // LANES)]
    return tiles[0] if len(tiles) == 1 else jnp.concatenate(tiles, axis=1)


def _exp_minus(s, m):
    return jnp.exp(_with_row_vector(s, m, jnp.subtract))


def _row_max(x):
    return jnp.max(_across_lane_tiles(x, jnp.maximum), axis=1, keepdims=True)


def _row_sum(x):
    return jnp.sum(_across_lane_tiles(x, jnp.add), axis=1, keepdims=True)


def _log_keep(z, mask):
    lk = -(jnp.maximum(z, 0.0) + jnp.log(1.0 + jnp.exp(-jnp.abs(z))))
    return lk if mask is None else jnp.where(mask, lk, 0.0)


def _sb_weights(z, um, carry, mask):
    lk = _log_keep(z, mask)
    blk = um.shape[0]
    nb = z.shape[1] // blk
    rest = [None] * nb
    for b in reversed(range(nb)):
        lkb = lk[:, b * blk:(b + 1) * blk]
        rest[b] = _with_row_vector(_dot_x2(lkb, um), carry, jnp.add)
        carry = carry + _row_sum(lkb)
    w = jnp.exp(z + lk + (rest[0] if nb == 1 else jnp.concatenate(rest, axis=1)))
    if mask is not None:
        w = jnp.where(mask, w, 0.0)
    return w.astype(BF16), carry


def _softmax_first(s, pv, m_ref, l_ref, acc_ref):
    m = _row_max(s)
    p = _exp_minus(s, m)
    l_ref[...] = _row_sum(p)
    acc_ref[...] = pv(p.astype(BF16))
    m_ref[...] = m


def _softmax_step(s, pv, m_ref, l_ref, acc_ref, mask=None):
    m_prev = m_ref[...]
    m_new = jnp.maximum(m_prev, _row_max(s))
    p = _exp_minus(s, m_new)
    if mask is not None:
        p = jnp.where(mask, p, 0.0)
    alpha = jnp.exp(m_prev - m_new)
    l_ref[...] = alpha * l_ref[...] + _row_sum(p)
    acc_ref[...] = alpha * acc_ref[...] + pv(p.astype(BF16))
    m_ref[...] = m_new


def _softmax_chains(scores, vt, refs, first):
    if first:
        ms = [_row_max(s) for s in scores]
    else:
        prev = [(m[...], a[...]) for m, a in refs]
        ms = [jnp.maximum(pm, _row_max(s)) for (pm, _), s in zip(prev, scores)]
    ps = [_exp_minus(s, m) for s, m in zip(scores, ms)]
    pvs = [_dot_nt(p.astype(BF16), vt) for p in ps]
    for i, (m_ref, acc_ref) in enumerate(refs):
        acc_ref[...] = pvs[i] if first else jnp.exp(prev[i][0] - ms[i]) * prev[i][1] + pvs[i]
        m_ref[...] = ms[i]


def _softmax_single_chains(scores, vt):
    ps = [_exp_minus(s, _row_max(s)) for s in scores]
    return [_dot_nt(p.astype(BF16), vt) for p in ps]


def _with_sum_row(vt):
    extra = jnp.where(_iota((SUM_ROWS, vt.shape[1]), 0) == 0, 1.0, 0.0).astype(BF16)
    return jnp.concatenate([vt, extra], axis=0)


def _normalized(acc):
    return acc[:, 0:HEAD_DIM] / jnp.maximum(acc[:, HEAD_DIM:HEAD_DIM + 1], 1e-30)


def _softmax_init(m_ref, l_ref, acc_ref):
    m_ref[...] = jnp.full(m_ref.shape, NEG, F32)
    l_ref[...] = jnp.zeros(l_ref.shape, F32)
    acc_ref[...] = jnp.zeros(acc_ref.shape, F32)


def _softmax_out(l_ref, acc_ref):
    return acc_ref[...] / jnp.maximum(l_ref[...], 1e-30)


def _softmax_scratch(r, width):
    return [pltpu.VMEM((r, 1), F32), pltpu.VMEM((r, 1), F32), pltpu.VMEM((r, width), F32)]


def _rank_select(imp, t, ns):
    blk = _iota(imp.shape, 1)
    cur = t // SLC_BLOCK
    valid = (blk * SLC_BLOCK <= t) & (blk < ns)
    forced = (blk == 0) | (blk == cur) | (blk == cur - 1)
    score = jnp.where(valid, jnp.where(forced, FORCE_SCORE, imp), -jnp.inf)
    cnt = jnp.zeros(imp.shape, F32)
    for j in range(ns):
        col = score[:, j:j + 1]
        cnt = cnt + jnp.where(col > score, 1.0, 0.0) + jnp.where(col == score, jnp.where(blk > j, 1.0, 0.0), 0.0)
    return jnp.where(valid & (cnt < N_SELECT), 1.0, 0.0)


def _rank_select_t(imp, q0, ns):
    rows = -(-ns // 8) * 8
    imp_t = imp.T[0:rows, :]
    blk = _iota(imp_t.shape, 0)
    t = q0 + _iota(imp_t.shape, 1)
    cur = t // SLC_BLOCK
    valid = (blk * SLC_BLOCK <= t) & (blk < ns)
    forced = (blk == 0) | (blk == cur) | (blk == cur - 1)
    score = jnp.where(valid, jnp.where(forced, FORCE_SCORE, imp_t), -jnp.inf)
    cnt = jnp.zeros(imp_t.shape, F32)
    for j in range(ns):
        row = score[j:j + 1, :]
        cnt = cnt + jnp.where(row > score, 1.0, 0.0) + jnp.where(row == score, jnp.where(blk > j, 1.0, 0.0), 0.0)
    sel_t = jnp.where(valid & (cnt < N_SELECT), 1.0, 0.0)
    return jnp.concatenate([sel_t, jnp.zeros((imp.shape[1] - rows, imp.shape[0]), F32)], axis=0).T


def _expand_blocks(sel, n_keys):
    nsp = sel.shape[1]
    blk = _iota((nsp, n_keys), 0)
    key = _iota((nsp, n_keys), 1)
    return _dot(sel, jnp.where(key // SLC_BLOCK == blk, 1.0, 0.0).astype(BF16))


def _sb_prompt_kernel(q_ref, kt_ref, vt_ref, o_ref, ktb, vtb, acc, car, *, tq, nh):
    qi = pl.program_id(2)

    @pl.when(qi == 0)
    def _():
        ktb[...] = kt_ref[0].astype(BF16)
        vtb[...] = vt_ref[0].astype(BF16)

    um = _later_matrix(tq)
    q0 = pl.multiple_of(qi * tq, tq)
    mask = _iota((tq, tq), 1) < _iota((tq, tq), 0)

    qs = [q_ref[0, h] for h in range(nh)]

    def tile(k0, first):
        kt = ktb[:, pl.ds(k0, tq)]
        vt = vtb[:, pl.ds(k0, tq)]
        hrows = lambda a, h: a[h * HEAD_DIM:(h + 1) * HEAD_DIM]
        prev = [(jnp.zeros((tq, 1), F32), None) if first else (car[h], acc[h]) for h in range(nh)]
        zs = [_dot(qs[h], hrows(kt, h)) for h in range(nh)]
        wc = [_sb_weights(zs[h], um, prev[h][0], mask if first else None) for h in range(nh)]
        pvs = [_dot_nt(wc[h][0], hrows(vt, h)) for h in range(nh)]
        for h in range(nh):
            acc[h] = pvs[h] if first else prev[h][1] + pvs[h]
            car[h] = wc[h][1]
        live = wc[0][1]
        for h in range(1, nh):
            live = jnp.maximum(live, wc[h][1])
        return jnp.max(live) > DEAD_LOG

    alive = tile(q0, True)

    def body(state):
        it, _ = state
        return it + 1, tile(pl.multiple_of((qi - 1 - it) * tq, tq), False)

    lax.while_loop(lambda state: (state[0] < qi) & state[1], body, (jnp.int32(0), alive))
    o_ref[0] = jnp.concatenate([acc[h] for h in range(nh)], axis=1).astype(BF16)


def _sb_prompt(q, kvt, tq, nh):
    b, n_h, s, _ = q.shape
    c = nh * HEAD_DIM
    ng = n_h // nh
    return pl.pallas_call(
        functools.partial(_sb_prompt_kernel, tq=tq, nh=nh), name="sb_prompt",
        grid=(b, ng, s // tq),
        in_specs=[pl.BlockSpec((1, nh, tq, HEAD_DIM), lambda i, g, j: (i, g, j, 0)),
                  pl.BlockSpec((1, c, s), lambda i, g, j: (i, g, 0)),
                  pl.BlockSpec((1, c, s), lambda i, g, j: (i, ng + g, 0))],
        out_specs=pl.BlockSpec((1, tq, c), lambda i, g, j: (i, j, g)),
        out_shape=jax.ShapeDtypeStruct((b, s, n_h * HEAD_DIM), BF16),
        scratch_shapes=[pltpu.VMEM((c, s), BF16), pltpu.VMEM((c, s), BF16),
                        pltpu.VMEM((nh, tq, HEAD_DIM), F32), pltpu.VMEM((nh, tq, 1), F32)],
        compiler_params=_params("arbitrary", "arbitrary", "arbitrary"),
    )(q, kvt, kvt)


def _fox_prompt_kernel(q_ref, kt_ref, vt_ref, lft_ref, o_ref, ktb, vtb, ncum, knorm, m_ref, l_ref, acc, *, tq, nh):
    g = pl.program_id(1)
    qi = pl.program_id(2)
    s_len = kt_ref.shape[2]

    @pl.when(qi == 0)
    def _():
        kb = kt_ref[0].astype(BF16)
        ktb[...] = kb
        vtb[...] = vt_ref[0].astype(BF16)
        kf = kb.astype(F32)
        for h in range(nh):
            kh = kf[h * HEAD_DIM:(h + 1) * HEAD_DIM]
            knorm[h] = jnp.full(knorm.shape[1:], jnp.sqrt(jnp.max(jnp.sum(kh * kh, axis=0, keepdims=True))), F32)
        um = _running_sum_matrix(tq, True)
        run = jnp.zeros((lft_ref.shape[1], tq), F32)
        for ch in range(s_len // tq):
            sl = slice(ch * tq, (ch + 1) * tq)
            rs = _dot_x3(lft_ref[0, :, sl], um)
            ncum[:, sl] = -(rs[:, :tq] + run)
            run = run + rs[:, tq:]

    q0 = pl.multiple_of(qi * tq, tq)
    mask = _iota((tq, tq), 1) <= _iota((tq, tq), 0)

    qs = [q_ref[0, h] for h in range(nh)]

    def tile(k0, first):
        kt = ktb[:, pl.ds(k0, tq)]
        vt = vtb[:, pl.ds(k0, tq)]
        bias = [ncum[pl.ds(g * nh + h, 1), pl.ds(k0, tq)] for h in range(nh)]
        hrows = lambda a, h: a[h * HEAD_DIM:(h + 1) * HEAD_DIM]
        if not first:
            prev = [(m_ref[h], l_ref[h], acc[h]) for h in range(nh)]
        ss = [_dot(qs[h], hrows(kt, h)) + bias[h] for h in range(nh)]
        if first:
            ss = [jnp.where(mask, s, NEG) for s in ss]
            ms = [_row_max(s) for s in ss]
        else:
            ms = [jnp.maximum(prev[h][0], _row_max(ss[h])) for h in range(nh)]
        ps = [_exp_minus(s, m) for s, m in zip(ss, ms)]
        pvs = [_dot_nt(ps[h].astype(BF16), hrows(vt, h)) for h in range(nh)]
        sums = [_row_sum(p) for p in ps]
        for h in range(nh):
            if first:
                l_ref[h] = sums[h]
                acc[h] = pvs[h]
            else:
                alpha = jnp.exp(prev[h][0] - ms[h])
                l_ref[h] = alpha * prev[h][1] + sums[h]
                acc[h] = alpha * prev[h][2] + pvs[h]
            m_ref[h] = ms[h]
        live = None
        for h in range(nh):
            bound = qn[h] * knorm[h][0:1, 0:1] + (jnp.min(bias[h], axis=1, keepdims=True) - ms[h])
            live = bound if live is None else jnp.maximum(live, bound)
        return jnp.max(live) > DEAD_LOG

    qn = [jnp.sqrt(jnp.sum(jnp.square(q.astype(F32)), axis=1, keepdims=True)) * 1.001 for q in qs]
    alive = tile(q0, True)

    def body(state):
        it, _ = state
        return it + 1, tile(pl.multiple_of((qi - 1 - it) * tq, tq), False)

    lax.while_loop(lambda state: (state[0] < qi) & state[1], body, (jnp.int32(0), alive))
    o_ref[0] = jnp.concatenate([_softmax_out(l_ref.at[h], acc.at[h]) for h in range(nh)], axis=1).astype(BF16)


def _fox_prompt(q, kvt, lft, tq, nh):
    b, n_h, s, _ = q.shape
    c = nh * HEAD_DIM
    ng = n_h // nh
    n_f = lft.shape[1]
    return pl.pallas_call(
        functools.partial(_fox_prompt_kernel, tq=tq, nh=nh), name="fox_prompt",
        grid=(b, ng, s // tq),
        in_specs=[pl.BlockSpec((1, nh, tq, HEAD_DIM), lambda i, g, j: (i, g, j, 0)),
                  pl.BlockSpec((1, c, s), lambda i, g, j: (i, g, 0)),
                  pl.BlockSpec((1, c, s), lambda i, g, j: (i, ng + g, 0)),
                  pl.BlockSpec((1, n_f, s), lambda i, g, j: (i, 0, 0))],
        out_specs=pl.BlockSpec((1, tq, c), lambda i, g, j: (i, j, g)),
        out_shape=jax.ShapeDtypeStruct((b, s, n_h * HEAD_DIM), BF16),
        scratch_shapes=[pltpu.VMEM((c, s), BF16), pltpu.VMEM((c, s), BF16), pltpu.VMEM((n_f, s), F32),
                        pltpu.VMEM((nh, 8, LANES), F32), pltpu.VMEM((nh, tq, 1), F32), pltpu.VMEM((nh, tq, 1), F32), pltpu.VMEM((nh, tq, HEAD_DIM), F32)],
        compiler_params=_params("arbitrary", "arbitrary", "arbitrary"),
    )(q, kvt, kvt, lft)


def _page_specs(n_pages, rows, npg):
    return [pl.BlockSpec((1, rows, PAGE), functools.partial(
        lambda i, s, pt, off: (pt[i, n_pages - 1 - s * npg - off], 0, 0), off=off)) for off in range(npg)]


def _sb_decode_kernel(pt_ref, q_ref, new_ref, *rest, tq, hg, npg):
    page_refs = rest[:npg]
    o_ref, qbd_ref, newp, acc, car = rest[npg:]
    st = pl.program_id(1)
    da = hg * HEAD_DIM
    r = hg * tq

    @pl.when(st == 0)
    def _():
        qbd_ref[...] = _block_diag_rows(q_ref[0], hg)
        newp[...] = jnp.zeros(newp.shape, F32)
        newp[0:tq, :] = new_ref[0]
        mask = _iota((r, PAGE), 1) < _row_query((r, PAGE), tq)
        w, c = _sb_weights(_dot_nt(qbd_ref[...], newp[:, 0:da].astype(BF16)), _later_matrix(PAGE),
                           jnp.zeros((r, 1), F32), mask)
        acc[...] = _dot(w, newp[:, da:2 * da].astype(BF16))
        car[...] = c

    kt = jnp.concatenate([pg[0, 0:da, :] for pg in page_refs[::-1]], axis=1).astype(BF16)
    vt = jnp.concatenate([pg[0, da:2 * da, :] for pg in page_refs[::-1]], axis=1).astype(BF16)
    w, c = _sb_weights(_dot(qbd_ref[...], kt), _later_matrix(2 * PAGE), car[...], None)
    acc[...] += _dot_nt(w, vt)
    car[...] = c

    @pl.when(st == pl.num_programs(1) - 1)
    def _():
        o_ref[0] = _gather_diag(acc[...], tq, hg).astype(BF16)


def _sb_decode(q, kv_new, cache_t, page_table, npg):
    b, tq, da = q.shape
    hg = da // HEAD_DIM
    n_pages = page_table.shape[1]
    r = hg * tq
    grid_spec = pltpu.PrefetchScalarGridSpec(
        num_scalar_prefetch=1,
        grid=(b, n_pages // npg),
        in_specs=[pl.BlockSpec((1, tq, da), lambda i, s, pt: (i, 0, 0)),
                  pl.BlockSpec((1, tq, 2 * da), lambda i, s, pt: (i, 0, 0))] + _page_specs(n_pages, 2 * da, npg),
        out_specs=pl.BlockSpec((1, tq, da), lambda i, s, pt: (i, 0, 0)),
        scratch_shapes=[pltpu.VMEM((r, da), BF16), pltpu.VMEM((PAGE, 2 * da), F32),
                        pltpu.VMEM((r, da), F32), pltpu.VMEM((r, 1), F32)],
    )
    return pl.pallas_call(
        functools.partial(_sb_decode_kernel, tq=tq, hg=hg, npg=npg), name="sb_decode",
        grid_spec=grid_spec,
        out_shape=jax.ShapeDtypeStruct((b, tq, da), BF16),
        compiler_params=_params("arbitrary", "arbitrary"),
    )(page_table, q, kv_new, *([cache_t] * npg))


def _fox_decode_kernel(pt_ref, q_ref, new_ref, lfnew_ref, *rest, tq, hg, npg):
    page_refs = rest[:npg]
    lf_refs = rest[npg:2 * npg]
    o_ref, qbd_ref, newp, lfp, run_ref, m_ref, l_ref, acc = rest[2 * npg:]
    st = pl.program_id(1)
    da = hg * HEAD_DIM
    r = hg * tq
    um = _running_sum_matrix(PAGE, False)

    def key_bias(lfts):
        rs = [_dot_x3(lft, um) for lft in lfts]
        run = run_ref[...]
        suf = [None] * len(lfts)
        for b in reversed(range(len(lfts))):
            suf[b] = rs[b][:, :PAGE] + run
            run = run + rs[b][:, PAGE:]
        run_ref[...] = run
        suf = suf[0] if len(suf) == 1 else jnp.concatenate(suf, axis=1)
        return jnp.concatenate([jnp.broadcast_to(suf[h:h + 1, :], (tq, suf.shape[1])) for h in range(hg)], axis=0)

    @pl.when(st == 0)
    def _():
        qbd_ref[...] = _block_diag_rows(q_ref[0], hg)
        newp[...] = jnp.zeros(newp.shape, F32)
        newp[0:tq, :] = new_ref[0]
        lfp[...] = jnp.zeros(lfp.shape, F32)
        lfp[0:tq, :] = lfnew_ref[0]
        run_ref[...] = jnp.zeros(run_ref.shape, F32)
        mask = _iota((r, PAGE), 1) <= _row_query((r, PAGE), tq)
        s = _dot_nt(qbd_ref[...], newp[:, 0:da].astype(BF16)) + key_bias([lfp[...].T[0:hg, :]])
        _softmax_first(jnp.where(mask, s, NEG), lambda p: _dot(p, newp[:, da:2 * da].astype(BF16)), m_ref, l_ref, acc)

    kt = jnp.concatenate([pg[0, 0:da, :] for pg in page_refs[::-1]], axis=1).astype(BF16)
    vt = jnp.concatenate([pg[0, da:2 * da, :] for pg in page_refs[::-1]], axis=1).astype(BF16)
    s = _dot(qbd_ref[...], kt) + key_bias([lf[0] for lf in lf_refs[::-1]])
    _softmax_step(s, lambda p: _dot_nt(p, vt), m_ref, l_ref, acc)

    @pl.when(st == pl.num_programs(1) - 1)
    def _():
        o_ref[0] = _gather_diag(_softmax_out(l_ref, acc), tq, hg).astype(BF16)


def _fox_decode(q, kv_new, lf_new, cache_t, cache_lf_t, page_table, npg):
    b, tq, da = q.shape
    hg = da // HEAD_DIM
    n_pages = page_table.shape[1]
    r = hg * tq
    assert cache_lf_t.shape[1] == hg
    grid_spec = pltpu.PrefetchScalarGridSpec(
        num_scalar_prefetch=1,
        grid=(b, n_pages // npg),
        in_specs=[pl.BlockSpec((1, tq, da), lambda i, s, pt: (i, 0, 0)),
                  pl.BlockSpec((1, tq, 2 * da), lambda i, s, pt: (i, 0, 0)),
                  pl.BlockSpec((1, tq, LANES), lambda i, s, pt: (i, 0, 0))]
        + _page_specs(n_pages, 2 * da, npg) + _page_specs(n_pages, hg, npg),
        out_specs=pl.BlockSpec((1, tq, da), lambda i, s, pt: (i, 0, 0)),
        scratch_shapes=[pltpu.VMEM((r, da), BF16), pltpu.VMEM((PAGE, 2 * da), F32),
                        pltpu.VMEM((PAGE, LANES), F32), pltpu.VMEM((hg, PAGE), F32)] + _softmax_scratch(r, da),
    )
    return pl.pallas_call(
        functools.partial(_fox_decode_kernel, tq=tq, hg=hg, npg=npg), name="fox_decode",
        grid_spec=grid_spec,
        out_shape=jax.ShapeDtypeStruct((b, tq, da), BF16),
        compiler_params=_params("arbitrary", "arbitrary"),
    )(page_table, q, kv_new, lf_new, *([cache_t] * npg), *([cache_lf_t] * npg))


def _compress(xb, bd_ref, pe_ref, n_chunk, dkv):
    first = [jnp.zeros((n_chunk, dkv), F32) for _ in range(2)]
    second = [jnp.zeros((n_chunk, dkv), F32) for _ in range(2)]
    for j in range(D_CMP):
        xj = jnp.concatenate([xb[lb, pl.ds(j, n_chunk, stride=D_CMP), :] for lb in range(xb.shape[0])], axis=1)
        for kv in range(2):
            xx = xj[:, kv * dkv:(kv + 1) * dkv]
            first[kv] += _dot((xx + pe_ref[kv, j:j + 1, :]).astype(BF16), bd_ref[kv, j])
            second[kv] += _dot((xx + pe_ref[kv, D_CMP + j:D_CMP + j + 1, :]).astype(BF16), bd_ref[kv, D_CMP + j])
    return [first[kv] + pltpu.roll(second[kv], n_chunk - 1, axis=0) for kv in range(2)]


def _compress_prompt_kernel(x_ref, bd_ref, pe_ref, bdn_ref, gk_ref, kct_ref, vct_ref, xb):
    n_chunk = x_ref.shape[1] // D_CMP
    dkv = bdn_ref.shape[0]
    for lb in range(xb.shape[0]):
        xb[lb] = x_ref[0, :, lb * LANES:(lb + 1) * LANES]
    kc, vc = _compress(xb, bd_ref, pe_ref, n_chunk, dkv)
    kct_ref[0] = (_head_norm(kc, bdn_ref[...]) * gk_ref[...]).T.astype(BF16)
    vct_ref[0] = vc.T.astype(BF16)


def _compress_prompt(cmp_kv, bd, pe, bdn, gk):
    b, s, w = cmp_kv.shape
    dkv = w // 2
    n_chunk = s // D_CMP
    const = lambda a: pl.BlockSpec(a.shape, lambda i: (0,) * a.ndim)
    ospec = pl.BlockSpec((1, dkv, n_chunk), lambda i: (i, 0, 0))
    return pl.pallas_call(
        _compress_prompt_kernel, name="compress_prompt",
        grid=(b,),
        in_specs=[pl.BlockSpec((1, s, w), lambda i: (i, 0, 0)), const(bd), const(pe), const(bdn), const(gk)],
        out_specs=[ospec, ospec],
        out_shape=[jax.ShapeDtypeStruct((b, dkv, n_chunk), BF16)] * 2,
        scratch_shapes=[pltpu.VMEM((w // LANES, s, LANES), F32)],
        compiler_params=_params("arbitrary"),
    )(cmp_kv, bd, pe, bdn, gk)


def _nsa_prompt_kernel(q_ref, kst_ref, vst_ref, kwt_ref, vwt_ref, kct_ref, vct_ref, gates_ref, band_ref, near_ref,
                       wadd_ref, ov_ref, o_ref, ksb, vsb, kwb, vwb, madd, m_ref, acc, ocmp, owin, *, tq, nz, ns):
    g = pl.program_id(1)
    qi = pl.program_id(2)
    s_len = kst_ref.shape[2]
    ncp = kct_ref.shape[2]
    wslab = WINDOW + tq

    @pl.when(qi == 0)
    def _():
        ksb[...] = kst_ref[0].astype(BF16)
        vsb[...] = _with_sum_row(vst_ref[0].astype(BF16))
        kwb[:, 0:WINDOW] = jnp.zeros((HEAD_DIM, WINDOW), BF16)
        vwb[:, 0:WINDOW] = jnp.zeros((HEAD_DIM + SUM_ROWS, WINDOW), BF16)
        kwb[:, WINDOW:] = kwt_ref[0].astype(BF16)
        vwb[:, WINDOW:] = _with_sum_row(vwt_ref[0].astype(BF16))

    q0 = pl.multiple_of(qi * tq, tq)
    zrows = lambda z: slice(z * tq, (z + 1) * tq)

    shift = (qi * (tq // D_CMP) + ncp - BAND_BACK) % ncp
    t_c = q0 + _iota((tq, ncp), 0)
    c_c = _iota((tq, ncp), 1)
    cmask = (t_c >= c_c * D_CMP + (L_CMP - 1)) & (c_c < ncp - 1)
    qs = [q_ref[0, z] for z in range(nz)]
    kct = kct_ref[0]
    vct = vct_ref[0]
    ss = [jnp.where(cmask, _dot(qs[z], kct) + pltpu.roll(band_ref[z], shift, axis=1), NEG) for z in range(nz)]
    ps = [jnp.where(cmask, _exp_minus(s, _row_max(s)), 0.0) for s in ss]
    ps = [_with_row_vector(p, jnp.maximum(_row_sum(p), 1e-30), jnp.divide) for p in ps]
    oc = [_dot_nt(p.astype(BF16), vct) for p in ps]
    pz = ps[0]
    for z in range(1, nz):
        pz = pz + ps[z]
    imp = _dot_x2(pz, ov_ref[...])

    kslab = kwb[:, pl.ds(q0, wslab)]
    vslab = vwb[:, pl.ds(q0, wslab)]
    wadd = wadd_ref[...] + jnp.where(_iota((tq, wslab), 1) >= WINDOW - q0, 0.0, NEG)
    ws = [_dot(qs[z], kslab) + wadd for z in range(nz)]
    ws = [jnp.concatenate([s[:, :wslab - 2 * tq], s[:, wslab - 2 * tq:] + near_ref[z]], axis=1)
          for z, s in enumerate(ws)]
    ow = _softmax_single_chains(ws, vslab)

    sel = _rank_select_t(imp, q0, ns)
    for z in range(nz):
        ocmp[zrows(z), :] = oc[z]
        owin[zrows(z), :] = _normalized(ow[z])
    madd[...] = (_expand_blocks(sel.astype(BF16), s_len) - 1.0) * (-NEG)
    refs = [(m_ref.at[zrows(z)], acc.at[zrows(z)]) for z in range(nz)]

    def slc_tile(k0, tk, extra, first):
        kt = ksb[:, pl.ds(k0, tk)]
        vt = vsb[:, pl.ds(k0, tk)]
        ma = madd[:, pl.ds(k0, tk)]
        scores = [_dot(qs[z], kt) + (ma if extra is None else ma + extra(z)) for z in range(nz)]
        _softmax_chains(scores, vt, refs, first)

    causal = jnp.where(_iota((tq, tq), 1) <= _iota((tq, tq), 0), 0.0, NEG)

    @pl.when(qi == 0)
    def _():
        slc_tile(0, tq, lambda z: near_ref[z, :, tq:2 * tq] + causal, True)

    @pl.when(qi >= 1)
    def _():
        edge = jnp.concatenate([jnp.zeros((tq, tq), F32), causal], axis=1)
        slc_tile(pl.multiple_of(q0 - tq, tq), 2 * tq, lambda z: near_ref[z] + edge, True)

    n_far = jnp.maximum(qi - 1, 0)
    per_far = FAR_TILE // tq

    def far_body(it, _):
        slc_tile(pl.multiple_of(it * FAR_TILE, FAR_TILE), FAR_TILE, None, False)
        return 0

    lax.fori_loop(0, n_far // per_far, far_body, 0)

    def rem_body(it, _):
        slc_tile(pl.multiple_of(((n_far // per_far) * per_far + it) * tq, tq), tq, None, False)
        return 0

    lax.fori_loop(0, n_far % per_far, rem_body, 0)

    n_heads = nz * pl.num_programs(1)
    src = _iota((LANES, LANES), 0)
    dst = _iota((LANES, LANES), 1)
    pick = jnp.where((dst < 3 * nz) & (src == (dst // nz) * n_heads + g * nz + dst % nz), 1.0, 0.0).astype(BF16)
    gsel = _dot_x3(gates_ref[0], pick)
    outs = []
    for z in range(nz):
        gate = lambda br: gsel[:, br * nz + z:br * nz + z + 1]
        outs.append(gate(0) * ocmp[zrows(z), :] + gate(1) * _normalized(acc[zrows(z), :]) + gate(2) * owin[zrows(z), :])
    o_ref[0] = jnp.concatenate(outs, axis=1).astype(BF16)


def _window_mask(tq):
    i = np.arange(tq)[:, None]
    c = np.arange(WINDOW + tq)[None, :]
    return jnp.asarray(np.where((c > i) & (c <= i + WINDOW), 0.0, NEG).astype(np.float32))


def _nsa_prompt(q, slct, wint, kct, vct, gates, band, near, tq, n_kv):
    b, n_heads, s, _ = q.shape
    nz = n_heads // n_kv
    r = nz * tq
    ncp = kct.shape[2]
    ns = -(-s // SLC_BLOCK)
    nsp = -(-ns // LANES) * LANES
    ov = _overlap_matrix(ncp, nsp)
    wadd = _window_mask(tq)
    kv_spec = lambda off: pl.BlockSpec((1, HEAD_DIM, s), lambda i, g, j: (i, off + g, 0))
    c_spec = pl.BlockSpec((1, HEAD_DIM, ncp), lambda i, g, j: (i, g, 0))
    tab_spec = lambda a: pl.BlockSpec((nz,) + a.shape[1:], lambda i, g, j: (g, 0, 0))
    return pl.pallas_call(
        functools.partial(_nsa_prompt_kernel, tq=tq, nz=nz, ns=ns), name="nsa_prompt",
        grid=(b, n_kv, s // tq),
        in_specs=[pl.BlockSpec((1, nz, tq, HEAD_DIM), lambda i, g, j: (i, g, j, 0)),
                  kv_spec(0), kv_spec(n_kv), kv_spec(0), kv_spec(n_kv), c_spec, c_spec,
                  pl.BlockSpec((1, tq, LANES), lambda i, g, j: (i, j, 0)),
                  tab_spec(band), tab_spec(near),
                  pl.BlockSpec(wadd.shape, lambda i, g, j: (0, 0)),
                  pl.BlockSpec(ov.shape, lambda i, g, j: (0, 0))],
        out_specs=pl.BlockSpec((1, tq, nz * HEAD_DIM), lambda i, g, j: (i, j, g)),
        out_shape=jax.ShapeDtypeStruct((b, s, n_heads * HEAD_DIM), BF16),
        scratch_shapes=[pltpu.VMEM((HEAD_DIM, s), BF16), pltpu.VMEM((HEAD_DIM + SUM_ROWS, s), BF16),
                        pltpu.VMEM((HEAD_DIM, WINDOW + s), BF16), pltpu.VMEM((HEAD_DIM + SUM_ROWS, WINDOW + s), BF16),
                        pltpu.VMEM((tq, s), F32), pltpu.VMEM((r, 1), F32), pltpu.VMEM((r, HEAD_DIM + SUM_ROWS), F32),
                        pltpu.VMEM((r, HEAD_DIM), F32), pltpu.VMEM((r, HEAD_DIM), F32)],
        compiler_params=_params("arbitrary", "arbitrary", "arbitrary"),
    )(q, slct, slct, wint, wint, kct, vct, gates, band, near, wadd, ov)


def _cmp_select(q, kcmp, vcmp, bias, ov, q0, tq, nc, ns, n_kv, n_rep):
    ncp = kcmp.shape[0]
    qbd = _nsa_qbd(q, n_kv, n_rep)
    r = qbd.shape[0]
    t = q0 + _row_query((r, ncp), tq)
    cidx = _iota((r, ncp), 1)
    mask = (t - (cidx * D_CMP + L_CMP - 1) >= 0) & (cidx < nc)
    s = jnp.where(mask, _dot_nt(qbd, kcmp) + bias, NEG)
    p = jnp.where(mask, jnp.exp(s - jnp.max(s, axis=1, keepdims=True)), 0.0)
    p = p / jnp.maximum(jnp.sum(p, axis=1, keepdims=True), 1e-30)
    o_cmp = _nsa_gather(_dot(p.astype(BF16), vcmp), tq, n_kv, n_rep)
    rz = n_kv * tq
    pz = p[0:rz]
    for z in range(1, n_rep):
        pz = pz + p[z * rz:(z + 1) * rz]
    imp = _dot_x2(pz, ov)
    return o_cmp, _rank_select(imp, q0 + _row_query(imp.shape, tq), ns)


def _overlap_matrix(ncp, nsp):
    ci = np.arange(ncp)[:, None]
    bj = np.arange(nsp)[None, :]
    ov = (ci * D_CMP <= bj * SLC_BLOCK + SLC_BLOCK - 1) & (ci * D_CMP + L_CMP - 1 >= bj * SLC_BLOCK)
    return jnp.asarray(ov.astype(np.float32), BF16)


def _cmp_decode_kernel(pt_ref, q_ref, bd_ref, pe_ref, bdn_ref, gk_ref, bias_ref, ov_ref, *rest,
                       tq, n_kv, n_rep, ns, npg, past):
    page_refs = rest[:npg]
    ocmp_ref, sel_ref, xbuf = rest[npg:]
    st = pl.program_id(1)
    n_steps = pl.num_programs(1)
    dkv = n_kv * HEAD_DIM
    n_chunk = past // D_CMP
    for off, pg in enumerate(page_refs):
        pos = (n_steps - 1 - st) * npg + (npg - 1 - off)
        rows = pg[0].T
        for lb in range(xbuf.shape[0]):
            xbuf[lb, pl.ds(pl.multiple_of(pos * PAGE, PAGE), PAGE), :] = rows[:, lb * LANES:(lb + 1) * LANES]

    @pl.when(st == n_steps - 1)
    def _():
        kc, vc = _compress(xbuf, bd_ref, pe_ref, n_chunk, dkv)
        kcb = (_head_norm(kc, bdn_ref[...]) * gk_ref[...]).astype(BF16)
        bias = bias_ref[...].reshape(n_kv * n_rep * tq, n_chunk)
        o_cmp, sel = _cmp_select(q_ref[0], kcb, vc.astype(BF16), bias, ov_ref[...], past, tq,
                                 n_chunk - 1, ns, n_kv, n_rep)
        ocmp_ref[0] = o_cmp
        sel_ref[0] = sel.astype(BF16)


def _cmp_decode(q, cache_t, page_table, bd, pe, bdn, gk, bias, n_kv, npg):
    b, tq, dq = q.shape
    dkv = n_kv * HEAD_DIM
    n_rep = dq // dkv
    n_pages = page_table.shape[1]
    past = n_pages * PAGE
    n_chunk = (past + tq) // D_CMP
    assert n_chunk == past // D_CMP
    ns = -(-(past + tq) // SLC_BLOCK)
    nsp = -(-ns // LANES) * LANES
    ov = _overlap_matrix(n_chunk, nsp)
    const = lambda a: pl.BlockSpec(a.shape, lambda i, s, pt: (0,) * a.ndim)
    grid_spec = pltpu.PrefetchScalarGridSpec(
        num_scalar_prefetch=1,
        grid=(b, n_pages // npg),
        in_specs=[pl.BlockSpec((1, tq, dq), lambda i, s, pt: (i, 0, 0)),
                  const(bd), const(pe), const(bdn), const(gk), const(bias), const(ov)]
        + _page_specs(n_pages, 2 * dkv, npg),
        out_specs=[pl.BlockSpec((1, tq, dq), lambda i, s, pt: (i, 0, 0)),
                   pl.BlockSpec((1, n_kv * tq, nsp), lambda i, s, pt: (i, 0, 0))],
        scratch_shapes=[pltpu.VMEM((2 * dkv // LANES, past, LANES), F32)],
    )
    return pl.pallas_call(
        functools.partial(_cmp_decode_kernel, tq=tq, n_kv=n_kv, n_rep=n_rep, ns=ns, npg=npg, past=past),
        name="cmp_decode",
        grid_spec=grid_spec,
        out_shape=[jax.ShapeDtypeStruct((b, tq, dq), F32), jax.ShapeDtypeStruct((b, n_kv * tq, nsp), BF16)],
        compiler_params=_params("arbitrary", "arbitrary"),
    )(page_table, q, bd, pe, bdn, gk, bias, ov, *([cache_t] * npg))


def _nsa_qbd(q, n_kv, n_rep):
    dkv = n_kv * HEAD_DIM
    return jnp.concatenate([_block_diag_rows(q[:, z * dkv:(z + 1) * dkv], n_kv) for z in range(n_rep)], axis=0)


def _nsa_gather(o_full, tq, n_kv, n_rep):
    rz = n_kv * tq
    return jnp.concatenate([_gather_diag(o_full[z * rz:(z + 1) * rz], tq, n_kv) for z in range(n_rep)], axis=1)


def _merge(gates, ege_ref, o_cmp, o_slc, o_win):
    g = [_dot_x2(gates, ege_ref[br]) for br in range(3)]
    return g[0] * o_cmp + g[1] * o_slc + g[2] * o_win


def _slc_decode_kernel(pt_ref, q_ref, slcnew_ref, winnew_ref, state_ref, sel_ref, gates_ref, ocmp_ref,
                       t0_ref, t1_ref, ege_ref, *rest, tq, n_kv, n_rep, npg, past):
    page_refs = rest[:npg]
    o_ref, qbd_ref, newp, kmask, m_ref, l_ref, acc = rest[npg:]
    st = pl.program_id(1)
    n_steps = pl.num_programs(1)
    dkv = n_kv * HEAD_DIM
    r = n_rep * n_kv * tq
    i_row = _row_query((r, PAGE), tq)
    j_col = _iota((r, PAGE), 1)

    def sel_mask(k0, width=PAGE):
        return jnp.concatenate([kmask[:, pl.ds(k0, width)]] * n_rep, axis=0) > 0.5

    def step(s, mask, pv):
        _softmax_step(jnp.where(mask, s, NEG) if mask is not None else s, pv, m_ref, l_ref, acc, mask)

    def new_tile(bias, mask):
        k = newp[:, 0:dkv].astype(BF16)
        v = newp[:, dkv:2 * dkv].astype(BF16)
        step(_dot_nt(qbd_ref[...], k) + bias, mask, lambda p: _dot(p, v))

    @pl.when(st == 0)
    def _():
        qbd_ref[...] = _nsa_qbd(q_ref[0], n_kv, n_rep)
        kmask[...] = _expand_blocks(sel_ref[0], kmask.shape[1])
        newp[...] = jnp.zeros(newp.shape, F32)
        newp[0:tq, :] = slcnew_ref[0]
        _softmax_init(m_ref, l_ref, acc)
        new_tile(t0_ref[...], sel_mask(past) & (j_col <= i_row))

    width = npg * PAGE
    k0 = pl.multiple_of((n_steps - 1 - st) * width, width)
    kt = jnp.concatenate([pg[0, 0:dkv, :] for pg in page_refs[::-1]], axis=1).astype(BF16)
    vt = jnp.concatenate([pg[0, dkv:2 * dkv, :] for pg in page_refs[::-1]], axis=1).astype(BF16)
    s = _dot(qbd_ref[...], kt)
    near = s[:, width - PAGE:] + jnp.where(st == 0, 1.0, 0.0) * t1_ref[...]
    s = jnp.concatenate([s[:, :width - PAGE], near], axis=1)
    step(s, sel_mask(k0, width), lambda p: _dot_nt(p, vt))

    @pl.when(st == n_steps - 1)
    def _():
        o_slc = _nsa_gather(_softmax_out(l_ref, acc), tq, n_kv, n_rep)
        _softmax_init(m_ref, l_ref, acc)
        newp[0:tq, :] = winnew_ref[0]
        new_tile(t0_ref[...], j_col <= i_row)
        n_back = WINDOW // PAGE
        for back in range(1, n_back + 1):
            sl = slice((n_back - back) * PAGE, (n_back - back + 1) * PAGE)
            s = _dot(qbd_ref[...], state_ref[0, 0:dkv, sl].astype(BF16))
            if back == 1:
                s = s + t1_ref[...]
            mask = (j_col > i_row) if back == n_back else None
            step(s, mask, lambda p, sl=sl: _dot_nt(p, state_ref[0, dkv:2 * dkv, sl].astype(BF16)))
        o_win = _nsa_gather(_softmax_out(l_ref, acc), tq, n_kv, n_rep)
        o_ref[0] = _merge(gates_ref[0], ege_ref, ocmp_ref[0], o_slc, o_win).astype(BF16)


def _slc_decode(q, slc_new, win_new, state_t, sel, gates, o_cmp, cache_t, page_table, t0, t1, ege, n_kv, npg):
    b, tq, dq = q.shape
    dkv = n_kv * HEAD_DIM
    n_rep = dq // dkv
    r = n_rep * n_kv * tq
    n_pages = page_table.shape[1]
    past = n_pages * PAGE
    const = lambda a: pl.BlockSpec(a.shape, lambda i, s, pt: (0,) * a.ndim)
    per_seq = lambda a: pl.BlockSpec((1,) + a.shape[1:], lambda i, s, pt: (i,) + (0,) * (a.ndim - 1))
    grid_spec = pltpu.PrefetchScalarGridSpec(
        num_scalar_prefetch=1,
        grid=(b, n_pages // npg),
        in_specs=[per_seq(q), per_seq(slc_new), per_seq(win_new), per_seq(state_t), per_seq(sel), per_seq(gates),
                  per_seq(o_cmp), const(t0), const(t1), const(ege)] + _page_specs(n_pages, 2 * dkv, npg),
        out_specs=pl.BlockSpec((1, tq, dq), lambda i, s, pt: (i, 0, 0)),
        scratch_shapes=[pltpu.VMEM((r, dkv), BF16), pltpu.VMEM((PAGE, 2 * dkv), F32),
                        pltpu.VMEM((n_kv * tq, past + PAGE), F32)] + _softmax_scratch(r, dkv),
    )
    return pl.pallas_call(
        functools.partial(_slc_decode_kernel, tq=tq, n_kv=n_kv, n_rep=n_rep, npg=npg, past=past),
        name="slc_decode",
        grid_spec=grid_spec,
        out_shape=jax.ShapeDtypeStruct((b, tq, dq), BF16),
        compiler_params=_params("arbitrary", "arbitrary"),
    )(page_table, q, slc_new, win_new, state_t, sel, gates, o_cmp, t0, t1, ege, *([cache_t] * npg))


def _rel_buckets(dist):
    n = np.maximum(dist, 0)
    exact = N_BUCKETS // 2
    nf = np.maximum(n, 1).astype(np.float64)
    large = exact + (np.log(nf / exact) / math.log(REL_MAX_DIST / exact) * (N_BUCKETS - exact)).astype(np.int64)
    return np.where(n < exact, n, np.minimum(large, N_BUCKETS - 1)).astype(np.int32)


def _head_block_diag(width, scale):
    h = np.arange(width) // HEAD_DIM
    return jnp.asarray((h[:, None] == h[None, :]).astype(np.float32) * scale, BF16)


class _NsaLayout:
    def __init__(self, n_heads, n_kv):
        n_rep = n_heads // n_kv
        self.n_kv, self.n_rep = n_kv, n_rep
        new = np.arange(n_heads)
        z, g = new // n_kv, new % n_kv
        self.head_perm = g * n_rep + z
        self.col_perm = (self.head_perm[:, None] * HEAD_DIM + np.arange(HEAD_DIM)[None, :]).reshape(-1)
        self.gate_perm = (np.arange(3)[:, None] * n_heads + self.head_perm[None, :]).reshape(-1)
        ege = np.zeros((3, LANES, n_heads * HEAD_DIM), np.float32)
        for br in range(3):
            for h in range(n_heads):
                ege[br, br * n_heads + h, h * HEAD_DIM:(h + 1) * HEAD_DIM] = 1.0
        self.ege = jnp.asarray(ege, BF16)


def _bucket_lookup(rel_hd, dist):
    buckets = jnp.asarray(_rel_buckets(dist).reshape(-1))
    onehot = (jnp.arange(N_BUCKETS, dtype=jnp.int32)[:, None] == buckets[None, :]).astype(F32)
    tab = jnp.dot(rel_hd, onehot, precision=lax.Precision.HIGHEST)
    return tab.reshape((rel_hd.shape[0],) + dist.shape)


def _near_bias(rel_hd, dist):
    far = rel_hd[:, N_BUCKETS - 1]
    tab = _bucket_lookup(rel_hd, dist) - far[:, None, None]
    return jnp.where(jnp.asarray(dist >= 0)[None], tab, 0.0)


def _toeplitz_tiles(rel_hd, tq, tk):
    i = np.arange(tq)[:, None]
    j = np.arange(tk)[None, :]
    return _near_bias(rel_hd, i - j), _near_bias(rel_hd, tk + i - j)


def _cmp_band(rel_hd, tq, ncp):
    i = np.arange(tq)[:, None]
    m = np.arange(ncp)[None, :]
    dist = D_CMP * (BAND_BACK - m) + i - (L_CMP - 1)
    return _near_bias(rel_hd, np.where(m < 2 * BAND_BACK + tq // D_CMP, dist, -1))


def _cmp_bias(rel_hd, qpos, n_chunk):
    dc = qpos[:, None] - (np.arange(n_chunk)[None, :] * D_CMP + L_CMP - 1)
    return _bucket_lookup(rel_hd, dc)


def kernel(x_prompt, x_sample, cache_sb_kv, cache_fox_kv, cache_fox_logf, cache_cmp_kv, cache_slc_kv, state_win_kv,
           page_table, c_prompt, c_sample, norm_gain, w_ada, b_ada, ffn_w_in, ffn_w_out, w_in_ab, b_forget,
           fox_qk_gain, w_out_ab, w_in_nsa, b_nsa_gate, nsa_qk_gain, cmp_w, cmp_pe, rel_bias, w_out_nsa):
    bp, s_len, d = x_prompt.shape
    bs, t_dec, _ = x_sample.shape
    n_pool = cache_sb_kv.shape[0]
    h_sb, h_fox = cache_sb_kv.shape[3], cache_fox_kv.shape[3]
    n_kv = cache_cmp_kv.shape[3]
    n_heads = rel_bias.shape[1]
    da = h_sb * HEAD_DIM
    dkv = n_kv * HEAD_DIM
    dq = n_heads * HEAD_DIM
    assert h_sb == h_fox
    n_pages = page_table.shape[1]
    past = n_pages * PAGE
    lay = _NsaLayout(n_heads, n_kv)

    ffn_in = ffn_w_in.astype(BF16)
    ffn_out = ffn_w_out.astype(BF16)
    w_ab = jnp.pad(w_in_ab, ((0, 0), (0, LANES - h_fox))).astype(BF16)
    bfp = jnp.pad(b_forget, (0, LANES - h_fox)).reshape(1, LANES)
    bd_ab = _head_block_diag(da, 1.0 / HEAD_DIM)
    gq_fox = jnp.tile(fox_qk_gain[0], h_fox).reshape(1, da)
    gk_fox = jnp.tile(fox_qk_gain[1], h_fox).reshape(1, da)
    wq_ab = jnp.concatenate([w_in_ab[:, 0:da], w_in_ab[:, 3 * da:4 * da]], axis=1).astype(BF16)
    wkvt_ab = jnp.concatenate([w_in_ab[:, da:3 * da], w_in_ab[:, 4 * da:6 * da]], axis=1).T.astype(BF16)
    wflt_ab = jnp.pad(w_in_ab[:, 6 * da:].T, ((0, 16 - h_fox), (0, 0))).astype(BF16)
    w_out_sb = w_out_ab[:da].astype(BF16)
    w_out_fox = w_out_ab[da:].astype(BF16)
    n_rep = n_heads // n_kv
    wq_swapped = w_in_nsa[:, 0:dq].reshape(d, n_kv, n_rep, HEAD_DIM).transpose(0, 2, 1, 3).reshape(d, dq)
    wg_swapped = w_in_nsa[:, dq + 6 * dkv:].reshape(d, 3, n_kv, n_rep).transpose(0, 1, 3, 2).reshape(d, 3 * n_heads)
    w_nsa = jnp.concatenate([wq_swapped, w_in_nsa[:, dq:dq + 6 * dkv], wg_swapped], axis=1)
    w_nsa = jnp.pad(w_nsa, ((0, 0), (0, LANES - 3 * n_heads))).astype(BF16)
    bg_swapped = b_nsa_gate.reshape(3, n_kv, n_rep).transpose(0, 2, 1).reshape(3 * n_heads)
    bg_perm = jnp.pad(bg_swapped, (0, LANES - 3 * n_heads)).reshape(1, LANES)
    wq_nsa = w_in_nsa[:, 0:dq].astype(BF16)
    wcmp_nsa = w_in_nsa[:, dq:dq + 2 * dkv].astype(BF16)
    wkvt_nsa = w_in_nsa[:, dq + 2 * dkv:dq + 6 * dkv].T.astype(BF16)
    wg_nsa = jnp.pad(w_in_nsa[:, dq + 6 * dkv:], ((0, 0), (0, LANES - 3 * n_heads))).astype(BF16)
    bg = jnp.pad(b_nsa_gate, (0, LANES - 3 * n_heads)).reshape(1, LANES)
    bd_kv = _head_block_diag(dkv, 1.0 / HEAD_DIM)
    tile_kv = lambda g: jnp.tile(g, n_kv).reshape(1, dkv)
    w_out_n = w_out_nsa.astype(BF16)
    w_out_n_perm = w_out_nsa.reshape(n_kv, n_rep, HEAD_DIM, d).transpose(1, 0, 2, 3).reshape(dq, d).astype(BF16)
    eye = jnp.eye(n_kv, dtype=F32)
    cmp_bd = jnp.einsum('gh,kjde->kjgdhe', eye, cmp_w).reshape(2, L_CMP, dkv, dkv).astype(BF16)
    cmp_pe_t = jnp.tile(cmp_pe, (1, 1, n_kv))
    rel_orig = rel_bias.T
    rel_perm = rel_bias.reshape(N_BUCKETS, n_kv, n_rep).transpose(0, 2, 1).reshape(N_BUCKETS, n_heads).T
    gk_cmp = tile_kv(nsa_qk_gain[1])

    mod = _modulation(jnp.concatenate([c_prompt, c_sample], axis=0), w_ada, b_ada)
    mod = mod.reshape(mod.shape[0], bp + bs, 3, 3, 1, d)

    def mods(l, sub, lo, hi):
        return tuple(mod[l, lo:hi, sub, k] for k in range(3))

    mp = lambda l, sub: mods(l, sub, 0, bp)
    tt = 512
    x = _ffn(x_prompt, mp(0, 0), norm_gain[0, 0], ffn_in[0, 0], ffn_out[0, 0], 1, tt)
    qsb, qfx, sbt, fxt, lft = _proj_ab_t(x, mp(0, 1), norm_gain[0, 1], wq_ab, wkvt_ab, wflt_ab, bd_ab, gq_fox,
                                         gk_fox.reshape(da, 1), b_forget.reshape(h_fox, 1), tt)
    o_sb = _sb_prompt(qsb, sbt, AB_TILE, 2)
    o_fox = _fox_prompt(qfx, fxt, lft, AB_TILE, 2)
    x = _outproj(x, mp(0, 1)[2], [o_sb, o_fox], [w_out_sb, w_out_fox], 1, tt)
    x = _ffn(x, mp(0, 2), norm_gain[0, 2], ffn_in[0, 1], ffn_out[0, 1], 1, tt)
    x = _ffn(x, mp(1, 0), norm_gain[1, 0], ffn_in[1, 0], ffn_out[1, 0], 1, tt)
    q, p_cmp, slct, wint, gates = _proj_nsa_t(x, mp(1, 1), norm_gain[1, 1], wq_nsa, wcmp_nsa, wkvt_nsa, wg_nsa, bd_kv,
                                              tile_kv(nsa_qk_gain[0]), tile_kv(nsa_qk_gain[2]).reshape(dkv, 1),
                                              tile_kv(nsa_qk_gain[3]).reshape(dkv, 1), bg, tt)
    kct, vct = _compress_prompt(p_cmp, cmp_bd, cmp_pe_t, bd_kv, gk_cmp)
    t0, t1 = _toeplitz_tiles(rel_orig, TILE, TILE)
    o = _nsa_prompt(q, slct, wint, kct, vct, gates, _cmp_band(rel_orig, TILE, s_len // D_CMP),
                    jnp.concatenate([t1, t0], axis=2), TILE, n_kv)
    x = _outproj(x, mp(1, 1)[2], [o], [w_out_n], 1, tt)
    y_p = _ffn(x, mp(1, 2), norm_gain[1, 2], ffn_in[1, 1], ffn_out[1, 1], 1, tt)

    c_sb = jnp.transpose(cache_sb_kv, (0, 2, 3, 4, 1)).reshape(n_pool, 2 * da, PAGE)
    c_fox = jnp.transpose(cache_fox_kv, (0, 2, 3, 4, 1)).reshape(n_pool, 2 * da, PAGE)
    c_lf = jnp.transpose(cache_fox_logf, (0, 2, 1))
    c_cmp = jnp.transpose(cache_cmp_kv, (0, 2, 3, 4, 1)).reshape(n_pool, 2 * dkv, PAGE)
    c_slc = jnp.transpose(cache_slc_kv, (0, 2, 3, 4, 1)).reshape(n_pool, 2 * dkv, PAGE)
    state_t = jnp.transpose(state_win_kv, (0, 2, 3, 4, 1)).reshape(bs, 2 * dkv, state_win_kv.shape[1])
    npg = min(n_pages, STEP_BYTES // (2 * da * PAGE * 4))
    npg_kv = min(n_pages, STEP_BYTES // (2 * dkv * PAGE * 4))
    assert n_pages % npg == 0 and n_pages % npg_kv == 0
    ms = lambda l, sub: mods(l, sub, bp, bp + bs)
    x = _ffn(x_sample, ms(0, 0), norm_gain[0, 0], ffn_in[0, 0], ffn_out[0, 0], bs, t_dec)
    qsb, s_sb, qfx, s_fox, s_logf, logfp = _proj_ab(x, ms(0, 1), norm_gain[0, 1], w_ab, bd_ab, gq_fox, gk_fox, bfp,
                                                    h_fox, bs, t_dec)
    o_sb = _sb_decode(qsb, s_sb, c_sb, page_table, npg)
    o_fox = _fox_decode(qfx, s_fox, logfp, c_fox, c_lf, page_table, npg)
    x = _outproj(x, ms(0, 1)[2], [o_sb, o_fox], [w_out_sb, w_out_fox], bs, t_dec)
    x = _ffn(x, ms(0, 2), norm_gain[0, 2], ffn_in[0, 1], ffn_out[0, 1], bs, t_dec)
    x = _ffn(x, ms(1, 0), norm_gain[1, 0], ffn_in[1, 0], ffn_out[1, 0], bs, t_dec)
    q, s_cmp, s_slc, s_win, gates = _proj_nsa(x, ms(1, 1), norm_gain[1, 1], w_nsa, bd_kv, tile_kv(nsa_qk_gain[0]),
                                              tile_kv(nsa_qk_gain[2]), tile_kv(nsa_qk_gain[3]), bg_perm, dq, bs, t_dec)
    bias = _cmp_bias(rel_perm, past + np.arange(t_dec), past // D_CMP)
    o_cmp, sel = _cmp_decode(q, c_cmp, page_table, cmp_bd, cmp_pe_t, bd_kv, gk_cmp, bias, n_kv, npg_kv)
    t0, t1 = _toeplitz_tiles(rel_perm, t_dec, PAGE)
    o = _slc_decode(q, s_slc, s_win, state_t, sel, gates, o_cmp, c_slc, page_table,
                    t0.reshape(-1, PAGE), t1.reshape(-1, PAGE), lay.ege, n_kv, npg_kv)
    x = _outproj(x, ms(1, 1)[2], [o], [w_out_n_perm], bs, t_dec)
    y_s = _ffn(x, ms(1, 2), norm_gain[1, 2], ffn_in[1, 1], ffn_out[1, 1], bs, t_dec)

    kv5 = lambda a, h: a.reshape(a.shape[0], a.shape[1], 2, h, HEAD_DIM)
    from_t = lambda a, h: kv5(jnp.transpose(a, (0, 2, 1)), h)
    win_len = min(WINDOW, s_len)
    state_rows = state_win_kv.reshape(bs, state_win_kv.shape[1], 2 * dkv)
    s_win_all = jnp.concatenate([state_rows, s_win], axis=1)
    new_len = min(WINDOW, s_win_all.shape[1])
    return (y_p, y_s, from_t(sbt, h_sb), kv5(s_sb, h_sb), from_t(fxt, h_fox), kv5(s_fox, h_fox),
            jnp.transpose(lft, (0, 2, 1)), s_logf, kv5(p_cmp, n_kv), kv5(s_cmp, n_kv), from_t(slct, n_kv),
            kv5(s_slc, n_kv), from_t(wint[:, :, s_len - win_len:], n_kv),
            kv5(s_win_all[:, s_win_all.shape[1] - new_len:], n_kv))
```

```python
import functools
import math

import numpy as np
import jax
import jax.numpy as jnp
from jax import lax
from jax.experimental import pallas as pl
from jax.experimental.pallas import tpu as pltpu

F32 = jnp.float32
BF16 = jnp.bfloat16

HEAD_DIM = 64
PAGE = 128
L_CMP = 32
D_CMP = 16
SLC_BLOCK = 64
N_SELECT = 16
WINDOW = 512
N_BUCKETS = 32
REL_MAX_DIST = 128
FORCE_SCORE = 1e4
NEG = -1e30
EPS = 1e-6
MACARON = 0.5
LANES = 128
VMEM_LIMIT = 56 * 1024 * 1024
FF_CHUNK = 256
TILE = 128
FAR_TILE = 256
AB_TILE = 256
STEP_BYTES = 8 * 1024 * 1024
DEAD_LOG = -104.0
SUM_ROWS = 16
BAND_BACK = (REL_MAX_DIST + L_CMP) // D_CMP


def _params(*sem):
    return pltpu.CompilerParams(dimension_semantics=sem, vmem_limit_bytes=VMEM_LIMIT)


def _dot(a, b):
    return jnp.dot(a, b, preferred_element_type=F32)


def _dot_nt(a, b):
    return lax.dot_general(a, b, (((1,), (1,)), ((), ())), preferred_element_type=F32)


def _split2(x):
    hi = x.astype(BF16)
    lo = (x - hi.astype(F32)).astype(BF16)
    return hi, lo


def _split3(x):
    hi = x.astype(BF16)
    r = x - hi.astype(F32)
    mid = r.astype(BF16)
    lo = (r - mid.astype(F32)).astype(BF16)
    return hi, mid, lo


def _dot_x2(x, w):
    hi, lo = _split2(x)
    return _dot(hi, w) + _dot(lo, w)


def _dot_x3(x, w):
    hi, mid, lo = _split3(x)
    return _dot(hi, w) + _dot(mid, w) + _dot(lo, w)


def _dot_l2(w, x):
    hi, lo = _split2(x)
    return _dot(w, hi) + _dot(w, lo)


def _log_sigmoid(x):
    return jnp.minimum(x, 0.0) - jnp.log(1.0 + jnp.exp(-jnp.abs(x)))


def _adanorm(x, g, scale, shift):
    ms = jnp.mean(x * x, axis=-1, keepdims=True)
    return (x * lax.rsqrt(ms + EPS) * g) * (1.0 + scale) + shift


def _head_norm(t, bd):
    return t * lax.rsqrt(_dot_x2(t * t, bd) + EPS)


def _head_norm_t(t, bd):
    return t * lax.rsqrt(_dot_l2(bd, t * t) + EPS)


def _iota(shape, dim):
    return lax.broadcasted_iota(jnp.int32, shape, dim)


def _mod_kernel(c_ref, w_ref, b_ref, o_ref):
    c = c_ref[...]
    cs = c * jax.nn.sigmoid(c)
    ch, cl = _split2(cs)
    wh, wl = _split2(w_ref[0])
    o_ref[0] = _dot(ch, wh) + _dot(cl, wh) + _dot(ch, wl) + b_ref[0]


def _modulation(c_all, w_ada, b_ada):
    depth, d, n = w_ada.shape
    rows = c_all.shape[0]
    tn = 1024
    return pl.pallas_call(
        _mod_kernel, name="modulation",
        grid=(depth, n // tn),
        in_specs=[pl.BlockSpec((rows, d), lambda l, j: (0, 0)),
                  pl.BlockSpec((1, d, tn), lambda l, j: (l, 0, j)),
                  pl.BlockSpec((1, 1, tn), lambda l, j: (l, 0, j))],
        out_specs=pl.BlockSpec((1, rows, tn), lambda l, j: (l, 0, j)),
        out_shape=jax.ShapeDtypeStruct((depth, rows, n), F32),
        compiler_params=_params("arbitrary", "arbitrary"),
    )(c_all, w_ada, b_ada.reshape(depth, 1, n))


def _ffn_kernel(x_ref, shift_ref, scale_ref, gate_ref, g_ref, win_ref, wout_ref, o_ref, hid_ref, *, d_ff):
    x = x_ref[...]
    bb, tt, d = x.shape
    h = _adanorm(x, g_ref[...], scale_ref[...], shift_ref[...]).reshape(bb * tt, d).astype(BF16)
    for c in range(d_ff // FF_CHUNK):
        a = _dot(h, win_ref[:, c * FF_CHUNK:(c + 1) * FF_CHUNK])
        b = _dot(h, win_ref[:, d_ff + c * FF_CHUNK:d_ff + (c + 1) * FF_CHUNK])
        hid_ref[:, c * FF_CHUNK:(c + 1) * FF_CHUNK] = (a * jax.nn.sigmoid(a) * b).astype(BF16)
    o = _dot(hid_ref[...], wout_ref[...]).reshape(bb, tt, d)
    o_ref[...] = x + (MACARON * gate_ref[...]) * o


def _ffn(x, mod3, g, w_in, w_out, bb, tt):
    b, t, d = x.shape
    d_ff = w_out.shape[0]
    shift, scale, gate = mod3
    mspec = pl.BlockSpec((bb, 1, d), lambda i, j: (i, 0, 0))
    xspec = pl.BlockSpec((bb, tt, d), lambda i, j: (i, j, 0))
    return pl.pallas_call(
        functools.partial(_ffn_kernel, d_ff=d_ff), name="ffn",
        grid=(b // bb, t // tt),
        in_specs=[xspec, mspec, mspec, mspec,
                  pl.BlockSpec((1, 1, d), lambda i, j: (0, 0, 0)),
                  pl.BlockSpec((d, 2 * d_ff), lambda i, j: (0, 0)),
                  pl.BlockSpec((d_ff, d), lambda i, j: (0, 0))],
        out_specs=xspec,
        out_shape=jax.ShapeDtypeStruct(x.shape, F32),
        scratch_shapes=[pltpu.VMEM((bb * tt, d_ff), BF16)],
        compiler_params=_params("arbitrary", "arbitrary"),
    )(x, shift, scale, gate, g.reshape(1, 1, d), w_in, w_out)


def _outproj_kernel(*refs, n_in):
    x_ref, gate_ref = refs[0], refs[1]
    o_refs = refs[2:2 + n_in]
    w_refs = refs[2 + n_in:2 + 2 * n_in]
    out_ref = refs[2 + 2 * n_in]
    x = x_ref[...]
    bb, tt, d = x.shape
    y = None
    for o_ref, w_ref in zip(o_refs, w_refs):
        o = o_ref[...].astype(F32).reshape(bb * tt, o_ref.shape[-1]).astype(BF16)
        part = _dot(o, w_ref[...])
        y = part if y is None else y + part
    out_ref[...] = x + gate_ref[...] * y.reshape(bb, tt, d)


def _outproj(x, gate, outs, weights, bb, tt):
    b, t, d = x.shape
    n_in = len(outs)
    xspec = pl.BlockSpec((bb, tt, d), lambda i, j: (i, j, 0))
    in_specs = [xspec, pl.BlockSpec((bb, 1, d), lambda i, j: (i, 0, 0))]
    in_specs += [pl.BlockSpec((bb, tt, o.shape[-1]), lambda i, j: (i, j, 0)) for o in outs]
    in_specs += [pl.BlockSpec(w.shape, lambda i, j: (0, 0)) for w in weights]
    return pl.pallas_call(
        functools.partial(_outproj_kernel, n_in=n_in), name="outproj",
        grid=(b // bb, t // tt),
        in_specs=in_specs,
        out_specs=xspec,
        out_shape=jax.ShapeDtypeStruct(x.shape, F32),
        compiler_params=_params("arbitrary", "arbitrary"),
    )(x, gate, *outs, *weights)


def _proj_ab_kernel(x_ref, shift_ref, scale_ref, g_ref, w_ref, bd_ref, gq_ref, gk_ref, bf_ref,
                    qsb_ref, sbkv_ref, qfx_ref, fxkv_ref, logf_ref, logfp_ref, *, da, n_f):
    x = x_ref[...]
    bb, tt, d = x.shape
    m = bb * tt
    h = _adanorm(x, g_ref[...], scale_ref[...], shift_ref[...]).reshape(m, d).astype(BF16)
    sc = HEAD_DIM ** -0.5
    qsb_ref[...] = (_dot(h, w_ref[:, 0:da]) * sc).reshape(bb, tt, da).astype(BF16)
    sbkv_ref[...] = _dot(h, w_ref[:, da:3 * da]).reshape(bb, tt, 2 * da)
    bd = bd_ref[...]
    qf = _head_norm(_dot(h, w_ref[:, 3 * da:4 * da]), bd) * gq_ref[...]
    qfx_ref[...] = (qf * sc).reshape(bb, tt, da).astype(BF16)
    kf = _head_norm(_dot(h, w_ref[:, 4 * da:5 * da]), bd) * gk_ref[...]
    fxkv_ref[:, :, 0:da] = kf.reshape(bb, tt, da)
    fxkv_ref[:, :, da:2 * da] = _dot(h, w_ref[:, 5 * da:6 * da]).reshape(bb, tt, da)
    lf = _log_sigmoid(_dot(h, w_ref[:, 6 * da:6 * da + LANES]) + bf_ref[...])
    lf = jnp.where(_iota(lf.shape, 1) < n_f, lf, 0.0)
    logfp_ref[...] = lf.reshape(bb, tt, LANES)
    logf_ref[...] = lf[:, 0:n_f].reshape(bb, tt, n_f)


def _proj_ab(x, mod3, g, w, bd, gq, gk, bfp, n_f, bb, tt):
    b, t, d = x.shape
    da = bd.shape[0]
    shift, scale, _ = mod3
    mspec = pl.BlockSpec((bb, 1, d), lambda i, j: (i, 0, 0))
    xspec = pl.BlockSpec((bb, tt, d), lambda i, j: (i, j, 0))
    row = lambda width: pl.BlockSpec((1, width), lambda i, j: (0, 0))
    ospec = lambda width: pl.BlockSpec((bb, tt, width), lambda i, j: (i, j, 0))
    return pl.pallas_call(
        functools.partial(_proj_ab_kernel, da=da, n_f=n_f), name="proj_ab",
        grid=(b // bb, t // tt),
        in_specs=[xspec, mspec, mspec, pl.BlockSpec((1, 1, d), lambda i, j: (0, 0, 0)),
                  pl.BlockSpec(w.shape, lambda i, j: (0, 0)), pl.BlockSpec(bd.shape, lambda i, j: (0, 0)),
                  row(da), row(da), row(LANES)],
        out_specs=[ospec(da), ospec(2 * da), ospec(da), ospec(2 * da), ospec(n_f), ospec(LANES)],
        out_shape=[jax.ShapeDtypeStruct((b, t, da), BF16), jax.ShapeDtypeStruct((b, t, 2 * da), F32),
                   jax.ShapeDtypeStruct((b, t, da), BF16), jax.ShapeDtypeStruct((b, t, 2 * da), F32),
                   jax.ShapeDtypeStruct((b, t, n_f), F32), jax.ShapeDtypeStruct((b, t, LANES), F32)],
        compiler_params=_params("arbitrary", "arbitrary"),
    )(x, shift, scale, g.reshape(1, 1, d), w, bd, gq, gk, bfp)


def _proj_ab_t_kernel(x_ref, shift_ref, scale_ref, g_ref, wq_ref, wkvt_ref, wflt_ref, bd_ref, gq_ref, gk_ref, bf_ref,
                      qsb_ref, qfx_ref, sbt_ref, fxt_ref, lft_ref, *, da, n_f):
    x = x_ref[0]
    h = _adanorm(x, g_ref[0], scale_ref[0], shift_ref[0]).astype(BF16)
    sc = HEAD_DIM ** -0.5
    bd = bd_ref[...]
    qa = _dot(h, wq_ref[:, 0:da]) * sc
    qf = _head_norm(_dot(h, wq_ref[:, da:2 * da]), bd) * (gq_ref[...] * sc)
    for hh in range(da // HEAD_DIM):
        qsb_ref[0, hh] = qa[:, hh * HEAD_DIM:(hh + 1) * HEAD_DIM].astype(BF16)
        qfx_ref[0, hh] = qf[:, hh * HEAD_DIM:(hh + 1) * HEAD_DIM].astype(BF16)
    sbt_ref[0] = _dot_nt(wkvt_ref[0:2 * da, :], h)
    kft = _head_norm_t(_dot_nt(wkvt_ref[2 * da:3 * da, :], h), bd)
    fxt_ref[0, 0:da, :] = kft * gk_ref[...]
    fxt_ref[0, da:2 * da, :] = _dot_nt(wkvt_ref[3 * da:4 * da, :], h)
    lft_ref[0] = _log_sigmoid(_dot_nt(wflt_ref[...], h)[0:n_f, :] + bf_ref[...])


def _proj_ab_t(x, mod3, g, wq, wkvt, wflt, bd, gq, gk_col, bf_col, tt):
    b, t, d = x.shape
    da = bd.shape[0]
    n_h = da // HEAD_DIM
    n_f = bf_col.shape[0]
    shift, scale, _ = mod3
    mspec = pl.BlockSpec((1, 1, d), lambda i, j: (i, 0, 0))
    const = lambda a: pl.BlockSpec(a.shape, lambda i, j: (0,) * a.ndim)
    qspec = pl.BlockSpec((1, n_h, tt, HEAD_DIM), lambda i, j: (i, 0, j, 0))
    tspec = lambda rows: pl.BlockSpec((1, rows, tt), lambda i, j: (i, 0, j))
    return pl.pallas_call(
        functools.partial(_proj_ab_t_kernel, da=da, n_f=n_f), name="proj_ab_t",
        grid=(b, t // tt),
        in_specs=[pl.BlockSpec((1, tt, d), lambda i, j: (i, j, 0)), mspec, mspec,
                  pl.BlockSpec((1, 1, d), lambda i, j: (0, 0, 0)),
                  const(wq), const(wkvt), const(wflt), const(bd), const(gq), const(gk_col), const(bf_col)],
        out_specs=[qspec, qspec, tspec(2 * da), tspec(2 * da), tspec(n_f)],
        out_shape=[jax.ShapeDtypeStruct((b, n_h, t, HEAD_DIM), BF16), jax.ShapeDtypeStruct((b, n_h, t, HEAD_DIM), BF16),
                   jax.ShapeDtypeStruct((b, 2 * da, t), F32), jax.ShapeDtypeStruct((b, 2 * da, t), F32),
                   jax.ShapeDtypeStruct((b, n_f, t), F32)],
        compiler_params=_params("arbitrary", "arbitrary"),
    )(x, shift, scale, g.reshape(1, 1, d), wq, wkvt, wflt, bd, gq, gk_col, bf_col)


def _proj_nsa_kernel(x_ref, shift_ref, scale_ref, g_ref, w_ref, bd_ref, gq_ref, gs_ref, gw_ref, bg_ref,
                     q_ref, cmp_ref, slc_ref, win_ref, gates_ref, *, dq, dkv):
    x = x_ref[...]
    bb, tt, d = x.shape
    m = bb * tt
    h = _adanorm(x, g_ref[...], scale_ref[...], shift_ref[...]).reshape(m, d).astype(BF16)
    bd = bd_ref[...]
    sc = HEAD_DIM ** -0.5
    for c in range(dq // dkv):
        qc = _head_norm(_dot(h, w_ref[:, c * dkv:(c + 1) * dkv]), bd) * gq_ref[...]
        q_ref[:, :, c * dkv:(c + 1) * dkv] = (qc * sc).reshape(bb, tt, dkv).astype(BF16)
    o = dq
    cmp_ref[...] = _dot(h, w_ref[:, o:o + 2 * dkv]).reshape(bb, tt, 2 * dkv)
    o += 2 * dkv
    ks = _head_norm(_dot(h, w_ref[:, o:o + dkv]), bd) * gs_ref[...]
    slc_ref[:, :, 0:dkv] = ks.reshape(bb, tt, dkv)
    slc_ref[:, :, dkv:2 * dkv] = _dot(h, w_ref[:, o + dkv:o + 2 * dkv]).reshape(bb, tt, dkv)
    o += 2 * dkv
    kw = _head_norm(_dot(h, w_ref[:, o:o + dkv]), bd) * gw_ref[...]
    win_ref[:, :, 0:dkv] = kw.reshape(bb, tt, dkv)
    win_ref[:, :, dkv:2 * dkv] = _dot(h, w_ref[:, o + dkv:o + 2 * dkv]).reshape(bb, tt, dkv)
    o += 2 * dkv
    gates_ref[...] = jax.nn.sigmoid(_dot(h, w_ref[:, o:o + LANES]) + bg_ref[...]).reshape(bb, tt, LANES)


def _proj_nsa(x, mod3, g, w, bd, gq, gs, gw, bg, dq, bb, tt):
    b, t, d = x.shape
    dkv = bd.shape[0]
    shift, scale, _ = mod3
    mspec = pl.BlockSpec((bb, 1, d), lambda i, j: (i, 0, 0))
    xspec = pl.BlockSpec((bb, tt, d), lambda i, j: (i, j, 0))
    row = lambda width: pl.BlockSpec((1, width), lambda i, j: (0, 0))
    ospec = lambda width: pl.BlockSpec((bb, tt, width), lambda i, j: (i, j, 0))
    return pl.pallas_call(
        functools.partial(_proj_nsa_kernel, dq=dq, dkv=dkv), name="proj_nsa",
        grid=(b // bb, t // tt),
        in_specs=[xspec, mspec, mspec, pl.BlockSpec((1, 1, d), lambda i, j: (0, 0, 0)),
                  pl.BlockSpec(w.shape, lambda i, j: (0, 0)), pl.BlockSpec(bd.shape, lambda i, j: (0, 0)),
                  row(dkv), row(dkv), row(dkv), row(LANES)],
        out_specs=[ospec(dq), ospec(2 * dkv), ospec(2 * dkv), ospec(2 * dkv), ospec(LANES)],
        out_shape=[jax.ShapeDtypeStruct((b, t, dq), BF16), jax.ShapeDtypeStruct((b, t, 2 * dkv), F32),
                   jax.ShapeDtypeStruct((b, t, 2 * dkv), F32), jax.ShapeDtypeStruct((b, t, 2 * dkv), F32),
                   jax.ShapeDtypeStruct((b, t, LANES), F32)],
        compiler_params=_params("arbitrary", "arbitrary"),
    )(x, shift, scale, g.reshape(1, 1, d), w, bd, gq, gs, gw, bg)


def _proj_nsa_t_kernel(x_ref, shift_ref, scale_ref, g_ref, wq_ref, wcmp_ref, wkvt_ref, wg_ref, bd_ref, gq_ref, gs_ref,
                       gw_ref, bg_ref, q_ref, cmp_ref, slct_ref, wint_ref, gates_ref, *, dq, dkv):
    x = x_ref[0]
    h = _adanorm(x, g_ref[0], scale_ref[0], shift_ref[0]).astype(BF16)
    bd = bd_ref[...]
    sc = HEAD_DIM ** -0.5
    per = dkv // HEAD_DIM
    for c in range(dq // dkv):
        qc = _head_norm(_dot(h, wq_ref[:, c * dkv:(c + 1) * dkv]), bd) * (gq_ref[...] * sc)
        for hh in range(per):
            q_ref[0, c * per + hh] = qc[:, hh * HEAD_DIM:(hh + 1) * HEAD_DIM].astype(BF16)
    cmp_ref[0] = _dot(h, wcmp_ref[...])
    slct_ref[0, 0:dkv, :] = _head_norm_t(_dot_nt(wkvt_ref[0:dkv, :], h), bd) * gs_ref[...]
    slct_ref[0, dkv:2 * dkv, :] = _dot_nt(wkvt_ref[dkv:2 * dkv, :], h)
    wint_ref[0, 0:dkv, :] = _head_norm_t(_dot_nt(wkvt_ref[2 * dkv:3 * dkv, :], h), bd) * gw_ref[...]
    wint_ref[0, dkv:2 * dkv, :] = _dot_nt(wkvt_ref[3 * dkv:4 * dkv, :], h)
    gates_ref[0] = jax.nn.sigmoid(_dot(h, wg_ref[...]) + bg_ref[...])


def _proj_nsa_t(x, mod3, g, wq, wcmp, wkvt, wg, bd, gq, gs_col, gw_col, bg, tt):
    b, t, d = x.shape
    dkv = bd.shape[0]
    dq = wq.shape[1]
    n_h = dq // HEAD_DIM
    shift, scale, _ = mod3
    mspec = pl.BlockSpec((1, 1, d), lambda i, j: (i, 0, 0))
    const = lambda a: pl.BlockSpec(a.shape, lambda i, j: (0,) * a.ndim)
    tspec = lambda rows: pl.BlockSpec((1, rows, tt), lambda i, j: (i, 0, j))
    rspec = lambda width: pl.BlockSpec((1, tt, width), lambda i, j: (i, j, 0))
    return pl.pallas_call(
        functools.partial(_proj_nsa_t_kernel, dq=dq, dkv=dkv), name="proj_nsa_t",
        grid=(b, t // tt),
        in_specs=[rspec(d), mspec, mspec, pl.BlockSpec((1, 1, d), lambda i, j: (0, 0, 0)),
                  const(wq), const(wcmp), const(wkvt), const(wg), const(bd), const(gq), const(gs_col), const(gw_col),
                  const(bg)],
        out_specs=[pl.BlockSpec((1, n_h, tt, HEAD_DIM), lambda i, j: (i, 0, j, 0)), rspec(2 * dkv),
                   tspec(2 * dkv), tspec(2 * dkv), rspec(LANES)],
        out_shape=[jax.ShapeDtypeStruct((b, n_h, t, HEAD_DIM), BF16), jax.ShapeDtypeStruct((b, t, 2 * dkv), F32),
                   jax.ShapeDtypeStruct((b, 2 * dkv, t), F32), jax.ShapeDtypeStruct((b, 2 * dkv, t), F32),
                   jax.ShapeDtypeStruct((b, t, LANES), F32)],
        compiler_params=_params("arbitrary", "arbitrary"),
    )(x, shift, scale, g.reshape(1, 1, d), wq, wcmp, wkvt, wg, bd, gq, gs_col, gw_col, bg)


def _block_diag_rows(q, n_heads):
    q = q.astype(F32)
    head = _iota(q.shape, 1) // HEAD_DIM
    return jnp.concatenate([jnp.where(head == h, q, 0.0) for h in range(n_heads)], axis=0).astype(BF16)


def _gather_diag(acc, tq, n_heads):
    head = _iota((tq, acc.shape[1]), 1) // HEAD_DIM
    out = jnp.zeros((tq, acc.shape[1]), F32)
    for h in range(n_heads):
        out = out + jnp.where(head == h, acc[h * tq:(h + 1) * tq, :], 0.0)
    return out


def _row_query(shape, tq):
    return _iota(shape, 0) % tq


def _running_sum_matrix(tk, prefix):
    j = _iota((tk, 2 * tk), 0)
    s = _iota((tk, 2 * tk), 1)
    tri = (j <= s) if prefix else (j > s)
    return jnp.where((s >= tk) | tri, 1.0, 0.0).astype(BF16)


def _later_matrix(tk):
    return jnp.where(_iota((tk, tk), 0) > _iota((tk, tk), 1), 1.0, 0.0).astype(BF16)


def _across_lane_tiles(x, op):
    out = x[:, 0:LANES]
    for c in range(1, x.shape[1] // LANES):
        out = op(out, x[:, c * LANES:(c + 1) * LANES])
    return out


def _with_row_vector(x, v, op):
    vb = jnp.broadcast_to(v, (x.shape[0], LANES))
    tiles = [op(x[:, c * LANES:(c + 1) * LANES], vb) for c in range(x.shape[1] // LANES)]
    return tiles[0] if len(tiles) == 1 else jnp.concatenate(tiles, axis=1)


def _exp_minus(s, m):
    return jnp.exp(_with_row_vector(s, m, jnp.subtract))


def _row_max(x):
    return jnp.max(_across_lane_tiles(x, jnp.maximum), axis=1, keepdims=True)


def _row_sum(x):
    return jnp.sum(_across_lane_tiles(x, jnp.add), axis=1, keepdims=True)


def _log_keep(z, mask):
    lk = -(jnp.maximum(z, 0.0) + jnp.log(1.0 + jnp.exp(-jnp.abs(z))))
    return lk if mask is None else jnp.where(mask, lk, 0.0)


def _sb_weights(z, um, carry, mask):
    lk = _log_keep(z, mask)
    blk = um.shape[0]
    nb = z.shape[1] // blk
    rest = [None] * nb
    for b in reversed(range(nb)):
        lkb = lk[:, b * blk:(b + 1) * blk]
        rest[b] = _with_row_vector(_dot_x2(lkb, um), carry, jnp.add)
        carry = carry + _row_sum(lkb)
    w = jnp.exp(z + lk + (rest[0] if nb == 1 else jnp.concatenate(rest, axis=1)))
    if mask is not None:
        w = jnp.where(mask, w, 0.0)
    return w.astype(BF16), carry


def _softmax_first(s, pv, m_ref, l_ref, acc_ref):
    m = _row_max(s)
    p = _exp_minus(s, m)
    l_ref[...] = _row_sum(p)
    acc_ref[...] = pv(p.astype(BF16))
    m_ref[...] = m


def _softmax_step(s, pv, m_ref, l_ref, acc_ref, mask=None):
    m_prev = m_ref[...]
    m_new = jnp.maximum(m_prev, _row_max(s))
    p = _exp_minus(s, m_new)
    if mask is not None:
        p = jnp.where(mask, p, 0.0)
    alpha = jnp.exp(m_prev - m_new)
    l_ref[...] = alpha * l_ref[...] + _row_sum(p)
    acc_ref[...] = alpha * acc_ref[...] + pv(p.astype(BF16))
    m_ref[...] = m_new


def _softmax_chains(scores, vt, refs, first):
    if first:
        ms = [_row_max(s) for s in scores]
    else:
        prev = [(m[...], a[...]) for m, a in refs]
        ms = [jnp.maximum(pm, _row_max(s)) for (pm, _), s in zip(prev, scores)]
    ps = [_exp_minus(s, m) for s, m in zip(scores, ms)]
    pvs = [_dot_nt(p.astype(BF16), vt) for p in ps]
    for i, (m_ref, acc_ref) in enumerate(refs):
        acc_ref[...] = pvs[i] if first else jnp.exp(prev[i][0] - ms[i]) * prev[i][1] + pvs[i]
        m_ref[...] = ms[i]


def _softmax_single_chains(scores, vt):
    ps = [_exp_minus(s, _row_max(s)) for s in scores]
    return [_dot_nt(p.astype(BF16), vt) for p in ps]


def _with_sum_row(vt):
    extra = jnp.where(_iota((SUM_ROWS, vt.shape[1]), 0) == 0, 1.0, 0.0).astype(BF16)
    return jnp.concatenate([vt, extra], axis=0)


def _normalized(acc):
    return acc[:, 0:HEAD_DIM] / jnp.maximum(acc[:, HEAD_DIM:HEAD_DIM + 1], 1e-30)


def _softmax_init(m_ref, l_ref, acc_ref):
    m_ref[...] = jnp.full(m_ref.shape, NEG, F32)
    l_ref[...] = jnp.zeros(l_ref.shape, F32)
    acc_ref[...] = jnp.zeros(acc_ref.shape, F32)


def _softmax_out(l_ref, acc_ref):
    return acc_ref[...] / jnp.maximum(l_ref[...], 1e-30)


def _softmax_scratch(r, width):
    return [pltpu.VMEM((r, 1), F32), pltpu.VMEM((r, 1), F32), pltpu.VMEM((r, width), F32)]


def _rank_select(imp, t, ns):
    blk = _iota(imp.shape, 1)
    cur = t // SLC_BLOCK
    valid = (blk * SLC_BLOCK <= t) & (blk < ns)
    forced = (blk == 0) | (blk == cur) | (blk == cur - 1)
    score = jnp.where(valid, jnp.where(forced, FORCE_SCORE, imp), -jnp.inf)
    cnt = jnp.zeros(imp.shape, F32)
    for j in range(ns):
        col = score[:, j:j + 1]
        cnt = cnt + jnp.where(col > score, 1.0, 0.0) + jnp.where(col == score, jnp.where(blk > j, 1.0, 0.0), 0.0)
    return jnp.where(valid & (cnt < N_SELECT), 1.0, 0.0)


def _rank_select_t(imp, q0, ns):
    rows = -(-ns // 8) * 8
    imp_t = imp.T[0:rows, :]
    blk = _iota(imp_t.shape, 0)
    t = q0 + _iota(imp_t.shape, 1)
    cur = t // SLC_BLOCK
    valid = (blk * SLC_BLOCK <= t) & (blk < ns)
    forced = (blk == 0) | (blk == cur) | (blk == cur - 1)
    score = jnp.where(valid, jnp.where(forced, FORCE_SCORE, imp_t), -jnp.inf)
    cnt = jnp.zeros(imp_t.shape, F32)
    for j in range(ns):
        row = score[j:j + 1, :]
        cnt = cnt + jnp.where(row > score, 1.0, 0.0) + jnp.where(row == score, jnp.where(blk > j, 1.0, 0.0), 0.0)
    sel_t = jnp.where(valid & (cnt < N_SELECT), 1.0, 0.0)
    return jnp.concatenate([sel_t, jnp.zeros((imp.shape[1] - rows, imp.shape[0]), F32)], axis=0).T


def _expand_blocks(sel, n_keys):
    nsp = sel.shape[1]
    blk = _iota((nsp, n_keys), 0)
    key = _iota((nsp, n_keys), 1)
    return _dot(sel, jnp.where(key // SLC_BLOCK == blk, 1.0, 0.0).astype(BF16))


def _sb_prompt_kernel(q_ref, kt_ref, vt_ref, o_ref, ktb, vtb, acc, car, *, tq, nh):
    qi = pl.program_id(2)

    @pl.when(qi == 0)
    def _():
        ktb[...] = kt_ref[0].astype(BF16)
        vtb[...] = vt_ref[0].astype(BF16)

    um = _later_matrix(tq)
    q0 = pl.multiple_of(qi * tq, tq)
    mask = _iota((tq, tq), 1) < _iota((tq, tq), 0)

    qs = [q_ref[0, h] for h in range(nh)]

    def tile(k0, first):
        kt = ktb[:, pl.ds(k0, tq)]
        vt = vtb[:, pl.ds(k0, tq)]
        hrows = lambda a, h: a[h * HEAD_DIM:(h + 1) * HEAD_DIM]
        prev = [(jnp.zeros((tq, 1), F32), None) if first else (car[h], acc[h]) for h in range(nh)]
        zs = [_dot(qs[h], hrows(kt, h)) for h in range(nh)]
        wc = [_sb_weights(zs[h], um, prev[h][0], mask if first else None) for h in range(nh)]
        pvs = [_dot_nt(wc[h][0], hrows(vt, h)) for h in range(nh)]
        for h in range(nh):
            acc[h] = pvs[h] if first else prev[h][1] + pvs[h]
            car[h] = wc[h][1]
        live = wc[0][1]
        for h in range(1, nh):
            live = jnp.maximum(live, wc[h][1])
        return jnp.max(live) > DEAD_LOG

    alive = tile(q0, True)

    def body(state):
        it, _ = state
        return it + 1, tile(pl.multiple_of((qi - 1 - it) * tq, tq), False)

    lax.while_loop(lambda state: (state[0] < qi) & state[1], body, (jnp.int32(0), alive))
    o_ref[0] = jnp.concatenate([acc[h] for h in range(nh)], axis=1).astype(BF16)


def _sb_prompt(q, kvt, tq, nh):
    b, n_h, s, _ = q.shape
    c = nh * HEAD_DIM
    ng = n_h // nh
    return pl.pallas_call(
        functools.partial(_sb_prompt_kernel, tq=tq, nh=nh), name="sb_prompt",
        grid=(b, ng, s // tq),
        in_specs=[pl.BlockSpec((1, nh, tq, HEAD_DIM), lambda i, g, j: (i, g, j, 0)),
                  pl.BlockSpec((1, c, s), lambda i, g, j: (i, g, 0)),
                  pl.BlockSpec((1, c, s), lambda i, g, j: (i, ng + g, 0))],
        out_specs=pl.BlockSpec((1, tq, c), lambda i, g, j: (i, j, g)),
        out_shape=jax.ShapeDtypeStruct((b, s, n_h * HEAD_DIM), BF16),
        scratch_shapes=[pltpu.VMEM((c, s), BF16), pltpu.VMEM((c, s), BF16),
                        pltpu.VMEM((nh, tq, HEAD_DIM), F32), pltpu.VMEM((nh, tq, 1), F32)],
        compiler_params=_params("arbitrary", "arbitrary", "arbitrary"),
    )(q, kvt, kvt)


def _fox_prompt_kernel(q_ref, kt_ref, vt_ref, lft_ref, o_ref, ktb, vtb, ncum, knorm, m_ref, l_ref, acc, *, tq, nh):
    g = pl.program_id(1)
    qi = pl.program_id(2)
    s_len = kt_ref.shape[2]

    @pl.when(qi == 0)
    def _():
        kb = kt_ref[0].astype(BF16)
        ktb[...] = kb
        vtb[...] = vt_ref[0].astype(BF16)
        kf = kb.astype(F32)
        for h in range(nh):
            kh = kf[h * HEAD_DIM:(h + 1) * HEAD_DIM]
            knorm[h] = jnp.full(knorm.shape[1:], jnp.sqrt(jnp.max(jnp.sum(kh * kh, axis=0, keepdims=True))), F32)
        um = _running_sum_matrix(tq, True)
        run = jnp.zeros((lft_ref.shape[1], tq), F32)
        for ch in range(s_len // tq):
            sl = slice(ch * tq, (ch + 1) * tq)
            rs = _dot_x3(lft_ref[0, :, sl], um)
            ncum[:, sl] = -(rs[:, :tq] + run)
            run = run + rs[:, tq:]

    q0 = pl.multiple_of(qi * tq, tq)
    mask = _iota((tq, tq), 1) <= _iota((tq, tq), 0)

    qs = [q_ref[0, h] for h in range(nh)]

    def tile(k0, first):
        kt = ktb[:, pl.ds(k0, tq)]
        vt = vtb[:, pl.ds(k0, tq)]
        bias = [ncum[pl.ds(g * nh + h, 1), pl.ds(k0, tq)] for h in range(nh)]
        hrows = lambda a, h: a[h * HEAD_DIM:(h + 1) * HEAD_DIM]
        if not first:
            prev = [(m_ref[h], l_ref[h], acc[h]) for h in range(nh)]
        ss = [_dot(qs[h], hrows(kt, h)) + bias[h] for h in range(nh)]
        if first:
            ss = [jnp.where(mask, s, NEG) for s in ss]
            ms = [_row_max(s) for s in ss]
        else:
            ms = [jnp.maximum(prev[h][0], _row_max(ss[h])) for h in range(nh)]
        ps = [_exp_minus(s, m) for s, m in zip(ss, ms)]
        pvs = [_dot_nt(ps[h].astype(BF16), hrows(vt, h)) for h in range(nh)]
        sums = [_row_sum(p) for p in ps]
        for h in range(nh):
            if first:
                l_ref[h] = sums[h]
                acc[h] = pvs[h]
            else:
                alpha = jnp.exp(prev[h][0] - ms[h])
                l_ref[h] = alpha * prev[h][1] + sums[h]
                acc[h] = alpha * prev[h][2] + pvs[h]
            m_ref[h] = ms[h]
        live = None
        for h in range(nh):
            bound = qn[h] * knorm[h][0:1, 0:1] + (jnp.min(bias[h], axis=1, keepdims=True) - ms[h])
            live = bound if live is None else jnp.maximum(live, bound)
        return jnp.max(live) > DEAD_LOG

    qn = [jnp.sqrt(jnp.sum(jnp.square(q.astype(F32)), axis=1, keepdims=True)) * 1.001 for q in qs]
    alive = tile(q0, True)

    def body(state):
        it, _ = state
        return it + 1, tile(pl.multiple_of((qi - 1 - it) * tq, tq), False)

    lax.while_loop(lambda state: (state[0] < qi) & state[1], body, (jnp.int32(0), alive))
    o_ref[0] = jnp.concatenate([_softmax_out(l_ref.at[h], acc.at[h]) for h in range(nh)], axis=1).astype(BF16)


def _fox_prompt(q, kvt, lft, tq, nh):
    b, n_h, s, _ = q.shape
    c = nh * HEAD_DIM
    ng = n_h // nh
    n_f = lft.shape[1]
    return pl.pallas_call(
        functools.partial(_fox_prompt_kernel, tq=tq, nh=nh), name="fox_prompt",
        grid=(b, ng, s // tq),
        in_specs=[pl.BlockSpec((1, nh, tq, HEAD_DIM), lambda i, g, j: (i, g, j, 0)),
                  pl.BlockSpec((1, c, s), lambda i, g, j: (i, g, 0)),
                  pl.BlockSpec((1, c, s), lambda i, g, j: (i, ng + g, 0)),
                  pl.BlockSpec((1, n_f, s), lambda i, g, j: (i, 0, 0))],
        out_specs=pl.BlockSpec((1, tq, c), lambda i, g, j: (i, j, g)),
        out_shape=jax.ShapeDtypeStruct((b, s, n_h * HEAD_DIM), BF16),
        scratch_shapes=[pltpu.VMEM((c, s), BF16), pltpu.VMEM((c, s), BF16), pltpu.VMEM((n_f, s), F32),
                        pltpu.VMEM((nh, 8, LANES), F32), pltpu.VMEM((nh, tq, 1), F32), pltpu.VMEM((nh, tq, 1), F32), pltpu.VMEM((nh, tq, HEAD_DIM), F32)],
        compiler_params=_params("arbitrary", "arbitrary", "arbitrary"),
    )(q, kvt, kvt, lft)


def _page_specs(n_pages, rows, npg):
    return [pl.BlockSpec((1, rows, PAGE), functools.partial(
        lambda i, s, pt, off: (pt[i, n_pages - 1 - s * npg - off], 0, 0), off=off)) for off in range(npg)]


def _sb_decode_kernel(pt_ref, q_ref, new_ref, *rest, tq, hg, npg):
    page_refs = rest[:npg]
    o_ref, qbd_ref, newp, acc, car = rest[npg:]
    st = pl.program_id(1)
    da = hg * HEAD_DIM
    r = hg * tq

    @pl.when(st == 0)
    def _():
        qbd_ref[...] = _block_diag_rows(q_ref[0], hg)
        newp[...] = jnp.zeros(newp.shape, F32)
        newp[0:tq, :] = new_ref[0]
        mask = _iota((r, PAGE), 1) < _row_query((r, PAGE), tq)
        w, c = _sb_weights(_dot_nt(qbd_ref[...], newp[:, 0:da].astype(BF16)), _later_matrix(PAGE),
                           jnp.zeros((r, 1), F32), mask)
        acc[...] = _dot(w, newp[:, da:2 * da].astype(BF16))
        car[...] = c

    kt = jnp.concatenate([pg[0, 0:da, :] for pg in page_refs[::-1]], axis=1).astype(BF16)
    vt = jnp.concatenate([pg[0, da:2 * da, :] for pg in page_refs[::-1]], axis=1).astype(BF16)
    w, c = _sb_weights(_dot(qbd_ref[...], kt), _later_matrix(2 * PAGE), car[...], None)
    acc[...] += _dot_nt(w, vt)
    car[...] = c

    @pl.when(st == pl.num_programs(1) - 1)
    def _():
        o_ref[0] = _gather_diag(acc[...], tq, hg).astype(BF16)


def _sb_decode(q, kv_new, cache_t, page_table, npg):
    b, tq, da = q.shape
    hg = da // HEAD_DIM
    n_pages = page_table.shape[1]
    r = hg * tq
    grid_spec = pltpu.PrefetchScalarGridSpec(
        num_scalar_prefetch=1,
        grid=(b, n_pages // npg),
        in_specs=[pl.BlockSpec((1, tq, da), lambda i, s, pt: (i, 0, 0)),
                  pl.BlockSpec((1, tq, 2 * da), lambda i, s, pt: (i, 0, 0))] + _page_specs(n_pages, 2 * da, npg),
        out_specs=pl.BlockSpec((1, tq, da), lambda i, s, pt: (i, 0, 0)),
        scratch_shapes=[pltpu.VMEM((r, da), BF16), pltpu.VMEM((PAGE, 2 * da), F32),
                        pltpu.VMEM((r, da), F32), pltpu.VMEM((r, 1), F32)],
    )
    return pl.pallas_call(
        functools.partial(_sb_decode_kernel, tq=tq, hg=hg, npg=npg), name="sb_decode",
        grid_spec=grid_spec,
        out_shape=jax.ShapeDtypeStruct((b, tq, da), BF16),
        compiler_params=_params("arbitrary", "arbitrary"),
    )(page_table, q, kv_new, *([cache_t] * npg))


def _fox_decode_kernel(pt_ref, q_ref, new_ref, lfnew_ref, *rest, tq, hg, npg):
    page_refs = rest[:npg]
    lf_refs = rest[npg:2 * npg]
    o_ref, qbd_ref, newp, lfp, run_ref, m_ref, l_ref, acc = rest[2 * npg:]
    st = pl.program_id(1)
    da = hg * HEAD_DIM
    r = hg * tq
    um = _running_sum_matrix(PAGE, False)

    def key_bias(lfts):
        rs = [_dot_x3(lft, um) for lft in lfts]
        run = run_ref[...]
        suf = [None] * len(lfts)
        for b in reversed(range(len(lfts))):
            suf[b] = rs[b][:, :PAGE] + run
            run = run + rs[b][:, PAGE:]
        run_ref[...] = run
        suf = suf[0] if len(suf) == 1 else jnp.concatenate(suf, axis=1)
        return jnp.concatenate([jnp.broadcast_to(suf[h:h + 1, :], (tq, suf.shape[1])) for h in range(hg)], axis=0)

    @pl.when(st == 0)
    def _():
        qbd_ref[...] = _block_diag_rows(q_ref[0], hg)
        newp[...] = jnp.zeros(newp.shape, F32)
        newp[0:tq, :] = new_ref[0]
        lfp[...] = jnp.zeros(lfp.shape, F32)
        lfp[0:tq, :] = lfnew_ref[0]
        run_ref[...] = jnp.zeros(run_ref.shape, F32)
        mask = _iota((r, PAGE), 1) <= _row_query((r, PAGE), tq)
        s = _dot_nt(qbd_ref[...], newp[:, 0:da].astype(BF16)) + key_bias([lfp[...].T[0:hg, :]])
        _softmax_first(jnp.where(mask, s, NEG), lambda p: _dot(p, newp[:, da:2 * da].astype(BF16)), m_ref, l_ref, acc)

    kt = jnp.concatenate([pg[0, 0:da, :] for pg in page_refs[::-1]], axis=1).astype(BF16)
    vt = jnp.concatenate([pg[0, da:2 * da, :] for pg in page_refs[::-1]], axis=1).astype(BF16)
    s = _dot(qbd_ref[...], kt) + key_bias([lf[0] for lf in lf_refs[::-1]])
    _softmax_step(s, lambda p: _dot_nt(p, vt), m_ref, l_ref, acc)

    @pl.when(st == pl.num_programs(1) - 1)
    def _():
        o_ref[0] = _gather_diag(_softmax_out(l_ref, acc), tq, hg).astype(BF16)


def _fox_decode(q, kv_new, lf_new, cache_t, cache_lf_t, page_table, npg):
    b, tq, da = q.shape
    hg = da // HEAD_DIM
    n_pages = page_table.shape[1]
    r = hg * tq
    assert cache_lf_t.shape[1] == hg
    grid_spec = pltpu.PrefetchScalarGridSpec(
        num_scalar_prefetch=1,
        grid=(b, n_pages // npg),
        in_specs=[pl.BlockSpec((1, tq, da), lambda i, s, pt: (i, 0, 0)),
                  pl.BlockSpec((1, tq, 2 * da), lambda i, s, pt: (i, 0, 0)),
                  pl.BlockSpec((1, tq, LANES), lambda i, s, pt: (i, 0, 0))]
        + _page_specs(n_pages, 2 * da, npg) + _page_specs(n_pages, hg, npg),
        out_specs=pl.BlockSpec((1, tq, da), lambda i, s, pt: (i, 0, 0)),
        scratch_shapes=[pltpu.VMEM((r, da), BF16), pltpu.VMEM((PAGE, 2 * da), F32),
                        pltpu.VMEM((PAGE, LANES), F32), pltpu.VMEM((hg, PAGE), F32)] + _softmax_scratch(r, da),
    )
    return pl.pallas_call(
        functools.partial(_fox_decode_kernel, tq=tq, hg=hg, npg=npg), name="fox_decode",
        grid_spec=grid_spec,
        out_shape=jax.ShapeDtypeStruct((b, tq, da), BF16),
        compiler_params=_params("arbitrary", "arbitrary"),
    )(page_table, q, kv_new, lf_new, *([cache_t] * npg), *([cache_lf_t] * npg))


def _compress(xb, bd_ref, pe_ref, n_chunk, dkv, pitch=None):
    first = [jnp.zeros((n_chunk, dkv), F32) for _ in range(2)]
    second = [jnp.zeros((n_chunk, dkv), F32) for _ in range(2)]
    for j in range(D_CMP):
        rows = pl.ds(j, n_chunk, stride=D_CMP) if pitch is None else pl.ds(j * pitch, n_chunk)
        xj = jnp.concatenate([xb[lb, rows, :] for lb in range(xb.shape[0])], axis=1)
        for kv in range(2):
            xx = xj[:, kv * dkv:(kv + 1) * dkv]
            first[kv] += _dot((xx + pe_ref[kv, j:j + 1, :]).astype(BF16), bd_ref[kv, j])
            second[kv] += _dot((xx + pe_ref[kv, D_CMP + j:D_CMP + j + 1, :]).astype(BF16), bd_ref[kv, D_CMP + j])
    return [first[kv] + pltpu.roll(second[kv], n_chunk - 1, axis=0) for kv in range(2)]


def _compress_prompt_kernel(x_ref, bd_ref, pe_ref, bdn_ref, gk_ref, kct_ref, vct_ref, xb):
    n_chunk = x_ref.shape[1] // D_CMP
    dkv = bdn_ref.shape[0]
    for lb in range(xb.shape[0]):
        xb[lb] = x_ref[0, :, lb * LANES:(lb + 1) * LANES]
    kc, vc = _compress(xb, bd_ref, pe_ref, n_chunk, dkv)
    kct_ref[0] = (_head_norm(kc, bdn_ref[...]) * gk_ref[...]).T.astype(BF16)
    vct_ref[0] = vc.T.astype(BF16)


def _compress_prompt(cmp_kv, bd, pe, bdn, gk):
    b, s, w = cmp_kv.shape
    dkv = w // 2
    n_chunk = s // D_CMP
    const = lambda a: pl.BlockSpec(a.shape, lambda i: (0,) * a.ndim)
    ospec = pl.BlockSpec((1, dkv, n_chunk), lambda i: (i, 0, 0))
    return pl.pallas_call(
        _compress_prompt_kernel, name="compress_prompt",
        grid=(b,),
        in_specs=[pl.BlockSpec((1, s, w), lambda i: (i, 0, 0)), const(bd), const(pe), const(bdn), const(gk)],
        out_specs=[ospec, ospec],
        out_shape=[jax.ShapeDtypeStruct((b, dkv, n_chunk), BF16)] * 2,
        scratch_shapes=[pltpu.VMEM((w // LANES, s, LANES), F32)],
        compiler_params=_params("arbitrary"),
    )(cmp_kv, bd, pe, bdn, gk)


def _nsa_prompt_kernel(q_ref, kst_ref, vst_ref, kwt_ref, vwt_ref, kct_ref, vct_ref, gates_ref, band_ref, near_ref,
                       wadd_ref, ov_ref, o_ref, ksb, vsb, kwb, vwb, madd, m_ref, acc, ocmp, owin, *, tq, nz, ns):
    g = pl.program_id(1)
    qi = pl.program_id(2)
    s_len = kst_ref.shape[2]
    ncp = kct_ref.shape[2]
    wslab = WINDOW + tq

    @pl.when(qi == 0)
    def _():
        ksb[...] = kst_ref[0].astype(BF16)
        vsb[...] = _with_sum_row(vst_ref[0].astype(BF16))
        kwb[:, 0:WINDOW] = jnp.zeros((HEAD_DIM, WINDOW), BF16)
        vwb[:, 0:WINDOW] = jnp.zeros((HEAD_DIM + SUM_ROWS, WINDOW), BF16)
        kwb[:, WINDOW:] = kwt_ref[0].astype(BF16)
        vwb[:, WINDOW:] = _with_sum_row(vwt_ref[0].astype(BF16))

    q0 = pl.multiple_of(qi * tq, tq)
    zrows = lambda z: slice(z * tq, (z + 1) * tq)

    shift = (qi * (tq // D_CMP) + ncp - BAND_BACK) % ncp
    t_c = q0 + _iota((tq, ncp), 0)
    c_c = _iota((tq, ncp), 1)
    cmask = (t_c >= c_c * D_CMP + (L_CMP - 1)) & (c_c < ncp - 1)
    qs = [q_ref[0, z] for z in range(nz)]
    kct = kct_ref[0]
    vct = vct_ref[0]
    ss = [jnp.where(cmask, _dot(qs[z], kct) + pltpu.roll(band_ref[z], shift, axis=1), NEG) for z in range(nz)]
    ps = [jnp.where(cmask, _exp_minus(s, _row_max(s)), 0.0) for s in ss]
    ps = [_with_row_vector(p, jnp.maximum(_row_sum(p), 1e-30), jnp.divide) for p in ps]
    oc = [_dot_nt(p.astype(BF16), vct) for p in ps]
    pz = ps[0]
    for z in range(1, nz):
        pz = pz + ps[z]
    imp = _dot_x2(pz, ov_ref[...])

    kslab = kwb[:, pl.ds(q0, wslab)]
    vslab = vwb[:, pl.ds(q0, wslab)]
    wadd = wadd_ref[...] + jnp.where(_iota((tq, wslab), 1) >= WINDOW - q0, 0.0, NEG)
    ws = [_dot(qs[z], kslab) + wadd for z in range(nz)]
    ws = [jnp.concatenate([s[:, :wslab - 2 * tq], s[:, wslab - 2 * tq:] + near_ref[z]], axis=1)
          for z, s in enumerate(ws)]
    ow = _softmax_single_chains(ws, vslab)

    sel = _rank_select_t(imp, q0, ns)
    for z in range(nz):
        ocmp[zrows(z), :] = oc[z]
        owin[zrows(z), :] = _normalized(ow[z])
    madd[...] = (_expand_blocks(sel.astype(BF16), s_len) - 1.0) * (-NEG)
    refs = [(m_ref.at[zrows(z)], acc.at[zrows(z)]) for z in range(nz)]

    def slc_tile(k0, tk, extra, first):
        kt = ksb[:, pl.ds(k0, tk)]
        vt = vsb[:, pl.ds(k0, tk)]
        ma = madd[:, pl.ds(k0, tk)]
        scores = [_dot(qs[z], kt) + (ma if extra is None else ma + extra(z)) for z in range(nz)]
        _softmax_chains(scores, vt, refs, first)

    causal = jnp.where(_iota((tq, tq), 1) <= _iota((tq, tq), 0), 0.0, NEG)

    @pl.when(qi == 0)
    def _():
        slc_tile(0, tq, lambda z: near_ref[z, :, tq:2 * tq] + causal, True)

    @pl.when(qi >= 1)
    def _():
        edge = jnp.concatenate([jnp.zeros((tq, tq), F32), causal], axis=1)
        slc_tile(pl.multiple_of(q0 - tq, tq), 2 * tq, lambda z: near_ref[z] + edge, True)

    n_far = jnp.maximum(qi - 1, 0)
    per_far = FAR_TILE // tq

    def far_body(it, _):
        slc_tile(pl.multiple_of(it * FAR_TILE, FAR_TILE), FAR_TILE, None, False)
        return 0

    lax.fori_loop(0, n_far // per_far, far_body, 0)

    def rem_body(it, _):
        slc_tile(pl.multiple_of(((n_far // per_far) * per_far + it) * tq, tq), tq, None, False)
        return 0

    lax.fori_loop(0, n_far % per_far, rem_body, 0)

    n_heads = nz * pl.num_programs(1)
    src = _iota((LANES, LANES), 0)
    dst = _iota((LANES, LANES), 1)
    pick = jnp.where((dst < 3 * nz) & (src == (dst // nz) * n_heads + g * nz + dst % nz), 1.0, 0.0).astype(BF16)
    gsel = _dot_x3(gates_ref[0], pick)
    outs = []
    for z in range(nz):
        gate = lambda br: gsel[:, br * nz + z:br * nz + z + 1]
        outs.append(gate(0) * ocmp[zrows(z), :] + gate(1) * _normalized(acc[zrows(z), :]) + gate(2) * owin[zrows(z), :])
    o_ref[0] = jnp.concatenate(outs, axis=1).astype(BF16)


def _window_mask(tq):
    i = np.arange(tq)[:, None]
    c = np.arange(WINDOW + tq)[None, :]
    return jnp.asarray(np.where((c > i) & (c <= i + WINDOW), 0.0, NEG).astype(np.float32))


def _nsa_prompt(q, slct, wint, kct, vct, gates, band, near, tq, n_kv):
    b, n_heads, s, _ = q.shape
    nz = n_heads // n_kv
    r = nz * tq
    ncp = kct.shape[2]
    ns = -(-s // SLC_BLOCK)
    nsp = -(-ns // LANES) * LANES
    ov = _overlap_matrix(ncp, nsp)
    wadd = _window_mask(tq)
    kv_spec = lambda off: pl.BlockSpec((1, HEAD_DIM, s), lambda i, g, j: (i, off + g, 0))
    c_spec = pl.BlockSpec((1, HEAD_DIM, ncp), lambda i, g, j: (i, g, 0))
    tab_spec = lambda a: pl.BlockSpec((nz,) + a.shape[1:], lambda i, g, j: (g, 0, 0))
    return pl.pallas_call(
        functools.partial(_nsa_prompt_kernel, tq=tq, nz=nz, ns=ns), name="nsa_prompt",
        grid=(b, n_kv, s // tq),
        in_specs=[pl.BlockSpec((1, nz, tq, HEAD_DIM), lambda i, g, j: (i, g, j, 0)),
                  kv_spec(0), kv_spec(n_kv), kv_spec(0), kv_spec(n_kv), c_spec, c_spec,
                  pl.BlockSpec((1, tq, LANES), lambda i, g, j: (i, j, 0)),
                  tab_spec(band), tab_spec(near),
                  pl.BlockSpec(wadd.shape, lambda i, g, j: (0, 0)),
                  pl.BlockSpec(ov.shape, lambda i, g, j: (0, 0))],
        out_specs=pl.BlockSpec((1, tq, nz * HEAD_DIM), lambda i, g, j: (i, j, g)),
        out_shape=jax.ShapeDtypeStruct((b, s, n_heads * HEAD_DIM), BF16),
        scratch_shapes=[pltpu.VMEM((HEAD_DIM, s), BF16), pltpu.VMEM((HEAD_DIM + SUM_ROWS, s), BF16),
                        pltpu.VMEM((HEAD_DIM, WINDOW + s), BF16), pltpu.VMEM((HEAD_DIM + SUM_ROWS, WINDOW + s), BF16),
                        pltpu.VMEM((tq, s), F32), pltpu.VMEM((r, 1), F32), pltpu.VMEM((r, HEAD_DIM + SUM_ROWS), F32),
                        pltpu.VMEM((r, HEAD_DIM), F32), pltpu.VMEM((r, HEAD_DIM), F32)],
        compiler_params=_params("arbitrary", "arbitrary", "arbitrary"),
    )(q, slct, slct, wint, wint, kct, vct, gates, band, near, wadd, ov)


def _cmp_select(q, kcmp, vcmp, bias, ov, q0, tq, nc, ns, n_kv, n_rep):
    ncp = kcmp.shape[0]
    qbd = _nsa_qbd(q, n_kv, n_rep)
    r = qbd.shape[0]
    t = q0 + _row_query((r, ncp), tq)
    cidx = _iota((r, ncp), 1)
    mask = (t - (cidx * D_CMP + L_CMP - 1) >= 0) & (cidx < nc)
    s = jnp.where(mask, _dot_nt(qbd, kcmp) + bias, NEG)
    p = jnp.where(mask, jnp.exp(s - jnp.max(s, axis=1, keepdims=True)), 0.0)
    p = p / jnp.maximum(jnp.sum(p, axis=1, keepdims=True), 1e-30)
    o_cmp = _nsa_gather(_dot(p.astype(BF16), vcmp), tq, n_kv, n_rep)
    rz = n_kv * tq
    pz = p[0:rz]
    for z in range(1, n_rep):
        pz = pz + p[z * rz:(z + 1) * rz]
    imp = _dot_x2(pz, ov)
    return o_cmp, _rank_select(imp, q0 + _row_query(imp.shape, tq), ns)


def _overlap_matrix(ncp, nsp):
    ci = np.arange(ncp)[:, None]
    bj = np.arange(nsp)[None, :]
    ov = (ci * D_CMP <= bj * SLC_BLOCK + SLC_BLOCK - 1) & (ci * D_CMP + L_CMP - 1 >= bj * SLC_BLOCK)
    return jnp.asarray(ov.astype(np.float32), BF16)


def _cmp_decode_kernel(pt_ref, q_ref, bd_ref, pe_ref, bdn_ref, gk_ref, bias_ref, ov_ref, *rest,
                       tq, n_kv, n_rep, ns, npg, past):
    page_refs = rest[:npg]
    ocmp_ref, sel_ref, xbuf = rest[npg:]
    st = pl.program_id(1)
    n_steps = pl.num_programs(1)
    dkv = n_kv * HEAD_DIM
    n_chunk = past // D_CMP
    pitch = xbuf.shape[1] // D_CMP
    for off, pg in enumerate(page_refs):
        pos = (n_steps - 1 - st) * npg + (npg - 1 - off)
        rows = pg[0].T
        for c in range(PAGE // D_CMP):
            for lb in range(xbuf.shape[0]):
                xbuf[lb, pl.ds(pos * (PAGE // D_CMP) + c, D_CMP, stride=pitch), :] = (
                    rows[c * D_CMP:(c + 1) * D_CMP, lb * LANES:(lb + 1) * LANES])

    @pl.when(st == n_steps - 1)
    def _():
        kc, vc = _compress(xbuf, bd_ref, pe_ref, n_chunk, dkv, pitch)
        kcb = (_head_norm(kc, bdn_ref[...]) * gk_ref[...]).astype(BF16)
        bias = bias_ref[...].reshape(n_kv * n_rep * tq, n_chunk)
        o_cmp, sel = _cmp_select(q_ref[0], kcb, vc.astype(BF16), bias, ov_ref[...], past, tq,
                                 n_chunk - 1, ns, n_kv, n_rep)
        ocmp_ref[0] = o_cmp
        sel_ref[0] = sel.astype(BF16)


def _odd_pitch(n):
    p = -(-n // 8)
    return 8 * (p if p % 2 == 1 else p + 1)


def _cmp_decode(q, cache_t, page_table, bd, pe, bdn, gk, bias, n_kv, npg):
    b, tq, dq = q.shape
    dkv = n_kv * HEAD_DIM
    n_rep = dq // dkv
    n_pages = page_table.shape[1]
    past = n_pages * PAGE
    n_chunk = (past + tq) // D_CMP
    assert n_chunk == past // D_CMP
    ns = -(-(past + tq) // SLC_BLOCK)
    nsp = -(-ns // LANES) * LANES
    ov = _overlap_matrix(n_chunk, nsp)
    const = lambda a: pl.BlockSpec(a.shape, lambda i, s, pt: (0,) * a.ndim)
    grid_spec = pltpu.PrefetchScalarGridSpec(
        num_scalar_prefetch=1,
        grid=(b, n_pages // npg),
        in_specs=[pl.BlockSpec((1, tq, dq), lambda i, s, pt: (i, 0, 0)),
                  const(bd), const(pe), const(bdn), const(gk), const(bias), const(ov)]
        + _page_specs(n_pages, 2 * dkv, npg),
        out_specs=[pl.BlockSpec((1, tq, dq), lambda i, s, pt: (i, 0, 0)),
                   pl.BlockSpec((1, n_kv * tq, nsp), lambda i, s, pt: (i, 0, 0))],
        scratch_shapes=[pltpu.VMEM((2 * dkv // LANES, D_CMP * _odd_pitch(n_chunk), LANES), F32)],
    )
    return pl.pallas_call(
        functools.partial(_cmp_decode_kernel, tq=tq, n_kv=n_kv, n_rep=n_rep, ns=ns, npg=npg, past=past),
        name="cmp_decode",
        grid_spec=grid_spec,
        out_shape=[jax.ShapeDtypeStruct((b, tq, dq), F32), jax.ShapeDtypeStruct((b, n_kv * tq, nsp), BF16)],
        compiler_params=_params("arbitrary", "arbitrary"),
    )(page_table, q, bd, pe, bdn, gk, bias, ov, *([cache_t] * npg))


def _nsa_qbd(q, n_kv, n_rep):
    dkv = n_kv * HEAD_DIM
    return jnp.concatenate([_block_diag_rows(q[:, z * dkv:(z + 1) * dkv], n_kv) for z in range(n_rep)], axis=0)


def _nsa_gather(o_full, tq, n_kv, n_rep):
    rz = n_kv * tq
    return jnp.concatenate([_gather_diag(o_full[z * rz:(z + 1) * rz], tq, n_kv) for z in range(n_rep)], axis=1)


def _merge(gates, ege_ref, o_cmp, o_slc, o_win):
    g = [_dot_x2(gates, ege_ref[br]) for br in range(3)]
    return g[0] * o_cmp + g[1] * o_slc + g[2] * o_win


def _slc_decode_kernel(pt_ref, q_ref, slcnew_ref, winnew_ref, state_ref, sel_ref, gates_ref, ocmp_ref,
                       t0_ref, t1_ref, ege_ref, *rest, tq, n_kv, n_rep, npg, past):
    page_refs = rest[:npg]
    o_ref, qbd_ref, newp, kmask, m_ref, l_ref, acc = rest[npg:]
    st = pl.program_id(1)
    n_steps = pl.num_programs(1)
    dkv = n_kv * HEAD_DIM
    r = n_rep * n_kv * tq
    i_row = _row_query((r, PAGE), tq)
    j_col = _iota((r, PAGE), 1)

    def sel_mask(k0, width=PAGE):
        return jnp.concatenate([kmask[:, pl.ds(k0, width)]] * n_rep, axis=0) > 0.5

    def step(s, mask, pv):
        _softmax_step(jnp.where(mask, s, NEG) if mask is not None else s, pv, m_ref, l_ref, acc, mask)

    def new_tile(bias, mask):
        k = newp[:, 0:dkv].astype(BF16)
        v = newp[:, dkv:2 * dkv].astype(BF16)
        step(_dot_nt(qbd_ref[...], k) + bias, mask, lambda p: _dot(p, v))

    @pl.when(st == 0)
    def _():
        qbd_ref[...] = _nsa_qbd(q_ref[0], n_kv, n_rep)
        kmask[...] = _expand_blocks(sel_ref[0], kmask.shape[1])
        newp[...] = jnp.zeros(newp.shape, F32)
        newp[0:tq, :] = slcnew_ref[0]
        _softmax_init(m_ref, l_ref, acc)
        new_tile(t0_ref[...], sel_mask(past) & (j_col <= i_row))

    width = npg * PAGE
    k0 = pl.multiple_of((n_steps - 1 - st) * width, width)
    kt = jnp.concatenate([pg[0, 0:dkv, :] for pg in page_refs[::-1]], axis=1).astype(BF16)
    vt = jnp.concatenate([pg[0, dkv:2 * dkv, :] for pg in page_refs[::-1]], axis=1).astype(BF16)
    s = _dot(qbd_ref[...], kt)
    near = s[:, width - PAGE:] + jnp.where(st == 0, 1.0, 0.0) * t1_ref[...]
    s = jnp.concatenate([s[:, :width - PAGE], near], axis=1)
    step(s, sel_mask(k0, width), lambda p: _dot_nt(p, vt))

    @pl.when(st == n_steps - 1)
    def _():
        o_slc = _nsa_gather(_softmax_out(l_ref, acc), tq, n_kv, n_rep)
        _softmax_init(m_ref, l_ref, acc)
        newp[0:tq, :] = winnew_ref[0]
        new_tile(t0_ref[...], j_col <= i_row)
        n_back = WINDOW // PAGE
        for back in range(1, n_back + 1):
            sl = slice((n_back - back) * PAGE, (n_back - back + 1) * PAGE)
            s = _dot(qbd_ref[...], state_ref[0, 0:dkv, sl].astype(BF16))
            if back == 1:
                s = s + t1_ref[...]
            mask = (j_col > i_row) if back == n_back else None
            step(s, mask, lambda p, sl=sl: _dot_nt(p, state_ref[0, dkv:2 * dkv, sl].astype(BF16)))
        o_win = _nsa_gather(_softmax_out(l_ref, acc), tq, n_kv, n_rep)
        o_ref[0] = _merge(gates_ref[0], ege_ref, ocmp_ref[0], o_slc, o_win).astype(BF16)


def _slc_decode(q, slc_new, win_new, state_t, sel, gates, o_cmp, cache_t, page_table, t0, t1, ege, n_kv, npg):
    b, tq, dq = q.shape
    dkv = n_kv * HEAD_DIM
    n_rep = dq // dkv
    r = n_rep * n_kv * tq
    n_pages = page_table.shape[1]
    past = n_pages * PAGE
    const = lambda a: pl.BlockSpec(a.shape, lambda i, s, pt: (0,) * a.ndim)
    per_seq = lambda a: pl.BlockSpec((1,) + a.shape[1:], lambda i, s, pt: (i,) + (0,) * (a.ndim - 1))
    grid_spec = pltpu.PrefetchScalarGridSpec(
        num_scalar_prefetch=1,
        grid=(b, n_pages // npg),
        in_specs=[per_seq(q), per_seq(slc_new), per_seq(win_new), per_seq(state_t), per_seq(sel), per_seq(gates),
                  per_seq(o_cmp), const(t0), const(t1), const(ege)] + _page_specs(n_pages, 2 * dkv, npg),
        out_specs=pl.BlockSpec((1, tq, dq), lambda i, s, pt: (i, 0, 0)),
        scratch_shapes=[pltpu.VMEM((r, dkv), BF16), pltpu.VMEM((PAGE, 2 * dkv), F32),
                        pltpu.VMEM((n_kv * tq, past + PAGE), F32)] + _softmax_scratch(r, dkv),
    )
    return pl.pallas_call(
        functools.partial(_slc_decode_kernel, tq=tq, n_kv=n_kv, n_rep=n_rep, npg=npg, past=past),
        name="slc_decode",
        grid_spec=grid_spec,
        out_shape=jax.ShapeDtypeStruct((b, tq, dq), BF16),
        compiler_params=_params("arbitrary", "arbitrary"),
    )(page_table, q, slc_new, win_new, state_t, sel, gates, o_cmp, t0, t1, ege, *([cache_t] * npg))


def _rel_buckets(dist):
    n = np.maximum(dist, 0)
    exact = N_BUCKETS // 2
    nf = np.maximum(n, 1).astype(np.float64)
    large = exact + (np.log(nf / exact) / math.log(REL_MAX_DIST / exact) * (N_BUCKETS - exact)).astype(np.int64)
    return np.where(n < exact, n, np.minimum(large, N_BUCKETS - 1)).astype(np.int32)


def _head_block_diag(width, scale):
    h = np.arange(width) // HEAD_DIM
    return jnp.asarray((h[:, None] == h[None, :]).astype(np.float32) * scale, BF16)


class _NsaLayout:
    def __init__(self, n_heads, n_kv):
        n_rep = n_heads // n_kv
        self.n_kv, self.n_rep = n_kv, n_rep
        new = np.arange(n_heads)
        z, g = new // n_kv, new % n_kv
        self.head_perm = g * n_rep + z
        self.col_perm = (self.head_perm[:, None] * HEAD_DIM + np.arange(HEAD_DIM)[None, :]).reshape(-1)
        self.gate_perm = (np.arange(3)[:, None] * n_heads + self.head_perm[None, :]).reshape(-1)
        ege = np.zeros((3, LANES, n_heads * HEAD_DIM), np.float32)
        for br in range(3):
            for h in range(n_heads):
                ege[br, br * n_heads + h, h * HEAD_DIM:(h + 1) * HEAD_DIM] = 1.0
        self.ege = jnp.asarray(ege, BF16)


def _bucket_lookup(rel_hd, dist):
    buckets = jnp.asarray(_rel_buckets(dist).reshape(-1))
    onehot = (jnp.arange(N_BUCKETS, dtype=jnp.int32)[:, None] == buckets[None, :]).astype(F32)
    tab = jnp.dot(rel_hd, onehot, precision=lax.Precision.HIGHEST)
    return tab.reshape((rel_hd.shape[0],) + dist.shape)


def _near_bias(rel_hd, dist):
    far = rel_hd[:, N_BUCKETS - 1]
    tab = _bucket_lookup(rel_hd, dist) - far[:, None, None]
    return jnp.where(jnp.asarray(dist >= 0)[None], tab, 0.0)


def _toeplitz_tiles(rel_hd, tq, tk):
    i = np.arange(tq)[:, None]
    j = np.arange(tk)[None, :]
    return _near_bias(rel_hd, i - j), _near_bias(rel_hd, tk + i - j)


def _cmp_band(rel_hd, tq, ncp):
    i = np.arange(tq)[:, None]
    m = np.arange(ncp)[None, :]
    dist = D_CMP * (BAND_BACK - m) + i - (L_CMP - 1)
    return _near_bias(rel_hd, np.where(m < 2 * BAND_BACK + tq // D_CMP, dist, -1))


def _cmp_bias(rel_hd, qpos, n_chunk):
    dc = qpos[:, None] - (np.arange(n_chunk)[None, :] * D_CMP + L_CMP - 1)
    return _bucket_lookup(rel_hd, dc)


def kernel(x_prompt, x_sample, cache_sb_kv, cache_fox_kv, cache_fox_logf, cache_cmp_kv, cache_slc_kv, state_win_kv,
           page_table, c_prompt, c_sample, norm_gain, w_ada, b_ada, ffn_w_in, ffn_w_out, w_in_ab, b_forget,
           fox_qk_gain, w_out_ab, w_in_nsa, b_nsa_gate, nsa_qk_gain, cmp_w, cmp_pe, rel_bias, w_out_nsa):
    bp, s_len, d = x_prompt.shape
    bs, t_dec, _ = x_sample.shape
    n_pool = cache_sb_kv.shape[0]
    h_sb, h_fox = cache_sb_kv.shape[3], cache_fox_kv.shape[3]
    n_kv = cache_cmp_kv.shape[3]
    n_heads = rel_bias.shape[1]
    da = h_sb * HEAD_DIM
    dkv = n_kv * HEAD_DIM
    dq = n_heads * HEAD_DIM
    assert h_sb == h_fox
    n_pages = page_table.shape[1]
    past = n_pages * PAGE
    lay = _NsaLayout(n_heads, n_kv)

    ffn_in = ffn_w_in.astype(BF16)
    ffn_out = ffn_w_out.astype(BF16)
    w_ab = jnp.pad(w_in_ab, ((0, 0), (0, LANES - h_fox))).astype(BF16)
    bfp = jnp.pad(b_forget, (0, LANES - h_fox)).reshape(1, LANES)
    bd_ab = _head_block_diag(da, 1.0 / HEAD_DIM)
    gq_fox = jnp.tile(fox_qk_gain[0], h_fox).reshape(1, da)
    gk_fox = jnp.tile(fox_qk_gain[1], h_fox).reshape(1, da)
    wq_ab = jnp.concatenate([w_in_ab[:, 0:da], w_in_ab[:, 3 * da:4 * da]], axis=1).astype(BF16)
    wkvt_ab = jnp.concatenate([w_in_ab[:, da:3 * da], w_in_ab[:, 4 * da:6 * da]], axis=1).T.astype(BF16)
    wflt_ab = jnp.pad(w_in_ab[:, 6 * da:].T, ((0, 16 - h_fox), (0, 0))).astype(BF16)
    w_out_sb = w_out_ab[:da].astype(BF16)
    w_out_fox = w_out_ab[da:].astype(BF16)
    n_rep = n_heads // n_kv
    wq_swapped = w_in_nsa[:, 0:dq].reshape(d, n_kv, n_rep, HEAD_DIM).transpose(0, 2, 1, 3).reshape(d, dq)
    wg_swapped = w_in_nsa[:, dq + 6 * dkv:].reshape(d, 3, n_kv, n_rep).transpose(0, 1, 3, 2).reshape(d, 3 * n_heads)
    w_nsa = jnp.concatenate([wq_swapped, w_in_nsa[:, dq:dq + 6 * dkv], wg_swapped], axis=1)
    w_nsa = jnp.pad(w_nsa, ((0, 0), (0, LANES - 3 * n_heads))).astype(BF16)
    bg_swapped = b_nsa_gate.reshape(3, n_kv, n_rep).transpose(0, 2, 1).reshape(3 * n_heads)
    bg_perm = jnp.pad(bg_swapped, (0, LANES - 3 * n_heads)).reshape(1, LANES)
    wq_nsa = w_in_nsa[:, 0:dq].astype(BF16)
    wcmp_nsa = w_in_nsa[:, dq:dq + 2 * dkv].astype(BF16)
    wkvt_nsa = w_in_nsa[:, dq + 2 * dkv:dq + 6 * dkv].T.astype(BF16)
    wg_nsa = jnp.pad(w_in_nsa[:, dq + 6 * dkv:], ((0, 0), (0, LANES - 3 * n_heads))).astype(BF16)
    bg = jnp.pad(b_nsa_gate, (0, LANES - 3 * n_heads)).reshape(1, LANES)
    bd_kv = _head_block_diag(dkv, 1.0 / HEAD_DIM)
    tile_kv = lambda g: jnp.tile(g, n_kv).reshape(1, dkv)
    w_out_n = w_out_nsa.astype(BF16)
    w_out_n_perm = w_out_nsa.reshape(n_kv, n_rep, HEAD_DIM, d).transpose(1, 0, 2, 3).reshape(dq, d).astype(BF16)
    eye = jnp.eye(n_kv, dtype=F32)
    cmp_bd = jnp.einsum('gh,kjde->kjgdhe', eye, cmp_w).reshape(2, L_CMP, dkv, dkv).astype(BF16)
    cmp_pe_t = jnp.tile(cmp_pe, (1, 1, n_kv))
    rel_orig = rel_bias.T
    rel_perm = rel_bias.reshape(N_BUCKETS, n_kv, n_rep).transpose(0, 2, 1).reshape(N_BUCKETS, n_heads).T
    gk_cmp = tile_kv(nsa_qk_gain[1])

    mod = _modulation(jnp.concatenate([c_prompt, c_sample], axis=0), w_ada, b_ada)
    mod = mod.reshape(mod.shape[0], bp + bs, 3, 3, 1, d)

    def mods(l, sub, lo, hi):
        return tuple(mod[l, lo:hi, sub, k] for k in range(3))

    mp = lambda l, sub: mods(l, sub, 0, bp)
    tt = 512
    x = _ffn(x_prompt, mp(0, 0), norm_gain[0, 0], ffn_in[0, 0], ffn_out[0, 0], 1, tt)
    qsb, qfx, sbt, fxt, lft = _proj_ab_t(x, mp(0, 1), norm_gain[0, 1], wq_ab, wkvt_ab, wflt_ab, bd_ab, gq_fox,
                                         gk_fox.reshape(da, 1), b_forget.reshape(h_fox, 1), tt)
    o_sb = _sb_prompt(qsb, sbt, AB_TILE, 2)
    o_fox = _fox_prompt(qfx, fxt, lft, AB_TILE, 2)
    x = _outproj(x, mp(0, 1)[2], [o_sb, o_fox], [w_out_sb, w_out_fox], 1, tt)
    x = _ffn(x, mp(0, 2), norm_gain[0, 2], ffn_in[0, 1], ffn_out[0, 1], 1, tt)
    x = _ffn(x, mp(1, 0), norm_gain[1, 0], ffn_in[1, 0], ffn_out[1, 0], 1, tt)
    q, p_cmp, slct, wint, gates = _proj_nsa_t(x, mp(1, 1), norm_gain[1, 1], wq_nsa, wcmp_nsa, wkvt_nsa, wg_nsa, bd_kv,
                                              tile_kv(nsa_qk_gain[0]), tile_kv(nsa_qk_gain[2]).reshape(dkv, 1),
                                              tile_kv(nsa_qk_gain[3]).reshape(dkv, 1), bg, tt)
    kct, vct = _compress_prompt(p_cmp, cmp_bd, cmp_pe_t, bd_kv, gk_cmp)
    t0, t1 = _toeplitz_tiles(rel_orig, TILE, TILE)
    o = _nsa_prompt(q, slct, wint, kct, vct, gates, _cmp_band(rel_orig, TILE, s_len // D_CMP),
                    jnp.concatenate([t1, t0], axis=2), TILE, n_kv)
    x = _outproj(x, mp(1, 1)[2], [o], [w_out_n], 1, tt)
    y_p = _ffn(x, mp(1, 2), norm_gain[1, 2], ffn_in[1, 1], ffn_out[1, 1], 1, tt)

    c_sb = jnp.transpose(cache_sb_kv, (0, 2, 3, 4, 1)).reshape(n_pool, 2 * da, PAGE)
    c_fox = jnp.transpose(cache_fox_kv, (0, 2, 3, 4, 1)).reshape(n_pool, 2 * da, PAGE)
    c_lf = jnp.transpose(cache_fox_logf, (0, 2, 1))
    c_cmp = jnp.transpose(cache_cmp_kv, (0, 2, 3, 4, 1)).reshape(n_pool, 2 * dkv, PAGE)
    c_slc = jnp.transpose(cache_slc_kv, (0, 2, 3, 4, 1)).reshape(n_pool, 2 * dkv, PAGE)
    state_t = jnp.transpose(state_win_kv, (0, 2, 3, 4, 1)).reshape(bs, 2 * dkv, state_win_kv.shape[1])
    npg = min(n_pages, STEP_BYTES // (2 * da * PAGE * 4))
    npg_kv = min(n_pages, STEP_BYTES // (2 * dkv * PAGE * 4))
    assert n_pages % npg == 0 and n_pages % npg_kv == 0
    ms = lambda l, sub: mods(l, sub, bp, bp + bs)
    x = _ffn(x_sample, ms(0, 0), norm_gain[0, 0], ffn_in[0, 0], ffn_out[0, 0], bs, t_dec)
    qsb, s_sb, qfx, s_fox, s_logf, logfp = _proj_ab(x, ms(0, 1), norm_gain[0, 1], w_ab, bd_ab, gq_fox, gk_fox, bfp,
                                                    h_fox, bs, t_dec)
    o_sb = _sb_decode(qsb, s_sb, c_sb, page_table, npg)
    o_fox = _fox_decode(qfx, s_fox, logfp, c_fox, c_lf, page_table, npg)
    x = _outproj(x, ms(0, 1)[2], [o_sb, o_fox], [w_out_sb, w_out_fox], bs, t_dec)
    x = _ffn(x, ms(0, 2), norm_gain[0, 2], ffn_in[0, 1], ffn_out[0, 1], bs, t_dec)
    x = _ffn(x, ms(1, 0), norm_gain[1, 0], ffn_in[1, 0], ffn_out[1, 0], bs, t_dec)
    q, s_cmp, s_slc, s_win, gates = _proj_nsa(x, ms(1, 1), norm_gain[1, 1], w_nsa, bd_kv, tile_kv(nsa_qk_gain[0]),
                                              tile_kv(nsa_qk_gain[2]), tile_kv(nsa_qk_gain[3]), bg_perm, dq, bs, t_dec)
    bias = _cmp_bias(rel_perm, past + np.arange(t_dec), past // D_CMP)
    o_cmp, sel = _cmp_decode(q, c_cmp, page_table, cmp_bd, cmp_pe_t, bd_kv, gk_cmp, bias, n_kv, npg_kv)
    t0, t1 = _toeplitz_tiles(rel_perm, t_dec, PAGE)
    o = _slc_decode(q, s_slc, s_win, state_t, sel, gates, o_cmp, c_slc, page_table,
                    t0.reshape(-1, PAGE), t1.reshape(-1, PAGE), lay.ege, n_kv, npg_kv)
    x = _outproj(x, ms(1, 1)[2], [o], [w_out_n_perm], bs, t_dec)
    y_s = _ffn(x, ms(1, 2), norm_gain[1, 2], ffn_in[1, 1], ffn_out[1, 1], bs, t_dec)

    kv5 = lambda a, h: a.reshape(a.shape[0], a.shape[1], 2, h, HEAD_DIM)
    from_t = lambda a, h: kv5(jnp.transpose(a, (0, 2, 1)), h)
    win_len = min(WINDOW, s_len)
    state_rows = state_win_kv.reshape(bs, state_win_kv.shape[1], 2 * dkv)
    s_win_all = jnp.concatenate([state_rows, s_win], axis=1)
    new_len = min(WINDOW, s_win_all.shape[1])
    return (y_p, y_s, from_t(sbt, h_sb), kv5(s_sb, h_sb), from_t(fxt, h_fox), kv5(s_fox, h_fox),
            jnp.transpose(lft, (0, 2, 1)), s_logf, kv5(p_cmp, n_kv), kv5(s_cmp, n_kv), from_t(slct, n_kv),
            kv5(s_slc, n_kv), from_t(wint[:, :, s_len - win_len:], n_kv),
            kv5(s_win_all[:, s_win_all.shape[1] - new_len:], n_kv))
```

```python
import functools
import math

import numpy as np
import jax
import jax.numpy as jnp
from jax import lax
from jax.experimental import pallas as pl
from jax.experimental.pallas import tpu as pltpu

F32 = jnp.float32
BF16 = jnp.bfloat16

HEAD_DIM = 64
PAGE = 128
L_CMP = 32
D_CMP = 16
SLC_BLOCK = 64
N_SELECT = 16
WINDOW = 512
N_BUCKETS = 32
REL_MAX_DIST = 128
FORCE_SCORE = 1e4
NEG = -1e30
EPS = 1e-6
MACARON = 0.5
LANES = 128
VMEM_LIMIT = 56 * 1024 * 1024
FF_CHUNK = 256
TILE = 128
FAR_TILE = 256
AB_TILE = 256
STEP_BYTES = 8 * 1024 * 1024
DEAD_LOG = -104.0
SUM_ROWS = 16
BAND_BACK = (REL_MAX_DIST + L_CMP) // D_CMP


def _params(*sem):
    return pltpu.CompilerParams(dimension_semantics=sem, vmem_limit_bytes=VMEM_LIMIT)


def _dot(a, b):
    return jnp.dot(a, b, preferred_element_type=F32)


def _dot_nt(a, b):
    return lax.dot_general(a, b, (((1,), (1,)), ((), ())), preferred_element_type=F32)


def _split2(x):
    hi = x.astype(BF16)
    lo = (x - hi.astype(F32)).astype(BF16)
    return hi, lo


def _split3(x):
    hi = x.astype(BF16)
    r = x - hi.astype(F32)
    mid = r.astype(BF16)
    lo = (r - mid.astype(F32)).astype(BF16)
    return hi, mid, lo


def _dot_x2(x, w):
    hi, lo = _split2(x)
    return _dot(hi, w) + _dot(lo, w)


def _dot_x3(x, w):
    hi, mid, lo = _split3(x)
    return _dot(hi, w) + _dot(mid, w) + _dot(lo, w)


def _dot_l2(w, x):
    hi, lo = _split2(x)
    return _dot(w, hi) + _dot(w, lo)


def _log_sigmoid(x):
    return jnp.minimum(x, 0.0) - jnp.log(1.0 + jnp.exp(-jnp.abs(x)))


def _adanorm(x, g, scale, shift):
    ms = jnp.mean(x * x, axis=-1, keepdims=True)
    return (x * lax.rsqrt(ms + EPS) * g) * (1.0 + scale) + shift


def _head_norm(t, bd):
    return t * lax.rsqrt(_dot_x2(t * t, bd) + EPS)


def _head_norm_t(t, bd):
    return t * lax.rsqrt(_dot_l2(bd, t * t) + EPS)


def _iota(shape, dim):
    return lax.broadcasted_iota(jnp.int32, shape, dim)


def _mod_kernel(c_ref, w_ref, b_ref, o_ref):
    c = c_ref[...]
    cs = c * jax.nn.sigmoid(c)
    ch, cl = _split2(cs)
    wh, wl = _split2(w_ref[0])
    o_ref[0] = _dot(ch, wh) + _dot(cl, wh) + _dot(ch, wl) + b_ref[0]


def _modulation(c_all, w_ada, b_ada):
    depth, d, n = w_ada.shape
    rows = c_all.shape[0]
    tn = 1024
    return pl.pallas_call(
        _mod_kernel, name="modulation",
        grid=(depth, n // tn),
        in_specs=[pl.BlockSpec((rows, d), lambda l, j: (0, 0)),
                  pl.BlockSpec((1, d, tn), lambda l, j: (l, 0, j)),
                  pl.BlockSpec((1, 1, tn), lambda l, j: (l, 0, j))],
        out_specs=pl.BlockSpec((1, rows, tn), lambda l, j: (l, 0, j)),
        out_shape=jax.ShapeDtypeStruct((depth, rows, n), F32),
        compiler_params=_params("arbitrary", "arbitrary"),
    )(c_all, w_ada, b_ada.reshape(depth, 1, n))


def _ffn_kernel(x_ref, shift_ref, scale_ref, gate_ref, g_ref, win_ref, wout_ref, o_ref, hid_ref, *, d_ff):
    x = x_ref[...]
    bb, tt, d = x.shape
    h = _adanorm(x, g_ref[...], scale_ref[...], shift_ref[...]).reshape(bb * tt, d).astype(BF16)
    for c in range(d_ff // FF_CHUNK):
        a = _dot(h, win_ref[:, c * FF_CHUNK:(c + 1) * FF_CHUNK])
        b = _dot(h, win_ref[:, d_ff + c * FF_CHUNK:d_ff + (c + 1) * FF_CHUNK])
        hid_ref[:, c * FF_CHUNK:(c + 1) * FF_CHUNK] = (a * jax.nn.sigmoid(a) * b).astype(BF16)
    o = _dot(hid_ref[...], wout_ref[...]).reshape(bb, tt, d)
    o_ref[...] = x + (MACARON * gate_ref[...]) * o


def _ffn(x, mod3, g, w_in, w_out, bb, tt):
    b, t, d = x.shape
    d_ff = w_out.shape[0]
    shift, scale, gate = mod3
    mspec = pl.BlockSpec((bb, 1, d), lambda i, j: (i, 0, 0))
    xspec = pl.BlockSpec((bb, tt, d), lambda i, j: (i, j, 0))
    return pl.pallas_call(
        functools.partial(_ffn_kernel, d_ff=d_ff), name="ffn",
        grid=(b // bb, t // tt),
        in_specs=[xspec, mspec, mspec, mspec,
                  pl.BlockSpec((1, 1, d), lambda i, j: (0, 0, 0)),
                  pl.BlockSpec((d, 2 * d_ff), lambda i, j: (0, 0)),
                  pl.BlockSpec((d_ff, d), lambda i, j: (0, 0))],
        out_specs=xspec,
        out_shape=jax.ShapeDtypeStruct(x.shape, F32),
        scratch_shapes=[pltpu.VMEM((bb * tt, d_ff), BF16)],
        compiler_params=_params("arbitrary", "arbitrary"),
    )(x, shift, scale, gate, g.reshape(1, 1, d), w_in, w_out)


def _mix_ffn_kernel(*refs, d_ff, n_in):
    x_ref, gmix_ref, shift_ref, scale_ref, gate_ref, g_ref, win_ref, wout_ref = refs[:8]
    o_refs = refs[8:8 + n_in]
    w_refs = refs[8 + n_in:8 + 2 * n_in]
    out_ref, hid_ref = refs[8 + 2 * n_in:]
    y = None
    for o_ref, w_ref in zip(o_refs, w_refs):
        part = _dot(o_ref[0], w_ref[...])
        y = part if y is None else y + part
    x = x_ref[0] + gmix_ref[0] * y
    h = _adanorm(x, g_ref[0], scale_ref[0], shift_ref[0]).astype(BF16)
    for c in range(d_ff // FF_CHUNK):
        a = _dot(h, win_ref[:, c * FF_CHUNK:(c + 1) * FF_CHUNK])
        b = _dot(h, win_ref[:, d_ff + c * FF_CHUNK:d_ff + (c + 1) * FF_CHUNK])
        hid_ref[:, c * FF_CHUNK:(c + 1) * FF_CHUNK] = (a * jax.nn.sigmoid(a) * b).astype(BF16)
    out_ref[0] = x + (MACARON * gate_ref[0]) * _dot(hid_ref[...], wout_ref[...])


def _mix_ffn(x, gmix, outs, weights, mod3, g, w_in, w_out, tt):
    b, t, d = x.shape
    d_ff = w_out.shape[0]
    n_in = len(outs)
    shift, scale, gate = mod3
    mspec = pl.BlockSpec((1, 1, d), lambda i, j: (i, 0, 0))
    xspec = pl.BlockSpec((1, tt, d), lambda i, j: (i, j, 0))
    once = lambda a: pl.BlockSpec(a.shape, lambda i, j: (0,) * a.ndim, pipeline_mode=pl.Buffered(1))
    return pl.pallas_call(
        functools.partial(_mix_ffn_kernel, d_ff=d_ff, n_in=n_in), name="mix_ffn",
        grid=(b, t // tt),
        in_specs=[xspec, mspec, mspec, mspec, mspec, pl.BlockSpec((1, 1, d), lambda i, j: (0, 0, 0)),
                  once(w_in), once(w_out)]
        + [pl.BlockSpec((1, tt, o.shape[-1]), lambda i, j: (i, j, 0)) for o in outs] + [once(w) for w in weights],
        out_specs=xspec,
        out_shape=jax.ShapeDtypeStruct(x.shape, F32),
        scratch_shapes=[pltpu.VMEM((tt, d_ff), BF16)],
        compiler_params=_params("arbitrary", "arbitrary"),
    )(x, gmix, shift, scale, gate, g.reshape(1, 1, d), w_in, w_out, *outs, *weights)


def _outproj_kernel(*refs, n_in):
    x_ref, gate_ref = refs[0], refs[1]
    o_refs = refs[2:2 + n_in]
    w_refs = refs[2 + n_in:2 + 2 * n_in]
    out_ref = refs[2 + 2 * n_in]
    x = x_ref[...]
    bb, tt, d = x.shape
    y = None
    for o_ref, w_ref in zip(o_refs, w_refs):
        o = o_ref[...].astype(F32).reshape(bb * tt, o_ref.shape[-1]).astype(BF16)
        part = _dot(o, w_ref[...])
        y = part if y is None else y + part
    out_ref[...] = x + gate_ref[...] * y.reshape(bb, tt, d)


def _outproj(x, gate, outs, weights, bb, tt):
    b, t, d = x.shape
    n_in = len(outs)
    xspec = pl.BlockSpec((bb, tt, d), lambda i, j: (i, j, 0))
    in_specs = [xspec, pl.BlockSpec((bb, 1, d), lambda i, j: (i, 0, 0))]
    in_specs += [pl.BlockSpec((bb, tt, o.shape[-1]), lambda i, j: (i, j, 0)) for o in outs]
    in_specs += [pl.BlockSpec(w.shape, lambda i, j: (0, 0)) for w in weights]
    return pl.pallas_call(
        functools.partial(_outproj_kernel, n_in=n_in), name="outproj",
        grid=(b // bb, t // tt),
        in_specs=in_specs,
        out_specs=xspec,
        out_shape=jax.ShapeDtypeStruct(x.shape, F32),
        compiler_params=_params("arbitrary", "arbitrary"),
    )(x, gate, *outs, *weights)


def _proj_ab_kernel(x_ref, shift_ref, scale_ref, g_ref, w_ref, bd_ref, gq_ref, gk_ref, bf_ref,
                    qsb_ref, sbkv_ref, qfx_ref, fxkv_ref, logf_ref, logfp_ref, *, da, n_f):
    x = x_ref[...]
    bb, tt, d = x.shape
    m = bb * tt
    h = _adanorm(x, g_ref[...], scale_ref[...], shift_ref[...]).reshape(m, d).astype(BF16)
    sc = HEAD_DIM ** -0.5
    qsb_ref[...] = (_dot(h, w_ref[:, 0:da]) * sc).reshape(bb, tt, da).astype(BF16)
    sbkv_ref[...] = _dot(h, w_ref[:, da:3 * da]).reshape(bb, tt, 2 * da)
    bd = bd_ref[...]
    qf = _head_norm(_dot(h, w_ref[:, 3 * da:4 * da]), bd) * gq_ref[...]
    qfx_ref[...] = (qf * sc).reshape(bb, tt, da).astype(BF16)
    kf = _head_norm(_dot(h, w_ref[:, 4 * da:5 * da]), bd) * gk_ref[...]
    fxkv_ref[:, :, 0:da] = kf.reshape(bb, tt, da)
    fxkv_ref[:, :, da:2 * da] = _dot(h, w_ref[:, 5 * da:6 * da]).reshape(bb, tt, da)
    lf = _log_sigmoid(_dot(h, w_ref[:, 6 * da:6 * da + LANES]) + bf_ref[...])
    lf = jnp.where(_iota(lf.shape, 1) < n_f, lf, 0.0)
    logfp_ref[...] = lf.reshape(bb, tt, LANES)
    logf_ref[...] = lf[:, 0:n_f].reshape(bb, tt, n_f)


def _proj_ab(x, mod3, g, w, bd, gq, gk, bfp, n_f, bb, tt):
    b, t, d = x.shape
    da = bd.shape[0]
    shift, scale, _ = mod3
    mspec = pl.BlockSpec((bb, 1, d), lambda i, j: (i, 0, 0))
    xspec = pl.BlockSpec((bb, tt, d), lambda i, j: (i, j, 0))
    row = lambda width: pl.BlockSpec((1, width), lambda i, j: (0, 0))
    ospec = lambda width: pl.BlockSpec((bb, tt, width), lambda i, j: (i, j, 0))
    return pl.pallas_call(
        functools.partial(_proj_ab_kernel, da=da, n_f=n_f), name="proj_ab",
        grid=(b // bb, t // tt),
        in_specs=[xspec, mspec, mspec, pl.BlockSpec((1, 1, d), lambda i, j: (0, 0, 0)),
                  pl.BlockSpec(w.shape, lambda i, j: (0, 0)), pl.BlockSpec(bd.shape, lambda i, j: (0, 0)),
                  row(da), row(da), row(LANES)],
        out_specs=[ospec(da), ospec(2 * da), ospec(da), ospec(2 * da), ospec(n_f), ospec(LANES)],
        out_shape=[jax.ShapeDtypeStruct((b, t, da), BF16), jax.ShapeDtypeStruct((b, t, 2 * da), F32),
                   jax.ShapeDtypeStruct((b, t, da), BF16), jax.ShapeDtypeStruct((b, t, 2 * da), F32),
                   jax.ShapeDtypeStruct((b, t, n_f), F32), jax.ShapeDtypeStruct((b, t, LANES), F32)],
        compiler_params=_params("arbitrary", "arbitrary"),
    )(x, shift, scale, g.reshape(1, 1, d), w, bd, gq, gk, bfp)


def _proj_ab_t_kernel(x_ref, shift_ref, scale_ref, g_ref, wq_ref, wkvt_ref, wflt_ref, bd_ref, gq_ref, gk_ref, bf_ref,
                      qsb_ref, qfx_ref, sbt_ref, fxt_ref, lft_ref, *, da, n_f):
    x = x_ref[0]
    h = _adanorm(x, g_ref[0], scale_ref[0], shift_ref[0]).astype(BF16)
    sc = HEAD_DIM ** -0.5
    bd = bd_ref[...]
    qa = _dot(h, wq_ref[:, 0:da]) * sc
    qf = _head_norm(_dot(h, wq_ref[:, da:2 * da]), bd) * (gq_ref[...] * sc)
    for hh in range(da // HEAD_DIM):
        qsb_ref[0, hh] = qa[:, hh * HEAD_DIM:(hh + 1) * HEAD_DIM].astype(BF16)
        qfx_ref[0, hh] = qf[:, hh * HEAD_DIM:(hh + 1) * HEAD_DIM].astype(BF16)
    sbt_ref[0] = _dot_nt(wkvt_ref[0:2 * da, :], h)
    kft = _head_norm_t(_dot_nt(wkvt_ref[2 * da:3 * da, :], h), bd)
    fxt_ref[0, 0:da, :] = kft * gk_ref[...]
    fxt_ref[0, da:2 * da, :] = _dot_nt(wkvt_ref[3 * da:4 * da, :], h)
    lft_ref[0] = _log_sigmoid(_dot_nt(wflt_ref[...], h)[0:n_f, :] + bf_ref[...])


def _proj_ab_t(x, mod3, g, wq, wkvt, wflt, bd, gq, gk_col, bf_col, tt):
    b, t, d = x.shape
    da = bd.shape[0]
    n_h = da // HEAD_DIM
    n_f = bf_col.shape[0]
    shift, scale, _ = mod3
    mspec = pl.BlockSpec((1, 1, d), lambda i, j: (i, 0, 0))
    const = lambda a: pl.BlockSpec(a.shape, lambda i, j: (0,) * a.ndim)
    qspec = pl.BlockSpec((1, n_h, tt, HEAD_DIM), lambda i, j: (i, 0, j, 0))
    tspec = lambda rows: pl.BlockSpec((1, rows, tt), lambda i, j: (i, 0, j))
    return pl.pallas_call(
        functools.partial(_proj_ab_t_kernel, da=da, n_f=n_f), name="proj_ab_t",
        grid=(b, t // tt),
        in_specs=[pl.BlockSpec((1, tt, d), lambda i, j: (i, j, 0)), mspec, mspec,
                  pl.BlockSpec((1, 1, d), lambda i, j: (0, 0, 0)),
                  const(wq), const(wkvt), const(wflt), const(bd), const(gq), const(gk_col), const(bf_col)],
        out_specs=[qspec, qspec, tspec(2 * da), tspec(2 * da), tspec(n_f)],
        out_shape=[jax.ShapeDtypeStruct((b, n_h, t, HEAD_DIM), BF16), jax.ShapeDtypeStruct((b, n_h, t, HEAD_DIM), BF16),
                   jax.ShapeDtypeStruct((b, 2 * da, t), F32), jax.ShapeDtypeStruct((b, 2 * da, t), F32),
                   jax.ShapeDtypeStruct((b, n_f, t), F32)],
        compiler_params=_params("arbitrary", "arbitrary"),
    )(x, shift, scale, g.reshape(1, 1, d), wq, wkvt, wflt, bd, gq, gk_col, bf_col)


def _proj_nsa_kernel(x_ref, shift_ref, scale_ref, g_ref, w_ref, bd_ref, gq_ref, gs_ref, gw_ref, bg_ref,
                     q_ref, cmp_ref, slc_ref, win_ref, gates_ref, *, dq, dkv):
    x = x_ref[...]
    bb, tt, d = x.shape
    m = bb * tt
    h = _adanorm(x, g_ref[...], scale_ref[...], shift_ref[...]).reshape(m, d).astype(BF16)
    bd = bd_ref[...]
    sc = HEAD_DIM ** -0.5
    for c in range(dq // dkv):
        qc = _head_norm(_dot(h, w_ref[:, c * dkv:(c + 1) * dkv]), bd) * gq_ref[...]
        q_ref[:, :, c * dkv:(c + 1) * dkv] = (qc * sc).reshape(bb, tt, dkv).astype(BF16)
    o = dq
    cmp_ref[...] = _dot(h, w_ref[:, o:o + 2 * dkv]).reshape(bb, tt, 2 * dkv)
    o += 2 * dkv
    ks = _head_norm(_dot(h, w_ref[:, o:o + dkv]), bd) * gs_ref[...]
    slc_ref[:, :, 0:dkv] = ks.reshape(bb, tt, dkv)
    slc_ref[:, :, dkv:2 * dkv] = _dot(h, w_ref[:, o + dkv:o + 2 * dkv]).reshape(bb, tt, dkv)
    o += 2 * dkv
    kw = _head_norm(_dot(h, w_ref[:, o:o + dkv]), bd) * gw_ref[...]
    win_ref[:, :, 0:dkv] = kw.reshape(bb, tt, dkv)
    win_ref[:, :, dkv:2 * dkv] = _dot(h, w_ref[:, o + dkv:o + 2 * dkv]).reshape(bb, tt, dkv)
    o += 2 * dkv
    gates_ref[...] = jax.nn.sigmoid(_dot(h, w_ref[:, o:o + LANES]) + bg_ref[...]).reshape(bb, tt, LANES)


def _proj_nsa(x, mod3, g, w, bd, gq, gs, gw, bg, dq, bb, tt):
    b, t, d = x.shape
    dkv = bd.shape[0]
    shift, scale, _ = mod3
    mspec = pl.BlockSpec((bb, 1, d), lambda i, j: (i, 0, 0))
    xspec = pl.BlockSpec((bb, tt, d), lambda i, j: (i, j, 0))
    row = lambda width: pl.BlockSpec((1, width), lambda i, j: (0, 0))
    ospec = lambda width: pl.BlockSpec((bb, tt, width), lambda i, j: (i, j, 0))
    return pl.pallas_call(
        functools.partial(_proj_nsa_kernel, dq=dq, dkv=dkv), name="proj_nsa",
        grid=(b // bb, t // tt),
        in_specs=[xspec, mspec, mspec, pl.BlockSpec((1, 1, d), lambda i, j: (0, 0, 0)),
                  pl.BlockSpec(w.shape, lambda i, j: (0, 0)), pl.BlockSpec(bd.shape, lambda i, j: (0, 0)),
                  row(dkv), row(dkv), row(dkv), row(LANES)],
        out_specs=[ospec(dq), ospec(2 * dkv), ospec(2 * dkv), ospec(2 * dkv), ospec(LANES)],
        out_shape=[jax.ShapeDtypeStruct((b, t, dq), BF16), jax.ShapeDtypeStruct((b, t, 2 * dkv), F32),
                   jax.ShapeDtypeStruct((b, t, 2 * dkv), F32), jax.ShapeDtypeStruct((b, t, 2 * dkv), F32),
                   jax.ShapeDtypeStruct((b, t, LANES), F32)],
        compiler_params=_params("arbitrary", "arbitrary"),
    )(x, shift, scale, g.reshape(1, 1, d), w, bd, gq, gs, gw, bg)


def _proj_nsa_t_kernel(x_ref, shift_ref, scale_ref, g_ref, wq_ref, wcmp_ref, wkvt_ref, wg_ref, bd_ref, gq_ref, gs_ref,
                       gw_ref, bg_ref, q_ref, cmp_ref, slct_ref, wint_ref, gates_ref, *, dq, dkv):
    x = x_ref[0]
    h = _adanorm(x, g_ref[0], scale_ref[0], shift_ref[0]).astype(BF16)
    bd = bd_ref[...]
    sc = HEAD_DIM ** -0.5
    per = dkv // HEAD_DIM
    for c in range(dq // dkv):
        qc = _head_norm(_dot(h, wq_ref[:, c * dkv:(c + 1) * dkv]), bd) * (gq_ref[...] * sc)
        for hh in range(per):
            q_ref[0, c * per + hh] = qc[:, hh * HEAD_DIM:(hh + 1) * HEAD_DIM].astype(BF16)
    cmp_ref[0] = _dot(h, wcmp_ref[...])
    slct_ref[0, 0:dkv, :] = _head_norm_t(_dot_nt(wkvt_ref[0:dkv, :], h), bd) * gs_ref[...]
    slct_ref[0, dkv:2 * dkv, :] = _dot_nt(wkvt_ref[dkv:2 * dkv, :], h)
    wint_ref[0, 0:dkv, :] = _head_norm_t(_dot_nt(wkvt_ref[2 * dkv:3 * dkv, :], h), bd) * gw_ref[...]
    wint_ref[0, dkv:2 * dkv, :] = _dot_nt(wkvt_ref[3 * dkv:4 * dkv, :], h)
    gates_ref[0] = jax.nn.sigmoid(_dot(h, wg_ref[...]) + bg_ref[...])


def _proj_nsa_t(x, mod3, g, wq, wcmp, wkvt, wg, bd, gq, gs_col, gw_col, bg, tt):
    b, t, d = x.shape
    dkv = bd.shape[0]
    dq = wq.shape[1]
    n_h = dq // HEAD_DIM
    shift, scale, _ = mod3
    mspec = pl.BlockSpec((1, 1, d), lambda i, j: (i, 0, 0))
    const = lambda a: pl.BlockSpec(a.shape, lambda i, j: (0,) * a.ndim)
    tspec = lambda rows: pl.BlockSpec((1, rows, tt), lambda i, j: (i, 0, j))
    rspec = lambda width: pl.BlockSpec((1, tt, width), lambda i, j: (i, j, 0))
    return pl.pallas_call(
        functools.partial(_proj_nsa_t_kernel, dq=dq, dkv=dkv), name="proj_nsa_t",
        grid=(b, t // tt),
        in_specs=[rspec(d), mspec, mspec, pl.BlockSpec((1, 1, d), lambda i, j: (0, 0, 0)),
                  const(wq), const(wcmp), const(wkvt), const(wg), const(bd), const(gq), const(gs_col), const(gw_col),
                  const(bg)],
        out_specs=[pl.BlockSpec((1, n_h, tt, HEAD_DIM), lambda i, j: (i, 0, j, 0)), rspec(2 * dkv),
                   tspec(2 * dkv), tspec(2 * dkv), rspec(LANES)],
        out_shape=[jax.ShapeDtypeStruct((b, n_h, t, HEAD_DIM), BF16), jax.ShapeDtypeStruct((b, t, 2 * dkv), F32),
                   jax.ShapeDtypeStruct((b, 2 * dkv, t), F32), jax.ShapeDtypeStruct((b, 2 * dkv, t), F32),
                   jax.ShapeDtypeStruct((b, t, LANES), F32)],
        compiler_params=_params("arbitrary", "arbitrary"),
    )(x, shift, scale, g.reshape(1, 1, d), wq, wcmp, wkvt, wg, bd, gq, gs_col, gw_col, bg)


def _block_diag_rows(q, n_heads):
    q = q.astype(F32)
    head = _iota(q.shape, 1) // HEAD_DIM
    return jnp.concatenate([jnp.where(head == h, q, 0.0) for h in range(n_heads)], axis=0).astype(BF16)


def _gather_diag(acc, tq, n_heads):
    head = _iota((tq, acc.shape[1]), 1) // HEAD_DIM
    out = jnp.zeros((tq, acc.shape[1]), F32)
    for h in range(n_heads):
        out = out + jnp.where(head == h, acc[h * tq:(h + 1) * tq, :], 0.0)
    return out


def _row_query(shape, tq):
    return _iota(shape, 0) % tq


def _running_sum_matrix(tk, prefix):
    j = _iota((tk, 2 * tk), 0)
    s = _iota((tk, 2 * tk), 1)
    tri = (j <= s) if prefix else (j > s)
    return jnp.where((s >= tk) | tri, 1.0, 0.0).astype(BF16)


def _later_matrix(tk):
    return jnp.where(_iota((tk, tk), 0) > _iota((tk, tk), 1), 1.0, 0.0).astype(BF16)


def _across_lane_tiles(x, op):
    out = x[:, 0:LANES]
    for c in range(1, x.shape[1] // LANES):
        out = op(out, x[:, c * LANES:(c + 1) * LANES])
    return out


def _with_row_vector(x, v, op):
    vb = jnp.broadcast_to(v, (x.shape[0], LANES))
    tiles = [op(x[:, c * LANES:(c + 1) * LANES], vb) for c in range(x.shape[1] // LANES)]
    return tiles[0] if len(tiles) == 1 else jnp.concatenate(tiles, axis=1)


def _exp_minus(s, m):
    return jnp.exp(_with_row_vector(s, m, jnp.subtract))


def _row_max(x):
    return jnp.max(_across_lane_tiles(x, jnp.maximum), axis=1, keepdims=True)


def _row_sum(x):
    return jnp.sum(_across_lane_tiles(x, jnp.add), axis=1, keepdims=True)


def _log_keep(z, mask):
    lk = -(jnp.maximum(z, 0.0) + jnp.log(1.0 + jnp.exp(-jnp.abs(z))))
    return lk if mask is None else jnp.where(mask, lk, 0.0)


def _sb_weights(z, um, carry, mask):
    lk = _log_keep(z, mask)
    blk = um.shape[0]
    nb = z.shape[1] // blk
    rest = [None] * nb
    for b in reversed(range(nb)):
        lkb = lk[:, b * blk:(b + 1) * blk]
        rest[b] = _with_row_vector(_dot_x2(lkb, um), carry, jnp.add)
        carry = carry + _row_sum(lkb)
    w = jnp.exp(z + lk + (rest[0] if nb == 1 else jnp.concatenate(rest, axis=1)))
    if mask is not None:
        w = jnp.where(mask, w, 0.0)
    return w.astype(BF16), carry


def _softmax_first(s, pv, m_ref, l_ref, acc_ref):
    m = _row_max(s)
    p = _exp_minus(s, m)
    l_ref[...] = _row_sum(p)
    acc_ref[...] = pv(p.astype(BF16))
    m_ref[...] = m


def _softmax_step(s, pv, m_ref, l_ref, acc_ref, mask=None):
    m_prev = m_ref[...]
    m_new = jnp.maximum(m_prev, _row_max(s))
    p = _exp_minus(s, m_new)
    if mask is not None:
        p = jnp.where(mask, p, 0.0)
    alpha = jnp.exp(m_prev - m_new)
    l_ref[...] = alpha * l_ref[...] + _row_sum(p)
    acc_ref[...] = alpha * acc_ref[...] + pv(p.astype(BF16))
    m_ref[...] = m_new


def _softmax_chains(scores, vt, refs, first):
    if first:
        ms = [_row_max(s) for s in scores]
    else:
        prev = [(m[...], a[...]) for m, a in refs]
        ms = [jnp.maximum(pm, _row_max(s)) for (pm, _), s in zip(prev, scores)]
    ps = [_exp_minus(s, m) for s, m in zip(scores, ms)]
    pvs = [_dot_nt(p.astype(BF16), vt) for p in ps]
    for i, (m_ref, acc_ref) in enumerate(refs):
        acc_ref[...] = pvs[i] if first else jnp.exp(prev[i][0] - ms[i]) * prev[i][1] + pvs[i]
        m_ref[...] = ms[i]


def _softmax_single_chains(scores, vt):
    ps = [_exp_minus(s, _row_max(s)) for s in scores]
    return [_dot_nt(p.astype(BF16), vt) for p in ps]


def _with_sum_row(vt):
    extra = jnp.where(_iota((SUM_ROWS, vt.shape[1]), 0) == 0, 1.0, 0.0).astype(BF16)
    return jnp.concatenate([vt, extra], axis=0)


def _normalized(acc):
    return acc[:, 0:HEAD_DIM] / jnp.maximum(acc[:, HEAD_DIM:HEAD_DIM + 1], 1e-30)


def _softmax_init(m_ref, l_ref, acc_ref):
    m_ref[...] = jnp.full(m_ref.shape, NEG, F32)
    l_ref[...] = jnp.zeros(l_ref.shape, F32)
    acc_ref[...] = jnp.zeros(acc_ref.shape, F32)


def _softmax_out(l_ref, acc_ref):
    return acc_ref[...] / jnp.maximum(l_ref[...], 1e-30)


def _softmax_scratch(r, width):
    return [pltpu.VMEM((r, 1), F32), pltpu.VMEM((r, 1), F32), pltpu.VMEM((r, width), F32)]


def _rank_select(imp, t, ns):
    blk = _iota(imp.shape, 1)
    cur = t // SLC_BLOCK
    valid = (blk * SLC_BLOCK <= t) & (blk < ns)
    forced = (blk == 0) | (blk == cur) | (blk == cur - 1)
    score = jnp.where(valid, jnp.where(forced, FORCE_SCORE, imp), -jnp.inf)
    cnt = jnp.zeros(imp.shape, F32)
    for j in range(ns):
        col = score[:, j:j + 1]
        cnt = cnt + jnp.where(col > score, 1.0, 0.0) + jnp.where(col == score, jnp.where(blk > j, 1.0, 0.0), 0.0)
    return jnp.where(valid & (cnt < N_SELECT), 1.0, 0.0)


def _rank_select_t(imp, q0, ns):
    rows = -(-ns // 8) * 8
    imp_t = imp.T[0:rows, :]
    blk = _iota(imp_t.shape, 0)
    t = q0 + _iota(imp_t.shape, 1)
    cur = t // SLC_BLOCK
    valid = (blk * SLC_BLOCK <= t) & (blk < ns)
    forced = (blk == 0) | (blk == cur) | (blk == cur - 1)
    score = jnp.where(valid, jnp.where(forced, FORCE_SCORE, imp_t), -jnp.inf)
    cnt = jnp.zeros(imp_t.shape, F32)
    for j in range(ns):
        row = score[j:j + 1, :]
        cnt = cnt + jnp.where(row > score, 1.0, 0.0) + jnp.where(row == score, jnp.where(blk > j, 1.0, 0.0), 0.0)
    sel_t = jnp.where(valid & (cnt < N_SELECT), 1.0, 0.0)
    return jnp.concatenate([sel_t, jnp.zeros((imp.shape[1] - rows, imp.shape[0]), F32)], axis=0).T


def _expand_blocks(sel, n_keys):
    nsp = sel.shape[1]
    blk = _iota((nsp, n_keys), 0)
    key = _iota((nsp, n_keys), 1)
    return _dot(sel, jnp.where(key // SLC_BLOCK == blk, 1.0, 0.0).astype(BF16))


def _sb_prompt_kernel(q_ref, kt_ref, vt_ref, o_ref, ktb, vtb, acc, car, *, tq, nh):
    qi = pl.program_id(2)

    @pl.when(qi == 0)
    def _():
        ktb[...] = kt_ref[0].astype(BF16)
        vtb[...] = vt_ref[0].astype(BF16)

    um = _later_matrix(tq)
    q0 = pl.multiple_of(qi * tq, tq)
    mask = _iota((tq, tq), 1) < _iota((tq, tq), 0)

    qs = [q_ref[0, h] for h in range(nh)]

    def tile(k0, first):
        kt = ktb[:, pl.ds(k0, tq)]
        vt = vtb[:, pl.ds(k0, tq)]
        hrows = lambda a, h: a[h * HEAD_DIM:(h + 1) * HEAD_DIM]
        prev = [(jnp.zeros((tq, 1), F32), None) if first else (car[h], acc[h]) for h in range(nh)]
        zs = [_dot(qs[h], hrows(kt, h)) for h in range(nh)]
        wc = [_sb_weights(zs[h], um, prev[h][0], mask if first else None) for h in range(nh)]
        pvs = [_dot_nt(wc[h][0], hrows(vt, h)) for h in range(nh)]
        for h in range(nh):
            acc[h] = pvs[h] if first else prev[h][1] + pvs[h]
            car[h] = wc[h][1]
        live = wc[0][1]
        for h in range(1, nh):
            live = jnp.maximum(live, wc[h][1])
        return jnp.max(live) > DEAD_LOG

    alive = tile(q0, True)

    def body(state):
        it, _ = state
        return it + 1, tile(pl.multiple_of((qi - 1 - it) * tq, tq), False)

    lax.while_loop(lambda state: (state[0] < qi) & state[1], body, (jnp.int32(0), alive))
    o_ref[0] = jnp.concatenate([acc[h] for h in range(nh)], axis=1).astype(BF16)


def _sb_prompt(q, kvt, tq, nh):
    b, n_h, s, _ = q.shape
    c = nh * HEAD_DIM
    ng = n_h // nh
    return pl.pallas_call(
        functools.partial(_sb_prompt_kernel, tq=tq, nh=nh), name="sb_prompt",
        grid=(b, ng, s // tq),
        in_specs=[pl.BlockSpec((1, nh, tq, HEAD_DIM), lambda i, g, j: (i, g, j, 0)),
                  pl.BlockSpec((1, c, s), lambda i, g, j: (i, g, 0)),
                  pl.BlockSpec((1, c, s), lambda i, g, j: (i, ng + g, 0))],
        out_specs=pl.BlockSpec((1, tq, c), lambda i, g, j: (i, j, g)),
        out_shape=jax.ShapeDtypeStruct((b, s, n_h * HEAD_DIM), BF16),
        scratch_shapes=[pltpu.VMEM((c, s), BF16), pltpu.VMEM((c, s), BF16),
                        pltpu.VMEM((nh, tq, HEAD_DIM), F32), pltpu.VMEM((nh, tq, 1), F32)],
        compiler_params=_params("arbitrary", "arbitrary", "arbitrary"),
    )(q, kvt, kvt)


def _fox_prompt_kernel(q_ref, kt_ref, vt_ref, lft_ref, o_ref, ktb, vtb, ncum, knorm, m_ref, l_ref, acc, *, tq, nh):
    g = pl.program_id(1)
    qi = pl.program_id(2)
    s_len = kt_ref.shape[2]

    @pl.when(qi == 0)
    def _():
        kb = kt_ref[0].astype(BF16)
        ktb[...] = kb
        vtb[...] = vt_ref[0].astype(BF16)
        kf = kb.astype(F32)
        for h in range(nh):
            kh = kf[h * HEAD_DIM:(h + 1) * HEAD_DIM]
            knorm[h] = jnp.full(knorm.shape[1:], jnp.sqrt(jnp.max(jnp.sum(kh * kh, axis=0, keepdims=True))), F32)
        um = _running_sum_matrix(tq, True)
        run = jnp.zeros((lft_ref.shape[1], tq), F32)
        for ch in range(s_len // tq):
            sl = slice(ch * tq, (ch + 1) * tq)
            rs = _dot_x3(lft_ref[0, :, sl], um)
            ncum[:, sl] = -(rs[:, :tq] + run)
            run = run + rs[:, tq:]

    q0 = pl.multiple_of(qi * tq, tq)
    mask = _iota((tq, tq), 1) <= _iota((tq, tq), 0)

    qs = [q_ref[0, h] for h in range(nh)]

    def tile(k0, first):
        kt = ktb[:, pl.ds(k0, tq)]
        vt = vtb[:, pl.ds(k0, tq)]
        bias = [ncum[pl.ds(g * nh + h, 1), pl.ds(k0, tq)] for h in range(nh)]
        hrows = lambda a, h: a[h * HEAD_DIM:(h + 1) * HEAD_DIM]
        if not first:
            prev = [(m_ref[h], l_ref[h], acc[h]) for h in range(nh)]
        ss = [_dot(qs[h], hrows(kt, h)) + bias[h] for h in range(nh)]
        if first:
            ss = [jnp.where(mask, s, NEG) for s in ss]
            ms = [_row_max(s) for s in ss]
        else:
            ms = [jnp.maximum(prev[h][0], _row_max(ss[h])) for h in range(nh)]
        ps = [_exp_minus(s, m) for s, m in zip(ss, ms)]
        pvs = [_dot_nt(ps[h].astype(BF16), hrows(vt, h)) for h in range(nh)]
        sums = [_row_sum(p) for p in ps]
        for h in range(nh):
            if first:
                l_ref[h] = sums[h]
                acc[h] = pvs[h]
            else:
                alpha = jnp.exp(prev[h][0] - ms[h])
                l_ref[h] = alpha * prev[h][1] + sums[h]
                acc[h] = alpha * prev[h][2] + pvs[h]
            m_ref[h] = ms[h]
        live = None
        for h in range(nh):
            bound = qn[h] * knorm[h][0:1, 0:1] + (jnp.min(bias[h], axis=1, keepdims=True) - ms[h])
            live = bound if live is None else jnp.maximum(live, bound)
        return jnp.max(live) > DEAD_LOG

    qn = [jnp.sqrt(jnp.sum(jnp.square(q.astype(F32)), axis=1, keepdims=True)) * 1.001 for q in qs]
    alive = tile(q0, True)

    def body(state):
        it, _ = state
        return it + 1, tile(pl.multiple_of((qi - 1 - it) * tq, tq), False)

    lax.while_loop(lambda state: (state[0] < qi) & state[1], body, (jnp.int32(0), alive))
    o_ref[0] = jnp.concatenate([_softmax_out(l_ref.at[h], acc.at[h]) for h in range(nh)], axis=1).astype(BF16)


def _fox_prompt(q, kvt, lft, tq, nh):
    b, n_h, s, _ = q.shape
    c = nh * HEAD_DIM
    ng = n_h // nh
    n_f = lft.shape[1]
    return pl.pallas_call(
        functools.partial(_fox_prompt_kernel, tq=tq, nh=nh), name="fox_prompt",
        grid=(b, ng, s // tq),
        in_specs=[pl.BlockSpec((1, nh, tq, HEAD_DIM), lambda i, g, j: (i, g, j, 0)),
                  pl.BlockSpec((1, c, s), lambda i, g, j: (i, g, 0)),
                  pl.BlockSpec((1, c, s), lambda i, g, j: (i, ng + g, 0)),
                  pl.BlockSpec((1, n_f, s), lambda i, g, j: (i, 0, 0))],
        out_specs=pl.BlockSpec((1, tq, c), lambda i, g, j: (i, j, g)),
        out_shape=jax.ShapeDtypeStruct((b, s, n_h * HEAD_DIM), BF16),
        scratch_shapes=[pltpu.VMEM((c, s), BF16), pltpu.VMEM((c, s), BF16), pltpu.VMEM((n_f, s), F32),
                        pltpu.VMEM((nh, 8, LANES), F32), pltpu.VMEM((nh, tq, 1), F32), pltpu.VMEM((nh, tq, 1), F32), pltpu.VMEM((nh, tq, HEAD_DIM), F32)],
        compiler_params=_params("arbitrary", "arbitrary", "arbitrary"),
    )(q, kvt, kvt, lft)


def _page_specs(n_pages, rows, npg):
    return [pl.BlockSpec((1, rows, PAGE), functools.partial(
        lambda i, s, pt, off: (pt[i, n_pages - 1 - s * npg - off], 0, 0), off=off)) for off in range(npg)]


def _sb_decode_kernel(pt_ref, q_ref, new_ref, *rest, tq, hg, npg):
    page_refs = rest[:npg]
    o_ref, qbd_ref, newp, acc, car = rest[npg:]
    st = pl.program_id(1)
    da = hg * HEAD_DIM
    r = hg * tq

    @pl.when(st == 0)
    def _():
        qbd_ref[...] = _block_diag_rows(q_ref[0], hg)
        newp[...] = jnp.zeros(newp.shape, F32)
        newp[0:tq, :] = new_ref[0]
        mask = _iota((r, PAGE), 1) < _row_query((r, PAGE), tq)
        w, c = _sb_weights(_dot_nt(qbd_ref[...], newp[:, 0:da].astype(BF16)), _later_matrix(PAGE),
                           jnp.zeros((r, 1), F32), mask)
        acc[...] = _dot(w, newp[:, da:2 * da].astype(BF16))
        car[...] = c

    kt = jnp.concatenate([pg[0, 0:da, :] for pg in page_refs[::-1]], axis=1).astype(BF16)
    vt = jnp.concatenate([pg[0, da:2 * da, :] for pg in page_refs[::-1]], axis=1).astype(BF16)
    w, c = _sb_weights(_dot(qbd_ref[...], kt), _later_matrix(2 * PAGE), car[...], None)
    acc[...] += _dot_nt(w, vt)
    car[...] = c

    @pl.when(st == pl.num_programs(1) - 1)
    def _():
        o_ref[0] = _gather_diag(acc[...], tq, hg).astype(BF16)


def _sb_decode(q, kv_new, cache_t, page_table, npg):
    b, tq, da = q.shape
    hg = da // HEAD_DIM
    n_pages = page_table.shape[1]
    r = hg * tq
    grid_spec = pltpu.PrefetchScalarGridSpec(
        num_scalar_prefetch=1,
        grid=(b, n_pages // npg),
        in_specs=[pl.BlockSpec((1, tq, da), lambda i, s, pt: (i, 0, 0)),
                  pl.BlockSpec((1, tq, 2 * da), lambda i, s, pt: (i, 0, 0))] + _page_specs(n_pages, 2 * da, npg),
        out_specs=pl.BlockSpec((1, tq, da), lambda i, s, pt: (i, 0, 0)),
        scratch_shapes=[pltpu.VMEM((r, da), BF16), pltpu.VMEM((PAGE, 2 * da), F32),
                        pltpu.VMEM((r, da), F32), pltpu.VMEM((r, 1), F32)],
    )
    return pl.pallas_call(
        functools.partial(_sb_decode_kernel, tq=tq, hg=hg, npg=npg), name="sb_decode",
        grid_spec=grid_spec,
        out_shape=jax.ShapeDtypeStruct((b, tq, da), BF16),
        compiler_params=_params("arbitrary", "arbitrary"),
    )(page_table, q, kv_new, *([cache_t] * npg))


def _fox_decode_kernel(pt_ref, q_ref, new_ref, lfnew_ref, *rest, tq, hg, npg):
    page_refs = rest[:npg]
    lf_refs = rest[npg:2 * npg]
    o_ref, qbd_ref, newp, lfp, run_ref, m_ref, l_ref, acc = rest[2 * npg:]
    st = pl.program_id(1)
    da = hg * HEAD_DIM
    r = hg * tq
    um = _running_sum_matrix(PAGE, False)

    def key_bias(lfts):
        rs = [_dot_x3(lft, um) for lft in lfts]
        run = run_ref[...]
        suf = [None] * len(lfts)
        for b in reversed(range(len(lfts))):
            suf[b] = rs[b][:, :PAGE] + run
            run = run + rs[b][:, PAGE:]
        run_ref[...] = run
        suf = suf[0] if len(suf) == 1 else jnp.concatenate(suf, axis=1)
        return jnp.concatenate([jnp.broadcast_to(suf[h:h + 1, :], (tq, suf.shape[1])) for h in range(hg)], axis=0)

    @pl.when(st == 0)
    def _():
        qbd_ref[...] = _block_diag_rows(q_ref[0], hg)
        newp[...] = jnp.zeros(newp.shape, F32)
        newp[0:tq, :] = new_ref[0]
        lfp[...] = jnp.zeros(lfp.shape, F32)
        lfp[0:tq, :] = lfnew_ref[0]
        run_ref[...] = jnp.zeros(run_ref.shape, F32)
        mask = _iota((r, PAGE), 1) <= _row_query((r, PAGE), tq)
        s = _dot_nt(qbd_ref[...], newp[:, 0:da].astype(BF16)) + key_bias([lfp[...].T[0:hg, :]])
        _softmax_first(jnp.where(mask, s, NEG), lambda p: _dot(p, newp[:, da:2 * da].astype(BF16)), m_ref, l_ref, acc)

    kt = jnp.concatenate([pg[0, 0:da, :] for pg in page_refs[::-1]], axis=1).astype(BF16)
    vt = jnp.concatenate([pg[0, da:2 * da, :] for pg in page_refs[::-1]], axis=1).astype(BF16)
    s = _dot(qbd_ref[...], kt) + key_bias([lf[0] for lf in lf_refs[::-1]])
    _softmax_step(s, lambda p: _dot_nt(p, vt), m_ref, l_ref, acc)

    @pl.when(st == pl.num_programs(1) - 1)
    def _():
        o_ref[0] = _gather_diag(_softmax_out(l_ref, acc), tq, hg).astype(BF16)


def _fox_decode(q, kv_new, lf_new, cache_t, cache_lf_t, page_table, npg):
    b, tq, da = q.shape
    hg = da // HEAD_DIM
    n_pages = page_table.shape[1]
    r = hg * tq
    assert cache_lf_t.shape[1] == hg
    grid_spec = pltpu.PrefetchScalarGridSpec(
        num_scalar_prefetch=1,
        grid=(b, n_pages // npg),
        in_specs=[pl.BlockSpec((1, tq, da), lambda i, s, pt: (i, 0, 0)),
                  pl.BlockSpec((1, tq, 2 * da), lambda i, s, pt: (i, 0, 0)),
                  pl.BlockSpec((1, tq, LANES), lambda i, s, pt: (i, 0, 0))]
        + _page_specs(n_pages, 2 * da, npg) + _page_specs(n_pages, hg, npg),
        out_specs=pl.BlockSpec((1, tq, da), lambda i, s, pt: (i, 0, 0)),
        scratch_shapes=[pltpu.VMEM((r, da), BF16), pltpu.VMEM((PAGE, 2 * da), F32),
                        pltpu.VMEM((PAGE, LANES), F32), pltpu.VMEM((hg, PAGE), F32)] + _softmax_scratch(r, da),
    )
    return pl.pallas_call(
        functools.partial(_fox_decode_kernel, tq=tq, hg=hg, npg=npg), name="fox_decode",
        grid_spec=grid_spec,
        out_shape=jax.ShapeDtypeStruct((b, tq, da), BF16),
        compiler_params=_params("arbitrary", "arbitrary"),
    )(page_table, q, kv_new, lf_new, *([cache_t] * npg), *([cache_lf_t] * npg))


def _compress(xb, bd_ref, pe_ref, n_chunk, dkv, pitch=None):
    first = [jnp.zeros((n_chunk, dkv), F32) for _ in range(2)]
    second = [jnp.zeros((n_chunk, dkv), F32) for _ in range(2)]
    for j in range(D_CMP):
        rows = pl.ds(j, n_chunk, stride=D_CMP) if pitch is None else pl.ds(j * pitch, n_chunk)
        xj = jnp.concatenate([xb[lb, rows, :] for lb in range(xb.shape[0])], axis=1)
        for kv in range(2):
            xx = xj[:, kv * dkv:(kv + 1) * dkv]
            first[kv] += _dot((xx + pe_ref[kv, j:j + 1, :]).astype(BF16), bd_ref[kv, j])
            second[kv] += _dot((xx + pe_ref[kv, D_CMP + j:D_CMP + j + 1, :]).astype(BF16), bd_ref[kv, D_CMP + j])
    return [first[kv] + pltpu.roll(second[kv], n_chunk - 1, axis=0) for kv in range(2)]


def _compress_prompt_kernel(x_ref, bd_ref, pe_ref, bdn_ref, gk_ref, kct_ref, vct_ref, xb):
    n_chunk = x_ref.shape[1] // D_CMP
    dkv = bdn_ref.shape[0]
    for lb in range(xb.shape[0]):
        xb[lb] = x_ref[0, :, lb * LANES:(lb + 1) * LANES]
    kc, vc = _compress(xb, bd_ref, pe_ref, n_chunk, dkv)
    kct_ref[0] = (_head_norm(kc, bdn_ref[...]) * gk_ref[...]).T.astype(BF16)
    vct_ref[0] = vc.T.astype(BF16)


def _compress_prompt(cmp_kv, bd, pe, bdn, gk):
    b, s, w = cmp_kv.shape
    dkv = w // 2
    n_chunk = s // D_CMP
    const = lambda a: pl.BlockSpec(a.shape, lambda i: (0,) * a.ndim)
    ospec = pl.BlockSpec((1, dkv, n_chunk), lambda i: (i, 0, 0))
    return pl.pallas_call(
        _compress_prompt_kernel, name="compress_prompt",
        grid=(b,),
        in_specs=[pl.BlockSpec((1, s, w), lambda i: (i, 0, 0)), const(bd), const(pe), const(bdn), const(gk)],
        out_specs=[ospec, ospec],
        out_shape=[jax.ShapeDtypeStruct((b, dkv, n_chunk), BF16)] * 2,
        scratch_shapes=[pltpu.VMEM((w // LANES, s, LANES), F32)],
        compiler_params=_params("arbitrary"),
    )(cmp_kv, bd, pe, bdn, gk)


def _nsa_prompt_kernel(q_ref, kst_ref, vst_ref, kwt_ref, vwt_ref, kct_ref, vct_ref, gates_ref, band_ref, near_ref,
                       wadd_ref, ov_ref, o_ref, ksb, vsb, kwb, vwb, madd, m_ref, acc, ocmp, owin, *, tq, nz, ns):
    g = pl.program_id(1)
    qi = pl.program_id(2)
    s_len = kst_ref.shape[2]
    ncp = kct_ref.shape[2]
    wslab = WINDOW + tq

    @pl.when(qi == 0)
    def _():
        ksb[...] = kst_ref[0].astype(BF16)
        vsb[...] = _with_sum_row(vst_ref[0].astype(BF16))
        kwb[:, 0:WINDOW] = jnp.zeros((HEAD_DIM, WINDOW), BF16)
        vwb[:, 0:WINDOW] = jnp.zeros((HEAD_DIM + SUM_ROWS, WINDOW), BF16)
        kwb[:, WINDOW:] = kwt_ref[0].astype(BF16)
        vwb[:, WINDOW:] = _with_sum_row(vwt_ref[0].astype(BF16))

    q0 = pl.multiple_of(qi * tq, tq)
    zrows = lambda z: slice(z * tq, (z + 1) * tq)

    shift = (qi * (tq // D_CMP) + ncp - BAND_BACK) % ncp
    t_c = q0 + _iota((tq, ncp), 0)
    c_c = _iota((tq, ncp), 1)
    cmask = (t_c >= c_c * D_CMP + (L_CMP - 1)) & (c_c < ncp - 1)
    qs = [q_ref[0, z] for z in range(nz)]
    kct = kct_ref[0]
    vct = vct_ref[0]
    ss = [jnp.where(cmask, _dot(qs[z], kct) + pltpu.roll(band_ref[z], shift, axis=1), NEG) for z in range(nz)]
    ps = [jnp.where(cmask, _exp_minus(s, _row_max(s)), 0.0) for s in ss]
    ps = [_with_row_vector(p, jnp.maximum(_row_sum(p), 1e-30), jnp.divide) for p in ps]
    oc = [_dot_nt(p.astype(BF16), vct) for p in ps]
    pz = ps[0]
    for z in range(1, nz):
        pz = pz + ps[z]
    imp = _dot_x2(pz, ov_ref[...])

    kslab = kwb[:, pl.ds(q0, wslab)]
    vslab = vwb[:, pl.ds(q0, wslab)]
    wadd = wadd_ref[...] + jnp.where(_iota((tq, wslab), 1) >= WINDOW - q0, 0.0, NEG)
    ws = [_dot(qs[z], kslab) + wadd for z in range(nz)]
    ws = [jnp.concatenate([s[:, :wslab - 2 * tq], s[:, wslab - 2 * tq:] + near_ref[z]], axis=1)
          for z, s in enumerate(ws)]
    ow = _softmax_single_chains(ws, vslab)

    sel = _rank_select_t(imp, q0, ns)
    for z in range(nz):
        ocmp[zrows(z), :] = oc[z]
        owin[zrows(z), :] = _normalized(ow[z])
    madd[...] = (_expand_blocks(sel.astype(BF16), s_len) - 1.0) * (-NEG)
    refs = [(m_ref.at[zrows(z)], acc.at[zrows(z)]) for z in range(nz)]

    def slc_tile(k0, tk, extra, first):
        kt = ksb[:, pl.ds(k0, tk)]
        vt = vsb[:, pl.ds(k0, tk)]
        ma = madd[:, pl.ds(k0, tk)]
        scores = [_dot(qs[z], kt) + (ma if extra is None else ma + extra(z)) for z in range(nz)]
        _softmax_chains(scores, vt, refs, first)

    causal = jnp.where(_iota((tq, tq), 1) <= _iota((tq, tq), 0), 0.0, NEG)

    @pl.when(qi == 0)
    def _():
        slc_tile(0, tq, lambda z: near_ref[z, :, tq:2 * tq] + causal, True)

    @pl.when(qi >= 1)
    def _():
        edge = jnp.concatenate([jnp.zeros((tq, tq), F32), causal], axis=1)
        slc_tile(pl.multiple_of(q0 - tq, tq), 2 * tq, lambda z: near_ref[z] + edge, True)

    n_far = jnp.maximum(qi - 1, 0)
    per_far = FAR_TILE // tq

    def far_body(it, _):
        slc_tile(pl.multiple_of(it * FAR_TILE, FAR_TILE), FAR_TILE, None, False)
        return 0

    lax.fori_loop(0, n_far // per_far, far_body, 0)

    def rem_body(it, _):
        slc_tile(pl.multiple_of(((n_far // per_far) * per_far + it) * tq, tq), tq, None, False)
        return 0

    lax.fori_loop(0, n_far % per_far, rem_body, 0)

    n_heads = nz * pl.num_programs(1)
    src = _iota((LANES, LANES), 0)
    dst = _iota((LANES, LANES), 1)
    pick = jnp.where((dst < 3 * nz) & (src == (dst // nz) * n_heads + g * nz + dst % nz), 1.0, 0.0).astype(BF16)
    gsel = _dot_x3(gates_ref[0], pick)
    outs = []
    for z in range(nz):
        gate = lambda br: gsel[:, br * nz + z:br * nz + z + 1]
        outs.append(gate(0) * ocmp[zrows(z), :] + gate(1) * _normalized(acc[zrows(z), :]) + gate(2) * owin[zrows(z), :])
    o_ref[0] = jnp.concatenate(outs, axis=1).astype(BF16)


def _window_mask(tq):
    i = np.arange(tq)[:, None]
    c = np.arange(WINDOW + tq)[None, :]
    return jnp.asarray(np.where((c > i) & (c <= i + WINDOW), 0.0, NEG).astype(np.float32))


def _nsa_prompt(q, slct, wint, kct, vct, gates, band, near, tq, n_kv):
    b, n_heads, s, _ = q.shape
    nz = n_heads // n_kv
    r = nz * tq
    ncp = kct.shape[2]
    ns = -(-s // SLC_BLOCK)
    nsp = -(-ns // LANES) * LANES
    ov = _overlap_matrix(ncp, nsp)
    wadd = _window_mask(tq)
    kv_spec = lambda off: pl.BlockSpec((1, HEAD_DIM, s), lambda i, g, j: (i, off + g, 0))
    c_spec = pl.BlockSpec((1, HEAD_DIM, ncp), lambda i, g, j: (i, g, 0))
    tab_spec = lambda a: pl.BlockSpec((nz,) + a.shape[1:], lambda i, g, j: (g, 0, 0))
    return pl.pallas_call(
        functools.partial(_nsa_prompt_kernel, tq=tq, nz=nz, ns=ns), name="nsa_prompt",
        grid=(b, n_kv, s // tq),
        in_specs=[pl.BlockSpec((1, nz, tq, HEAD_DIM), lambda i, g, j: (i, g, j, 0)),
                  kv_spec(0), kv_spec(n_kv), kv_spec(0), kv_spec(n_kv), c_spec, c_spec,
                  pl.BlockSpec((1, tq, LANES), lambda i, g, j: (i, j, 0)),
                  tab_spec(band), tab_spec(near),
                  pl.BlockSpec(wadd.shape, lambda i, g, j: (0, 0)),
                  pl.BlockSpec(ov.shape, lambda i, g, j: (0, 0))],
        out_specs=pl.BlockSpec((1, tq, nz * HEAD_DIM), lambda i, g, j: (i, j, g)),
        out_shape=jax.ShapeDtypeStruct((b, s, n_heads * HEAD_DIM), BF16),
        scratch_shapes=[pltpu.VMEM((HEAD_DIM, s), BF16), pltpu.VMEM((HEAD_DIM + SUM_ROWS, s), BF16),
                        pltpu.VMEM((HEAD_DIM, WINDOW + s), BF16), pltpu.VMEM((HEAD_DIM + SUM_ROWS, WINDOW + s), BF16),
                        pltpu.VMEM((tq, s), F32), pltpu.VMEM((r, 1), F32), pltpu.VMEM((r, HEAD_DIM + SUM_ROWS), F32),
                        pltpu.VMEM((r, HEAD_DIM), F32), pltpu.VMEM((r, HEAD_DIM), F32)],
        compiler_params=_params("arbitrary", "arbitrary", "arbitrary"),
    )(q, slct, slct, wint, wint, kct, vct, gates, band, near, wadd, ov)


def _cmp_select(q, kcmp, vcmp, bias, ov, q0, tq, nc, ns, n_kv, n_rep):
    ncp = kcmp.shape[0]
    qbd = _nsa_qbd(q, n_kv, n_rep)
    r = qbd.shape[0]
    t = q0 + _row_query((r, ncp), tq)
    cidx = _iota((r, ncp), 1)
    mask = (t - (cidx * D_CMP + L_CMP - 1) >= 0) & (cidx < nc)
    s = jnp.where(mask, _dot_nt(qbd, kcmp) + bias, NEG)
    p = jnp.where(mask, jnp.exp(s - jnp.max(s, axis=1, keepdims=True)), 0.0)
    p = p / jnp.maximum(jnp.sum(p, axis=1, keepdims=True), 1e-30)
    o_cmp = _nsa_gather(_dot(p.astype(BF16), vcmp), tq, n_kv, n_rep)
    rz = n_kv * tq
    pz = p[0:rz]
    for z in range(1, n_rep):
        pz = pz + p[z * rz:(z + 1) * rz]
    imp = _dot_x2(pz, ov)
    return o_cmp, _rank_select(imp, q0 + _row_query(imp.shape, tq), ns)


def _overlap_matrix(ncp, nsp):
    ci = np.arange(ncp)[:, None]
    bj = np.arange(nsp)[None, :]
    ov = (ci * D_CMP <= bj * SLC_BLOCK + SLC_BLOCK - 1) & (ci * D_CMP + L_CMP - 1 >= bj * SLC_BLOCK)
    return jnp.asarray(ov.astype(np.float32), BF16)


def _cmp_decode_kernel(pt_ref, q_ref, bd_ref, pe_ref, bdn_ref, gk_ref, bias_ref, ov_ref, *rest,
                       tq, n_kv, n_rep, ns, npg, past):
    page_refs = rest[:npg]
    ocmp_ref, sel_ref, xbuf = rest[npg:]
    st = pl.program_id(1)
    n_steps = pl.num_programs(1)
    dkv = n_kv * HEAD_DIM
    n_chunk = past // D_CMP
    pitch = xbuf.shape[1] // D_CMP
    for off, pg in enumerate(page_refs):
        pos = (n_steps - 1 - st) * npg + (npg - 1 - off)
        rows = pg[0].T
        for c in range(PAGE // D_CMP):
            for lb in range(xbuf.shape[0]):
                xbuf[lb, pl.ds(pos * (PAGE // D_CMP) + c, D_CMP, stride=pitch), :] = (
                    rows[c * D_CMP:(c + 1) * D_CMP, lb * LANES:(lb + 1) * LANES])

    @pl.when(st == n_steps - 1)
    def _():
        kc, vc = _compress(xbuf, bd_ref, pe_ref, n_chunk, dkv, pitch)
        kcb = (_head_norm(kc, bdn_ref[...]) * gk_ref[...]).astype(BF16)
        bias = bias_ref[...].reshape(n_kv * n_rep * tq, n_chunk)
        o_cmp, sel = _cmp_select(q_ref[0], kcb, vc.astype(BF16), bias, ov_ref[...], past, tq,
                                 n_chunk - 1, ns, n_kv, n_rep)
        ocmp_ref[0] = o_cmp
        sel_ref[0] = sel.astype(BF16)


def _odd_pitch(n):
    p = -(-n // 8)
    return 8 * (p if p % 2 == 1 else p + 1)


def _cmp_decode(q, cache_t, page_table, bd, pe, bdn, gk, bias, n_kv, npg):
    b, tq, dq = q.shape
    dkv = n_kv * HEAD_DIM
    n_rep = dq // dkv
    n_pages = page_table.shape[1]
    past = n_pages * PAGE
    n_chunk = (past + tq) // D_CMP
    assert n_chunk == past // D_CMP
    ns = -(-(past + tq) // SLC_BLOCK)
    nsp = -(-ns // LANES) * LANES
    ov = _overlap_matrix(n_chunk, nsp)
    const = lambda a: pl.BlockSpec(a.shape, lambda i, s, pt: (0,) * a.ndim)
    grid_spec = pltpu.PrefetchScalarGridSpec(
        num_scalar_prefetch=1,
        grid=(b, n_pages // npg),
        in_specs=[pl.BlockSpec((1, tq, dq), lambda i, s, pt: (i, 0, 0)),
                  const(bd), const(pe), const(bdn), const(gk), const(bias), const(ov)]
        + _page_specs(n_pages, 2 * dkv, npg),
        out_specs=[pl.BlockSpec((1, tq, dq), lambda i, s, pt: (i, 0, 0)),
                   pl.BlockSpec((1, n_kv * tq, nsp), lambda i, s, pt: (i, 0, 0))],
        scratch_shapes=[pltpu.VMEM((2 * dkv // LANES, D_CMP * _odd_pitch(n_chunk), LANES), F32)],
    )
    return pl.pallas_call(
        functools.partial(_cmp_decode_kernel, tq=tq, n_kv=n_kv, n_rep=n_rep, ns=ns, npg=npg, past=past),
        name="cmp_decode",
        grid_spec=grid_spec,
        out_shape=[jax.ShapeDtypeStruct((b, tq, dq), F32), jax.ShapeDtypeStruct((b, n_kv * tq, nsp), BF16)],
        compiler_params=_params("arbitrary", "arbitrary"),
    )(page_table, q, bd, pe, bdn, gk, bias, ov, *([cache_t] * npg))


def _nsa_qbd(q, n_kv, n_rep):
    dkv = n_kv * HEAD_DIM
    return jnp.concatenate([_block_diag_rows(q[:, z * dkv:(z + 1) * dkv], n_kv) for z in range(n_rep)], axis=0)


def _nsa_gather(o_full, tq, n_kv, n_rep):
    rz = n_kv * tq
    return jnp.concatenate([_gather_diag(o_full[z * rz:(z + 1) * rz], tq, n_kv) for z in range(n_rep)], axis=1)


def _merge(gates, ege_ref, o_cmp, o_slc, o_win):
    g = [_dot_x2(gates, ege_ref[br]) for br in range(3)]
    return g[0] * o_cmp + g[1] * o_slc + g[2] * o_win


def _slc_decode_kernel(pt_ref, q_ref, slcnew_ref, winnew_ref, state_ref, sel_ref, gates_ref, ocmp_ref,
                       t0_ref, t1_ref, ege_ref, *rest, tq, n_kv, n_rep, npg, past):
    page_refs = rest[:npg]
    o_ref, qbd_ref, newp, kmask, m_ref, l_ref, acc = rest[npg:]
    st = pl.program_id(1)
    n_steps = pl.num_programs(1)
    dkv = n_kv * HEAD_DIM
    r = n_rep * n_kv * tq
    i_row = _row_query((r, PAGE), tq)
    j_col = _iota((r, PAGE), 1)

    def sel_mask(k0, width=PAGE):
        return jnp.concatenate([kmask[:, pl.ds(k0, width)]] * n_rep, axis=0) > 0.5

    def step(s, mask, pv):
        _softmax_step(jnp.where(mask, s, NEG) if mask is not None else s, pv, m_ref, l_ref, acc, mask)

    def new_tile(bias, mask):
        k = newp[:, 0:dkv].astype(BF16)
        v = newp[:, dkv:2 * dkv].astype(BF16)
        step(_dot_nt(qbd_ref[...], k) + bias, mask, lambda p: _dot(p, v))

    @pl.when(st == 0)
    def _():
        qbd_ref[...] = _nsa_qbd(q_ref[0], n_kv, n_rep)
        kmask[...] = _expand_blocks(sel_ref[0], kmask.shape[1])
        newp[...] = jnp.zeros(newp.shape, F32)
        newp[0:tq, :] = slcnew_ref[0]
        _softmax_init(m_ref, l_ref, acc)
        new_tile(t0_ref[...], sel_mask(past) & (j_col <= i_row))

    width = npg * PAGE
    k0 = pl.multiple_of((n_steps - 1 - st) * width, width)
    kt = jnp.concatenate([pg[0, 0:dkv, :] for pg in page_refs[::-1]], axis=1).astype(BF16)
    vt = jnp.concatenate([pg[0, dkv:2 * dkv, :] for pg in page_refs[::-1]], axis=1).astype(BF16)
    s = _dot(qbd_ref[...], kt)
    near = s[:, width - PAGE:] + jnp.where(st == 0, 1.0, 0.0) * t1_ref[...]
    s = jnp.concatenate([s[:, :width - PAGE], near], axis=1)
    step(s, sel_mask(k0, width), lambda p: _dot_nt(p, vt))

    @pl.when(st == n_steps - 1)
    def _():
        o_slc = _nsa_gather(_softmax_out(l_ref, acc), tq, n_kv, n_rep)
        _softmax_init(m_ref, l_ref, acc)
        newp[0:tq, :] = winnew_ref[0]
        new_tile(t0_ref[...], j_col <= i_row)
        n_back = WINDOW // PAGE
        for back in range(1, n_back + 1):
            sl = slice((n_back - back) * PAGE, (n_back - back + 1) * PAGE)
            s = _dot(qbd_ref[...], state_ref[0, 0:dkv, sl].astype(BF16))
            if back == 1:
                s = s + t1_ref[...]
            mask = (j_col > i_row) if back == n_back else None
            step(s, mask, lambda p, sl=sl: _dot_nt(p, state_ref[0, dkv:2 * dkv, sl].astype(BF16)))
        o_win = _nsa_gather(_softmax_out(l_ref, acc), tq, n_kv, n_rep)
        o_ref[0] = _merge(gates_ref[0], ege_ref, ocmp_ref[0], o_slc, o_win).astype(BF16)


def _slc_decode(q, slc_new, win_new, state_t, sel, gates, o_cmp, cache_t, page_table, t0, t1, ege, n_kv, npg):
    b, tq, dq = q.shape
    dkv = n_kv * HEAD_DIM
    n_rep = dq // dkv
    r = n_rep * n_kv * tq
    n_pages = page_table.shape[1]
    past = n_pages * PAGE
    const = lambda a: pl.BlockSpec(a.shape, lambda i, s, pt: (0,) * a.ndim)
    per_seq = lambda a: pl.BlockSpec((1,) + a.shape[1:], lambda i, s, pt: (i,) + (0,) * (a.ndim - 1))
    grid_spec = pltpu.PrefetchScalarGridSpec(
        num_scalar_prefetch=1,
        grid=(b, n_pages // npg),
        in_specs=[per_seq(q), per_seq(slc_new), per_seq(win_new), per_seq(state_t), per_seq(sel), per_seq(gates),
                  per_seq(o_cmp), const(t0), const(t1), const(ege)] + _page_specs(n_pages, 2 * dkv, npg),
        out_specs=pl.BlockSpec((1, tq, dq), lambda i, s, pt: (i, 0, 0)),
        scratch_shapes=[pltpu.VMEM((r, dkv), BF16), pltpu.VMEM((PAGE, 2 * dkv), F32),
                        pltpu.VMEM((n_kv * tq, past + PAGE), F32)] + _softmax_scratch(r, dkv),
    )
    return pl.pallas_call(
        functools.partial(_slc_decode_kernel, tq=tq, n_kv=n_kv, n_rep=n_rep, npg=npg, past=past),
        name="slc_decode",
        grid_spec=grid_spec,
        out_shape=jax.ShapeDtypeStruct((b, tq, dq), BF16),
        compiler_params=_params("arbitrary", "arbitrary"),
    )(page_table, q, slc_new, win_new, state_t, sel, gates, o_cmp, t0, t1, ege, *([cache_t] * npg))


def _rel_buckets(dist):
    n = np.maximum(dist, 0)
    exact = N_BUCKETS // 2
    nf = np.maximum(n, 1).astype(np.float64)
    large = exact + (np.log(nf / exact) / math.log(REL_MAX_DIST / exact) * (N_BUCKETS - exact)).astype(np.int64)
    return np.where(n < exact, n, np.minimum(large, N_BUCKETS - 1)).astype(np.int32)


def _head_block_diag(width, scale):
    h = np.arange(width) // HEAD_DIM
    return jnp.asarray((h[:, None] == h[None, :]).astype(np.float32) * scale, BF16)


class _NsaLayout:
    def __init__(self, n_heads, n_kv):
        n_rep = n_heads // n_kv
        self.n_kv, self.n_rep = n_kv, n_rep
        new = np.arange(n_heads)
        z, g = new // n_kv, new % n_kv
        self.head_perm = g * n_rep + z
        self.col_perm = (self.head_perm[:, None] * HEAD_DIM + np.arange(HEAD_DIM)[None, :]).reshape(-1)
        self.gate_perm = (np.arange(3)[:, None] * n_heads + self.head_perm[None, :]).reshape(-1)
        ege = np.zeros((3, LANES, n_heads * HEAD_DIM), np.float32)
        for br in range(3):
            for h in range(n_heads):
                ege[br, br * n_heads + h, h * HEAD_DIM:(h + 1) * HEAD_DIM] = 1.0
        self.ege = jnp.asarray(ege, BF16)


def _bucket_lookup(rel_hd, dist):
    buckets = jnp.asarray(_rel_buckets(dist).reshape(-1))
    onehot = (jnp.arange(N_BUCKETS, dtype=jnp.int32)[:, None] == buckets[None, :]).astype(F32)
    tab = jnp.dot(rel_hd, onehot, precision=lax.Precision.HIGHEST)
    return tab.reshape((rel_hd.shape[0],) + dist.shape)


def _near_bias(rel_hd, dist):
    far = rel_hd[:, N_BUCKETS - 1]
    tab = _bucket_lookup(rel_hd, dist) - far[:, None, None]
    return jnp.where(jnp.asarray(dist >= 0)[None], tab, 0.0)


def _toeplitz_tiles(rel_hd, tq, tk):
    i = np.arange(tq)[:, None]
    j = np.arange(tk)[None, :]
    return _near_bias(rel_hd, i - j), _near_bias(rel_hd, tk + i - j)


def _cmp_band(rel_hd, tq, ncp):
    i = np.arange(tq)[:, None]
    m = np.arange(ncp)[None, :]
    dist = D_CMP * (BAND_BACK - m) + i - (L_CMP - 1)
    return _near_bias(rel_hd, np.where(m < 2 * BAND_BACK + tq // D_CMP, dist, -1))


def _cmp_bias(rel_hd, qpos, n_chunk):
    dc = qpos[:, None] - (np.arange(n_chunk)[None, :] * D_CMP + L_CMP - 1)
    return _bucket_lookup(rel_hd, dc)


def kernel(x_prompt, x_sample, cache_sb_kv, cache_fox_kv, cache_fox_logf, cache_cmp_kv, cache_slc_kv, state_win_kv,
           page_table, c_prompt, c_sample, norm_gain, w_ada, b_ada, ffn_w_in, ffn_w_out, w_in_ab, b_forget,
           fox_qk_gain, w_out_ab, w_in_nsa, b_nsa_gate, nsa_qk_gain, cmp_w, cmp_pe, rel_bias, w_out_nsa):
    bp, s_len, d = x_prompt.shape
    bs, t_dec, _ = x_sample.shape
    n_pool = cache_sb_kv.shape[0]
    h_sb, h_fox = cache_sb_kv.shape[3], cache_fox_kv.shape[3]
    n_kv = cache_cmp_kv.shape[3]
    n_heads = rel_bias.shape[1]
    da = h_sb * HEAD_DIM
    dkv = n_kv * HEAD_DIM
    dq = n_heads * HEAD_DIM
    assert h_sb == h_fox
    n_pages = page_table.shape[1]
    past = n_pages * PAGE
    lay = _NsaLayout(n_heads, n_kv)

    ffn_in = ffn_w_in.astype(BF16)
    ffn_out = ffn_w_out.astype(BF16)
    w_ab = jnp.pad(w_in_ab, ((0, 0), (0, LANES - h_fox))).astype(BF16)
    bfp = jnp.pad(b_forget, (0, LANES - h_fox)).reshape(1, LANES)
    bd_ab = _head_block_diag(da, 1.0 / HEAD_DIM)
    gq_fox = jnp.tile(fox_qk_gain[0], h_fox).reshape(1, da)
    gk_fox = jnp.tile(fox_qk_gain[1], h_fox).reshape(1, da)
    wq_ab = jnp.concatenate([w_in_ab[:, 0:da], w_in_ab[:, 3 * da:4 * da]], axis=1).astype(BF16)
    wkvt_ab = jnp.concatenate([w_in_ab[:, da:3 * da], w_in_ab[:, 4 * da:6 * da]], axis=1).T.astype(BF16)
    wflt_ab = jnp.pad(w_in_ab[:, 6 * da:].T, ((0, 16 - h_fox), (0, 0))).astype(BF16)
    w_out_sb = w_out_ab[:da].astype(BF16)
    w_out_fox = w_out_ab[da:].astype(BF16)
    n_rep = n_heads // n_kv
    wq_swapped = w_in_nsa[:, 0:dq].reshape(d, n_kv, n_rep, HEAD_DIM).transpose(0, 2, 1, 3).reshape(d, dq)
    wg_swapped = w_in_nsa[:, dq + 6 * dkv:].reshape(d, 3, n_kv, n_rep).transpose(0, 1, 3, 2).reshape(d, 3 * n_heads)
    w_nsa = jnp.concatenate([wq_swapped, w_in_nsa[:, dq:dq + 6 * dkv], wg_swapped], axis=1)
    w_nsa = jnp.pad(w_nsa, ((0, 0), (0, LANES - 3 * n_heads))).astype(BF16)
    bg_swapped = b_nsa_gate.reshape(3, n_kv, n_rep).transpose(0, 2, 1).reshape(3 * n_heads)
    bg_perm = jnp.pad(bg_swapped, (0, LANES - 3 * n_heads)).reshape(1, LANES)
    wq_nsa = w_in_nsa[:, 0:dq].astype(BF16)
    wcmp_nsa = w_in_nsa[:, dq:dq + 2 * dkv].astype(BF16)
    wkvt_nsa = w_in_nsa[:, dq + 2 * dkv:dq + 6 * dkv].T.astype(BF16)
    wg_nsa = jnp.pad(w_in_nsa[:, dq + 6 * dkv:], ((0, 0), (0, LANES - 3 * n_heads))).astype(BF16)
    bg = jnp.pad(b_nsa_gate, (0, LANES - 3 * n_heads)).reshape(1, LANES)
    bd_kv = _head_block_diag(dkv, 1.0 / HEAD_DIM)
    tile_kv = lambda g: jnp.tile(g, n_kv).reshape(1, dkv)
    w_out_n = w_out_nsa.astype(BF16)
    w_out_n_perm = w_out_nsa.reshape(n_kv, n_rep, HEAD_DIM, d).transpose(1, 0, 2, 3).reshape(dq, d).astype(BF16)
    eye = jnp.eye(n_kv, dtype=F32)
    cmp_bd = jnp.einsum('gh,kjde->kjgdhe', eye, cmp_w).reshape(2, L_CMP, dkv, dkv).astype(BF16)
    cmp_pe_t = jnp.tile(cmp_pe, (1, 1, n_kv))
    rel_orig = rel_bias.T
    rel_perm = rel_bias.reshape(N_BUCKETS, n_kv, n_rep).transpose(0, 2, 1).reshape(N_BUCKETS, n_heads).T
    gk_cmp = tile_kv(nsa_qk_gain[1])

    mod = _modulation(jnp.concatenate([c_prompt, c_sample], axis=0), w_ada, b_ada)
    mod = mod.reshape(mod.shape[0], bp + bs, 3, 3, 1, d)

    def mods(l, sub, lo, hi):
        return tuple(mod[l, lo:hi, sub, k] for k in range(3))

    mp = lambda l, sub: mods(l, sub, 0, bp)
    tt = 512
    x = _ffn(x_prompt, mp(0, 0), norm_gain[0, 0], ffn_in[0, 0], ffn_out[0, 0], 1, tt)
    qsb, qfx, sbt, fxt, lft = _proj_ab_t(x, mp(0, 1), norm_gain[0, 1], wq_ab, wkvt_ab, wflt_ab, bd_ab, gq_fox,
                                         gk_fox.reshape(da, 1), b_forget.reshape(h_fox, 1), tt)
    o_sb = _sb_prompt(qsb, sbt, AB_TILE, 2)
    o_fox = _fox_prompt(qfx, fxt, lft, AB_TILE, 2)
    x = _mix_ffn(x, mp(0, 1)[2], [o_sb, o_fox], [w_out_sb, w_out_fox], mp(0, 2), norm_gain[0, 2],
                 ffn_in[0, 1], ffn_out[0, 1], tt)
    x = _ffn(x, mp(1, 0), norm_gain[1, 0], ffn_in[1, 0], ffn_out[1, 0], 1, tt)
    q, p_cmp, slct, wint, gates = _proj_nsa_t(x, mp(1, 1), norm_gain[1, 1], wq_nsa, wcmp_nsa, wkvt_nsa, wg_nsa, bd_kv,
                                              tile_kv(nsa_qk_gain[0]), tile_kv(nsa_qk_gain[2]).reshape(dkv, 1),
                                              tile_kv(nsa_qk_gain[3]).reshape(dkv, 1), bg, tt)
    kct, vct = _compress_prompt(p_cmp, cmp_bd, cmp_pe_t, bd_kv, gk_cmp)
    t0, t1 = _toeplitz_tiles(rel_orig, TILE, TILE)
    o = _nsa_prompt(q, slct, wint, kct, vct, gates, _cmp_band(rel_orig, TILE, s_len // D_CMP),
                    jnp.concatenate([t1, t0], axis=2), TILE, n_kv)
    y_p = _mix_ffn(x, mp(1, 1)[2], [o], [w_out_n], mp(1, 2), norm_gain[1, 2], ffn_in[1, 1], ffn_out[1, 1], tt)

    c_sb = jnp.transpose(cache_sb_kv, (0, 2, 3, 4, 1)).reshape(n_pool, 2 * da, PAGE)
    c_fox = jnp.transpose(cache_fox_kv, (0, 2, 3, 4, 1)).reshape(n_pool, 2 * da, PAGE)
    c_lf = jnp.transpose(cache_fox_logf, (0, 2, 1))
    c_cmp = jnp.transpose(cache_cmp_kv, (0, 2, 3, 4, 1)).reshape(n_pool, 2 * dkv, PAGE)
    c_slc = jnp.transpose(cache_slc_kv, (0, 2, 3, 4, 1)).reshape(n_pool, 2 * dkv, PAGE)
    state_t = jnp.transpose(state_win_kv, (0, 2, 3, 4, 1)).reshape(bs, 2 * dkv, state_win_kv.shape[1])
    npg = min(n_pages, STEP_BYTES // (2 * da * PAGE * 4))
    npg_kv = min(n_pages, STEP_BYTES // (2 * dkv * PAGE * 4))
    assert n_pages % npg == 0 and n_pages % npg_kv == 0
    ms = lambda l, sub: mods(l, sub, bp, bp + bs)
    x = _ffn(x_sample, ms(0, 0), norm_gain[0, 0], ffn_in[0, 0], ffn_out[0, 0], bs, t_dec)
    qsb, s_sb, qfx, s_fox, s_logf, logfp = _proj_ab(x, ms(0, 1), norm_gain[0, 1], w_ab, bd_ab, gq_fox, gk_fox, bfp,
                                                    h_fox, bs, t_dec)
    o_sb = _sb_decode(qsb, s_sb, c_sb, page_table, npg)
    o_fox = _fox_decode(qfx, s_fox, logfp, c_fox, c_lf, page_table, npg)
    x = _outproj(x, ms(0, 1)[2], [o_sb, o_fox], [w_out_sb, w_out_fox], bs, t_dec)
    x = _ffn(x, ms(0, 2), norm_gain[0, 2], ffn_in[0, 1], ffn_out[0, 1], bs, t_dec)
    x = _ffn(x, ms(1, 0), norm_gain[1, 0], ffn_in[1, 0], ffn_out[1, 0], bs, t_dec)
    q, s_cmp, s_slc, s_win, gates = _proj_nsa(x, ms(1, 1), norm_gain[1, 1], w_nsa, bd_kv, tile_kv(nsa_qk_gain[0]),
                                              tile_kv(nsa_qk_gain[2]), tile_kv(nsa_qk_gain[3]), bg_perm, dq, bs, t_dec)
    bias = _cmp_bias(rel_perm, past + np.arange(t_dec), past // D_CMP)
    o_cmp, sel = _cmp_decode(q, c_cmp, page_table, cmp_bd, cmp_pe_t, bd_kv, gk_cmp, bias, n_kv, npg_kv)
    t0, t1 = _toeplitz_tiles(rel_perm, t_dec, PAGE)
    o = _slc_decode(q, s_slc, s_win, state_t, sel, gates, o_cmp, c_slc, page_table,
                    t0.reshape(-1, PAGE), t1.reshape(-1, PAGE), lay.ege, n_kv, npg_kv)
    x = _outproj(x, ms(1, 1)[2], [o], [w_out_n_perm], bs, t_dec)
    y_s = _ffn(x, ms(1, 2), norm_gain[1, 2], ffn_in[1, 1], ffn_out[1, 1], bs, t_dec)

    kv5 = lambda a, h: a.reshape(a.shape[0], a.shape[1], 2, h, HEAD_DIM)
    from_t = lambda a, h: kv5(jnp.transpose(a, (0, 2, 1)), h)
    win_len = min(WINDOW, s_len)
    state_rows = state_win_kv.reshape(bs, state_win_kv.shape[1], 2 * dkv)
    s_win_all = jnp.concatenate([state_rows, s_win], axis=1)
    new_len = min(WINDOW, s_win_all.shape[1])
    return (y_p, y_s, from_t(sbt, h_sb), kv5(s_sb, h_sb), from_t(fxt, h_fox), kv5(s_fox, h_fox),
            jnp.transpose(lft, (0, 2, 1)), s_logf, kv5(p_cmp, n_kv), kv5(s_cmp, n_kv), from_t(slct, n_kv),
            kv5(s_slc, n_kv), from_t(wint[:, :, s_len - win_len:], n_kv),
            kv5(s_win_all[:, s_win_all.shape[1] - new_len:], n_kv))
```
